```python
import math, functools
import jax, jax.numpy as jnp
from jax import lax
import numpy as np

D_MODEL = 2048
BATCH = 4
SEQ = 2048
DEPTH = 2
DEC_BATCH = 8
DEC_SEQ = 4
PAST_LEN = 16384
PAGE_SIZE = 128

D_MIX = D_MODEL
RWKV_WIDTH = D_MIX // 4
RWKV_HEAD = 64
RWKV_HEADS = RWKV_WIDTH // RWKV_HEAD
DECAY_LORA = 64
ICLR_LORA = 64
GATE_LORA = 128
RWKV_PROJ = 3 * RWKV_WIDTH + DECAY_LORA + ICLR_LORA + GATE_LORA
RWKV_SPLITS = (RWKV_WIDTH, 2 * RWKV_WIDTH, 3 * RWKV_WIDTH, 3 * RWKV_WIDTH + DECAY_LORA, 3 * RWKV_WIDTH + DECAY_LORA + ICLR_LORA)
GN_EPS = 64e-5
LRU_WIDTH = D_MIX // 4
LRU_BLOCKS = 8
LRU_BLOCK = LRU_WIDTH // LRU_BLOCKS
CONV_W = 4
RG_C = 8.0
ATTN_WIDTH = D_MIX - RWKV_WIDTH - LRU_WIDTH
ATTN_HEAD = 128
ATTN_HEADS = ATTN_WIDTH // ATTN_HEAD
DIL_PAIRS = ((128, 1), (512, 4), (2048, 16))
ATTN_WINDOW = 2048
N_BUCKETS = 32
REL_MAX_DIST = ATTN_WINDOW
N_IN = RWKV_PROJ + 2 * LRU_WIDTH + 3 * ATTN_WIDTH
IN_SPLITS = (RWKV_PROJ, RWKV_PROJ + LRU_WIDTH, RWKV_PROJ + 2 * LRU_WIDTH, RWKV_PROJ + 2 * LRU_WIDTH + ATTN_WIDTH, RWKV_PROJ + 2 * LRU_WIDTH + 2 * ATTN_WIDTH)
D_FF = 4 * D_MODEL
RMS_EPS = 1e-6
NEG = -1e30

kernel_name = 'hybrid_rwkv7_rglru_dilated_attn_step'


def rms_norm(x, g):
    xf = x.astype(jnp.float32)
    y = xf * lax.rsqrt(jnp.mean(xf * xf, axis=-1, keepdims=True) + RMS_EPS)
    return (y * g.astype(jnp.float32)).astype(x.dtype)


def t5_bucket(dist):
    exact = N_BUCKETS // 2
    d = jnp.maximum(dist, 1).astype(jnp.float32)
    large = exact + (jnp.log(d / exact) / math.log(REL_MAX_DIST / exact) * (N_BUCKETS - exact)).astype(jnp.int32)
    return jnp.where(dist < exact, dist, jnp.minimum(large, N_BUCKETS - 1))


def rwkv7_mix(p, shift_prev, s0, mu, w0, w_up, a0, a_up, g_up, k_k, k_a, r_k, lnx_g, lnx_b):
    f32 = jnp.float32
    B, T, _ = p.shape
    shifted = jnp.concatenate([shift_prev[:, None].astype(p.dtype), p[:, :-1]], axis=1)
    m = p + (shifted - p) * mu
    r, k, v, wd, ad, gd = jnp.split(m, RWKV_SPLITS, axis=-1)
    w_log = -jax.nn.softplus(-(w0 + jnp.tanh(wd) @ w_up)) - 0.5
    decay = jnp.exp(-jnp.exp(w_log.astype(f32)))
    a = jax.nn.sigmoid(a0 + ad @ a_up)
    g = jax.nn.sigmoid(gd) @ g_up

    def heads(t):
        return t.reshape(B, T, RWKV_HEADS, RWKV_HEAD).astype(f32)

    kk = heads(k * k_k)
    kk = kk / jnp.maximum(jnp.sqrt(jnp.sum(kk * kk, axis=-1, keepdims=True)), 1e-12)
    k = k * (1.0 + (a - 1.0) * k_a)
    rh, wh, kh, vh, ah = heads(r), heads(decay), heads(k), heads(v), heads(a)

    def step(S, inp):
        r_t, w_t, k_t, v_t, kk_t, a_t = inp
        sa = jnp.einsum('bhvk,bhk->bhv', S, -kk_t)
        S = S * w_t[:, :, None, :] + sa[..., None] * (kk_t * a_t)[:, :, None, :] + v_t[..., None] * k_t[:, :, None, :]
        return S, jnp.einsum('bhvk,bhk->bhv', S, r_t)

    seq = tuple(jnp.moveaxis(t, 1, 0) for t in (rh, wh, kh, vh, kk, ah))
    s_fin, y = lax.scan(step, s0.astype(f32), seq)
    y = jnp.moveaxis(y, 0, 1)
    mean = jnp.mean(y, axis=-1, keepdims=True)
    var = jnp.mean(jnp.square(y - mean), axis=-1, keepdims=True)
    y = ((y - mean) * lax.rsqrt(var + GN_EPS)).reshape(B, T, RWKV_WIDTH) * lnx_g + lnx_b
    bonus = jnp.sum(rh * kh * r_k.astype(f32), axis=-1, keepdims=True) * vh
    y = (y + bonus.reshape(B, T, RWKV_WIDTH)) * g
    return y.astype(p.dtype), p[:, -1], s_fin


def rglru_mix(xb, gb, conv_prev, h0, conv_w, conv_b, wa, ba, wx, bx, lam):
    f32 = jnp.float32
    B, T, C = xb.shape
    xp = jnp.concatenate([conv_prev.astype(xb.dtype), xb], axis=1)
    xc = conv_b + sum(xp[:, i:i + T] * conv_w[i] for i in range(CONV_W))
    xh = xc.reshape(B, T, LRU_BLOCKS, LRU_BLOCK)
    gate_r = jax.nn.sigmoid(jnp.einsum('btnc,ncd->btnd', xh, wa).reshape(B, T, C) + ba)
    gate_i = jax.nn.sigmoid(jnp.einsum('btnc,ncd->btnd', xh, wx).reshape(B, T, C) + bx)
    log_a = (-RG_C * gate_r * jax.nn.softplus(-lam)).astype(f32)
    a = jnp.exp(log_a)
    u = jnp.sqrt(-jnp.expm1(2.0 * log_a)) * (gate_i * xc).astype(f32)

    def combine(left, right):
        a_l, u_l = left
        a_r, u_r = right
        return a_l * a_r, a_r * u_l + u_r

    a_cum, u_cum = lax.associative_scan(combine, (a, u), axis=1)
    h = a_cum * h0[:, None].astype(f32) + u_cum
    y = h * jax.nn.gelu(gb.astype(f32))
    return y.astype(xb.dtype), h[:, -1], xp[:, -(CONV_W - 1):]


def _banded_dilated(q, k, v, rel_bias, win, dil):
    B, S, H, Dh = q.shape
    span = win // dil
    L = S // dil
    nblk = -(-L // span)
    Lp = nblk * span

    def to_blocks(t):
        t = t.reshape(B, L, dil, H, Dh).transpose(0, 2, 1, 3, 4)
        t = jnp.pad(t, ((0, 0), (0, 0), (0, Lp - L), (0, 0), (0, 0)))
        return t.reshape(B, dil, nblk, span, H, Dh)

    def with_prev(t):
        prev = jnp.pad(t, ((0, 0), (0, 0), (1, 0), (0, 0), (0, 0), (0, 0)))[:, :, :-1]
        return jnp.concatenate([prev, t], axis=3)

    qb = to_blocks(q)
    kb = with_prev(to_blocks(k))
    vb = with_prev(to_blocks(v))
    qi = jnp.arange(span)[:, None]
    kj = jnp.arange(2 * span)[None, :]
    delta = qi + span - kj
    key_sub = jnp.arange(nblk)[:, None, None] * span + kj[None] - span
    valid = (delta >= 0) & (delta <= span) & (key_sub >= 0)
    bias = rel_bias[t5_bucket(jnp.clip(delta, 0, span) * dil)].astype(jnp.float32)
    s = jnp.einsum('bcnqhd,bcnkhd->bcnhqk', qb, kb) * (Dh ** -0.5) + jnp.moveaxis(bias, -1, 0)
    s = jnp.where(valid[:, None], s, NEG)
    m = jnp.max(s, axis=-1, keepdims=True)
    pr = jnp.exp(s - m)
    den = jnp.sum(pr, axis=-1, keepdims=True)
    o = jnp.einsum('bcnhqk,bcnkhd->bcnqhd', pr / den, vb)
    lse = jnp.moveaxis((m + jnp.log(den))[..., 0], 3, 4)

    def from_blocks(t):
        t = t.reshape((B, dil, Lp) + t.shape[4:])[:, :, :L]
        t = jnp.swapaxes(t, 1, 2)
        return t.reshape((B, S) + t.shape[3:])

    return from_blocks(o), from_blocks(lse)


def merge_groups(outs, lses):
    wts = jax.nn.softmax(jnp.stack(lses, axis=0), axis=0)
    return jnp.einsum('gbth,gbthd->bthd', wts, jnp.stack(outs, axis=0))


def attn_prompt(q, k, v, rel_bias):
    dt = q.dtype
    q, k, v = (t.astype(jnp.float32) for t in (q, k, v))
    outs, lses = [], []
    for win, dil in DIL_PAIRS:
        o, lse = _banded_dilated(q, k, v, rel_bias, win, dil)
        outs.append(o)
        lses.append(lse)
    return merge_groups(outs, lses).astype(dt)


def attn_sample(q, k, v, k_buf, v_buf, rel_bias):
    dt = q.dtype
    q, k, v, k_buf, v_buf = (t.astype(jnp.float32) for t in (q, k, v, k_buf, v_buf))
    T, Dh = q.shape[1], q.shape[-1]
    Wb = k_buf.shape[1]
    k_all = jnp.concatenate([k_buf, k], axis=1)
    v_all = jnp.concatenate([v_buf, v], axis=1)
    outs, lses = [], []
    for win, dil in DIL_PAIRS:
        span = win // dil
        steps = jnp.arange(span + 1)
        idx = Wb + jnp.arange(T)[:, None] - dil * steps[None, :]
        valid = idx >= 0
        idx = jnp.maximum(idx, 0)
        kg = k_all[:, idx]
        vg = v_all[:, idx]
        bias = rel_bias[t5_bucket(dil * steps)].astype(jnp.float32).T
        s = jnp.einsum('bthd,btshd->bths', q, kg) * (Dh ** -0.5) + bias
        s = jnp.where(valid[:, None, :], s, NEG)
        m = jnp.max(s, axis=-1, keepdims=True)
        pr = jnp.exp(s - m)
        den = jnp.sum(pr, axis=-1, keepdims=True)
        outs.append(jnp.einsum('bths,btshd->bthd', pr / den, vg))
        lses.append((m + jnp.log(den))[..., 0])
    return merge_groups(outs, lses).astype(dt)


def decoder_layer(x, l, prm, shift_prev, wkv0, conv_prev, h0, attend):
    B, T, _ = x.shape
    h = rms_norm(x, prm['norm_mix_pre'][l])
    proj = h @ prm['w_in'][l]
    p_rwkv, lru_x, lru_g, q, k, v = jnp.split(proj, IN_SPLITS, axis=-1)
    y_a, shift_new, wkv_new = rwkv7_mix(p_rwkv, shift_prev, wkv0, prm['rwkv_mu'][l], prm['rwkv_w0'][l], prm['rwkv_w_up'][l], prm['rwkv_a0'][l], prm['rwkv_a_up'][l], prm['rwkv_g_up'][l], prm['rwkv_k_k'][l], prm['rwkv_k_a'][l], prm['rwkv_r_k'][l], prm['rwkv_lnx_g'][l], prm['rwkv_lnx_b'][l])
    y_b, h_new, conv_new = rglru_mix(lru_x, lru_g, conv_prev, h0, prm['lru_conv_w'][l], prm['lru_conv_b'][l], prm['lru_wa'][l], prm['lru_ba'][l], prm['lru_wx'][l], prm['lru_bx'][l], prm['lru_lambda'][l])
    q, k, v = (t.reshape(B, T, ATTN_HEADS, ATTN_HEAD) for t in (q, k, v))
    y_c = attend(q, k, v).reshape(B, T, ATTN_WIDTH)
    mixed = jnp.concatenate([y_a, y_b, y_c], axis=-1) @ prm['w_out'][l]
    x = x + rms_norm(mixed, prm['norm_mix_post'][l])
    hf = rms_norm(x, prm['norm_ffn_pre'][l])
    f = jnp.square(jax.nn.relu(hf @ prm['ffn_w1'][l])) @ prm['ffn_w2'][l]
    x = x + rms_norm(f, prm['norm_ffn_post'][l])
    return x, (shift_new, wkv_new, conv_new, h_new, k, v)


def setup_inputs(seed: int = 0) -> dict:
    key = jax.random.key(seed)
    ks = list(jax.random.split(key, 48))

    def nrm(shape, scale=1.0):
        return scale * jax.random.normal(ks.pop(), shape, jnp.float32)

    def unif(shape, lo, hi):
        return jax.random.uniform(ks.pop(), shape, jnp.float32, lo, hi)

    cache_win = min(ATTN_WINDOW, PAST_LEN)
    a_init = unif((DEPTH, LRU_WIDTH), 0.9, 0.999) ** (1.0 / RG_C)
    return {
        'x_prompt': nrm((BATCH, SEQ, D_MODEL)),
        'x_sample': nrm((DEC_BATCH, DEC_SEQ, D_MODEL)),
        'state_rwkv_wkv': nrm((DEPTH, DEC_BATCH, RWKV_HEADS, RWKV_HEAD, RWKV_HEAD), 0.5),
        'state_rwkv_shift': nrm((DEPTH, DEC_BATCH, RWKV_PROJ)),
        'state_lru_h': nrm((DEPTH, DEC_BATCH, LRU_WIDTH), 0.5),
        'state_lru_conv': nrm((DEPTH, DEC_BATCH, CONV_W - 1, LRU_WIDTH)),
        'cache_attn_k': nrm((DEPTH, DEC_BATCH, cache_win, ATTN_HEADS, ATTN_HEAD)),
        'cache_attn_v': nrm((DEPTH, DEC_BATCH, cache_win, ATTN_HEADS, ATTN_HEAD)),
        'rel_bias': nrm((N_BUCKETS, ATTN_HEADS), 0.5),
        'norm_mix_pre': 1.0 + nrm((DEPTH, D_MODEL), 0.05),
        'norm_mix_post': 1.0 + nrm((DEPTH, D_MODEL), 0.05),
        'norm_ffn_pre': 1.0 + nrm((DEPTH, D_MODEL), 0.05),
        'norm_ffn_post': 1.0 + nrm((DEPTH, D_MODEL), 0.05),
        'w_in': nrm((DEPTH, D_MODEL, N_IN), D_MODEL ** -0.5),
        'w_out': nrm((DEPTH, D_MIX, D_MODEL), D_MIX ** -0.5),
        'rwkv_mu': unif((DEPTH, RWKV_PROJ), 0.0, 1.0),
        'rwkv_w0': unif((DEPTH, RWKV_WIDTH), -6.0, 0.0),
        'rwkv_w_up': nrm((DEPTH, DECAY_LORA, RWKV_WIDTH), 0.1),
        'rwkv_a0': nrm((DEPTH, RWKV_WIDTH), 0.5),
        'rwkv_a_up': nrm((DEPTH, ICLR_LORA, RWKV_WIDTH), 0.1),
        'rwkv_g_up': nrm((DEPTH, GATE_LORA, RWKV_WIDTH), GATE_LORA ** -0.5),
        'rwkv_k_k': 0.85 + nrm((DEPTH, RWKV_WIDTH), 0.05),
        'rwkv_k_a': 1.0 + nrm((DEPTH, RWKV_WIDTH), 0.05),
        'rwkv_r_k': nrm((DEPTH, RWKV_HEADS, RWKV_HEAD), 0.1),
        'rwkv_lnx_g': 1.0 + nrm((DEPTH, RWKV_WIDTH), 0.05),
        'rwkv_lnx_b': nrm((DEPTH, RWKV_WIDTH), 0.02),
        'lru_conv_w': nrm((DEPTH, CONV_W, LRU_WIDTH), CONV_W ** -0.5),
        'lru_conv_b': nrm((DEPTH, LRU_WIDTH), 0.02),
        'lru_wa': nrm((DEPTH, LRU_BLOCKS, LRU_BLOCK, LRU_BLOCK), LRU_BLOCK ** -0.5),
        'lru_ba': nrm((DEPTH, LRU_WIDTH), 0.02),
        'lru_wx': nrm((DEPTH, LRU_BLOCKS, LRU_BLOCK, LRU_BLOCK), LRU_BLOCK ** -0.5),
        'lru_bx': nrm((DEPTH, LRU_WIDTH), 0.02),
        'lru_lambda': jnp.log(a_init / (1.0 - a_init)),
        'ffn_w1': nrm((DEPTH, D_MODEL, D_FF), D_MODEL ** -0.5),
        'ffn_w2': nrm((DEPTH, D_FF, D_MODEL), D_FF ** -0.5),
    }


def reference(x_prompt, x_sample, state_rwkv_wkv, state_rwkv_shift, state_lru_h, state_lru_conv, cache_attn_k, cache_attn_v, rel_bias, norm_mix_pre, norm_mix_post, norm_ffn_pre, norm_ffn_post, w_in, w_out, rwkv_mu, rwkv_w0, rwkv_w_up, rwkv_a0, rwkv_a_up, rwkv_g_up, rwkv_k_k, rwkv_k_a, rwkv_r_k, rwkv_lnx_g, rwkv_lnx_b, lru_conv_w, lru_conv_b, lru_wa, lru_ba, lru_wx, lru_bx, lru_lambda, ffn_w1, ffn_w2):
    prm = {'norm_mix_pre': norm_mix_pre, 'norm_mix_post': norm_mix_post, 'norm_ffn_pre': norm_ffn_pre, 'norm_ffn_post': norm_ffn_post, 'w_in': w_in, 'w_out': w_out, 'rwkv_mu': rwkv_mu, 'rwkv_w0': rwkv_w0, 'rwkv_w_up': rwkv_w_up, 'rwkv_a0': rwkv_a0, 'rwkv_a_up': rwkv_a_up, 'rwkv_g_up': rwkv_g_up, 'rwkv_k_k': rwkv_k_k, 'rwkv_k_a': rwkv_k_a, 'rwkv_r_k': rwkv_r_k, 'rwkv_lnx_g': rwkv_lnx_g, 'rwkv_lnx_b': rwkv_lnx_b, 'lru_conv_w': lru_conv_w, 'lru_conv_b': lru_conv_b, 'lru_wa': lru_wa, 'lru_ba': lru_ba, 'lru_wx': lru_wx, 'lru_bx': lru_bx, 'lru_lambda': lru_lambda, 'ffn_w1': ffn_w1, 'ffn_w2': ffn_w2}
    B = x_prompt.shape[0]
    keep = min(ATTN_WINDOW, x_prompt.shape[1])
    attend_prompt = functools.partial(attn_prompt, rel_bias=rel_bias)
    xp, xs = x_prompt, x_sample
    new_p, new_s = [], []
    for l in range(DEPTH):
        xp, st_p = decoder_layer(xp, l, prm, jnp.zeros((B, RWKV_PROJ), xp.dtype), jnp.zeros((B, RWKV_HEADS, RWKV_HEAD, RWKV_HEAD), jnp.float32), jnp.zeros((B, CONV_W - 1, LRU_WIDTH), xp.dtype), jnp.zeros((B, LRU_WIDTH), jnp.float32), attend_prompt)
        attend_sample = functools.partial(attn_sample, k_buf=cache_attn_k[l], v_buf=cache_attn_v[l], rel_bias=rel_bias)
        xs, st_s = decoder_layer(xs, l, prm, state_rwkv_shift[l], state_rwkv_wkv[l], state_lru_conv[l], state_lru_h[l], attend_sample)
        new_p.append(st_p)
        new_s.append(st_s)
    shift_p = jnp.stack([s[0] for s in new_p])
    shift_s = jnp.stack([s[0] for s in new_s])
    wkv_p = jnp.stack([s[1] for s in new_p])
    wkv_s = jnp.stack([s[1] for s in new_s])
    conv_p = jnp.stack([s[2] for s in new_p])
    conv_s = jnp.stack([s[2] for s in new_s])
    h_p = jnp.stack([s[3] for s in new_p])
    h_s = jnp.stack([s[3] for s in new_s])
    k_p = jnp.stack([s[4][:, -keep:] for s in new_p])
    k_s = jnp.stack([s[4] for s in new_s])
    v_p = jnp.stack([s[5][:, -keep:] for s in new_p])
    v_s = jnp.stack([s[5] for s in new_s])
    return (xp, xs, wkv_p, wkv_s, shift_p, shift_s, h_p, h_s, conv_p, conv_s, k_p, k_s, v_p, v_s)
```

```python
import functools
import math

import numpy as np
import jax
import jax.numpy as jnp
from jax import lax
from jax.experimental import pallas as pl
from jax.experimental.pallas import tpu as pltpu

F32 = jnp.float32
BF16 = jnp.bfloat16

D_MODEL = 2048
RWKV_WIDTH = 512
RWKV_HEAD = 64
RWKV_HEADS = RWKV_WIDTH // RWKV_HEAD
DECAY_LORA = 64
ICLR_LORA = 64
GATE_LORA = 128
LORA_WIDTH = DECAY_LORA + ICLR_LORA + GATE_LORA
RWKV_PROJ = 3 * RWKV_WIDTH + LORA_WIDTH
GN_EPS = 64e-5
LRU_WIDTH = 512
LRU_BLOCKS = 8
LRU_BLOCK = LRU_WIDTH // LRU_BLOCKS
CONV_W = 4
RG_C = 8.0
ATTN_WIDTH = 1024
ATTN_HEAD = 128
ATTN_HEADS = ATTN_WIDTH // ATTN_HEAD
DIL_PAIRS = ((128, 1), (512, 4), (2048, 16))
ATTN_SPAN = 128
ATTN_WINDOW = 2048
N_BUCKETS = 32
REL_MAX_DIST = ATTN_WINDOW
D_FF = 4 * D_MODEL
RMS_EPS = 1e-6
NEG = -1e30

LANES = 128
SUBLANES = 8
VMEM_LIMIT_BYTES = 56 * 1024 * 1024

ROW_TILE = 1024
OUT_ROW_TILE = 256
FFN_ROW_TILE = 512
FFN_COL_TILE = 512
RWKV_CHUNK = 64
LRU_CHUNK = 256
ATTN_MERGE_ROWS = 256


def _cparams(sem):
    return pltpu.CompilerParams(dimension_semantics=sem, vmem_limit_bytes=VMEM_LIMIT_BYTES)


def _dot(a, b):
    return jnp.dot(a.astype(BF16), b.astype(BF16), preferred_element_type=F32)


def _dot_nt(a, b):
    return lax.dot_general(a.astype(BF16), b.astype(BF16), (((1,), (1,)), ((), ())), preferred_element_type=F32)


def _dot_tn(a, b):
    return lax.dot_general(a.astype(BF16), b.astype(BF16), (((0,), (0,)), ((), ())), preferred_element_type=F32)


def _split_bf16(x, parts):
    out, rem = [], x
    for _ in range(parts):
        hi = rem.astype(BF16)
        out.append(hi)
        rem = rem - hi.astype(F32)
    return out


def _dot_exact_rhs01(x, m01, parts):
    acc = None
    for xp in _split_bf16(x, parts):
        t = jnp.dot(xp, m01, preferred_element_type=F32)
        acc = t if acc is None else acc + t
    return acc


def _dot_exact_lhs01(m01, x, parts):
    acc = None
    for xp in _split_bf16(x, parts):
        t = jnp.dot(m01, xp, preferred_element_type=F32)
        acc = t if acc is None else acc + t
    return acc


def _softplus(z):
    return jnp.maximum(z, 0.0) + jnp.log1p(jnp.exp(-jnp.abs(z)))


def _mix_dtype(rows):
    return BF16 if rows % (2 * SUBLANES) == 0 else F32


def _rms(x, g):
    ms = jnp.mean(x * x, axis=-1, keepdims=True)
    return x * lax.rsqrt(ms + RMS_EPS) * g


def _norm_proj_kernel(x_ref, g_ref, w_ref, o_ref, h_scr):
    @pl.when(pl.program_id(1) == 0)
    def _():
        h_scr[...] = _rms(x_ref[...], g_ref[...]).astype(BF16)

    o_ref[...] = jnp.dot(h_scr[...], w_ref[...], preferred_element_type=F32)


def _norm_proj(x, g, w, col_tile):
    m, d = x.shape
    n = w.shape[1]
    tm = min(ROW_TILE, m)
    return pl.pallas_call(
        _norm_proj_kernel,
        grid=(m // tm, n // col_tile),
        in_specs=[
            pl.BlockSpec((tm, d), lambda i, j: (i, 0)),
            pl.BlockSpec((1, d), lambda i, j: (0, 0)),
            pl.BlockSpec((d, col_tile), lambda i, j: (0, j)),
        ],
        out_specs=pl.BlockSpec((tm, col_tile), lambda i, j: (i, j)),
        out_shape=jax.ShapeDtypeStruct((m, n), F32),
        scratch_shapes=[pltpu.VMEM((tm, d), BF16)],
        compiler_params=_cparams(("parallel", "arbitrary")),
        name="norm_proj",
    )(x, g, w)


def _out_proj_kernel(ya_ref, yb_ref, yc_ref, w_ref, g_ref, x_ref, o_ref):
    acc = jnp.dot(ya_ref[...].astype(BF16), w_ref[0:RWKV_WIDTH, :], preferred_element_type=F32)
    acc = acc + jnp.dot(yb_ref[...].astype(BF16), w_ref[RWKV_WIDTH:RWKV_WIDTH + LRU_WIDTH, :], preferred_element_type=F32)
    acc = acc + jnp.dot(yc_ref[...].astype(BF16), w_ref[RWKV_WIDTH + LRU_WIDTH:, :], preferred_element_type=F32)
    o_ref[...] = x_ref[...] + _rms(acc, g_ref[...])


def _out_proj(ya, yb, yc, w, g, x):
    m, d = x.shape
    tm = min(OUT_ROW_TILE, m)
    return pl.pallas_call(
        _out_proj_kernel,
        grid=(m // tm,),
        in_specs=[
            pl.BlockSpec((tm, RWKV_WIDTH), lambda i: (i, 0)),
            pl.BlockSpec((tm, LRU_WIDTH), lambda i: (i, 0)),
            pl.BlockSpec((tm, ATTN_WIDTH), lambda i: (i, 0)),
            pl.BlockSpec((d, d), lambda i: (0, 0)),
            pl.BlockSpec((1, d), lambda i: (0, 0)),
            pl.BlockSpec((tm, d), lambda i: (i, 0)),
        ],
        out_specs=pl.BlockSpec((tm, d), lambda i: (i, 0)),
        out_shape=jax.ShapeDtypeStruct((m, d), F32),
        compiler_params=_cparams(("parallel",)),
        name="out_proj",
    )(ya, yb, yc, w, g, x)


def _ffn_kernel(x_ref, g1_ref, w1_ref, w2_ref, g2_ref, o_ref, h_scr, acc_scr):
    j = pl.program_id(1)

    @pl.when(j == 0)
    def _():
        h_scr[...] = _rms(x_ref[...], g1_ref[...]).astype(BF16)
        acc_scr[...] = jnp.zeros_like(acc_scr)

    u = jnp.dot(h_scr[...], w1_ref[...], preferred_element_type=F32)
    u = jnp.square(jnp.maximum(u, 0.0)).astype(BF16)
    acc_scr[...] += jnp.dot(u, w2_ref[...], preferred_element_type=F32)

    @pl.when(j == pl.num_programs(1) - 1)
    def _():
        o_ref[...] = x_ref[...] + _rms(acc_scr[...], g2_ref[...])


def _ffn(x, g1, w1, w2, g2):
    m, d = x.shape
    f = w1.shape[1]
    tm = min(FFN_ROW_TILE, m)
    tf = FFN_COL_TILE
    return pl.pallas_call(
        _ffn_kernel,
        grid=(m // tm, f // tf),
        in_specs=[
            pl.BlockSpec((tm, d), lambda i, j: (i, 0)),
            pl.BlockSpec((1, d), lambda i, j: (0, 0)),
            pl.BlockSpec((d, tf), lambda i, j: (0, j)),
            pl.BlockSpec((tf, d), lambda i, j: (j, 0)),
            pl.BlockSpec((1, d), lambda i, j: (0, 0)),
        ],
        out_specs=pl.BlockSpec((tm, d), lambda i, j: (i, 0)),
        out_shape=jax.ShapeDtypeStruct((m, d), F32),
        scratch_shapes=[pltpu.VMEM((tm, d), BF16), pltpu.VMEM((tm, d), F32)],
        compiler_params=_cparams(("parallel", "arbitrary")),
        name="ffn",
    )(x, g1, w1, w2, g2)


def _rwkv_kernel(p_ref, shift_ref, s0_ref, mu_ref, w0_ref, wup_ref, a0_ref, aup_ref, gup_ref, kk_ref, ka_ref,
                 rk_ref, lg_ref, lb_ref, ones_ref, tri_ref,
                 y_ref, sout_ref, s_scr, prev_scr, *, chunk, t_valid):
    C = chunk
    W = RWKV_WIDTH
    H = RWKV_HEADS
    c = pl.program_id(1)

    @pl.when(c == 0)
    def _():
        s_scr[...] = s0_ref[...]
        prev_scr[...] = shift_ref[...]

    ones_bd = ones_ref[...]

    def segsum(x):
        return _dot_exact_rhs01(x, ones_bd, 2)

    p = p_ref[...]
    row = lax.broadcasted_iota(jnp.int32, (C, 1), 0)
    shifted = jnp.where(row == 0, prev_scr[...], pltpu.roll(p, 1, axis=0))
    prev_scr[...] = p_ref[C - 1:C, :]
    m = p + (shifted - p) * mu_ref[...]
    r = m[:, 0:W]
    k = m[:, W:2 * W]
    v = m[:, 2 * W:3 * W]
    x = m[:, 3 * W:]

    w = w0_ref[...] + _dot(jnp.tanh(x), wup_ref[...])
    loga = -jnp.exp(-_softplus(-w) - 0.5)
    a = jax.nn.sigmoid(a0_ref[...] + _dot(x, aup_ref[...]))
    g = _dot(jax.nn.sigmoid(x), gup_ref[...])

    kk = k * kk_ref[...]
    kk = kk / jnp.maximum(jnp.sqrt(segsum(kk * kk)), 1e-12)
    k2 = k * (1.0 + (a - 1.0) * ka_ref[...])
    if t_valid < C:
        live = row < t_valid
        loga = jnp.where(live, loga, 0.0)
        kk = jnp.where(live, kk, 0.0)
        k2 = jnp.where(live, k2, 0.0)

    cl = _dot_exact_lhs01(tri_ref[...], loga, 3)
    cl_last = cl[C - 1:C, :]
    e_in = jnp.exp(cl)
    e_neg = jnp.exp(-cl)
    e_rem = jnp.exp(cl_last - cl)
    kka = kk * a
    al = -kk * jnp.exp(cl - loga)
    be = kka * e_neg
    kt = k2 * e_neg
    rt = r * e_in
    bh = kka * e_rem
    kh = k2 * e_rem

    lane_head = lax.broadcasted_iota(jnp.int32, (1, W), 1) // RWKV_HEAD
    hm = [(lane_head == h).astype(F32) for h in range(H)]

    def stack_heads(z):
        return jnp.concatenate([z * hm[h] for h in range(H)], axis=0).astype(BF16)

    def diag_heads(z):
        acc = z[0:C, :] * hm[0]
        for h in range(1, H):
            acc = acc + z[h * C:(h + 1) * C, :] * hm[h]
        return acc

    al_st = stack_heads(al)
    rt_st = stack_heads(rt)
    ti = lax.broadcasted_iota(jnp.int32, (C, C), 0)
    si = lax.broadcasted_iota(jnp.int32, (C, C), 1)
    strict = jnp.concatenate([ti > si] * H, axis=0)
    incl = jnp.concatenate([ti >= si] * H, axis=0)
    be_b = be.astype(BF16)
    kt_b = kt.astype(BF16)
    n_ab = jnp.where(strict, _dot_nt(al_st, be_b), 0.0)
    a_ak = jnp.where(strict, _dot_nt(al_st, kt_b), 0.0)
    a_rb = jnp.where(incl, _dot_nt(rt_st, be_b), 0.0)
    a_rk = jnp.where(incl, _dot_nt(rt_st, kt_b), 0.0)

    def bmm(x3, y3):
        return lax.dot_general(x3.astype(BF16), y3.astype(BF16), (((2,), (1,)), ((0,), (0,))),
                               preferred_element_type=F32)

    n3 = n_ab.reshape(H, C, C)
    eye = (ti == si).astype(F32)
    t3 = n3 + eye[None]
    pw = n3
    for _ in range(max(int(math.log2(C)) - 1, 0)):
        pw = bmm(pw, pw)
        t3 = t3 + bmm(t3, pw)
    t_st = t3.reshape(H * C, C)

    s_prev = s_scr[...]
    s_b = s_prev.astype(BF16)
    v_b = v.astype(BF16)
    rhs = _dot_nt(al, s_b) + diag_heads(_dot(a_ak, v_b))
    u = diag_heads(_dot(t_st, rhs))
    y = _dot_nt(rt, s_b) + diag_heads(_dot(a_rb, u) + _dot(a_rk, v_b))

    pad = LANES - 2 * C
    uv = [u, v] + ([jnp.zeros((pad, W), F32)] if pad > 0 else [])
    bk = [bh, kh] + ([jnp.zeros((pad, W), F32)] if pad > 0 else [])
    upd = _dot_tn(jnp.concatenate(uv, axis=0), jnp.concatenate(bk, axis=0))
    s_new = s_prev * jnp.exp(cl_last) + upd * ones_bd.astype(F32)
    s_scr[...] = s_new

    @pl.when(c == pl.num_programs(1) - 1)
    def _():
        sout_ref[...] = s_new

    inv_n = 1.0 / RWKV_HEAD
    mean = segsum(y) * inv_n
    d = y - mean
    var = segsum(d * d) * inv_n
    yn = d * lax.rsqrt(var + GN_EPS) * lg_ref[...] + lb_ref[...]
    bonus = segsum(r * k2 * rk_ref[...]) * v
    y_ref[...] = ((yn + bonus) * g).astype(y_ref.dtype)


def _rwkv(p, shift_prev, s0_bd, prm, batch, t_len, t_valid, chunk):
    W = RWKV_WIDTH
    nc = t_len // chunk
    vec = lambda n: pl.BlockSpec((1, n), lambda b, c: (0, 0))
    mat = lambda r, n: pl.BlockSpec((r, n), lambda b, c: (0, 0))
    kern = functools.partial(_rwkv_kernel, chunk=chunk, t_valid=t_valid)
    return pl.pallas_call(
        kern,
        grid=(batch, nc),
        in_specs=[
            pl.BlockSpec((chunk, RWKV_PROJ), lambda b, c: (b * nc + c, 0)),
            pl.BlockSpec((None, 1, RWKV_PROJ), lambda b, c: (b, 0, 0)),
            pl.BlockSpec((None, W, W), lambda b, c: (b, 0, 0)),
            vec(RWKV_PROJ), vec(W), mat(LORA_WIDTH, W), vec(W), mat(LORA_WIDTH, W), mat(LORA_WIDTH, W),
            vec(W), vec(W), vec(W), vec(W), vec(W), mat(W, W), mat(chunk, chunk),
        ],
        out_specs=[
            pl.BlockSpec((chunk, W), lambda b, c: (b * nc + c, 0)),
            pl.BlockSpec((None, W, W), lambda b, c: (b, 0, 0)),
        ],
        out_shape=[
            jax.ShapeDtypeStruct((batch * t_len, W), _mix_dtype(chunk)),
            jax.ShapeDtypeStruct((batch, W, W), F32),
        ],
        scratch_shapes=[pltpu.VMEM((W, W), F32), pltpu.VMEM((1, RWKV_PROJ), F32)],
        compiler_params=_cparams(("parallel", "arbitrary")),
        name="rwkv7",
    )(p, shift_prev, s0_bd, prm["mu"], prm["w0"], prm["w_up"], prm["a0"], prm["a_up"], prm["g_up"],
      prm["k_k"], prm["k_a"], prm["r_k"], prm["lnx_g"], prm["lnx_b"], prm["ones_bd"], prm["tri"][chunk])


def _lru_kernel(x_ref, gt_ref, conv_ref, h0_ref, cw_ref, cb_ref, wa_ref, ba_ref, wx_ref, bx_ref, lam_ref,
                y_ref, hout_ref, tail_scr, h_scr, *, chunk, t_valid):
    C = chunk
    W = LRU_WIDTH
    c = pl.program_id(1)

    @pl.when(c == 0)
    def _():
        tail_scr[...] = jnp.zeros_like(tail_scr)
        tail_scr[SUBLANES - (CONV_W - 1):, :] = conv_ref[...]
        h_scr[...] = h0_ref[...]

    x = x_ref[...]
    ext = jnp.concatenate([tail_scr[...], x], axis=0)
    tail_scr[...] = x_ref[C - SUBLANES:, :]
    xc = cb_ref[...] + x * cw_ref[CONV_W - 1:CONV_W, :]
    for dlt in range(1, CONV_W):
        sh = pltpu.roll(ext, dlt, axis=0)[SUBLANES:, :]
        xc = xc + sh * cw_ref[CONV_W - 1 - dlt:CONV_W - dlt, :]

    gate_r = jax.nn.sigmoid(_dot(xc, wa_ref[...]) + ba_ref[...])
    gate_i = jax.nn.sigmoid(_dot(xc, wx_ref[...]) + bx_ref[...])
    log_a = -RG_C * gate_r * _softplus(-lam_ref[...])
    a = jnp.exp(log_a)
    th = jnp.tanh(log_a)
    u = jnp.sqrt(-2.0 * th / (1.0 - th)) * (gate_i * xc)

    row = lax.broadcasted_iota(jnp.int32, (C, 1), 0)
    dlt = 1
    while dlt < C:
        keep = row >= dlt
        a_sh = jnp.where(keep, pltpu.roll(a, dlt, axis=0), 1.0)
        u_sh = jnp.where(keep, pltpu.roll(u, dlt, axis=0), 0.0)
        u = a * u_sh + u
        a = a * a_sh
        dlt *= 2
    h = a * h_scr[...] + u
    last = min(t_valid, C) - 1
    h_scr[...] = h[last:last + 1, :]
    hout_ref[...] = h[last:last + 1, :]
    y_ref[...] = (h * jax.nn.gelu(gt_ref[...])).astype(y_ref.dtype)


def _lru(xg, conv_prev, h0, prm, batch, t_len, t_valid, chunk):
    W = LRU_WIDTH
    nc = t_len // chunk
    vec = lambda n: pl.BlockSpec((1, n), lambda b, c: (0, 0))
    kern = functools.partial(_lru_kernel, chunk=chunk, t_valid=t_valid)
    return pl.pallas_call(
        kern,
        grid=(batch, nc),
        in_specs=[
            pl.BlockSpec((chunk, W), lambda b, c: (b * nc + c, 0)),
            pl.BlockSpec((chunk, W), lambda b, c: (b * nc + c, 1)),
            pl.BlockSpec((None, CONV_W - 1, W), lambda b, c: (b, 0, 0)),
            pl.BlockSpec((None, 1, W), lambda b, c: (b, 0, 0)),
            pl.BlockSpec((CONV_W, W), lambda b, c: (0, 0)),
            vec(W),
            pl.BlockSpec((W, W), lambda b, c: (0, 0)), vec(W),
            pl.BlockSpec((W, W), lambda b, c: (0, 0)), vec(W),
            vec(W),
        ],
        out_specs=[
            pl.BlockSpec((chunk, W), lambda b, c: (b * nc + c, 0)),
            pl.BlockSpec((None, 1, W), lambda b, c: (b, 0, 0)),
        ],
        out_shape=[
            jax.ShapeDtypeStruct((batch * t_len, W), _mix_dtype(chunk)),
            jax.ShapeDtypeStruct((batch, 1, W), F32),
        ],
        scratch_shapes=[pltpu.VMEM((SUBLANES, W), F32), pltpu.VMEM((1, W), F32)],
        compiler_params=_cparams(("parallel", "arbitrary")),
        name="rglru",
    )(xg, xg, conv_prev, h0, prm["conv_w"], prm["conv_b"], prm["wa_bd"], prm["ba"], prm["wx_bd"], prm["bx"],
      prm["lam"])


def _attn_prompt_kernel(q_ref, k_ref, v_ref, bias_ref, o_ref, og_scr, lse_scr, *, seq):
    scale = ATTN_HEAD ** -0.5
    B = ATTN_SPAN
    for gi, (win, dil) in enumerate(DIL_PAIRS):
        nblk = seq // (dil * B)
        bias_prev = bias_ref[gi, :, 0:B]
        bias_cur = bias_ref[gi, :, B:2 * B]

        def body(it, carry, gi=gi, dil=dil, bias_prev=bias_prev, bias_cur=bias_cur):
            cls = it % dil
            blk = it // dil
            qs = cls + blk * (B * dil)
            ps = cls + jnp.maximum(blk - 1, 0) * (B * dil)

            def rows(ref, start):
                if dil == 1:
                    return ref[pl.ds(start, B), :]
                return ref[pl.ds(start, B, stride=dil), :]

            qb = rows(q_ref, qs).astype(BF16)
            s_c = _dot_nt(qb, rows(k_ref, qs)) * scale + bias_cur
            s_p = _dot_nt(qb, rows(k_ref, ps)) * scale + (bias_prev + jnp.where(blk == 0, NEG, 0.0))
            mx = jnp.maximum(jnp.max(s_c, axis=-1, keepdims=True), jnp.max(s_p, axis=-1, keepdims=True))
            p_c = jnp.exp(s_c - mx)
            p_p = jnp.exp(s_p - mx)
            den = jnp.sum(p_c, axis=-1, keepdims=True) + jnp.sum(p_p, axis=-1, keepdims=True)
            o = _dot(p_c / den, rows(v_ref, qs)) + _dot(p_p / den, rows(v_ref, ps))
            lse = jnp.broadcast_to(mx + jnp.log(den), (B, ATTN_HEAD))
            if dil == 1:
                og_scr[gi, pl.ds(qs, B), :] = o
                lse_scr[gi, pl.ds(qs, B), :] = lse
            else:
                og_scr[gi, pl.ds(qs, B, stride=dil), :] = o
                lse_scr[gi, pl.ds(qs, B, stride=dil), :] = lse
            return carry

        lax.fori_loop(0, dil * nblk, body, 0)

    R = ATTN_MERGE_ROWS

    def merge(i, carry):
        sl = pl.ds(pl.multiple_of(i * R, R), R)
        ls = [lse_scr[gi, sl, :] for gi in range(len(DIL_PAIRS))]
        mx = functools.reduce(jnp.maximum, ls)
        ws = [jnp.exp(l - mx) for l in ls]
        tot = functools.reduce(lambda a, b: a + b, ws)
        acc = (ws[0] / tot) * og_scr[0, sl, :]
        for gi in range(1, len(DIL_PAIRS)):
            acc = acc + (ws[gi] / tot) * og_scr[gi, sl, :]
        o_ref[sl, :] = acc.astype(o_ref.dtype)
        return carry

    lax.fori_loop(0, seq // R, merge, 0)


def _attn_prompt(qkv, bias, batch, seq):
    H = ATTN_HEADS
    G = len(DIL_PAIRS)
    kern = functools.partial(_attn_prompt_kernel, seq=seq)
    return pl.pallas_call(
        kern,
        grid=(batch, H),
        in_specs=[
            pl.BlockSpec((seq, ATTN_HEAD), lambda b, h: (b, h)),
            pl.BlockSpec((seq, ATTN_HEAD), lambda b, h: (b, H + h)),
            pl.BlockSpec((seq, ATTN_HEAD), lambda b, h: (b, 2 * H + h)),
            pl.BlockSpec((G, None, ATTN_SPAN, 2 * ATTN_SPAN), lambda b, h: (0, h, 0, 0)),
        ],
        out_specs=pl.BlockSpec((seq, ATTN_HEAD), lambda b, h: (b, h)),
        out_shape=jax.ShapeDtypeStruct((batch * seq, ATTN_WIDTH), BF16),
        scratch_shapes=[pltpu.VMEM((G, seq, ATTN_HEAD), F32), pltpu.VMEM((G, seq, ATTN_HEAD), F32)],
        compiler_params=_cparams(("parallel", "parallel")),
        name="attn_prompt",
    )(qkv, qkv, qkv, bias)


def _attn_sample_kernel(q_ref, k_ref, v_ref, kc_ref, vc_ref, bias_ref, o_ref, kall_scr, vall_scr, *, t_new, win):
    scale = ATTN_HEAD ** -0.5
    total = kall_scr.shape[0]
    kall_scr[0:win, :] = kc_ref[...]
    vall_scr[0:win, :] = vc_ref[...]
    kall_scr[win:, :] = jnp.zeros((total - win, ATTN_HEAD), F32)
    vall_scr[win:, :] = jnp.zeros((total - win, ATTN_HEAD), F32)
    kall_scr[win:win + t_new, :] = k_ref[...]
    vall_scr[win:win + t_new, :] = v_ref[...]
    s = _dot_nt(q_ref[...], kall_scr[...]) * scale
    v_b = vall_scr[...].astype(BF16)
    outs, lses = [], []
    for gi in range(len(DIL_PAIRS)):
        sg = s + bias_ref[gi]
        mx = jnp.max(sg, axis=-1, keepdims=True)
        pr = jnp.exp(sg - mx)
        den = jnp.sum(pr, axis=-1, keepdims=True)
        outs.append(_dot(pr / den, v_b))
        lses.append(mx + jnp.log(den))
    mx = functools.reduce(jnp.maximum, lses)
    ws = [jnp.exp(l - mx) for l in lses]
    tot = functools.reduce(lambda a, b: a + b, ws)
    acc = (ws[0] / tot) * outs[0]
    for gi in range(1, len(DIL_PAIRS)):
        acc = acc + (ws[gi] / tot) * outs[gi]
    o_ref[...] = acc.astype(o_ref.dtype)


def _attn_sample(qkv, k_cache, v_cache, bias, layer, batch, t_new):
    H = ATTN_HEADS
    G = len(DIL_PAIRS)
    win = k_cache.shape[2]
    total = bias.shape[-1]
    kern = functools.partial(_attn_sample_kernel, t_new=t_new, win=win)
    new = lambda off: pl.BlockSpec((None, t_new, ATTN_HEAD), lambda b, h: (b, 0, off + h))
    cache = pl.BlockSpec((None, None, win, ATTN_HEAD), lambda b, h: (layer, b, 0, h))
    return pl.pallas_call(
        kern,
        grid=(batch, H),
        in_specs=[new(0), new(H), new(2 * H), cache, cache,
                  pl.BlockSpec((G, None, t_new, total), lambda b, h: (0, h, 0, 0))],
        out_specs=pl.BlockSpec((None, t_new, ATTN_HEAD), lambda b, h: (b, 0, h)),
        out_shape=jax.ShapeDtypeStruct((batch, t_new, ATTN_WIDTH), F32),
        scratch_shapes=[pltpu.VMEM((total, ATTN_HEAD), F32), pltpu.VMEM((total, ATTN_HEAD), F32)],
        compiler_params=_cparams(("parallel", "parallel")),
        name="attn_sample",
    )(qkv, qkv, qkv, k_cache, v_cache, bias)


def _t5_bucket_static(dist):
    dist = np.asarray(dist, np.int64)
    exact = N_BUCKETS // 2
    d = np.maximum(dist, 1).astype(np.float64)
    large = exact + (np.log(d / exact) / math.log(REL_MAX_DIST / exact) * (N_BUCKETS - exact)).astype(np.int64)
    return np.where(dist < exact, dist, np.minimum(large, N_BUCKETS - 1))


def _prompt_bias(rel_bias):
    B = ATTN_SPAN
    qi = np.arange(B)[:, None]
    kj = np.arange(2 * B)[None, :]
    delta = qi + B - kj
    valid = (delta >= 0) & (delta <= B)
    tabs = []
    for _, dil in DIL_PAIRS:
        idx = _t5_bucket_static(np.clip(delta, 0, B) * dil)
        t = jnp.moveaxis(rel_bias[idx].astype(F32), -1, 0)
        tabs.append(jnp.where(valid[None], t, NEG))
    return jnp.stack(tabs)


def _sample_bias(rel_bias, t_new, win, total):
    r = np.arange(total)[None, :]
    t = np.arange(t_new)[:, None]
    dist = win + t - r
    tabs = []
    for _, dil in DIL_PAIRS:
        assert win >= dil * ATTN_SPAN
        valid = (dist >= 0) & (dist % dil == 0) & (dist <= dil * ATTN_SPAN)
        idx = _t5_bucket_static(np.clip(dist, 0, dil * ATTN_SPAN))
        tb = jnp.moveaxis(rel_bias[idx].astype(F32), -1, 0)
        tabs.append(jnp.where(valid[None], tb, NEG))
    return jnp.stack(tabs)


def _block_diag(blocks):
    n, c, d = blocks.shape
    eye = jnp.eye(n, dtype=blocks.dtype)
    return jnp.einsum("ncd,nm->ncmd", blocks, eye).reshape(n * c, n * d)


def _tri_ones(c):
    return jnp.asarray(np.tril(np.ones((c, c), np.float32)), BF16)


def _layer_params(l, a):
    W = RWKV_WIDTH
    row = lambda t: t[l].reshape(1, -1).astype(F32)

    def lora_pad(w, off):
        z = jnp.zeros((LORA_WIDTH, W), F32)
        return z.at[off:off + w.shape[0]].set(w).astype(BF16)

    head_of = np.arange(W) // RWKV_HEAD
    ones_bd = jnp.asarray((head_of[:, None] == head_of[None, :]).astype(np.float32), BF16)
    rwkv = dict(
        mu=row(a["rwkv_mu"]), w0=row(a["rwkv_w0"]), a0=row(a["rwkv_a0"]),
        w_up=lora_pad(a["rwkv_w_up"][l], 0),
        a_up=lora_pad(a["rwkv_a_up"][l], DECAY_LORA),
        g_up=lora_pad(a["rwkv_g_up"][l], DECAY_LORA + ICLR_LORA),
        k_k=row(a["rwkv_k_k"]), k_a=row(a["rwkv_k_a"]), r_k=row(a["rwkv_r_k"]),
        lnx_g=row(a["rwkv_lnx_g"]), lnx_b=row(a["rwkv_lnx_b"]),
        ones_bd=ones_bd, tri={c: _tri_ones(c) for c in (RWKV_CHUNK, SUBLANES)},
    )
    lru = dict(
        conv_w=a["lru_conv_w"][l].astype(F32), conv_b=row(a["lru_conv_b"]),
        wa_bd=_block_diag(a["lru_wa"][l]).astype(BF16), ba=row(a["lru_ba"]),
        wx_bd=_block_diag(a["lru_wx"][l]).astype(BF16), bx=row(a["lru_bx"]),
        lam=row(a["lru_lambda"]),
    )
    w_in = a["w_in"][l].astype(BF16)
    c0, c1 = RWKV_PROJ, RWKV_PROJ + 2 * LRU_WIDTH
    return dict(
        rwkv=rwkv, lru=lru,
        w_rwkv=w_in[:, :c0], w_lru=w_in[:, c0:c1], w_qkv=w_in[:, c1:],
        w_out=a["w_out"][l].astype(BF16),
        w1=a["ffn_w1"][l].astype(BF16), w2=a["ffn_w2"][l].astype(BF16),
        g_mix_pre=row(a["norm_mix_pre"]), g_mix_post=row(a["norm_mix_post"]),
        g_ffn_pre=row(a["norm_ffn_pre"]), g_ffn_post=row(a["norm_ffn_post"]),
    )


def _state_to_bd(s):
    return jax.vmap(_block_diag)(s)


def _bd_to_state(s_bd):
    b = s_bd.shape[0]
    s5 = s_bd.reshape(b, RWKV_HEADS, RWKV_HEAD, RWKV_HEADS, RWKV_HEAD)
    idx = jnp.arange(RWKV_HEADS)
    return s5[:, idx, :, idx, :].transpose(1, 0, 2, 3)


def _pad_rows(x, t_pad):
    b, t, n = x.shape
    if t_pad > t:
        x = jnp.pad(x, ((0, 0), (0, t_pad - t), (0, 0)))
    return x.reshape(b * t_pad, n)


def _layer(x, lp, batch, t_len, shift_prev, s0_bd, conv_prev, h0, attend):
    p = _norm_proj(x, lp["g_mix_pre"], lp["w_rwkv"], 256)
    xg = _norm_proj(x, lp["g_mix_pre"], lp["w_lru"], 512)
    qkv = _norm_proj(x, lp["g_mix_pre"], lp["w_qkv"], 512)

    if t_len % RWKV_CHUNK == 0:
        y_a, s_fin = _rwkv(p, shift_prev, s0_bd, lp["rwkv"], batch, t_len, t_len, RWKV_CHUNK)
        y_b, h_fin = _lru(xg, conv_prev, h0, lp["lru"], batch, t_len, t_len, LRU_CHUNK)
    else:
        assert t_len <= SUBLANES
        pp = _pad_rows(p.reshape(batch, t_len, -1), SUBLANES)
        y_a, s_fin = _rwkv(pp, shift_prev, s0_bd, lp["rwkv"], batch, SUBLANES, t_len, SUBLANES)
        y_a = y_a.reshape(batch, SUBLANES, -1)[:, :t_len].reshape(batch * t_len, -1)
        xgp = _pad_rows(xg.reshape(batch, t_len, -1), SUBLANES)
        y_b, h_fin = _lru(xgp, conv_prev, h0, lp["lru"], batch, SUBLANES, t_len, SUBLANES)
        y_b = y_b.reshape(batch, SUBLANES, -1)[:, :t_len].reshape(batch * t_len, -1)
    y_c = attend(qkv)

    x = _out_proj(y_a, y_b, y_c, lp["w_out"], lp["g_mix_post"], x)
    x = _ffn(x, lp["g_ffn_pre"], lp["w1"], lp["w2"], lp["g_ffn_post"])

    p3 = p.reshape(batch, t_len, RWKV_PROJ)
    xg3 = xg.reshape(batch, t_len, 2 * LRU_WIDTH)
    lru_x = xg3[:, :, :LRU_WIDTH]
    if t_len >= CONV_W - 1:
        conv_new = lru_x[:, t_len - (CONV_W - 1):]
    else:
        conv_new = jnp.concatenate([conv_prev, lru_x], axis=1)[:, -(CONV_W - 1):]
    qkv3 = qkv.reshape(batch, t_len, 3, ATTN_HEADS, ATTN_HEAD)
    state = (p3[:, -1], _bd_to_state(s_fin), conv_new, h_fin.reshape(batch, LRU_WIDTH), qkv3[:, :, 1], qkv3[:, :, 2])
    return x, state


def kernel(x_prompt, x_sample, state_rwkv_wkv, state_rwkv_shift, state_lru_h, state_lru_conv, cache_attn_k, cache_attn_v, rel_bias, norm_mix_pre, norm_mix_post, norm_ffn_pre, norm_ffn_post, w_in, w_out, rwkv_mu, rwkv_w0, rwkv_w_up, rwkv_a0, rwkv_a_up, rwkv_g_up, rwkv_k_k, rwkv_k_a, rwkv_r_k, rwkv_lnx_g, rwkv_lnx_b, lru_conv_w, lru_conv_b, lru_wa, lru_ba, lru_wx, lru_bx, lru_lambda, ffn_w1, ffn_w2):
    a = dict(norm_mix_pre=norm_mix_pre, norm_mix_post=norm_mix_post, norm_ffn_pre=norm_ffn_pre,
             norm_ffn_post=norm_ffn_post, w_in=w_in, w_out=w_out, rwkv_mu=rwkv_mu, rwkv_w0=rwkv_w0,
             rwkv_w_up=rwkv_w_up, rwkv_a0=rwkv_a0, rwkv_a_up=rwkv_a_up, rwkv_g_up=rwkv_g_up, rwkv_k_k=rwkv_k_k,
             rwkv_k_a=rwkv_k_a, rwkv_r_k=rwkv_r_k, rwkv_lnx_g=rwkv_lnx_g, rwkv_lnx_b=rwkv_lnx_b,
             lru_conv_w=lru_conv_w, lru_conv_b=lru_conv_b, lru_wa=lru_wa, lru_ba=lru_ba, lru_wx=lru_wx,
             lru_bx=lru_bx, lru_lambda=lru_lambda, ffn_w1=ffn_w1, ffn_w2=ffn_w2)
    depth = w_in.shape[0]
    pb, seq, _ = x_prompt.shape
    sb, t_new, _ = x_sample.shape
    win = cache_attn_k.shape[2]
    keep = min(ATTN_WINDOW, seq)
    total = win + LANES
    assert t_new <= LANES and seq % (DIL_PAIRS[-1][1] * ATTN_SPAN) == 0

    bias_p = _prompt_bias(rel_bias)
    bias_s = _sample_bias(rel_bias, t_new, win, total)
    k_cache = cache_attn_k.reshape(depth, sb, win, ATTN_WIDTH)
    v_cache = cache_attn_v.reshape(depth, sb, win, ATTN_WIDTH)

    xp = x_prompt.reshape(pb * seq, D_MODEL)
    xs = x_sample.reshape(sb * t_new, D_MODEL)
    new_p, new_s = [], []
    for l in range(depth):
        lp = _layer_params(l, a)
        attend_p = lambda qkv: _attn_prompt(qkv, bias_p, pb, seq)
        xp, st_p = _layer(xp, lp, pb, seq,
                          jnp.zeros((pb, 1, RWKV_PROJ), F32), jnp.zeros((pb, RWKV_WIDTH, RWKV_WIDTH), F32),
                          jnp.zeros((pb, CONV_W - 1, LRU_WIDTH), F32), jnp.zeros((pb, 1, LRU_WIDTH), F32), attend_p)
        attend_s = lambda qkv, l=l: _attn_sample(qkv.reshape(sb, t_new, 3 * ATTN_WIDTH), k_cache, v_cache, bias_s,
                                                 l, sb, t_new).reshape(sb * t_new, ATTN_WIDTH)
        xs, st_s = _layer(xs, lp, sb, t_new,
                          state_rwkv_shift[l].reshape(sb, 1, RWKV_PROJ), _state_to_bd(state_rwkv_wkv[l]),
                          state_lru_conv[l], state_lru_h[l].reshape(sb, 1, LRU_WIDTH), attend_s)
        new_p.append(st_p)
        new_s.append(st_s)

    stack = lambda sts, i: jnp.stack([s[i] for s in sts])
    k_p = jnp.stack([s[4][:, -keep:] for s in new_p])
    v_p = jnp.stack([s[5][:, -keep:] for s in new_p])
    return (xp.reshape(pb, seq, D_MODEL), xs.reshape(sb, t_new, D_MODEL),
            stack(new_p, 1), stack(new_s, 1), stack(new_p, 0), stack(new_s, 0),
            stack(new_p, 3), stack(new_s, 3), stack(new_p, 2), stack(new_s, 2),
            k_p, stack(new_s, 4), v_p, stack(new_s, 5))
```

```python
import functools
import math

import numpy as np
import jax
import jax.numpy as jnp
from jax import lax
from jax.experimental import pallas as pl
from jax.experimental.pallas import tpu as pltpu

F32 = jnp.float32
BF16 = jnp.bfloat16

D_MODEL = 2048
RWKV_WIDTH = 512
RWKV_HEAD = 64
RWKV_HEADS = RWKV_WIDTH // RWKV_HEAD
DECAY_LORA = 64
ICLR_LORA = 64
GATE_LORA = 128
LORA_WIDTH = DECAY_LORA + ICLR_LORA + GATE_LORA
RWKV_PROJ = 3 * RWKV_WIDTH + LORA_WIDTH
GN_EPS = 64e-5
LRU_WIDTH = 512
LRU_BLOCKS = 8
LRU_BLOCK = LRU_WIDTH // LRU_BLOCKS
CONV_W = 4
RG_C = 8.0
ATTN_WIDTH = 1024
ATTN_HEAD = 128
ATTN_HEADS = ATTN_WIDTH // ATTN_HEAD
DIL_PAIRS = ((128, 1), (512, 4), (2048, 16))
ATTN_SPAN = 128
ATTN_WINDOW = 2048
N_BUCKETS = 32
REL_MAX_DIST = ATTN_WINDOW
D_FF = 4 * D_MODEL
RMS_EPS = 1e-6
NEG = -1e30

LANES = 128
SUBLANES = 8
VMEM_LIMIT_BYTES = 56 * 1024 * 1024

ROW_TILE = 1024
OUT_ROW_TILE = 256
FFN_ROW_TILE = 512
FFN_COL_TILE = 512
RWKV_CHUNK = 64
RWKV_SEQS_PER_STEP = 4
LRU_CHUNK = 256
ATTN_MERGE_ROWS = 256
ATTN_UNROLL = 5


def _cparams(sem):
    return pltpu.CompilerParams(dimension_semantics=sem, vmem_limit_bytes=VMEM_LIMIT_BYTES)


def _dot(a, b):
    return jnp.dot(a.astype(BF16), b.astype(BF16), preferred_element_type=F32)


def _dot_nt(a, b):
    return lax.dot_general(a.astype(BF16), b.astype(BF16), (((1,), (1,)), ((), ())), preferred_element_type=F32)


def _dot_tn(a, b):
    return lax.dot_general(a.astype(BF16), b.astype(BF16), (((0,), (0,)), ((), ())), preferred_element_type=F32)


def _split_bf16(x, parts):
    out, rem = [], x
    for _ in range(parts):
        hi = rem.astype(BF16)
        out.append(hi)
        rem = rem - hi.astype(F32)
    return out


def _dot_exact_rhs01(x, m01, parts):
    acc = None
    for xp in _split_bf16(x, parts):
        t = jnp.dot(xp, m01, preferred_element_type=F32)
        acc = t if acc is None else acc + t
    return acc


def _dot_exact_lhs01(m01, x, parts):
    acc = None
    for xp in _split_bf16(x, parts):
        t = jnp.dot(m01, xp, preferred_element_type=F32)
        acc = t if acc is None else acc + t
    return acc


def _softplus(z):
    return jnp.maximum(z, 0.0) + jnp.log1p(jnp.exp(-jnp.abs(z)))


def _mix_dtype(rows):
    return BF16 if rows % (2 * SUBLANES) == 0 else F32


def _rms(x, g):
    ms = jnp.mean(x * x, axis=-1, keepdims=True)
    return x * lax.rsqrt(ms + RMS_EPS) * g


def _norm_proj_kernel(x_ref, g_ref, w_ref, o_ref, h_scr):
    @pl.when(pl.program_id(1) == 0)
    def _():
        h_scr[...] = _rms(x_ref[...], g_ref[...]).astype(BF16)

    o_ref[...] = jnp.dot(h_scr[...], w_ref[...], preferred_element_type=F32)


def _norm_proj(x, g, w, col_tile):
    m, d = x.shape
    n = w.shape[1]
    tm = min(ROW_TILE, m)
    return pl.pallas_call(
        _norm_proj_kernel,
        grid=(m // tm, n // col_tile),
        in_specs=[
            pl.BlockSpec((tm, d), lambda i, j: (i, 0)),
            pl.BlockSpec((1, d), lambda i, j: (0, 0)),
            pl.BlockSpec((d, col_tile), lambda i, j: (0, j)),
        ],
        out_specs=pl.BlockSpec((tm, col_tile), lambda i, j: (i, j)),
        out_shape=jax.ShapeDtypeStruct((m, n), F32),
        scratch_shapes=[pltpu.VMEM((tm, d), BF16)],
        compiler_params=_cparams(("parallel", "arbitrary")),
        name="norm_proj",
    )(x, g, w)


def _out_proj_kernel(ya_ref, yb_ref, yc_ref, w_ref, g_ref, x_ref, o_ref):
    acc = jnp.dot(ya_ref[...].astype(BF16), w_ref[0:RWKV_WIDTH, :], preferred_element_type=F32)
    acc = acc + jnp.dot(yb_ref[...].astype(BF16), w_ref[RWKV_WIDTH:RWKV_WIDTH + LRU_WIDTH, :], preferred_element_type=F32)
    acc = acc + jnp.dot(yc_ref[...].astype(BF16), w_ref[RWKV_WIDTH + LRU_WIDTH:, :], preferred_element_type=F32)
    o_ref[...] = x_ref[...] + _rms(acc, g_ref[...])


def _out_proj(ya, yb, yc, w, g, x):
    m, d = x.shape
    tm = min(OUT_ROW_TILE, m)
    return pl.pallas_call(
        _out_proj_kernel,
        grid=(m // tm,),
        in_specs=[
            pl.BlockSpec((tm, RWKV_WIDTH), lambda i: (i, 0)),
            pl.BlockSpec((tm, LRU_WIDTH), lambda i: (i, 0)),
            pl.BlockSpec((tm, ATTN_WIDTH), lambda i: (i, 0)),
            pl.BlockSpec((d, d), lambda i: (0, 0)),
            pl.BlockSpec((1, d), lambda i: (0, 0)),
            pl.BlockSpec((tm, d), lambda i: (i, 0)),
        ],
        out_specs=pl.BlockSpec((tm, d), lambda i: (i, 0)),
        out_shape=jax.ShapeDtypeStruct((m, d), F32),
        compiler_params=_cparams(("parallel",)),
        name="out_proj",
    )(ya, yb, yc, w, g, x)


def _ffn_kernel(x_ref, g1_ref, w1_ref, w2_ref, g2_ref, o_ref, h_scr, acc_scr):
    j = pl.program_id(1)

    @pl.when(j == 0)
    def _():
        h_scr[...] = _rms(x_ref[...], g1_ref[...]).astype(BF16)
        acc_scr[...] = jnp.zeros_like(acc_scr)

    u = jnp.dot(h_scr[...], w1_ref[...], preferred_element_type=F32)
    u = jnp.square(jnp.maximum(u, 0.0)).astype(BF16)
    acc_scr[...] += jnp.dot(u, w2_ref[...], preferred_element_type=F32)

    @pl.when(j == pl.num_programs(1) - 1)
    def _():
        o_ref[...] = x_ref[...] + _rms(acc_scr[...], g2_ref[...])


def _ffn(x, g1, w1, w2, g2):
    m, d = x.shape
    f = w1.shape[1]
    tm = min(FFN_ROW_TILE, m)
    tf = FFN_COL_TILE
    return pl.pallas_call(
        _ffn_kernel,
        grid=(m // tm, f // tf),
        in_specs=[
            pl.BlockSpec((tm, d), lambda i, j: (i, 0)),
            pl.BlockSpec((1, d), lambda i, j: (0, 0)),
            pl.BlockSpec((d, tf), lambda i, j: (0, j)),
            pl.BlockSpec((tf, d), lambda i, j: (j, 0)),
            pl.BlockSpec((1, d), lambda i, j: (0, 0)),
        ],
        out_specs=pl.BlockSpec((tm, d), lambda i, j: (i, 0)),
        out_shape=jax.ShapeDtypeStruct((m, d), F32),
        scratch_shapes=[pltpu.VMEM((tm, d), BF16), pltpu.VMEM((tm, d), F32)],
        compiler_params=_cparams(("parallel", "arbitrary")),
        name="ffn",
    )(x, g1, w1, w2, g2)


def _rwkv_kernel(p_ref, shift_ref, s0_ref, mu_ref, w0_ref, wup_ref, a0_ref, aup_ref, gup_ref, kk_ref, ka_ref,
                 rk_ref, lg_ref, lb_ref, ones_ref, tri_ref,
                 y_ref, sout_ref, s_scr, prev_scr, *, chunk, t_valid):
    C = chunk
    W = RWKV_WIDTH
    H = RWKV_HEADS
    nb = p_ref.shape[0]
    c = pl.program_id(1)

    @pl.when(c == 0)
    def _():
        s_scr[...] = s0_ref[...]
        prev_scr[...] = shift_ref[...]

    ones_bd = ones_ref[...]
    inv_n = 1.0 / RWKV_HEAD

    def segsum(xs):
        parts = []
        for x in xs:
            hi = x.astype(BF16).astype(F32)
            parts += [hi, x - hi]
        z = jnp.dot(jnp.concatenate(parts, axis=0).astype(BF16), ones_bd, preferred_element_type=F32)
        return [z[(2 * i) * C:(2 * i + 1) * C] + z[(2 * i + 1) * C:(2 * i + 2) * C] for i in range(len(xs))]

    row = lax.broadcasted_iota(jnp.int32, (C, 1), 0)
    lane_head = lax.broadcasted_iota(jnp.int32, (1, W), 1) // RWKV_HEAD
    hm = [(lane_head == h).astype(F32) for h in range(H)]
    ti = lax.broadcasted_iota(jnp.int32, (C, C), 0)
    si = lax.broadcasted_iota(jnp.int32, (C, C), 1)
    low = jnp.concatenate([ti > si] * H + [ti >= si] * H, axis=0)
    eye = (ti == si).astype(F32)

    def stack_heads(z):
        return [z * hm[h] for h in range(H)]

    def diag_heads(z):
        acc = z[0:C, :] * hm[0]
        for h in range(1, H):
            acc = acc + z[h * C:(h + 1) * C, :] * hm[h]
        return acc

    def bmm(x3, y3):
        return lax.dot_general(x3.astype(BF16), y3.astype(BF16), (((2,), (1,)), ((0,), (0,))),
                               preferred_element_type=F32)

    def one_sequence(bi):
        p = p_ref[bi]
        shifted = jnp.where(row == 0, prev_scr[bi], pltpu.roll(p, 1, axis=0))
        prev_scr[bi] = p_ref[bi, C - 1:C, :]
        m = p + (shifted - p) * mu_ref[...]
        r = m[:, 0:W]
        k = m[:, W:2 * W]
        v = m[:, 2 * W:3 * W]
        x = m[:, 3 * W:]
        w = w0_ref[...] + _dot(jnp.tanh(x), wup_ref[...])
        a = jax.nn.sigmoid(a0_ref[...] + _dot(x, aup_ref[...]))
        g = _dot(jax.nn.sigmoid(x), gup_ref[...])
        loga = -jnp.exp(-_softplus(-w) - 0.5)
        kk = k * kk_ref[...]
        k2 = k * (1.0 + (a - 1.0) * ka_ref[...])
        yield
        kk_ss, rk_ss = segsum([kk * kk, r * k2 * rk_ref[...]])
        kk = kk / jnp.maximum(jnp.sqrt(kk_ss), 1e-12)
        bonus = rk_ss * v
        if t_valid < C:
            live = row < t_valid
            loga = jnp.where(live, loga, 0.0)
            kk = jnp.where(live, kk, 0.0)
            k2 = jnp.where(live, k2, 0.0)
        cl = _dot_exact_lhs01(tri_ref[...], loga, 3)
        yield
        cl_last = cl[C - 1:C, :]
        e_neg = jnp.exp(-cl)
        e_rem = jnp.exp(cl_last - cl)
        kka = kk * a
        al = -kk * jnp.exp(cl - loga)
        rt = r * jnp.exp(cl)
        be = (kka * e_neg).astype(BF16)
        kt = (k2 * e_neg).astype(BF16)
        bh = kka * e_rem
        kh = k2 * e_rem
        lhs = jnp.concatenate(stack_heads(al) + stack_heads(rt), axis=0).astype(BF16)
        g_b = jnp.where(low, _dot_nt(lhs, be), 0.0)
        g_k = jnp.where(low, _dot_nt(lhs, kt), 0.0)
        s_prev = s_scr[bi]
        proj = _dot_nt(jnp.concatenate([al, rt], axis=0), s_prev)
        yield
        n3 = g_b[0:H * C].reshape(H, C, C)
        t3 = n3 + eye[None]
        pw = n3
        for _ in range(max(int(math.log2(C)) - 1, 0)):
            pw = bmm(pw, pw)
            yield
            t3 = t3 + bmm(t3, pw)
            yield
        v_b = v.astype(BF16)
        av = _dot(g_k, v_b)
        yield
        rhs = proj[0:C] + diag_heads(av[0:H * C])
        u = diag_heads(_dot(t3.reshape(H * C, C), rhs))
        yield
        y = proj[C:] + diag_heads(_dot(g_b[H * C:], u) + av[H * C:])
        pad = LANES - 2 * C
        uv = [u, v] + ([jnp.zeros((pad, W), F32)] if pad > 0 else [])
        bk = [bh, kh] + ([jnp.zeros((pad, W), F32)] if pad > 0 else [])
        upd = _dot_tn(jnp.concatenate(uv, axis=0), jnp.concatenate(bk, axis=0))
        s_new = s_prev * jnp.exp(cl_last) + upd * ones_bd.astype(F32)
        s_scr[bi] = s_new

        @pl.when(c == pl.num_programs(1) - 1)
        def _():
            sout_ref[bi] = s_new

        yield
        mean = segsum([y])[0] * inv_n
        yield
        d = y - mean
        var = segsum([d * d])[0] * inv_n
        yn = d * lax.rsqrt(var + GN_EPS) * lg_ref[...] + lb_ref[...]
        y_ref[bi] = ((yn + bonus) * g).astype(y_ref.dtype)

    live_seqs = [one_sequence(bi) for bi in range(nb)]
    while live_seqs:
        live_seqs = [s for s in live_seqs if next(s, StopIteration) is not StopIteration]


def _rwkv(p, shift_prev, s0_bd, prm, batch, t_len, t_valid, chunk):
    W = RWKV_WIDTH
    nc = t_len // chunk
    nb = _largest_divisor(batch, RWKV_SEQS_PER_STEP)
    vec = lambda n: pl.BlockSpec((1, n), lambda b, c: (0, 0))
    mat = lambda r, n: pl.BlockSpec((r, n), lambda b, c: (0, 0))
    kern = functools.partial(_rwkv_kernel, chunk=chunk, t_valid=t_valid)
    return pl.pallas_call(
        kern,
        grid=(batch // nb, nc),
        in_specs=[
            pl.BlockSpec((nb, chunk, RWKV_PROJ), lambda b, c: (b, c, 0)),
            pl.BlockSpec((nb, 1, RWKV_PROJ), lambda b, c: (b, 0, 0)),
            pl.BlockSpec((nb, W, W), lambda b, c: (b, 0, 0)),
            vec(RWKV_PROJ), vec(W), mat(LORA_WIDTH, W), vec(W), mat(LORA_WIDTH, W), mat(LORA_WIDTH, W),
            vec(W), vec(W), vec(W), vec(W), vec(W), mat(W, W), mat(chunk, chunk),
        ],
        out_specs=[
            pl.BlockSpec((nb, chunk, W), lambda b, c: (b, c, 0)),
            pl.BlockSpec((nb, W, W), lambda b, c: (b, 0, 0)),
        ],
        out_shape=[
            jax.ShapeDtypeStruct((batch, t_len, W), _mix_dtype(chunk)),
            jax.ShapeDtypeStruct((batch, W, W), F32),
        ],
        scratch_shapes=[pltpu.VMEM((nb, W, W), F32), pltpu.VMEM((nb, 1, RWKV_PROJ), F32)],
        compiler_params=_cparams(("parallel", "arbitrary")),
        name="rwkv7",
    )(p, shift_prev, s0_bd, prm["mu"], prm["w0"], prm["w_up"], prm["a0"], prm["a_up"], prm["g_up"],
      prm["k_k"], prm["k_a"], prm["r_k"], prm["lnx_g"], prm["lnx_b"], prm["ones_bd"], prm["tri"][chunk])


def _lru_kernel(x_ref, gt_ref, conv_ref, h0_ref, cw_ref, cb_ref, wa_ref, ba_ref, wx_ref, bx_ref, lam_ref,
                y_ref, hout_ref, tail_scr, h_scr, *, chunk, t_valid):
    C = chunk
    W = LRU_WIDTH
    c = pl.program_id(1)

    @pl.when(c == 0)
    def _():
        tail_scr[...] = jnp.zeros_like(tail_scr)
        tail_scr[SUBLANES - (CONV_W - 1):, :] = conv_ref[...]
        h_scr[...] = h0_ref[...]

    x = x_ref[...]
    ext = jnp.concatenate([tail_scr[...], x], axis=0)
    tail_scr[...] = x_ref[C - SUBLANES:, :]
    xc = cb_ref[...] + x * cw_ref[CONV_W - 1:CONV_W, :]
    for dlt in range(1, CONV_W):
        sh = pltpu.roll(ext, dlt, axis=0)[SUBLANES:, :]
        xc = xc + sh * cw_ref[CONV_W - 1 - dlt:CONV_W - dlt, :]

    gate_r = jax.nn.sigmoid(_dot(xc, wa_ref[...]) + ba_ref[...])
    gate_i = jax.nn.sigmoid(_dot(xc, wx_ref[...]) + bx_ref[...])
    log_a = -RG_C * gate_r * _softplus(-lam_ref[...])
    a = jnp.exp(log_a)
    th = jnp.tanh(log_a)
    u = jnp.sqrt(-2.0 * th / (1.0 - th)) * (gate_i * xc)

    row = lax.broadcasted_iota(jnp.int32, (C, 1), 0)
    dlt = 1
    while dlt < C:
        keep = row >= dlt
        a_sh = jnp.where(keep, pltpu.roll(a, dlt, axis=0), 1.0)
        u_sh = jnp.where(keep, pltpu.roll(u, dlt, axis=0), 0.0)
        u = a * u_sh + u
        a = a * a_sh
        dlt *= 2
    h = a * h_scr[...] + u
    last = min(t_valid, C) - 1
    h_scr[...] = h[last:last + 1, :]
    hout_ref[...] = h[last:last + 1, :]
    y_ref[...] = (h * jax.nn.gelu(gt_ref[...])).astype(y_ref.dtype)


def _lru(xg, conv_prev, h0, prm, batch, t_len, t_valid, chunk):
    W = LRU_WIDTH
    nc = t_len // chunk
    vec = lambda n: pl.BlockSpec((1, n), lambda b, c: (0, 0))
    kern = functools.partial(_lru_kernel, chunk=chunk, t_valid=t_valid)
    return pl.pallas_call(
        kern,
        grid=(batch, nc),
        in_specs=[
            pl.BlockSpec((chunk, W), lambda b, c: (b * nc + c, 0)),
            pl.BlockSpec((chunk, W), lambda b, c: (b * nc + c, 1)),
            pl.BlockSpec((None, CONV_W - 1, W), lambda b, c: (b, 0, 0)),
            pl.BlockSpec((None, 1, W), lambda b, c: (b, 0, 0)),
            pl.BlockSpec((CONV_W, W), lambda b, c: (0, 0)),
            vec(W),
            pl.BlockSpec((W, W), lambda b, c: (0, 0)), vec(W),
            pl.BlockSpec((W, W), lambda b, c: (0, 0)), vec(W),
            vec(W),
        ],
        out_specs=[
            pl.BlockSpec((chunk, W), lambda b, c: (b * nc + c, 0)),
            pl.BlockSpec((None, 1, W), lambda b, c: (b, 0, 0)),
        ],
        out_shape=[
            jax.ShapeDtypeStruct((batch * t_len, W), _mix_dtype(chunk)),
            jax.ShapeDtypeStruct((batch, 1, W), F32),
        ],
        scratch_shapes=[pltpu.VMEM((SUBLANES, W), F32), pltpu.VMEM((1, W), F32)],
        compiler_params=_cparams(("parallel", "arbitrary")),
        name="rglru",
    )(xg, xg, conv_prev, h0, prm["conv_w"], prm["conv_b"], prm["wa_bd"], prm["ba"], prm["wx_bd"], prm["bx"],
      prm["lam"])


def _attn_blocks(q_ref, k_ref, v_ref, bias_ref, og_scr, lse_scr, gi, dil, blocks):
    scale = ATTN_HEAD ** -0.5
    B = ATTN_SPAN

    def rows(start, first):
        if first is True:
            start, n = start, B
        elif first is False:
            start, n = start - B * dil, 2 * B
        else:
            start, n = jnp.maximum(start - B * dil, 0), 2 * B
        return pl.ds(start, n) if dil == 1 else pl.ds(start, n, stride=dil)

    scores = []
    for qs, first in blocks:
        qb = q_ref[rows(qs, True), :]
        if first is True:
            bias = bias_ref[gi, :, B:]
        elif first is False:
            bias = bias_ref[gi]
        else:
            own_then_masked = jnp.concatenate([bias_ref[gi, :, B:], jnp.full((B, B), NEG, F32)], axis=1)
            bias = jnp.where(first, own_then_masked, bias_ref[gi])
        scores.append(_dot_nt(qb, k_ref[rows(qs, first), :]) * scale + bias)
    probs = []
    for s in scores:
        mx = jnp.max(s, axis=-1, keepdims=True)
        pr = jnp.exp(s - mx)
        den = jnp.sum(pr, axis=-1, keepdims=True)
        probs.append(((pr / den).astype(BF16), mx + jnp.log(den)))
    outs = [_dot(pn, v_ref[rows(qs, first), :]) for (qs, first), (pn, _) in zip(blocks, probs)]
    for (qs, _), o, (_, lse) in zip(blocks, outs, probs):
        og_scr[gi, rows(qs, True), :] = o
        lse_scr[gi, rows(qs, True), :] = jnp.broadcast_to(lse, (B, ATTN_HEAD))


def _largest_divisor(n, cap):
    return max(u for u in range(1, cap + 1) if n % u == 0)


def _attn_prompt_kernel(q_ref, k_ref, v_ref, bias_ref, o_ref, og_scr, lse_scr, *, seq):
    B = ATTN_SPAN
    for gi, (win, dil) in enumerate(DIL_PAIRS):
        nblk = seq // (dil * B)
        run = functools.partial(_attn_blocks, q_ref, k_ref, v_ref, bias_ref, og_scr, lse_scr, gi, dil)
        if nblk == 1:
            per = _largest_divisor(dil, ATTN_UNROLL)

            def body(i, carry, run=run, per=per):
                run([(i * per + u, True) for u in range(per)])
                return carry

            lax.fori_loop(0, dil // per, body, 0)
        elif dil == 1:
            per = _largest_divisor(nblk, ATTN_UNROLL)

            def body(i, carry, run=run, per=per):
                run([(pl.multiple_of((i * per + u) * B, B), (i == 0) if u == 0 else False) for u in range(per)])
                return carry

            lax.fori_loop(0, nblk // per, body, 0)
        else:
            def body(cls, carry, run=run, dil=dil, nblk=nblk):
                run([(cls + blk * (B * dil), blk == 0) for blk in range(nblk)])
                return carry

            lax.fori_loop(0, dil, body, 0)

    R = ATTN_MERGE_ROWS

    def merge(i, carry):
        sl = pl.ds(pl.multiple_of(i * R, R), R)
        ls = [lse_scr[gi, sl, :] for gi in range(len(DIL_PAIRS))]
        mx = functools.reduce(jnp.maximum, ls)
        ws = [jnp.exp(l - mx) for l in ls]
        tot = functools.reduce(lambda a, b: a + b, ws)
        acc = (ws[0] / tot) * og_scr[0, sl, :]
        for gi in range(1, len(DIL_PAIRS)):
            acc = acc + (ws[gi] / tot) * og_scr[gi, sl, :]
        o_ref[sl, :] = acc.astype(o_ref.dtype)
        return carry

    lax.fori_loop(0, seq // R, merge, 0)


def _attn_prompt(qkv, bias, batch, seq):
    H = ATTN_HEADS
    G = len(DIL_PAIRS)
    kern = functools.partial(_attn_prompt_kernel, seq=seq)
    return pl.pallas_call(
        kern,
        grid=(batch, H),
        in_specs=[
            pl.BlockSpec((seq, ATTN_HEAD), lambda b, h: (b, h)),
            pl.BlockSpec((seq, ATTN_HEAD), lambda b, h: (b, H + h)),
            pl.BlockSpec((seq, ATTN_HEAD), lambda b, h: (b, 2 * H + h)),
            pl.BlockSpec((G, None, ATTN_SPAN, 2 * ATTN_SPAN), lambda b, h: (0, h, 0, 0)),
        ],
        out_specs=pl.BlockSpec((seq, ATTN_HEAD), lambda b, h: (b, h)),
        out_shape=jax.ShapeDtypeStruct((batch * seq, ATTN_WIDTH), BF16),
        scratch_shapes=[pltpu.VMEM((G, seq, ATTN_HEAD), F32), pltpu.VMEM((G, seq, ATTN_HEAD), F32)],
        compiler_params=_cparams(("parallel", "parallel")),
        name="attn_prompt",
    )(qkv, qkv, qkv, bias)


def _attn_sample_kernel(qkv_ref, kc_ref, vc_ref, bias_ref, o_ref, q_scr, kn_scr, vn_scr, *, t_new, win):
    scale = ATTN_HEAD ** -0.5
    H = ATTN_HEADS
    G = len(DIL_PAIRS)
    TP = q_scr.shape[0]
    q_scr[...] = jnp.zeros_like(q_scr)
    kn_scr[...] = jnp.zeros_like(kn_scr)
    vn_scr[...] = jnp.zeros_like(vn_scr)
    for h in range(H):
        lanes = slice(h * ATTN_HEAD, (h + 1) * ATTN_HEAD)
        q_scr[0:t_new, :] = qkv_ref[:, lanes]
        kn_scr[0:t_new, :] = qkv_ref[:, ATTN_WIDTH + h * ATTN_HEAD:ATTN_WIDTH + (h + 1) * ATTN_HEAD]
        vn_scr[0:t_new, :] = qkv_ref[:, 2 * ATTN_WIDTH + h * ATTN_HEAD:2 * ATTN_WIDTH + (h + 1) * ATTN_HEAD]
        qb = q_scr[...].astype(BF16)
        kc = kc_ref[pl.ds(h, win, stride=H), :]
        vc = vc_ref[pl.ds(h, win, stride=H), :]
        s = jnp.concatenate([_dot_nt(qb, kc), _dot_nt(qb, kn_scr[...])], axis=1) * scale
        prs, lses = [], []
        for gi in range(G):
            sg = s + bias_ref[gi, h]
            mx = jnp.max(sg, axis=-1, keepdims=True)
            pr = jnp.exp(sg - mx)
            den = jnp.sum(pr, axis=-1, keepdims=True)
            prs.append(pr / den)
            lses.append(mx + jnp.log(den))
        pall = jnp.concatenate(prs, axis=0)
        oall = _dot(pall[:, :win], vc) + _dot(pall[:, win:], vn_scr[...])
        mx = functools.reduce(jnp.maximum, lses)
        ws = [jnp.exp(l - mx) for l in lses]
        tot = functools.reduce(lambda a, b: a + b, ws)
        acc = (ws[0] / tot) * oall[0:TP]
        for gi in range(1, G):
            acc = acc + (ws[gi] / tot) * oall[gi * TP:(gi + 1) * TP]
        o_ref[:, lanes] = acc[0:t_new].astype(o_ref.dtype)


def _attn_sample(qkv, k_cache, v_cache, bias, layer, batch, t_new):
    H = ATTN_HEADS
    win = k_cache.shape[2] // H
    tp = bias.shape[2]
    kern = functools.partial(_attn_sample_kernel, t_new=t_new, win=win)
    cache = pl.BlockSpec((None, None, win * H, ATTN_HEAD), lambda b: (layer, b, 0, 0))
    return pl.pallas_call(
        kern,
        grid=(batch,),
        in_specs=[pl.BlockSpec((None, t_new, 3 * ATTN_WIDTH), lambda b: (b, 0, 0)), cache, cache,
                  pl.BlockSpec(bias.shape, lambda b: (0, 0, 0, 0))],
        out_specs=pl.BlockSpec((None, t_new, ATTN_WIDTH), lambda b: (b, 0, 0)),
        out_shape=jax.ShapeDtypeStruct((batch, t_new, ATTN_WIDTH), F32),
        scratch_shapes=[pltpu.VMEM((tp, ATTN_HEAD), F32), pltpu.VMEM((LANES, ATTN_HEAD), F32),
                        pltpu.VMEM((LANES, ATTN_HEAD), F32)],
        compiler_params=_cparams(("parallel",)),
        name="attn_sample",
    )(qkv, k_cache, v_cache, bias)


def _bias_kernel(idx_ref, rb_ref, o_ref):
    h = pl.program_id(0)
    idx = idx_ref[...]
    acc = jnp.full(idx.shape, NEG, F32)
    for b in range(N_BUCKETS):
        acc = jnp.where(idx == b, rb_ref[b, h], acc)
    o_ref[...] = acc


def _bias_table(idx, rel_bias):
    g, r, c = idx.shape
    heads = rel_bias.shape[1]
    return pl.pallas_call(
        _bias_kernel,
        grid=(heads,),
        in_specs=[pl.BlockSpec((g, r, c), lambda h: (0, 0, 0)),
                  pl.BlockSpec(memory_space=pltpu.SMEM)],
        out_specs=pl.BlockSpec((g, None, r, c), lambda h: (0, h, 0, 0)),
        out_shape=jax.ShapeDtypeStruct((g, heads, r, c), F32),
        compiler_params=_cparams(("parallel",)),
        name="bias_table",
    )(jnp.asarray(idx, jnp.int32), rel_bias.astype(F32))


def _t5_bucket_static(dist):
    dist = np.asarray(dist, np.int64)
    exact = N_BUCKETS // 2
    d = np.maximum(dist, 1).astype(np.float64)
    large = exact + (np.log(d / exact) / math.log(REL_MAX_DIST / exact) * (N_BUCKETS - exact)).astype(np.int64)
    return np.where(dist < exact, dist, np.minimum(large, N_BUCKETS - 1))


def _prompt_bias_idx():
    B = ATTN_SPAN
    qi = np.arange(B)[:, None]
    kj = np.arange(2 * B)[None, :]
    delta = qi + B - kj
    valid = (delta >= 0) & (delta <= B)
    return np.stack([np.where(valid, _t5_bucket_static(np.clip(delta, 0, B) * dil), -1) for _, dil in DIL_PAIRS])


def _sample_bias_idx(t_new, t_pad, win, total):
    r = np.arange(total)[None, :]
    t = np.arange(t_pad)[:, None]
    dist = win + t - r
    tabs = []
    for _, dil in DIL_PAIRS:
        assert win >= dil * ATTN_SPAN
        valid = (dist >= 0) & (dist % dil == 0) & (dist <= dil * ATTN_SPAN) & (t < t_new)
        tabs.append(np.where(valid, _t5_bucket_static(np.clip(dist, 0, dil * ATTN_SPAN)), -1))
    return np.stack(tabs)


def _block_diag(blocks):
    n, c, d = blocks.shape
    eye = jnp.eye(n, dtype=blocks.dtype)
    return jnp.einsum("ncd,nm->ncmd", blocks, eye).reshape(n * c, n * d)


def _tri_ones(c):
    return jnp.asarray(np.tril(np.ones((c, c), np.float32)), BF16)


def _layer_params(l, a):
    W = RWKV_WIDTH
    row = lambda t: t[l].reshape(1, -1).astype(F32)

    def lora_pad(w, off):
        z = jnp.zeros((LORA_WIDTH, W), F32)
        return z.at[off:off + w.shape[0]].set(w).astype(BF16)

    head_of = np.arange(W) // RWKV_HEAD
    ones_bd = jnp.asarray((head_of[:, None] == head_of[None, :]).astype(np.float32), BF16)
    rwkv = dict(
        mu=row(a["rwkv_mu"]), w0=row(a["rwkv_w0"]), a0=row(a["rwkv_a0"]),
        w_up=lora_pad(a["rwkv_w_up"][l], 0),
        a_up=lora_pad(a["rwkv_a_up"][l], DECAY_LORA),
        g_up=lora_pad(a["rwkv_g_up"][l], DECAY_LORA + ICLR_LORA),
        k_k=row(a["rwkv_k_k"]), k_a=row(a["rwkv_k_a"]), r_k=row(a["rwkv_r_k"]),
        lnx_g=row(a["rwkv_lnx_g"]), lnx_b=row(a["rwkv_lnx_b"]),
        ones_bd=ones_bd, tri={c: _tri_ones(c) for c in (RWKV_CHUNK, SUBLANES)},
    )
    lru = dict(
        conv_w=a["lru_conv_w"][l].astype(F32), conv_b=row(a["lru_conv_b"]),
        wa_bd=_block_diag(a["lru_wa"][l]).astype(BF16), ba=row(a["lru_ba"]),
        wx_bd=_block_diag(a["lru_wx"][l]).astype(BF16), bx=row(a["lru_bx"]),
        lam=row(a["lru_lambda"]),
    )
    w_in = a["w_in"][l].astype(BF16)
    c0, c1 = RWKV_PROJ, RWKV_PROJ + 2 * LRU_WIDTH
    return dict(
        rwkv=rwkv, lru=lru,
        w_rwkv=w_in[:, :c0], w_lru=w_in[:, c0:c1], w_qkv=w_in[:, c1:],
        w_out=a["w_out"][l].astype(BF16),
        w1=a["ffn_w1"][l].astype(BF16), w2=a["ffn_w2"][l].astype(BF16),
        g_mix_pre=row(a["norm_mix_pre"]), g_mix_post=row(a["norm_mix_post"]),
        g_ffn_pre=row(a["norm_ffn_pre"]), g_ffn_post=row(a["norm_ffn_post"]),
    )


def _state_to_bd(s):
    return jax.vmap(_block_diag)(s)


def _bd_to_state(s_bd):
    b = s_bd.shape[0]
    s5 = s_bd.reshape(b, RWKV_HEADS, RWKV_HEAD, RWKV_HEADS, RWKV_HEAD)
    idx = jnp.arange(RWKV_HEADS)
    return s5[:, idx, :, idx, :].transpose(1, 0, 2, 3)


def _pad_rows(x, t_pad):
    b, t, n = x.shape
    if t_pad > t:
        x = jnp.pad(x, ((0, 0), (0, t_pad - t), (0, 0)))
    return x.reshape(b * t_pad, n)


def _layer(x, lp, batch, t_len, shift_prev, s0_bd, conv_prev, h0, attend):
    p = _norm_proj(x, lp["g_mix_pre"], lp["w_rwkv"], 256)
    xg = _norm_proj(x, lp["g_mix_pre"], lp["w_lru"], 512)
    qkv = _norm_proj(x, lp["g_mix_pre"], lp["w_qkv"], 512)

    p3 = p.reshape(batch, t_len, RWKV_PROJ)
    if t_len % RWKV_CHUNK == 0:
        y_a, s_fin = _rwkv(p3, shift_prev, s0_bd, lp["rwkv"], batch, t_len, t_len, RWKV_CHUNK)
        y_a = y_a.reshape(batch * t_len, -1)
        y_b, h_fin = _lru(xg, conv_prev, h0, lp["lru"], batch, t_len, t_len, LRU_CHUNK)
    else:
        assert t_len <= SUBLANES
        pp = jnp.pad(p3, ((0, 0), (0, SUBLANES - t_len), (0, 0)))
        y_a, s_fin = _rwkv(pp, shift_prev, s0_bd, lp["rwkv"], batch, SUBLANES, t_len, SUBLANES)
        y_a = y_a[:, :t_len].reshape(batch * t_len, -1)
        xgp = _pad_rows(xg.reshape(batch, t_len, -1), SUBLANES)
        y_b, h_fin = _lru(xgp, conv_prev, h0, lp["lru"], batch, SUBLANES, t_len, SUBLANES)
        y_b = y_b.reshape(batch, SUBLANES, -1)[:, :t_len].reshape(batch * t_len, -1)
    y_c = attend(qkv)

    x = _out_proj(y_a, y_b, y_c, lp["w_out"], lp["g_mix_post"], x)
    x = _ffn(x, lp["g_ffn_pre"], lp["w1"], lp["w2"], lp["g_ffn_post"])

    xg3 = xg.reshape(batch, t_len, 2 * LRU_WIDTH)
    lru_x = xg3[:, :, :LRU_WIDTH]
    if t_len >= CONV_W - 1:
        conv_new = lru_x[:, t_len - (CONV_W - 1):]
    else:
        conv_new = jnp.concatenate([conv_prev, lru_x], axis=1)[:, -(CONV_W - 1):]
    qkv3 = qkv.reshape(batch, t_len, 3, ATTN_HEADS, ATTN_HEAD)
    state = (p3[:, -1], _bd_to_state(s_fin), conv_new, h_fin.reshape(batch, LRU_WIDTH), qkv3[:, :, 1], qkv3[:, :, 2])
    return x, state


def kernel(x_prompt, x_sample, state_rwkv_wkv, state_rwkv_shift, state_lru_h, state_lru_conv, cache_attn_k, cache_attn_v, rel_bias, norm_mix_pre, norm_mix_post, norm_ffn_pre, norm_ffn_post, w_in, w_out, rwkv_mu, rwkv_w0, rwkv_w_up, rwkv_a0, rwkv_a_up, rwkv_g_up, rwkv_k_k, rwkv_k_a, rwkv_r_k, rwkv_lnx_g, rwkv_lnx_b, lru_conv_w, lru_conv_b, lru_wa, lru_ba, lru_wx, lru_bx, lru_lambda, ffn_w1, ffn_w2):
    a = dict(norm_mix_pre=norm_mix_pre, norm_mix_post=norm_mix_post, norm_ffn_pre=norm_ffn_pre,
             norm_ffn_post=norm_ffn_post, w_in=w_in, w_out=w_out, rwkv_mu=rwkv_mu, rwkv_w0=rwkv_w0,
             rwkv_w_up=rwkv_w_up, rwkv_a0=rwkv_a0, rwkv_a_up=rwkv_a_up, rwkv_g_up=rwkv_g_up, rwkv_k_k=rwkv_k_k,
             rwkv_k_a=rwkv_k_a, rwkv_r_k=rwkv_r_k, rwkv_lnx_g=rwkv_lnx_g, rwkv_lnx_b=rwkv_lnx_b,
             lru_conv_w=lru_conv_w, lru_conv_b=lru_conv_b, lru_wa=lru_wa, lru_ba=lru_ba, lru_wx=lru_wx,
             lru_bx=lru_bx, lru_lambda=lru_lambda, ffn_w1=ffn_w1, ffn_w2=ffn_w2)
    depth = w_in.shape[0]
    pb, seq, _ = x_prompt.shape
    sb, t_new, _ = x_sample.shape
    win = cache_attn_k.shape[2]
    keep = min(ATTN_WINDOW, seq)
    total = win + LANES
    assert t_new <= LANES and seq % (DIL_PAIRS[-1][1] * ATTN_SPAN) == 0

    bias_p = _bias_table(_prompt_bias_idx(), rel_bias)
    bias_s = _bias_table(_sample_bias_idx(t_new, SUBLANES, win, total), rel_bias)
    k_cache = cache_attn_k.reshape(depth, sb, win * ATTN_HEADS, ATTN_HEAD)
    v_cache = cache_attn_v.reshape(depth, sb, win * ATTN_HEADS, ATTN_HEAD)

    xp = x_prompt.reshape(pb * seq, D_MODEL)
    xs = x_sample.reshape(sb * t_new, D_MODEL)
    new_p, new_s = [], []
    for l in range(depth):
        lp = _layer_params(l, a)
        attend_p = lambda qkv: _attn_prompt(qkv, bias_p, pb, seq)
        xp, st_p = _layer(xp, lp, pb, seq,
                          jnp.zeros((pb, 1, RWKV_PROJ), F32), jnp.zeros((pb, RWKV_WIDTH, RWKV_WIDTH), F32),
                          jnp.zeros((pb, CONV_W - 1, LRU_WIDTH), F32), jnp.zeros((pb, 1, LRU_WIDTH), F32), attend_p)
        attend_s = lambda qkv, l=l: _attn_sample(qkv.reshape(sb, t_new, 3 * ATTN_WIDTH), k_cache, v_cache, bias_s,
                                                 l, sb, t_new).reshape(sb * t_new, ATTN_WIDTH)
        xs, st_s = _layer(xs, lp, sb, t_new,
                          state_rwkv_shift[l].reshape(sb, 1, RWKV_PROJ), _state_to_bd(state_rwkv_wkv[l]),
                          state_lru_conv[l], state_lru_h[l].reshape(sb, 1, LRU_WIDTH), attend_s)
        new_p.append(st_p)
        new_s.append(st_s)

    stack = lambda sts, i: jnp.stack([s[i] for s in sts])
    k_p = jnp.stack([s[4][:, -keep:] for s in new_p])
    v_p = jnp.stack([s[5][:, -keep:] for s in new_p])
    return (xp.reshape(pb, seq, D_MODEL), xs.reshape(sb, t_new, D_MODEL),
            stack(new_p, 1), stack(new_s, 1), stack(new_p, 0), stack(new_s, 0),
            stack(new_p, 3), stack(new_s, 3), stack(new_p, 2), stack(new_s, 2),
            k_p, stack(new_s, 4), v_p, stack(new_s, 5))
```

```python
import functools
import math

import numpy as np
import jax
import jax.numpy as jnp
from jax import lax
from jax.experimental import pallas as pl
from jax.experimental.pallas import tpu as pltpu

F32 = jnp.float32
BF16 = jnp.bfloat16

D_MODEL = 2048
RWKV_WIDTH = 512
RWKV_HEAD = 64
RWKV_HEADS = RWKV_WIDTH // RWKV_HEAD
DECAY_LORA = 64
ICLR_LORA = 64
GATE_LORA = 128
LORA_WIDTH = DECAY_LORA + ICLR_LORA + GATE_LORA
RWKV_PROJ = 3 * RWKV_WIDTH + LORA_WIDTH
GN_EPS = 64e-5
LRU_WIDTH = 512
LRU_BLOCKS = 8
LRU_BLOCK = LRU_WIDTH // LRU_BLOCKS
CONV_W = 4
RG_C = 8.0
ATTN_WIDTH = 1024
ATTN_HEAD = 128
ATTN_HEADS = ATTN_WIDTH // ATTN_HEAD
DIL_PAIRS = ((128, 1), (512, 4), (2048, 16))
ATTN_SPAN = 128
ATTN_WINDOW = 2048
N_BUCKETS = 32
REL_MAX_DIST = ATTN_WINDOW
D_FF = 4 * D_MODEL
RMS_EPS = 1e-6
NEG = -1e30

LANES = 128
SUBLANES = 8
VMEM_LIMIT_BYTES = 56 * 1024 * 1024

ROW_TILE = 1024
IN_COL_TILE = 256
OUT_ROW_TILE = 256
FFN_ROW_TILE = 512
FFN_COL_TILE = 512
RWKV_CHUNK = 64
RWKV_SEQS_PER_STEP = 4
LRU_CHUNK = 256
ATTN_MERGE_ROWS = 256
ATTN_UNROLL = 5


def _cparams(sem):
    return pltpu.CompilerParams(dimension_semantics=sem, vmem_limit_bytes=VMEM_LIMIT_BYTES)


def _dot(a, b):
    return jnp.dot(a.astype(BF16), b.astype(BF16), preferred_element_type=F32)


def _dot_nt(a, b):
    return lax.dot_general(a.astype(BF16), b.astype(BF16), (((1,), (1,)), ((), ())), preferred_element_type=F32)


def _dot_tn(a, b):
    return lax.dot_general(a.astype(BF16), b.astype(BF16), (((0,), (0,)), ((), ())), preferred_element_type=F32)


def _split_bf16(x, parts):
    out, rem = [], x
    for _ in range(parts):
        hi = rem.astype(BF16)
        out.append(hi)
        rem = rem - hi.astype(F32)
    return out


def _dot_exact_rhs01(x, m01, parts):
    acc = None
    for xp in _split_bf16(x, parts):
        t = jnp.dot(xp, m01, preferred_element_type=F32)
        acc = t if acc is None else acc + t
    return acc


def _dot_exact_lhs01(m01, x, parts):
    acc = None
    for xp in _split_bf16(x, parts):
        t = jnp.dot(m01, xp, preferred_element_type=F32)
        acc = t if acc is None else acc + t
    return acc


def _softplus(z):
    return jnp.maximum(z, 0.0) + jnp.log1p(jnp.exp(-jnp.abs(z)))


def _mix_dtype(rows):
    return BF16 if rows % (2 * SUBLANES) == 0 else F32


def _rms(x, g):
    ms = jnp.mean(x * x, axis=-1, keepdims=True)
    return x * lax.rsqrt(ms + RMS_EPS) * g


PXQ_WIDTH = RWKV_PROJ + 2 * LRU_WIDTH + ATTN_WIDTH
PXQ_TILES = PXQ_WIDTH // IN_COL_TILE
KV_TILES = ATTN_WIDTH // IN_COL_TILE
HEADS_PER_TILE = IN_COL_TILE // ATTN_HEAD


def _in_proj_kernel(x_ref, g_ref, w_ref, pxq_ref, k_ref, v_ref, h_scr):
    j = pl.program_id(1)
    tm = x_ref.shape[0]

    @pl.when(j == 0)
    def _():
        h_scr[...] = _rms(x_ref[...], g_ref[...]).astype(BF16)

    acc = jnp.dot(h_scr[...], w_ref[...], preferred_element_type=F32)

    @pl.when(j < PXQ_TILES)
    def _():
        pxq_ref[...] = acc

    def scatter_heads(o_ref, tile):
        for hh in range(HEADS_PER_TILE):
            head = tile * HEADS_PER_TILE + hh
            o_ref[pl.ds(head, tm, stride=ATTN_HEADS), :] = acc[:, hh * ATTN_HEAD:(hh + 1) * ATTN_HEAD]

    @pl.when((j >= PXQ_TILES) & (j < PXQ_TILES + KV_TILES))
    def _():
        scatter_heads(k_ref, j - PXQ_TILES)

    @pl.when(j >= PXQ_TILES + KV_TILES)
    def _():
        scatter_heads(v_ref, j - PXQ_TILES - KV_TILES)


def _in_proj(x, g, w, layer):
    m, d = x.shape
    n = w.shape[2]
    assert n == PXQ_WIDTH + 2 * ATTN_WIDTH
    tm = min(ROW_TILE, m)
    return pl.pallas_call(
        _in_proj_kernel,
        grid=(m // tm, n // IN_COL_TILE),
        in_specs=[
            pl.BlockSpec((tm, d), lambda i, j: (i, 0)),
            pl.BlockSpec((None, 1, d), lambda i, j: (layer, 0, 0)),
            pl.BlockSpec((None, d, IN_COL_TILE), lambda i, j: (layer, 0, j)),
        ],
        out_specs=[
            pl.BlockSpec((tm, IN_COL_TILE), lambda i, j: (i, jnp.minimum(j, PXQ_TILES - 1))),
            pl.BlockSpec((tm * ATTN_HEADS, ATTN_HEAD), lambda i, j: (i, 0)),
            pl.BlockSpec((tm * ATTN_HEADS, ATTN_HEAD), lambda i, j: (i, 0)),
        ],
        out_shape=[
            jax.ShapeDtypeStruct((m, PXQ_WIDTH), F32),
            jax.ShapeDtypeStruct((m * ATTN_HEADS, ATTN_HEAD), F32),
            jax.ShapeDtypeStruct((m * ATTN_HEADS, ATTN_HEAD), F32),
        ],
        scratch_shapes=[pltpu.VMEM((tm, d), BF16)],
        compiler_params=_cparams(("parallel", "arbitrary")),
        name="in_proj",
    )(x, g, w)


def _out_proj_kernel(ya_ref, yb_ref, yc_ref, w_ref, g_ref, x_ref, o_ref):
    acc = jnp.dot(ya_ref[...].astype(BF16), w_ref[0:RWKV_WIDTH, :], preferred_element_type=F32)
    acc = acc + jnp.dot(yb_ref[...].astype(BF16), w_ref[RWKV_WIDTH:RWKV_WIDTH + LRU_WIDTH, :], preferred_element_type=F32)
    acc = acc + jnp.dot(yc_ref[...].astype(BF16), w_ref[RWKV_WIDTH + LRU_WIDTH:, :], preferred_element_type=F32)
    o_ref[...] = x_ref[...] + _rms(acc, g_ref[...])


def _out_proj(ya, yb, yc, w, g, x, layer):
    m, d = x.shape
    tm = min(OUT_ROW_TILE, m)
    return pl.pallas_call(
        _out_proj_kernel,
        grid=(m // tm,),
        in_specs=[
            pl.BlockSpec((tm, RWKV_WIDTH), lambda i: (i, 0)),
            pl.BlockSpec((tm, LRU_WIDTH), lambda i: (i, 0)),
            pl.BlockSpec((tm, ATTN_WIDTH), lambda i: (i, 0)),
            pl.BlockSpec((None, d, d), lambda i: (layer, 0, 0)),
            pl.BlockSpec((None, 1, d), lambda i: (layer, 0, 0)),
            pl.BlockSpec((tm, d), lambda i: (i, 0)),
        ],
        out_specs=pl.BlockSpec((tm, d), lambda i: (i, 0)),
        out_shape=jax.ShapeDtypeStruct((m, d), F32),
        compiler_params=_cparams(("parallel",)),
        name="out_proj",
    )(ya, yb, yc, w, g, x)


def _ffn_kernel(x_ref, g1_ref, w1_ref, w2_ref, g2_ref, o_ref, h_scr, acc_scr):
    j = pl.program_id(1)

    @pl.when(j == 0)
    def _():
        h_scr[...] = _rms(x_ref[...], g1_ref[...]).astype(BF16)
        acc_scr[...] = jnp.zeros_like(acc_scr)

    u = jnp.dot(h_scr[...], w1_ref[...], preferred_element_type=F32)
    u = jnp.square(jnp.maximum(u, 0.0)).astype(BF16)
    acc_scr[...] += jnp.dot(u, w2_ref[...], preferred_element_type=F32)

    @pl.when(j == pl.num_programs(1) - 1)
    def _():
        o_ref[...] = x_ref[...] + _rms(acc_scr[...], g2_ref[...])


def _ffn(x, g1, w1, w2, g2, layer):
    m, d = x.shape
    f = w1.shape[2]
    tm = min(FFN_ROW_TILE, m)
    tf = FFN_COL_TILE
    return pl.pallas_call(
        _ffn_kernel,
        grid=(m // tm, f // tf),
        in_specs=[
            pl.BlockSpec((tm, d), lambda i, j: (i, 0)),
            pl.BlockSpec((None, 1, d), lambda i, j: (layer, 0, 0)),
            pl.BlockSpec((None, d, tf), lambda i, j: (layer, 0, j)),
            pl.BlockSpec((None, tf, d), lambda i, j: (layer, j, 0)),
            pl.BlockSpec((None, 1, d), lambda i, j: (layer, 0, 0)),
        ],
        out_specs=pl.BlockSpec((tm, d), lambda i, j: (i, 0)),
        out_shape=jax.ShapeDtypeStruct((m, d), F32),
        scratch_shapes=[pltpu.VMEM((tm, d), BF16), pltpu.VMEM((tm, d), F32)],
        compiler_params=_cparams(("parallel", "arbitrary")),
        name="ffn",
    )(x, g1, w1, w2, g2)


def _rwkv_kernel(p_ref, shift_ref, s0_ref, mu_ref, w0_ref, wup_ref, a0_ref, aup_ref, gup_ref, kk_ref, ka_ref,
                 rk_ref, lg_ref, lb_ref, ones_ref, tri_ref,
                 y_ref, sout_ref, s_scr, prev_scr, *, chunk, t_valid):
    C = chunk
    W = RWKV_WIDTH
    H = RWKV_HEADS
    nb = p_ref.shape[0]
    c = pl.program_id(1)

    @pl.when(c == 0)
    def _():
        s_scr[...] = s0_ref[...]
        prev_scr[...] = shift_ref[...]

    ones_bd = ones_ref[...]
    inv_n = 1.0 / RWKV_HEAD

    def segsum(xs):
        parts = []
        for x in xs:
            hi = x.astype(BF16).astype(F32)
            parts += [hi, x - hi]
        z = jnp.dot(jnp.concatenate(parts, axis=0).astype(BF16), ones_bd, preferred_element_type=F32)
        return [z[(2 * i) * C:(2 * i + 1) * C] + z[(2 * i + 1) * C:(2 * i + 2) * C] for i in range(len(xs))]

    row = lax.broadcasted_iota(jnp.int32, (C, 1), 0)
    lane_head = lax.broadcasted_iota(jnp.int32, (1, W), 1) // RWKV_HEAD
    hm = [(lane_head == h).astype(F32) for h in range(H)]
    ti = lax.broadcasted_iota(jnp.int32, (C, C), 0)
    si = lax.broadcasted_iota(jnp.int32, (C, C), 1)
    low = jnp.concatenate([ti > si] * H + [ti >= si] * H, axis=0)
    eye = (ti == si).astype(F32)

    def stack_heads(z):
        return [z * hm[h] for h in range(H)]

    def diag_heads(z):
        acc = z[0:C, :] * hm[0]
        for h in range(1, H):
            acc = acc + z[h * C:(h + 1) * C, :] * hm[h]
        return acc

    def bmm(x3, y3):
        return lax.dot_general(x3.astype(BF16), y3.astype(BF16), (((2,), (1,)), ((0,), (0,))),
                               preferred_element_type=F32)

    def one_sequence(bi):
        p = p_ref[bi]
        shifted = jnp.where(row == 0, prev_scr[bi], pltpu.roll(p, 1, axis=0))
        prev_scr[bi] = p_ref[bi, C - 1:C, :]
        m = p + (shifted - p) * mu_ref[...]
        r = m[:, 0:W]
        k = m[:, W:2 * W]
        v = m[:, 2 * W:3 * W]
        x = m[:, 3 * W:]
        w = w0_ref[...] + _dot(jnp.tanh(x), wup_ref[...])
        a = jax.nn.sigmoid(a0_ref[...] + _dot(x, aup_ref[...]))
        g = _dot(jax.nn.sigmoid(x), gup_ref[...])
        loga = -jnp.exp(-_softplus(-w) - 0.5)
        kk = k * kk_ref[...]
        k2 = k * (1.0 + (a - 1.0) * ka_ref[...])
        yield
        kk_ss, rk_ss = segsum([kk * kk, r * k2 * rk_ref[...]])
        kk = kk / jnp.maximum(jnp.sqrt(kk_ss), 1e-12)
        bonus = rk_ss * v
        if t_valid < C:
            live = row < t_valid
            loga = jnp.where(live, loga, 0.0)
            kk = jnp.where(live, kk, 0.0)
            k2 = jnp.where(live, k2, 0.0)
        cl = _dot_exact_lhs01(tri_ref[...], loga, 3)
        yield
        cl_last = cl[C - 1:C, :]
        e_neg = jnp.exp(-cl)
        e_rem = jnp.exp(cl_last - cl)
        kka = kk * a
        al = -kk * jnp.exp(cl - loga)
        rt = r * jnp.exp(cl)
        be = (kka * e_neg).astype(BF16)
        kt = (k2 * e_neg).astype(BF16)
        bh = kka * e_rem
        kh = k2 * e_rem
        lhs = jnp.concatenate(stack_heads(al) + stack_heads(rt), axis=0).astype(BF16)
        g_b = jnp.where(low, _dot_nt(lhs, be), 0.0)
        g_k = jnp.where(low, _dot_nt(lhs, kt), 0.0)
        s_prev = s_scr[bi]
        proj = _dot_nt(jnp.concatenate([al, rt], axis=0), s_prev)
        yield
        n3 = g_b[0:H * C].reshape(H, C, C)
        t3 = n3 + eye[None]
        pw = n3
        for _ in range(max(int(math.log2(C)) - 1, 0)):
            pw = bmm(pw, pw)
            yield
            t3 = t3 + bmm(t3, pw)
            yield
        v_b = v.astype(BF16)
        av = _dot(g_k, v_b)
        yield
        rhs = proj[0:C] + diag_heads(av[0:H * C])
        u = diag_heads(_dot(t3.reshape(H * C, C), rhs))
        yield
        y = proj[C:] + diag_heads(_dot(g_b[H * C:], u) + av[H * C:])
        pad = LANES - 2 * C
        uv = [u, v] + ([jnp.zeros((pad, W), F32)] if pad > 0 else [])
        bk = [bh, kh] + ([jnp.zeros((pad, W), F32)] if pad > 0 else [])
        upd = _dot_tn(jnp.concatenate(uv, axis=0), jnp.concatenate(bk, axis=0))
        s_new = s_prev * jnp.exp(cl_last) + upd * ones_bd.astype(F32)
        s_scr[bi] = s_new

        @pl.when(c == pl.num_programs(1) - 1)
        def _():
            sout_ref[bi] = s_new

        yield
        mean = segsum([y])[0] * inv_n
        yield
        d = y - mean
        var = segsum([d * d])[0] * inv_n
        yn = d * lax.rsqrt(var + GN_EPS) * lg_ref[...] + lb_ref[...]
        y_ref[bi] = ((yn + bonus) * g).astype(y_ref.dtype)

    live_seqs = [one_sequence(bi) for bi in range(nb)]
    while live_seqs:
        live_seqs = [s for s in live_seqs if next(s, StopIteration) is not StopIteration]


def _rwkv(p, shift_prev, s0_bd, prm, batch, t_len, t_valid, chunk):
    W = RWKV_WIDTH
    nc = t_len // chunk
    nb = _largest_divisor(batch, RWKV_SEQS_PER_STEP)
    vec = lambda n: pl.BlockSpec((1, n), lambda b, c: (0, 0))
    mat = lambda r, n: pl.BlockSpec((r, n), lambda b, c: (0, 0))
    kern = functools.partial(_rwkv_kernel, chunk=chunk, t_valid=t_valid)
    return pl.pallas_call(
        kern,
        grid=(batch // nb, nc),
        in_specs=[
            pl.BlockSpec((nb, chunk, RWKV_PROJ), lambda b, c: (b, c, 0)),
            pl.BlockSpec((nb, 1, RWKV_PROJ), lambda b, c: (b, 0, 0)),
            pl.BlockSpec((nb, W, W), lambda b, c: (b, 0, 0)),
            vec(RWKV_PROJ), vec(W), mat(LORA_WIDTH, W), vec(W), mat(LORA_WIDTH, W), mat(LORA_WIDTH, W),
            vec(W), vec(W), vec(W), vec(W), vec(W), mat(W, W), mat(chunk, chunk),
        ],
        out_specs=[
            pl.BlockSpec((nb, chunk, W), lambda b, c: (b, c, 0)),
            pl.BlockSpec((nb, W, W), lambda b, c: (b, 0, 0)),
        ],
        out_shape=[
            jax.ShapeDtypeStruct((batch, t_len, W), _mix_dtype(chunk)),
            jax.ShapeDtypeStruct((batch, W, W), F32),
        ],
        scratch_shapes=[pltpu.VMEM((nb, W, W), F32), pltpu.VMEM((nb, 1, RWKV_PROJ), F32)],
        compiler_params=_cparams(("parallel", "arbitrary")),
        name="rwkv7",
    )(p, shift_prev, s0_bd, prm["mu"], prm["w0"], prm["w_up"], prm["a0"], prm["a_up"], prm["g_up"],
      prm["k_k"], prm["k_a"], prm["r_k"], prm["lnx_g"], prm["lnx_b"], prm["ones_bd"], prm["tri"][chunk])


def _lru_kernel(x0_ref, x1_ref, g0_ref, g1_ref, conv_ref, h0_ref, cw_ref, cb_ref, wa_ref, ba_ref, wx_ref, bx_ref,
                lam_ref, y_ref, hout_ref, tail_scr, h_scr, *, chunk, t_valid):
    C = chunk
    c = pl.program_id(1)

    @pl.when(c == 0)
    def _():
        tail_scr[...] = jnp.zeros_like(tail_scr)
        tail_scr[SUBLANES - (CONV_W - 1):, :] = conv_ref[...]
        h_scr[...] = h0_ref[...]

    x = jnp.concatenate([x0_ref[...], x1_ref[...]], axis=1)
    gt = jnp.concatenate([g0_ref[...], g1_ref[...]], axis=1)
    ext = jnp.concatenate([tail_scr[...], x], axis=0)
    tail_scr[...] = x[C - SUBLANES:, :]
    xc = cb_ref[...] + x * cw_ref[CONV_W - 1:CONV_W, :]
    for dlt in range(1, CONV_W):
        sh = pltpu.roll(ext, dlt, axis=0)[SUBLANES:, :]
        xc = xc + sh * cw_ref[CONV_W - 1 - dlt:CONV_W - dlt, :]

    gate_r = jax.nn.sigmoid(_dot(xc, wa_ref[...]) + ba_ref[...])
    gate_i = jax.nn.sigmoid(_dot(xc, wx_ref[...]) + bx_ref[...])
    log_a = -RG_C * gate_r * _softplus(-lam_ref[...])
    a = jnp.exp(log_a)
    th = jnp.tanh(log_a)
    u = jnp.sqrt(-2.0 * th / (1.0 - th)) * (gate_i * xc)

    row = lax.broadcasted_iota(jnp.int32, (C, 1), 0)
    dlt = 1
    while dlt < C:
        keep = row >= dlt
        a_sh = jnp.where(keep, pltpu.roll(a, dlt, axis=0), 1.0)
        u_sh = jnp.where(keep, pltpu.roll(u, dlt, axis=0), 0.0)
        u = a * u_sh + u
        a = a * a_sh
        dlt *= 2
    h = a * h_scr[...] + u
    last = min(t_valid, C) - 1
    h_scr[...] = h[last:last + 1, :]
    hout_ref[...] = h[last:last + 1, :]
    y_ref[...] = (h * jax.nn.gelu(gt)).astype(y_ref.dtype)


def _lru(pxq, conv_prev, h0, prm, batch, t_len, t_valid, chunk):
    W = LRU_WIDTH
    nc = t_len // chunk
    half = W // 2
    col0 = RWKV_PROJ // half
    assert RWKV_PROJ % half == 0
    part = lambda k: pl.BlockSpec((chunk, half), lambda b, c: (b * nc + c, col0 + k))
    vec = lambda n: pl.BlockSpec((1, n), lambda b, c: (0, 0))
    kern = functools.partial(_lru_kernel, chunk=chunk, t_valid=t_valid)
    return pl.pallas_call(
        kern,
        grid=(batch, nc),
        in_specs=[
            part(0), part(1), part(2), part(3),
            pl.BlockSpec((None, CONV_W - 1, W), lambda b, c: (b, 0, 0)),
            pl.BlockSpec((None, 1, W), lambda b, c: (b, 0, 0)),
            pl.BlockSpec((CONV_W, W), lambda b, c: (0, 0)),
            vec(W),
            pl.BlockSpec((W, W), lambda b, c: (0, 0)), vec(W),
            pl.BlockSpec((W, W), lambda b, c: (0, 0)), vec(W),
            vec(W),
        ],
        out_specs=[
            pl.BlockSpec((chunk, W), lambda b, c: (b * nc + c, 0)),
            pl.BlockSpec((None, 1, W), lambda b, c: (b, 0, 0)),
        ],
        out_shape=[
            jax.ShapeDtypeStruct((batch * t_len, W), _mix_dtype(chunk)),
            jax.ShapeDtypeStruct((batch, 1, W), F32),
        ],
        scratch_shapes=[pltpu.VMEM((SUBLANES, W), F32), pltpu.VMEM((1, W), F32)],
        compiler_params=_cparams(("parallel", "arbitrary")),
        name="rglru",
    )(pxq, pxq, pxq, pxq, conv_prev, h0, prm["conv_w"], prm["conv_b"], prm["wa_bd"], prm["ba"], prm["wx_bd"],
      prm["bx"], prm["lam"])


def _attn_blocks(q_ref, k_ref, v_ref, bias_ref, og_scr, lse_scr, gi, dil, blocks):
    scale = ATTN_HEAD ** -0.5
    B = ATTN_SPAN

    def rows(start, first):
        if first is True:
            start, n = start, B
        elif first is False:
            start, n = start - B * dil, 2 * B
        else:
            start, n = jnp.maximum(start - B * dil, 0), 2 * B
        return pl.ds(start, n) if dil == 1 else pl.ds(start, n, stride=dil)

    scores = []
    for qs, first in blocks:
        qb = q_ref[rows(qs, True), :]
        if first is True:
            bias = bias_ref[gi, :, B:]
        elif first is False:
            bias = bias_ref[gi]
        else:
            own_then_masked = jnp.concatenate([bias_ref[gi, :, B:], jnp.full((B, B), NEG, F32)], axis=1)
            bias = jnp.where(first, own_then_masked, bias_ref[gi])
        scores.append(_dot_nt(qb, k_ref[rows(qs, first), :]) * scale + bias)
    probs = []
    for s in scores:
        mx = jnp.max(s, axis=-1, keepdims=True)
        pr = jnp.exp(s - mx)
        den = jnp.sum(pr, axis=-1, keepdims=True)
        probs.append(((pr / den).astype(BF16), mx + jnp.log(den)))
    outs = [_dot(pn, v_ref[rows(qs, first), :]) for (qs, first), (pn, _) in zip(blocks, probs)]
    for (qs, _), o, (_, lse) in zip(blocks, outs, probs):
        og_scr[gi, rows(qs, True), :] = o
        lse_scr[gi, rows(qs, True), :] = jnp.broadcast_to(lse, (B, ATTN_HEAD))


def _largest_divisor(n, cap):
    return max(u for u in range(1, cap + 1) if n % u == 0)


def _attn_prompt_kernel(q_ref, kil_ref, vil_ref, bias_ref, o_ref, k_ref, v_ref, og_scr, lse_scr, *, seq):
    B = ATTN_SPAN
    h = pl.program_id(1)
    R = ATTN_MERGE_ROWS

    def gather_head(i, carry):
        dst = pl.ds(pl.multiple_of(i * R, R), R)
        src = pl.ds(h + i * (R * ATTN_HEADS), R, stride=ATTN_HEADS)
        k_ref[dst, :] = kil_ref[src, :]
        v_ref[dst, :] = vil_ref[src, :]
        return carry

    lax.fori_loop(0, seq // R, gather_head, 0)

    for gi, (win, dil) in enumerate(DIL_PAIRS):
        nblk = seq // (dil * B)
        run = functools.partial(_attn_blocks, q_ref, k_ref, v_ref, bias_ref, og_scr, lse_scr, gi, dil)
        if nblk == 1:
            per = _largest_divisor(dil, ATTN_UNROLL)

            def body(i, carry, run=run, per=per):
                run([(i * per + u, True) for u in range(per)])
                return carry

            lax.fori_loop(0, dil // per, body, 0)
        elif dil == 1:
            per = _largest_divisor(nblk, ATTN_UNROLL)

            def body(i, carry, run=run, per=per):
                run([(pl.multiple_of((i * per + u) * B, B), (i == 0) if u == 0 else False) for u in range(per)])
                return carry

            lax.fori_loop(0, nblk // per, body, 0)
        else:
            def body(cls, carry, run=run, dil=dil, nblk=nblk):
                run([(cls + blk * (B * dil), blk == 0) for blk in range(nblk)])
                return carry

            lax.fori_loop(0, dil, body, 0)

    def merge(i, carry):
        sl = pl.ds(pl.multiple_of(i * R, R), R)
        ls = [lse_scr[gi, sl, :] for gi in range(len(DIL_PAIRS))]
        mx = functools.reduce(jnp.maximum, ls)
        ws = [jnp.exp(l - mx) for l in ls]
        tot = functools.reduce(lambda a, b: a + b, ws)
        acc = (ws[0] / tot) * og_scr[0, sl, :]
        for gi in range(1, len(DIL_PAIRS)):
            acc = acc + (ws[gi] / tot) * og_scr[gi, sl, :]
        o_ref[sl, :] = acc.astype(o_ref.dtype)
        return carry

    lax.fori_loop(0, seq // R, merge, 0)


def _attn_prompt(pxq, k_il, v_il, bias, batch, seq):
    H = ATTN_HEADS
    G = len(DIL_PAIRS)
    q_col0 = (PXQ_WIDTH - ATTN_WIDTH) // ATTN_HEAD
    kern = functools.partial(_attn_prompt_kernel, seq=seq)
    kv = pl.BlockSpec((seq * H, ATTN_HEAD), lambda b, h: (b, 0))
    return pl.pallas_call(
        kern,
        grid=(batch, H),
        in_specs=[
            pl.BlockSpec((seq, ATTN_HEAD), lambda b, h: (b, q_col0 + h)),
            kv, kv,
            pl.BlockSpec((G, None, ATTN_SPAN, 2 * ATTN_SPAN), lambda b, h: (0, h, 0, 0)),
        ],
        out_specs=pl.BlockSpec((seq, ATTN_HEAD), lambda b, h: (b, h)),
        out_shape=jax.ShapeDtypeStruct((batch * seq, ATTN_WIDTH), BF16),
        scratch_shapes=[pltpu.VMEM((seq, ATTN_HEAD), F32), pltpu.VMEM((seq, ATTN_HEAD), F32),
                        pltpu.VMEM((G, seq, ATTN_HEAD), F32), pltpu.VMEM((G, seq, ATTN_HEAD), F32)],
        compiler_params=_cparams(("parallel", "arbitrary")),
        name="attn_prompt",
    )(pxq, k_il, v_il, bias)


def _attn_sample_kernel(pxq_ref, kn_ref, vn_ref, kc_ref, vc_ref, bias_ref, o_ref, q_scr, kn_scr, vn_scr, *, t_new, win):
    scale = ATTN_HEAD ** -0.5
    H = ATTN_HEADS
    G = len(DIL_PAIRS)
    TP = q_scr.shape[0]
    q_col0 = PXQ_WIDTH - ATTN_WIDTH
    q_scr[...] = jnp.zeros_like(q_scr)
    kn_scr[...] = jnp.zeros_like(kn_scr)
    vn_scr[...] = jnp.zeros_like(vn_scr)
    for h in range(H):
        lanes = slice(h * ATTN_HEAD, (h + 1) * ATTN_HEAD)
        q_scr[0:t_new, :] = pxq_ref[:, q_col0 + h * ATTN_HEAD:q_col0 + (h + 1) * ATTN_HEAD]
        kn_scr[0:t_new, :] = kn_ref[pl.ds(h, t_new, stride=H), :]
        vn_scr[0:t_new, :] = vn_ref[pl.ds(h, t_new, stride=H), :]
        qb = q_scr[...].astype(BF16)
        kc = kc_ref[pl.ds(h, win, stride=H), :]
        vc = vc_ref[pl.ds(h, win, stride=H), :]
        s = jnp.concatenate([_dot_nt(qb, kc), _dot_nt(qb, kn_scr[...])], axis=1) * scale
        prs, lses = [], []
        for gi in range(G):
            sg = s + bias_ref[gi, h]
            mx = jnp.max(sg, axis=-1, keepdims=True)
            pr = jnp.exp(sg - mx)
            den = jnp.sum(pr, axis=-1, keepdims=True)
            prs.append(pr / den)
            lses.append(mx + jnp.log(den))
        pall = jnp.concatenate(prs, axis=0)
        oall = _dot(pall[:, :win], vc) + _dot(pall[:, win:], vn_scr[...])
        mx = functools.reduce(jnp.maximum, lses)
        ws = [jnp.exp(l - mx) for l in lses]
        tot = functools.reduce(lambda a, b: a + b, ws)
        acc = (ws[0] / tot) * oall[0:TP]
        for gi in range(1, G):
            acc = acc + (ws[gi] / tot) * oall[gi * TP:(gi + 1) * TP]
        o_ref[:, lanes] = acc[0:t_new].astype(o_ref.dtype)


def _attn_sample(pxq, k_new, v_new, k_cache, v_cache, bias, layer, batch, t_new):
    H = ATTN_HEADS
    win = k_cache.shape[2] // H
    tp = bias.shape[2]
    kern = functools.partial(_attn_sample_kernel, t_new=t_new, win=win)
    cache = pl.BlockSpec((None, None, win * H, ATTN_HEAD), lambda b: (layer, b, 0, 0))
    new = pl.BlockSpec((None, t_new * H, ATTN_HEAD), lambda b: (b, 0, 0))
    return pl.pallas_call(
        kern,
        grid=(batch,),
        in_specs=[pl.BlockSpec((None, t_new, PXQ_WIDTH), lambda b: (b, 0, 0)), new, new, cache, cache,
                  pl.BlockSpec(bias.shape, lambda b: (0, 0, 0, 0))],
        out_specs=pl.BlockSpec((None, t_new, ATTN_WIDTH), lambda b: (b, 0, 0)),
        out_shape=jax.ShapeDtypeStruct((batch, t_new, ATTN_WIDTH), F32),
        scratch_shapes=[pltpu.VMEM((tp, ATTN_HEAD), F32), pltpu.VMEM((LANES, ATTN_HEAD), F32),
                        pltpu.VMEM((LANES, ATTN_HEAD), F32)],
        compiler_params=_cparams(("parallel",)),
        name="attn_sample",
    )(pxq, k_new, v_new, k_cache, v_cache, bias)


def _bias_kernel(idx_ref, rb_ref, o_ref):
    h = pl.program_id(0)
    idx = idx_ref[...]
    acc = jnp.full(idx.shape, NEG, F32)
    for b in range(N_BUCKETS):
        acc = jnp.where(idx == b, rb_ref[b, h], acc)
    o_ref[...] = acc


def _bias_table(idx, rel_bias):
    g, r, c = idx.shape
    heads = rel_bias.shape[1]
    return pl.pallas_call(
        _bias_kernel,
        grid=(heads,),
        in_specs=[pl.BlockSpec((g, r, c), lambda h: (0, 0, 0)),
                  pl.BlockSpec(memory_space=pltpu.SMEM)],
        out_specs=pl.BlockSpec((g, None, r, c), lambda h: (0, h, 0, 0)),
        out_shape=jax.ShapeDtypeStruct((g, heads, r, c), F32),
        compiler_params=_cparams(("parallel",)),
        name="bias_table",
    )(jnp.asarray(idx, jnp.int32), rel_bias.astype(F32))


def _t5_bucket_static(dist):
    dist = np.asarray(dist, np.int64)
    exact = N_BUCKETS // 2
    d = np.maximum(dist, 1).astype(np.float64)
    large = exact + (np.log(d / exact) / math.log(REL_MAX_DIST / exact) * (N_BUCKETS - exact)).astype(np.int64)
    return np.where(dist < exact, dist, np.minimum(large, N_BUCKETS - 1))


def _prompt_bias_idx():
    B = ATTN_SPAN
    qi = np.arange(B)[:, None]
    kj = np.arange(2 * B)[None, :]
    delta = qi + B - kj
    valid = (delta >= 0) & (delta <= B)
    return np.stack([np.where(valid, _t5_bucket_static(np.clip(delta, 0, B) * dil), -1) for _, dil in DIL_PAIRS])


def _sample_bias_idx(t_new, t_pad, win, total):
    r = np.arange(total)[None, :]
    t = np.arange(t_pad)[:, None]
    dist = win + t - r
    tabs = []
    for _, dil in DIL_PAIRS:
        assert win >= dil * ATTN_SPAN
        valid = (dist >= 0) & (dist % dil == 0) & (dist <= dil * ATTN_SPAN) & (t < t_new)
        tabs.append(np.where(valid, _t5_bucket_static(np.clip(dist, 0, dil * ATTN_SPAN)), -1))
    return np.stack(tabs)


def _block_diag(blocks):
    n, c, d = blocks.shape
    eye = jnp.eye(n, dtype=blocks.dtype)
    return jnp.einsum("ncd,nm->ncmd", blocks, eye).reshape(n * c, n * d)


def _tri_ones(c):
    return jnp.asarray(np.tril(np.ones((c, c), np.float32)), BF16)


def _layer_params(l, a):
    W = RWKV_WIDTH
    row = lambda t: t[l].reshape(1, -1).astype(F32)

    def lora_pad(w, off):
        z = jnp.zeros((LORA_WIDTH, W), F32)
        return z.at[off:off + w.shape[0]].set(w).astype(BF16)

    head_of = np.arange(W) // RWKV_HEAD
    ones_bd = jnp.asarray((head_of[:, None] == head_of[None, :]).astype(np.float32), BF16)
    rwkv = dict(
        mu=row(a["rwkv_mu"]), w0=row(a["rwkv_w0"]), a0=row(a["rwkv_a0"]),
        w_up=lora_pad(a["rwkv_w_up"][l], 0),
        a_up=lora_pad(a["rwkv_a_up"][l], DECAY_LORA),
        g_up=lora_pad(a["rwkv_g_up"][l], DECAY_LORA + ICLR_LORA),
        k_k=row(a["rwkv_k_k"]), k_a=row(a["rwkv_k_a"]), r_k=row(a["rwkv_r_k"]),
        lnx_g=row(a["rwkv_lnx_g"]), lnx_b=row(a["rwkv_lnx_b"]),
        ones_bd=ones_bd, tri={c: _tri_ones(c) for c in (RWKV_CHUNK, SUBLANES)},
    )
    lru = dict(
        conv_w=a["lru_conv_w"][l].astype(F32), conv_b=row(a["lru_conv_b"]),
        wa_bd=_block_diag(a["lru_wa"][l]).astype(BF16), ba=row(a["lru_ba"]),
        wx_bd=_block_diag(a["lru_wx"][l]).astype(BF16), bx=row(a["lru_bx"]),
        lam=row(a["lru_lambda"]),
    )
    return dict(rwkv=rwkv, lru=lru)


def _shared_params(a):
    gain = lambda t: t.reshape(t.shape[0], 1, -1).astype(F32)
    return dict(
        w_in=a["w_in"].astype(BF16), w_out=a["w_out"].astype(BF16),
        w1=a["ffn_w1"].astype(BF16), w2=a["ffn_w2"].astype(BF16),
        g_mix_pre=gain(a["norm_mix_pre"]), g_mix_post=gain(a["norm_mix_post"]),
        g_ffn_pre=gain(a["norm_ffn_pre"]), g_ffn_post=gain(a["norm_ffn_post"]),
    )


def _state_to_bd(s):
    return jax.vmap(_block_diag)(s)


def _bd_to_state(s_bd):
    b = s_bd.shape[0]
    s5 = s_bd.reshape(b, RWKV_HEADS, RWKV_HEAD, RWKV_HEADS, RWKV_HEAD)
    idx = jnp.arange(RWKV_HEADS)
    return s5[:, idx, :, idx, :].transpose(1, 0, 2, 3)


def _layer(x, layer, sp, lp, batch, t_len, shift_prev, s0_bd, conv_prev, h0, attend):
    pxq, k_il, v_il = _in_proj(x, sp["g_mix_pre"], sp["w_in"], layer)
    pxq3 = pxq.reshape(batch, t_len, PXQ_WIDTH)
    if t_len % RWKV_CHUNK == 0:
        y_a, s_fin = _rwkv(pxq3, shift_prev, s0_bd, lp["rwkv"], batch, t_len, t_len, RWKV_CHUNK)
        y_a = y_a.reshape(batch * t_len, -1)
        y_b, h_fin = _lru(pxq, conv_prev, h0, lp["lru"], batch, t_len, t_len, LRU_CHUNK)
    else:
        assert t_len <= SUBLANES
        padded = jnp.pad(pxq3, ((0, 0), (0, SUBLANES - t_len), (0, 0)))
        y_a, s_fin = _rwkv(padded, shift_prev, s0_bd, lp["rwkv"], batch, SUBLANES, t_len, SUBLANES)
        y_a = y_a[:, :t_len].reshape(batch * t_len, -1)
        y_b, h_fin = _lru(padded.reshape(batch * SUBLANES, PXQ_WIDTH), conv_prev, h0, lp["lru"], batch, SUBLANES,
                          t_len, SUBLANES)
        y_b = y_b.reshape(batch, SUBLANES, -1)[:, :t_len].reshape(batch * t_len, -1)
    y_c = attend(pxq, k_il, v_il)

    x = _out_proj(y_a, y_b, y_c, sp["w_out"], sp["g_mix_post"], x, layer)
    x = _ffn(x, sp["g_ffn_pre"], sp["w1"], sp["w2"], sp["g_ffn_post"], layer)

    lru_x = pxq3[:, :, RWKV_PROJ:RWKV_PROJ + LRU_WIDTH]
    if t_len >= CONV_W - 1:
        conv_new = lru_x[:, t_len - (CONV_W - 1):]
    else:
        conv_new = jnp.concatenate([conv_prev, lru_x], axis=1)[:, -(CONV_W - 1):]
    k_new = k_il.reshape(batch, t_len, ATTN_HEADS, ATTN_HEAD)
    v_new = v_il.reshape(batch, t_len, ATTN_HEADS, ATTN_HEAD)
    state = (pxq3[:, -1, :RWKV_PROJ], _bd_to_state(s_fin), conv_new, h_fin.reshape(batch, LRU_WIDTH), k_new, v_new)
    return x, state


def kernel(x_prompt, x_sample, state_rwkv_wkv, state_rwkv_shift, state_lru_h, state_lru_conv, cache_attn_k, cache_attn_v, rel_bias, norm_mix_pre, norm_mix_post, norm_ffn_pre, norm_ffn_post, w_in, w_out, rwkv_mu, rwkv_w0, rwkv_w_up, rwkv_a0, rwkv_a_up, rwkv_g_up, rwkv_k_k, rwkv_k_a, rwkv_r_k, rwkv_lnx_g, rwkv_lnx_b, lru_conv_w, lru_conv_b, lru_wa, lru_ba, lru_wx, lru_bx, lru_lambda, ffn_w1, ffn_w2):
    a = dict(norm_mix_pre=norm_mix_pre, norm_mix_post=norm_mix_post, norm_ffn_pre=norm_ffn_pre,
             norm_ffn_post=norm_ffn_post, w_in=w_in, w_out=w_out, rwkv_mu=rwkv_mu, rwkv_w0=rwkv_w0,
             rwkv_w_up=rwkv_w_up, rwkv_a0=rwkv_a0, rwkv_a_up=rwkv_a_up, rwkv_g_up=rwkv_g_up, rwkv_k_k=rwkv_k_k,
             rwkv_k_a=rwkv_k_a, rwkv_r_k=rwkv_r_k, rwkv_lnx_g=rwkv_lnx_g, rwkv_lnx_b=rwkv_lnx_b,
             lru_conv_w=lru_conv_w, lru_conv_b=lru_conv_b, lru_wa=lru_wa, lru_ba=lru_ba, lru_wx=lru_wx,
             lru_bx=lru_bx, lru_lambda=lru_lambda, ffn_w1=ffn_w1, ffn_w2=ffn_w2)
    depth = w_in.shape[0]
    pb, seq, _ = x_prompt.shape
    sb, t_new, _ = x_sample.shape
    win = cache_attn_k.shape[2]
    keep = min(ATTN_WINDOW, seq)
    total = win + LANES
    assert t_new <= LANES and seq % (DIL_PAIRS[-1][1] * ATTN_SPAN) == 0

    bias_p = _bias_table(_prompt_bias_idx(), rel_bias)
    bias_s = _bias_table(_sample_bias_idx(t_new, SUBLANES, win, total), rel_bias)
    k_cache = cache_attn_k.reshape(depth, sb, win * ATTN_HEADS, ATTN_HEAD)
    v_cache = cache_attn_v.reshape(depth, sb, win * ATTN_HEADS, ATTN_HEAD)

    xp = x_prompt.reshape(pb * seq, D_MODEL)
    xs = x_sample.reshape(sb * t_new, D_MODEL)
    new_p, new_s = [], []
    sp = _shared_params(a)
    for l in range(depth):
        lp = _layer_params(l, a)
        attend_p = lambda pxq, k_il, v_il: _attn_prompt(pxq, k_il, v_il, bias_p, pb, seq)
        xp, st_p = _layer(xp, l, sp, lp, pb, seq,
                          jnp.zeros((pb, 1, RWKV_PROJ), F32), jnp.zeros((pb, RWKV_WIDTH, RWKV_WIDTH), F32),
                          jnp.zeros((pb, CONV_W - 1, LRU_WIDTH), F32), jnp.zeros((pb, 1, LRU_WIDTH), F32), attend_p)
        attend_s = lambda pxq, k_il, v_il, l=l: _attn_sample(
            pxq.reshape(sb, t_new, PXQ_WIDTH), k_il.reshape(sb, t_new * ATTN_HEADS, ATTN_HEAD),
            v_il.reshape(sb, t_new * ATTN_HEADS, ATTN_HEAD), k_cache, v_cache, bias_s, l, sb, t_new,
        ).reshape(sb * t_new, ATTN_WIDTH)
        xs, st_s = _layer(xs, l, sp, lp, sb, t_new,
                          state_rwkv_shift[l].reshape(sb, 1, RWKV_PROJ), _state_to_bd(state_rwkv_wkv[l]),
                          state_lru_conv[l], state_lru_h[l].reshape(sb, 1, LRU_WIDTH), attend_s)
        new_p.append(st_p)
        new_s.append(st_s)

    stack = lambda sts, i: jnp.stack([s[i] for s in sts])
    k_p = jnp.stack([s[4][:, -keep:] for s in new_p])
    v_p = jnp.stack([s[5][:, -keep:] for s in new_p])
    return (xp.reshape(pb, seq, D_MODEL), xs.reshape(sb, t_new, D_MODEL),
            stack(new_p, 1), stack(new_s, 1), stack(new_p, 0), stack(new_s, 0),
            stack(new_p, 3), stack(new_s, 3), stack(new_p, 2), stack(new_s, 2),
            k_p, stack(new_s, 4), v_p, stack(new_s, 5))
```

```python
import functools
import math

import numpy as np
import jax
import jax.numpy as jnp
from jax import lax
from jax.experimental import pallas as pl
from jax.experimental.pallas import tpu as pltpu

F32 = jnp.float32
BF16 = jnp.bfloat16

D_MODEL = 2048
RWKV_WIDTH = 512
RWKV_HEAD = 64
RWKV_HEADS = RWKV_WIDTH // RWKV_HEAD
DECAY_LORA = 64
ICLR_LORA = 64
GATE_LORA = 128
LORA_WIDTH = DECAY_LORA + ICLR_LORA + GATE_LORA
RWKV_PROJ = 3 * RWKV_WIDTH + LORA_WIDTH
GN_EPS = 64e-5
LRU_WIDTH = 512
LRU_BLOCKS = 8
LRU_BLOCK = LRU_WIDTH // LRU_BLOCKS
CONV_W = 4
RG_C = 8.0
ATTN_WIDTH = 1024
ATTN_HEAD = 128
ATTN_HEADS = ATTN_WIDTH // ATTN_HEAD
DIL_PAIRS = ((128, 1), (512, 4), (2048, 16))
ATTN_SPAN = 128
ATTN_WINDOW = 2048
N_BUCKETS = 32
REL_MAX_DIST = ATTN_WINDOW
D_FF = 4 * D_MODEL
RMS_EPS = 1e-6
NEG = -1e30

LANES = 128
SUBLANES = 8
MXU_TILE = 256
VMEM_LIMIT_BYTES = 56 * 1024 * 1024

ROW_TILE = 1024
IN_COL_TILE = 256
OUT_ROW_TILE = 256
FFN_ROW_TILE = 512
FFN_COL_TILE = 1024
RWKV_CHUNK = 64
RWKV_SEQS_PER_STEP = 4
LRU_CHUNK = 256
ATTN_MERGE_ROWS = 256
ATTN_UNROLL = 5


def _cparams(sem):
    return pltpu.CompilerParams(dimension_semantics=sem, vmem_limit_bytes=VMEM_LIMIT_BYTES)


def _dot(a, b):
    return jnp.dot(a.astype(BF16), b.astype(BF16), preferred_element_type=F32)


def _dot_nt(a, b):
    return lax.dot_general(a.astype(BF16), b.astype(BF16), (((1,), (1,)), ((), ())), preferred_element_type=F32)


def _dot_tn(a, b):
    return lax.dot_general(a.astype(BF16), b.astype(BF16), (((0,), (0,)), ((), ())), preferred_element_type=F32)


def _softplus(z):
    return jnp.maximum(z, 0.0) + jnp.log1p(jnp.exp(-jnp.abs(z)))


def _mix_dtype(rows):
    return BF16 if rows % (2 * SUBLANES) == 0 else F32


def _rms(x, g):
    ms = jnp.mean(x * x, axis=-1, keepdims=True)
    return x * lax.rsqrt(ms + RMS_EPS) * g


PXQ_WIDTH = RWKV_PROJ + 2 * LRU_WIDTH + ATTN_WIDTH
PXQ_TILES = PXQ_WIDTH // IN_COL_TILE
KV_TILES = ATTN_WIDTH // IN_COL_TILE
HEADS_PER_TILE = IN_COL_TILE // ATTN_HEAD


def _in_proj_kernel(x_ref, g_ref, w_ref, pxq_ref, k_ref, v_ref, h_scr):
    j = pl.program_id(1)
    tm = x_ref.shape[0]

    @pl.when(j == 0)
    def _():
        h_scr[...] = _rms(x_ref[...], g_ref[...]).astype(BF16)

    acc = jnp.dot(h_scr[...], w_ref[...], preferred_element_type=F32)

    @pl.when(j < PXQ_TILES)
    def _():
        pxq_ref[...] = acc

    def scatter_heads(o_ref, tile):
        for hh in range(HEADS_PER_TILE):
            head = tile * HEADS_PER_TILE + hh
            o_ref[pl.ds(head, tm, stride=ATTN_HEADS), :] = acc[:, hh * ATTN_HEAD:(hh + 1) * ATTN_HEAD]

    @pl.when((j >= PXQ_TILES) & (j < PXQ_TILES + KV_TILES))
    def _():
        scatter_heads(k_ref, j - PXQ_TILES)

    @pl.when(j >= PXQ_TILES + KV_TILES)
    def _():
        scatter_heads(v_ref, j - PXQ_TILES - KV_TILES)


def _in_proj(x, g, w, layer):
    m, d = x.shape
    n = w.shape[2]
    assert n == PXQ_WIDTH + 2 * ATTN_WIDTH
    tm = min(ROW_TILE, m)
    return pl.pallas_call(
        _in_proj_kernel,
        grid=(m // tm, n // IN_COL_TILE),
        in_specs=[
            pl.BlockSpec((tm, d), lambda i, j: (i, 0)),
            pl.BlockSpec((None, 1, d), lambda i, j: (layer, 0, 0)),
            pl.BlockSpec((None, d, IN_COL_TILE), lambda i, j: (layer, 0, j)),
        ],
        out_specs=[
            pl.BlockSpec((tm, IN_COL_TILE), lambda i, j: (i, jnp.minimum(j, PXQ_TILES - 1))),
            pl.BlockSpec((tm * ATTN_HEADS, ATTN_HEAD), lambda i, j: (i, 0)),
            pl.BlockSpec((tm * ATTN_HEADS, ATTN_HEAD), lambda i, j: (i, 0)),
        ],
        out_shape=[
            jax.ShapeDtypeStruct((m, PXQ_WIDTH), F32),
            jax.ShapeDtypeStruct((m * ATTN_HEADS, ATTN_HEAD), F32),
            jax.ShapeDtypeStruct((m * ATTN_HEADS, ATTN_HEAD), F32),
        ],
        scratch_shapes=[pltpu.VMEM((tm, d), BF16)],
        compiler_params=_cparams(("parallel", "arbitrary")),
        name="in_proj",
    )(x, g, w)


def _out_proj_kernel(ya_ref, yb_ref, yc_ref, w_ref, g_ref, x_ref, o_ref):
    acc = jnp.dot(ya_ref[...].astype(BF16), w_ref[0:RWKV_WIDTH, :], preferred_element_type=F32)
    acc = acc + jnp.dot(yb_ref[...].astype(BF16), w_ref[RWKV_WIDTH:RWKV_WIDTH + LRU_WIDTH, :], preferred_element_type=F32)
    acc = acc + jnp.dot(yc_ref[...].astype(BF16), w_ref[RWKV_WIDTH + LRU_WIDTH:, :], preferred_element_type=F32)
    o_ref[...] = x_ref[...] + _rms(acc, g_ref[...])


def _out_proj(ya, yb, yc, w, g, x, layer):
    m, d = x.shape
    tm = min(OUT_ROW_TILE, m)
    return pl.pallas_call(
        _out_proj_kernel,
        grid=(m // tm,),
        in_specs=[
            pl.BlockSpec((tm, RWKV_WIDTH), lambda i: (i, 0)),
            pl.BlockSpec((tm, LRU_WIDTH), lambda i: (i, 0)),
            pl.BlockSpec((tm, ATTN_WIDTH), lambda i: (i, 0)),
            pl.BlockSpec((None, d, d), lambda i: (layer, 0, 0)),
            pl.BlockSpec((None, 1, d), lambda i: (layer, 0, 0)),
            pl.BlockSpec((tm, d), lambda i: (i, 0)),
        ],
        out_specs=pl.BlockSpec((tm, d), lambda i: (i, 0)),
        out_shape=jax.ShapeDtypeStruct((m, d), F32),
        compiler_params=_cparams(("parallel",)),
        name="out_proj",
    )(ya, yb, yc, w, g, x)


def _ffn_kernel(x_ref, g1_ref, w1_ref, w2_ref, g2_ref, o_ref, h_scr, acc_scr):
    j = pl.program_id(1)

    @pl.when(j == 0)
    def _():
        h_scr[...] = _rms(x_ref[...], g1_ref[...]).astype(BF16)
        acc_scr[...] = jnp.zeros_like(acc_scr)

    u = jnp.dot(h_scr[...], w1_ref[...], preferred_element_type=F32)
    u = jnp.square(jnp.maximum(u, 0.0)).astype(BF16)
    acc_scr[...] += jnp.dot(u, w2_ref[...], preferred_element_type=F32)

    @pl.when(j == pl.num_programs(1) - 1)
    def _():
        o_ref[...] = x_ref[...] + _rms(acc_scr[...], g2_ref[...])


def _ffn(x, g1, w1, w2, g2, layer):
    m, d = x.shape
    f = w1.shape[2]
    tm = min(FFN_ROW_TILE, m)
    tf = FFN_COL_TILE
    return pl.pallas_call(
        _ffn_kernel,
        grid=(m // tm, f // tf),
        in_specs=[
            pl.BlockSpec((tm, d), lambda i, j: (i, 0)),
            pl.BlockSpec((None, 1, d), lambda i, j: (layer, 0, 0)),
            pl.BlockSpec((None, d, tf), lambda i, j: (layer, 0, j)),
            pl.BlockSpec((None, tf, d), lambda i, j: (layer, j, 0)),
            pl.BlockSpec((None, 1, d), lambda i, j: (layer, 0, 0)),
        ],
        out_specs=pl.BlockSpec((tm, d), lambda i, j: (i, 0)),
        out_shape=jax.ShapeDtypeStruct((m, d), F32),
        scratch_shapes=[pltpu.VMEM((tm, d), BF16), pltpu.VMEM((tm, d), F32)],
        compiler_params=_cparams(("parallel", "arbitrary")),
        name="ffn",
    )(x, g1, w1, w2, g2)


def _rwkv_groups(chunk):
    return 2 if (RWKV_HEADS * chunk) % (2 * MXU_TILE) == 0 else 1


def _rwkv_kernel(p_ref, shift_ref, s0_ref, mu_ref, w0_ref, wup_ref, a0_ref, aup_ref, gup_ref, kk_ref, ka_ref,
                 rk_ref, lg_ref, lb_ref, ones_ref,
                 y_ref, sout_ref, s_scr, prev_scr, *, chunk, t_valid):
    C = chunk
    W = RWKV_WIDTH
    H = RWKV_HEADS
    N = RWKV_HEAD
    nb = p_ref.shape[0]
    c = pl.program_id(1)

    G = _rwkv_groups(C)
    hpg = H // G
    wl = W // G
    gc = hpg * C

    @pl.when(c == 0)
    def _():
        s_scr[...] = jnp.zeros_like(s_scr)
        for h in range(H):
            o = (h % hpg) * N
            s_scr[:, h // hpg, o:o + N, o:o + N] = s0_ref[:, h]
        prev_scr[...] = shift_ref[...]

    inv_n = 1.0 / RWKV_HEAD
    shared = dict(wup=[wup_ref], aup=[aup_ref], gup=[gup_ref],
                  ones=[ones_ref.at[g * wl:(g + 1) * wl, g * wl:(g + 1) * wl] for g in range(G)])

    def hi_lo(x):
        hi = x.astype(BF16).astype(F32)
        return [hi, x - hi]

    row = lax.broadcasted_iota(jnp.int32, (C, 1), 0)

    def cumsum_rows(z):
        dlt = 1
        while dlt < C:
            z = z + jnp.where(row >= dlt, pltpu.roll(z, dlt, axis=0), 0.0)
            dlt *= 2
        return z

    ti = lax.broadcasted_iota(jnp.int32, (C, H * C), 0)
    si = lax.broadcasted_iota(jnp.int32, (C, H * C), 1) % C
    strict = ti > si
    incl = ti >= si
    eye = (ti == si).astype(F32)
    blk_r = lax.broadcasted_iota(jnp.int32, (gc, 1), 0) // C
    mask_ch = blk_r == lax.broadcasted_iota(jnp.int32, (1, wl), 1) // N
    mask_cc = blk_r == lax.broadcasted_iota(jnp.int32, (1, gc), 1) // C
    mask_ss = (lax.broadcasted_iota(jnp.int32, (wl, 1), 0) // N
               == lax.broadcasted_iota(jnp.int32, (1, wl), 1) // N)

    def block_diag(x, mask, width):
        out = []
        for g in range(G):
            tiled = jnp.concatenate([x[:, g * width:(g + 1) * width]] * hpg, axis=0)
            out.append(jnp.where(mask, tiled, 0.0).astype(BF16))
        return out

    def per_head(a_cat, bds):
        a_b = a_cat.astype(BF16)
        return jnp.concatenate([jnp.dot(a_b[:, g * gc:(g + 1) * gc], bds[g], preferred_element_type=F32)
                                for g in range(G)], axis=1)

    def one_sequence(bi):
        p = p_ref[bi]
        shifted = jnp.where(row == 0, prev_scr[bi], pltpu.roll(p, 1, axis=0))
        prev_scr[bi] = p_ref[bi, C - 1:C, :]
        m = p + (shifted - p) * mu_ref[...]
        r = m[:, 0:W]
        k = m[:, W:2 * W]
        v = m[:, 2 * W:3 * W]
        x = m[:, 3 * W:]
        lw, la, gate = yield [("wup", jnp.tanh(x)), ("aup", x), ("gup", jax.nn.sigmoid(x))]
        w = w0_ref[...] + lw
        a = jax.nn.sigmoid(a0_ref[...] + la)
        loga = -jnp.exp(-_softplus(-w) - 0.5)
        kk = k * kk_ref[...]
        k2 = k * (1.0 + (a - 1.0) * ka_ref[...])
        (ss,) = yield [("ones", jnp.concatenate(hi_lo(kk * kk) + hi_lo(r * k2 * rk_ref[...]), axis=0))]
        kk = kk / jnp.maximum(jnp.sqrt(ss[0:C] + ss[C:2 * C]), 1e-12)
        bonus = (ss[2 * C:3 * C] + ss[3 * C:]) * v
        if t_valid < C:
            live = row < t_valid
            loga = jnp.where(live, loga, 0.0)
            kk = jnp.where(live, kk, 0.0)
            k2 = jnp.where(live, k2, 0.0)
        cl = cumsum_rows(loga)
        cl_last = cl[C - 1:C, :]
        e_neg = jnp.exp(-cl)
        e_rem = jnp.exp(cl_last - cl)
        kka = kk * a
        al = -kk * jnp.exp(cl - loga)
        rt = r * jnp.exp(cl)
        be_bd = block_diag(kka * e_neg, mask_ch, wl)
        kt_bd = block_diag(k2 * e_neg, mask_ch, wl)
        bh = kka * e_rem
        kh = k2 * e_rem
        lhs = jnp.concatenate([al, rt], axis=0).astype(BF16)
        lhs_g = [lhs[:, g * wl:(g + 1) * wl] for g in range(G)]
        g_b = jnp.concatenate([_dot_nt(lhs_g[g], be_bd[g]) for g in range(G)], axis=1)
        g_k = jnp.concatenate([_dot_nt(lhs_g[g], kt_bd[g]) for g in range(G)], axis=1)
        n_cat = jnp.where(strict, g_b[0:C], 0.0)
        a_ak = jnp.where(strict, g_k[0:C], 0.0)
        a_rb = jnp.where(incl, g_b[C:], 0.0)
        a_rk = jnp.where(incl, g_k[C:], 0.0)
        s_prev = [s_scr[bi, g] for g in range(G)]
        proj = jnp.concatenate([_dot_nt(lhs_g[g], s_prev[g]) for g in range(G)], axis=1)
        doublings = max(int(math.log2(C)) - 1, 0)
        t_cat = eye + n_cat
        pw = n_cat
        if doublings:
            pw = per_head(n_cat, block_diag(n_cat, mask_cc, gc))
        yield None
        for it in range(doublings):
            pw_bd = block_diag(pw, mask_cc, gc)
            if it < doublings - 1:
                both = per_head(jnp.concatenate([t_cat, pw], axis=0), pw_bd)
                t_cat = t_cat + both[0:C]
                pw = both[C:]
            else:
                t_cat = t_cat + per_head(t_cat, pw_bd)
            yield None
        v_bd = block_diag(v, mask_ch, wl)
        rhs = proj[0:C] + per_head(a_ak, v_bd)
        yield None
        u = per_head(t_cat, block_diag(rhs, mask_ch, wl))
        yield None
        y = proj[C:] + per_head(a_rb, block_diag(u, mask_ch, wl)) + per_head(a_rk, v_bd)
        pad = LANES - 2 * C
        uv = jnp.concatenate([u, v] + ([jnp.zeros((pad, W), F32)] if pad > 0 else []), axis=0).astype(BF16)
        bk = jnp.concatenate([bh, kh] + ([jnp.zeros((pad, W), F32)] if pad > 0 else []), axis=0).astype(BF16)
        decay = jnp.exp(cl_last)
        for g in range(G):
            lanes = slice(g * wl, (g + 1) * wl)
            upd = _dot_tn(uv[:, lanes], bk[:, lanes])
            s_scr[bi, g] = s_prev[g] * decay[:, lanes] + jnp.where(mask_ss, upd, 0.0)

        @pl.when(c == pl.num_programs(1) - 1)
        def _():
            for h in range(H):
                o = (h % hpg) * N
                sout_ref[bi, h] = s_scr[bi, h // hpg, o:o + N, o:o + N]

        (sy,) = yield [("ones", jnp.concatenate(hi_lo(y), axis=0))]
        d = y - (sy[0:C] + sy[C:]) * inv_n
        (sd,) = yield [("ones", jnp.concatenate(hi_lo(d * d), axis=0))]
        var = (sd[0:C] + sd[C:]) * inv_n
        yn = d * lax.rsqrt(var + GN_EPS) * lg_ref[...] + lb_ref[...]
        y_ref[bi] = ((yn + bonus) * gate).astype(y_ref.dtype)

    seqs = [one_sequence(bi) for bi in range(nb)]
    replies = [None] * nb
    while True:
        asks, finished = [], 0
        for seq, reply in zip(seqs, replies):
            try:
                asks.append(seq.send(reply))
            except StopIteration:
                finished += 1
        if finished:
            assert finished == nb
            break
        if asks[0] is None:
            replies = [None] * nb
            continue
        replies = [[] for _ in range(nb)]
        for qi, (name, _) in enumerate(asks[0]):
            lhs = jnp.concatenate([ask[qi][1] for ask in asks], axis=0).astype(BF16)
            parts = shared[name]
            kw = lhs.shape[1] // len(parts)
            z = jnp.concatenate([jnp.dot(lhs[:, g * kw:(g + 1) * kw], part[...], preferred_element_type=F32)
                                 for g, part in enumerate(parts)], axis=1)
            rows = z.shape[0] // nb
            for si in range(nb):
                replies[si].append(z[si * rows:(si + 1) * rows])


def _rwkv(p, shift_prev, s0, prm, batch, t_len, t_valid, chunk):
    W = RWKV_WIDTH
    nc = t_len // chunk
    nb = _largest_divisor(batch, RWKV_SEQS_PER_STEP)
    vec = lambda n: pl.BlockSpec((1, n), lambda b, c: (0, 0))
    mat = lambda r, n: pl.BlockSpec((r, n), lambda b, c: (0, 0))
    state = pl.BlockSpec((nb, RWKV_HEADS, RWKV_HEAD, RWKV_HEAD), lambda b, c: (b, 0, 0, 0))
    groups = _rwkv_groups(chunk)
    kern = functools.partial(_rwkv_kernel, chunk=chunk, t_valid=t_valid)
    return pl.pallas_call(
        kern,
        grid=(batch // nb, nc),
        in_specs=[
            pl.BlockSpec((nb, chunk, RWKV_PROJ), lambda b, c: (b, c, 0)),
            pl.BlockSpec((nb, 1, RWKV_PROJ), lambda b, c: (b, 0, 0)),
            state,
            vec(RWKV_PROJ), vec(W), mat(LORA_WIDTH, W), vec(W), mat(LORA_WIDTH, W), mat(LORA_WIDTH, W),
            vec(W), vec(W), vec(W), vec(W), vec(W), mat(W, W),
        ],
        out_specs=[pl.BlockSpec((nb, chunk, W), lambda b, c: (b, c, 0)), state],
        out_shape=[
            jax.ShapeDtypeStruct((batch, t_len, W), _mix_dtype(chunk)),
            jax.ShapeDtypeStruct((batch, RWKV_HEADS, RWKV_HEAD, RWKV_HEAD), F32),
        ],
        scratch_shapes=[pltpu.VMEM((nb, groups, W // groups, W // groups), F32),
                        pltpu.VMEM((nb, 1, RWKV_PROJ), F32)],
        compiler_params=_cparams(("parallel", "arbitrary")),
        name="rwkv7",
    )(p, shift_prev, s0, prm["mu"], prm["w0"], prm["w_up"], prm["a0"], prm["a_up"], prm["g_up"],
      prm["k_k"], prm["k_a"], prm["r_k"], prm["lnx_g"], prm["lnx_b"], prm["ones_bd"])


def _lru_kernel(x0_ref, x1_ref, g0_ref, g1_ref, conv_ref, h0_ref, cw_ref, cb_ref, wa_ref, ba_ref, wx_ref, bx_ref,
                lam_ref, y_ref, hout_ref, tail_scr, h_scr, *, chunk, t_valid):
    C = chunk
    c = pl.program_id(1)

    @pl.when(c == 0)
    def _():
        tail_scr[...] = jnp.zeros_like(tail_scr)
        tail_scr[SUBLANES - (CONV_W - 1):, :] = conv_ref[...]
        h_scr[...] = h0_ref[...]

    x = jnp.concatenate([x0_ref[...], x1_ref[...]], axis=1)
    gt = jnp.concatenate([g0_ref[...], g1_ref[...]], axis=1)
    ext = jnp.concatenate([tail_scr[...], x], axis=0)
    tail_scr[...] = x[C - SUBLANES:, :]
    xc = cb_ref[...] + x * cw_ref[CONV_W - 1:CONV_W, :]
    for dlt in range(1, CONV_W):
        sh = pltpu.roll(ext, dlt, axis=0)[SUBLANES:, :]
        xc = xc + sh * cw_ref[CONV_W - 1 - dlt:CONV_W - dlt, :]

    gate_r = jax.nn.sigmoid(_dot(xc, wa_ref[...]) + ba_ref[...])
    gate_i = jax.nn.sigmoid(_dot(xc, wx_ref[...]) + bx_ref[...])
    log_a = -RG_C * gate_r * _softplus(-lam_ref[...])
    a = jnp.exp(log_a)
    th = jnp.tanh(log_a)
    u = jnp.sqrt(-2.0 * th / (1.0 - th)) * (gate_i * xc)

    row = lax.broadcasted_iota(jnp.int32, (C, 1), 0)
    dlt = 1
    while dlt < C:
        keep = row >= dlt
        a_sh = jnp.where(keep, pltpu.roll(a, dlt, axis=0), 1.0)
        u_sh = jnp.where(keep, pltpu.roll(u, dlt, axis=0), 0.0)
        u = a * u_sh + u
        a = a * a_sh
        dlt *= 2
    h = a * h_scr[...] + u
    last = min(t_valid, C) - 1
    h_scr[...] = h[last:last + 1, :]
    hout_ref[...] = h[last:last + 1, :]
    y_ref[...] = (h * jax.nn.gelu(gt)).astype(y_ref.dtype)


def _lru(pxq, conv_prev, h0, prm, batch, t_len, t_valid, chunk):
    W = LRU_WIDTH
    nc = t_len // chunk
    half = W // 2
    col0 = RWKV_PROJ // half
    assert RWKV_PROJ % half == 0
    part = lambda k: pl.BlockSpec((chunk, half), lambda b, c: (b * nc + c, col0 + k))
    vec = lambda n: pl.BlockSpec((1, n), lambda b, c: (0, 0))
    kern = functools.partial(_lru_kernel, chunk=chunk, t_valid=t_valid)
    return pl.pallas_call(
        kern,
        grid=(batch, nc),
        in_specs=[
            part(0), part(1), part(2), part(3),
            pl.BlockSpec((None, CONV_W - 1, W), lambda b, c: (b, 0, 0)),
            pl.BlockSpec((None, 1, W), lambda b, c: (b, 0, 0)),
            pl.BlockSpec((CONV_W, W), lambda b, c: (0, 0)),
            vec(W),
            pl.BlockSpec((W, W), lambda b, c: (0, 0)), vec(W),
            pl.BlockSpec((W, W), lambda b, c: (0, 0)), vec(W),
            vec(W),
        ],
        out_specs=[
            pl.BlockSpec((chunk, W), lambda b, c: (b * nc + c, 0)),
            pl.BlockSpec((None, 1, W), lambda b, c: (b, 0, 0)),
        ],
        out_shape=[
            jax.ShapeDtypeStruct((batch * t_len, W), _mix_dtype(chunk)),
            jax.ShapeDtypeStruct((batch, 1, W), F32),
        ],
        scratch_shapes=[pltpu.VMEM((SUBLANES, W), F32), pltpu.VMEM((1, W), F32)],
        compiler_params=_cparams(("parallel", "arbitrary")),
        name="rglru",
    )(pxq, pxq, pxq, pxq, conv_prev, h0, prm["conv_w"], prm["conv_b"], prm["wa_bd"], prm["ba"], prm["wx_bd"],
      prm["bx"], prm["lam"])


def _attn_blocks(q_ref, k_ref, v_ref, bias_ref, og_scr, lse_scr, gi, dil, blocks):
    scale = ATTN_HEAD ** -0.5
    B = ATTN_SPAN

    def rows(start, first):
        if first is True:
            start, n = start, B
        elif first is False:
            start, n = start - B * dil, 2 * B
        else:
            start, n = jnp.maximum(start - B * dil, 0), 2 * B
        return pl.ds(start, n) if dil == 1 else pl.ds(start, n, stride=dil)

    scores = []
    for qs, first in blocks:
        qb = q_ref[rows(qs, True), :]
        if first is True:
            bias = bias_ref[gi, :, B:]
        elif first is False:
            bias = bias_ref[gi]
        else:
            own_then_masked = jnp.concatenate([bias_ref[gi, :, B:], jnp.full((B, B), NEG, F32)], axis=1)
            bias = jnp.where(first, own_then_masked, bias_ref[gi])
        scores.append(_dot_nt(qb, k_ref[rows(qs, first), :]) * scale + bias)
    probs = []
    for s in scores:
        mx = jnp.max(s, axis=-1, keepdims=True)
        pr = jnp.exp(s - mx)
        den = jnp.sum(pr, axis=-1, keepdims=True)
        probs.append(((pr / den).astype(BF16), mx + jnp.log(den)))
    outs = [_dot(pn, v_ref[rows(qs, first), :]) for (qs, first), (pn, _) in zip(blocks, probs)]
    for (qs, _), o, (_, lse) in zip(blocks, outs, probs):
        og_scr[gi, rows(qs, True), :] = o
        lse_scr[gi, rows(qs, True), :] = jnp.broadcast_to(lse, (B, ATTN_HEAD))


def _largest_divisor(n, cap):
    return max(u for u in range(1, cap + 1) if n % u == 0)


def _attn_prompt_kernel(q_ref, kil_ref, vil_ref, bias_ref, o_ref, k_ref, v_ref, og_scr, lse_scr, *, seq):
    B = ATTN_SPAN
    h = pl.program_id(1)
    R = ATTN_MERGE_ROWS

    def gather_head(i, carry):
        dst = pl.ds(pl.multiple_of(i * R, R), R)
        src = pl.ds(h + i * (R * ATTN_HEADS), R, stride=ATTN_HEADS)
        k_ref[dst, :] = kil_ref[src, :]
        v_ref[dst, :] = vil_ref[src, :]
        return carry

    lax.fori_loop(0, seq // R, gather_head, 0)

    for gi, (win, dil) in enumerate(DIL_PAIRS):
        nblk = seq // (dil * B)
        run = functools.partial(_attn_blocks, q_ref, k_ref, v_ref, bias_ref, og_scr, lse_scr, gi, dil)
        if nblk == 1:
            per = _largest_divisor(dil, ATTN_UNROLL)

            def body(i, carry, run=run, per=per):
                run([(i * per + u, True) for u in range(per)])
                return carry

            lax.fori_loop(0, dil // per, body, 0)
        elif dil == 1:
            per = _largest_divisor(nblk, ATTN_UNROLL)

            def body(i, carry, run=run, per=per):
                run([(pl.multiple_of((i * per + u) * B, B), (i == 0) if u == 0 else False) for u in range(per)])
                return carry

            lax.fori_loop(0, nblk // per, body, 0)
        else:
            def body(cls, carry, run=run, dil=dil, nblk=nblk):
                run([(cls + blk * (B * dil), blk == 0) for blk in range(nblk)])
                return carry

            lax.fori_loop(0, dil, body, 0)

    def merge(i, carry):
        sl = pl.ds(pl.multiple_of(i * R, R), R)
        ls = [lse_scr[gi, sl, :] for gi in range(len(DIL_PAIRS))]
        mx = functools.reduce(jnp.maximum, ls)
        ws = [jnp.exp(l - mx) for l in ls]
        tot = functools.reduce(lambda a, b: a + b, ws)
        acc = (ws[0] / tot) * og_scr[0, sl, :]
        for gi in range(1, len(DIL_PAIRS)):
            acc = acc + (ws[gi] / tot) * og_scr[gi, sl, :]
        o_ref[sl, :] = acc.astype(o_ref.dtype)
        return carry

    lax.fori_loop(0, seq // R, merge, 0)


def _attn_prompt(pxq, k_il, v_il, bias, batch, seq):
    H = ATTN_HEADS
    G = len(DIL_PAIRS)
    q_col0 = (PXQ_WIDTH - ATTN_WIDTH) // ATTN_HEAD
    kern = functools.partial(_attn_prompt_kernel, seq=seq)
    kv = pl.BlockSpec((seq * H, ATTN_HEAD), lambda b, h: (b, 0))
    return pl.pallas_call(
        kern,
        grid=(batch, H),
        in_specs=[
            pl.BlockSpec((seq, ATTN_HEAD), lambda b, h: (b, q_col0 + h)),
            kv, kv,
            pl.BlockSpec((G, None, ATTN_SPAN, 2 * ATTN_SPAN), lambda b, h: (0, h, 0, 0)),
        ],
        out_specs=pl.BlockSpec((seq, ATTN_HEAD), lambda b, h: (b, h)),
        out_shape=jax.ShapeDtypeStruct((batch * seq, ATTN_WIDTH), BF16),
        scratch_shapes=[pltpu.VMEM((seq, ATTN_HEAD), F32), pltpu.VMEM((seq, ATTN_HEAD), F32),
                        pltpu.VMEM((G, seq, ATTN_HEAD), F32), pltpu.VMEM((G, seq, ATTN_HEAD), F32)],
        compiler_params=_cparams(("parallel", "arbitrary")),
        name="attn_prompt",
    )(pxq, k_il, v_il, bias)


def _attn_sample_kernel(pxq_ref, kn_ref, vn_ref, kc_ref, vc_ref, bias_ref, o_ref, q_scr, kn_scr, vn_scr, *, t_new, win):
    scale = ATTN_HEAD ** -0.5
    H = ATTN_HEADS
    G = len(DIL_PAIRS)
    TP = q_scr.shape[0]
    q_col0 = PXQ_WIDTH - ATTN_WIDTH
    q_scr[...] = jnp.zeros_like(q_scr)
    kn_scr[...] = jnp.zeros_like(kn_scr)
    vn_scr[...] = jnp.zeros_like(vn_scr)
    for h in range(H):
        lanes = slice(h * ATTN_HEAD, (h + 1) * ATTN_HEAD)
        q_scr[0:t_new, :] = pxq_ref[:, q_col0 + h * ATTN_HEAD:q_col0 + (h + 1) * ATTN_HEAD]
        kn_scr[0:t_new, :] = kn_ref[pl.ds(h, t_new, stride=H), :]
        vn_scr[0:t_new, :] = vn_ref[pl.ds(h, t_new, stride=H), :]
        qb = q_scr[...].astype(BF16)
        kc = kc_ref[pl.ds(h, win, stride=H), :]
        vc = vc_ref[pl.ds(h, win, stride=H), :]
        s = jnp.concatenate([_dot_nt(qb, kc), _dot_nt(qb, kn_scr[...])], axis=1) * scale
        prs, lses = [], []
        for gi in range(G):
            sg = s + bias_ref[gi, h]
            mx = jnp.max(sg, axis=-1, keepdims=True)
            pr = jnp.exp(sg - mx)
            den = jnp.sum(pr, axis=-1, keepdims=True)
            prs.append(pr / den)
            lses.append(mx + jnp.log(den))
        pall = jnp.concatenate(prs, axis=0)
        oall = _dot(pall[:, :win], vc) + _dot(pall[:, win:], vn_scr[...])
        mx = functools.reduce(jnp.maximum, lses)
        ws = [jnp.exp(l - mx) for l in lses]
        tot = functools.reduce(lambda a, b: a + b, ws)
        acc = (ws[0] / tot) * oall[0:TP]
        for gi in range(1, G):
            acc = acc + (ws[gi] / tot) * oall[gi * TP:(gi + 1) * TP]
        o_ref[:, lanes] = acc[0:t_new].astype(o_ref.dtype)


def _attn_sample(pxq, k_new, v_new, k_cache, v_cache, bias, layer, batch, t_new):
    H = ATTN_HEADS
    win = k_cache.shape[2] // H
    tp = bias.shape[2]
    kern = functools.partial(_attn_sample_kernel, t_new=t_new, win=win)
    cache = pl.BlockSpec((None, None, win * H, ATTN_HEAD), lambda b: (layer, b, 0, 0))
    new = pl.BlockSpec((None, t_new * H, ATTN_HEAD), lambda b: (b, 0, 0))
    return pl.pallas_call(
        kern,
        grid=(batch,),
        in_specs=[pl.BlockSpec((None, t_new, PXQ_WIDTH), lambda b: (b, 0, 0)), new, new, cache, cache,
                  pl.BlockSpec(bias.shape, lambda b: (0, 0, 0, 0))],
        out_specs=pl.BlockSpec((None, t_new, ATTN_WIDTH), lambda b: (b, 0, 0)),
        out_shape=jax.ShapeDtypeStruct((batch, t_new, ATTN_WIDTH), F32),
        scratch_shapes=[pltpu.VMEM((tp, ATTN_HEAD), F32), pltpu.VMEM((LANES, ATTN_HEAD), F32),
                        pltpu.VMEM((LANES, ATTN_HEAD), F32)],
        compiler_params=_cparams(("parallel",)),
        name="attn_sample",
    )(pxq, k_new, v_new, k_cache, v_cache, bias)


def _bias_kernel(idx_ref, rb_ref, o_ref):
    h = pl.program_id(0)
    idx = idx_ref[...]
    acc = jnp.full(idx.shape, NEG, F32)
    for b in range(N_BUCKETS):
        acc = jnp.where(idx == b, rb_ref[b, h], acc)
    o_ref[...] = acc


def _bias_table(idx, rel_bias):
    g, r, c = idx.shape
    heads = rel_bias.shape[1]
    return pl.pallas_call(
        _bias_kernel,
        grid=(heads,),
        in_specs=[pl.BlockSpec((g, r, c), lambda h: (0, 0, 0)),
                  pl.BlockSpec(memory_space=pltpu.SMEM)],
        out_specs=pl.BlockSpec((g, None, r, c), lambda h: (0, h, 0, 0)),
        out_shape=jax.ShapeDtypeStruct((g, heads, r, c), F32),
        compiler_params=_cparams(("parallel",)),
        name="bias_table",
    )(jnp.asarray(idx, jnp.int32), rel_bias.astype(F32))


def _t5_bucket_static(dist):
    dist = np.asarray(dist, np.int64)
    exact = N_BUCKETS // 2
    d = np.maximum(dist, 1).astype(np.float64)
    large = exact + (np.log(d / exact) / math.log(REL_MAX_DIST / exact) * (N_BUCKETS - exact)).astype(np.int64)
    return np.where(dist < exact, dist, np.minimum(large, N_BUCKETS - 1))


def _prompt_bias_idx():
    B = ATTN_SPAN
    qi = np.arange(B)[:, None]
    kj = np.arange(2 * B)[None, :]
    delta = qi + B - kj
    valid = (delta >= 0) & (delta <= B)
    return np.stack([np.where(valid, _t5_bucket_static(np.clip(delta, 0, B) * dil), -1) for _, dil in DIL_PAIRS])


def _sample_bias_idx(t_new, t_pad, win, total):
    r = np.arange(total)[None, :]
    t = np.arange(t_pad)[:, None]
    dist = win + t - r
    tabs = []
    for _, dil in DIL_PAIRS:
        assert win >= dil * ATTN_SPAN
        valid = (dist >= 0) & (dist % dil == 0) & (dist <= dil * ATTN_SPAN) & (t < t_new)
        tabs.append(np.where(valid, _t5_bucket_static(np.clip(dist, 0, dil * ATTN_SPAN)), -1))
    return np.stack(tabs)


def _block_diag(blocks):
    n, c, d = blocks.shape
    eye = jnp.eye(n, dtype=blocks.dtype)
    return jnp.einsum("ncd,nm->ncmd", blocks, eye).reshape(n * c, n * d)


def _layer_params(l, a):
    W = RWKV_WIDTH
    row = lambda t: t[l].reshape(1, -1).astype(F32)

    def lora_pad(w, off):
        z = jnp.zeros((LORA_WIDTH, W), F32)
        return z.at[off:off + w.shape[0]].set(w).astype(BF16)

    head_of = np.arange(W) // RWKV_HEAD
    ones_bd = jnp.asarray((head_of[:, None] == head_of[None, :]).astype(np.float32), BF16)
    rwkv = dict(
        mu=row(a["rwkv_mu"]), w0=row(a["rwkv_w0"]), a0=row(a["rwkv_a0"]),
        w_up=lora_pad(a["rwkv_w_up"][l], 0),
        a_up=lora_pad(a["rwkv_a_up"][l], DECAY_LORA),
        g_up=lora_pad(a["rwkv_g_up"][l], DECAY_LORA + ICLR_LORA),
        k_k=row(a["rwkv_k_k"]), k_a=row(a["rwkv_k_a"]), r_k=row(a["rwkv_r_k"]),
        lnx_g=row(a["rwkv_lnx_g"]), lnx_b=row(a["rwkv_lnx_b"]),
        ones_bd=ones_bd,
    )
    lru = dict(
        conv_w=a["lru_conv_w"][l].astype(F32), conv_b=row(a["lru_conv_b"]),
        wa_bd=_block_diag(a["lru_wa"][l]).astype(BF16), ba=row(a["lru_ba"]),
        wx_bd=_block_diag(a["lru_wx"][l]).astype(BF16), bx=row(a["lru_bx"]),
        lam=row(a["lru_lambda"]),
    )
    return dict(rwkv=rwkv, lru=lru)


def _shared_params(a):
    gain = lambda t: t.reshape(t.shape[0], 1, -1).astype(F32)
    return dict(
        w_in=a["w_in"].astype(BF16), w_out=a["w_out"].astype(BF16),
        w1=a["ffn_w1"].astype(BF16), w2=a["ffn_w2"].astype(BF16),
        g_mix_pre=gain(a["norm_mix_pre"]), g_mix_post=gain(a["norm_mix_post"]),
        g_ffn_pre=gain(a["norm_ffn_pre"]), g_ffn_post=gain(a["norm_ffn_post"]),
    )


def _layer(x, layer, sp, lp, batch, t_len, shift_prev, s0, conv_prev, h0, attend):
    pxq, k_il, v_il = _in_proj(x, sp["g_mix_pre"], sp["w_in"], layer)
    pxq3 = pxq.reshape(batch, t_len, PXQ_WIDTH)
    if t_len % RWKV_CHUNK == 0:
        y_a, s_fin = _rwkv(pxq3, shift_prev, s0, lp["rwkv"], batch, t_len, t_len, RWKV_CHUNK)
        y_a = y_a.reshape(batch * t_len, -1)
        y_b, h_fin = _lru(pxq, conv_prev, h0, lp["lru"], batch, t_len, t_len, LRU_CHUNK)
    else:
        assert t_len <= SUBLANES
        padded = jnp.pad(pxq3, ((0, 0), (0, SUBLANES - t_len), (0, 0)))
        y_a, s_fin = _rwkv(padded, shift_prev, s0, lp["rwkv"], batch, SUBLANES, t_len, SUBLANES)
        y_a = y_a[:, :t_len].reshape(batch * t_len, -1)
        y_b, h_fin = _lru(padded.reshape(batch * SUBLANES, PXQ_WIDTH), conv_prev, h0, lp["lru"], batch, SUBLANES,
                          t_len, SUBLANES)
        y_b = y_b.reshape(batch, SUBLANES, -1)[:, :t_len].reshape(batch * t_len, -1)
    y_c = attend(pxq, k_il, v_il)

    x = _out_proj(y_a, y_b, y_c, sp["w_out"], sp["g_mix_post"], x, layer)
    x = _ffn(x, sp["g_ffn_pre"], sp["w1"], sp["w2"], sp["g_ffn_post"], layer)

    lru_x = pxq3[:, :, RWKV_PROJ:RWKV_PROJ + LRU_WIDTH]
    if t_len >= CONV_W - 1:
        conv_new = lru_x[:, t_len - (CONV_W - 1):]
    else:
        conv_new = jnp.concatenate([conv_prev, lru_x], axis=1)[:, -(CONV_W - 1):]
    k_new = k_il.reshape(batch, t_len, ATTN_HEADS, ATTN_HEAD)
    v_new = v_il.reshape(batch, t_len, ATTN_HEADS, ATTN_HEAD)
    state = (pxq3[:, -1, :RWKV_PROJ], s_fin, conv_new, h_fin.reshape(batch, LRU_WIDTH), k_new, v_new)
    return x, state


def kernel(x_prompt, x_sample, state_rwkv_wkv, state_rwkv_shift, state_lru_h, state_lru_conv, cache_attn_k, cache_attn_v, rel_bias, norm_mix_pre, norm_mix_post, norm_ffn_pre, norm_ffn_post, w_in, w_out, rwkv_mu, rwkv_w0, rwkv_w_up, rwkv_a0, rwkv_a_up, rwkv_g_up, rwkv_k_k, rwkv_k_a, rwkv_r_k, rwkv_lnx_g, rwkv_lnx_b, lru_conv_w, lru_conv_b, lru_wa, lru_ba, lru_wx, lru_bx, lru_lambda, ffn_w1, ffn_w2):
    a = dict(norm_mix_pre=norm_mix_pre, norm_mix_post=norm_mix_post, norm_ffn_pre=norm_ffn_pre,
             norm_ffn_post=norm_ffn_post, w_in=w_in, w_out=w_out, rwkv_mu=rwkv_mu, rwkv_w0=rwkv_w0,
             rwkv_w_up=rwkv_w_up, rwkv_a0=rwkv_a0, rwkv_a_up=rwkv_a_up, rwkv_g_up=rwkv_g_up, rwkv_k_k=rwkv_k_k,
             rwkv_k_a=rwkv_k_a, rwkv_r_k=rwkv_r_k, rwkv_lnx_g=rwkv_lnx_g, rwkv_lnx_b=rwkv_lnx_b,
             lru_conv_w=lru_conv_w, lru_conv_b=lru_conv_b, lru_wa=lru_wa, lru_ba=lru_ba, lru_wx=lru_wx,
             lru_bx=lru_bx, lru_lambda=lru_lambda, ffn_w1=ffn_w1, ffn_w2=ffn_w2)
    depth = w_in.shape[0]
    pb, seq, _ = x_prompt.shape
    sb, t_new, _ = x_sample.shape
    win = cache_attn_k.shape[2]
    keep = min(ATTN_WINDOW, seq)
    total = win + LANES
    assert t_new <= LANES and seq % (DIL_PAIRS[-1][1] * ATTN_SPAN) == 0

    bias_p = _bias_table(_prompt_bias_idx(), rel_bias)
    bias_s = _bias_table(_sample_bias_idx(t_new, SUBLANES, win, total), rel_bias)
    k_cache = cache_attn_k.reshape(depth, sb, win * ATTN_HEADS, ATTN_HEAD)
    v_cache = cache_attn_v.reshape(depth, sb, win * ATTN_HEADS, ATTN_HEAD)

    xp = x_prompt.reshape(pb * seq, D_MODEL)
    xs = x_sample.reshape(sb * t_new, D_MODEL)
    new_p, new_s = [], []
    sp = _shared_params(a)
    for l in range(depth):
        lp = _layer_params(l, a)
        attend_p = lambda pxq, k_il, v_il: _attn_prompt(pxq, k_il, v_il, bias_p, pb, seq)
        xp, st_p = _layer(xp, l, sp, lp, pb, seq,
                          jnp.zeros((pb, 1, RWKV_PROJ), F32), jnp.zeros((pb, RWKV_HEADS, RWKV_HEAD, RWKV_HEAD), F32),
                          jnp.zeros((pb, CONV_W - 1, LRU_WIDTH), F32), jnp.zeros((pb, 1, LRU_WIDTH), F32), attend_p)
        attend_s = lambda pxq, k_il, v_il, l=l: _attn_sample(
            pxq.reshape(sb, t_new, PXQ_WIDTH), k_il.reshape(sb, t_new * ATTN_HEADS, ATTN_HEAD),
            v_il.reshape(sb, t_new * ATTN_HEADS, ATTN_HEAD), k_cache, v_cache, bias_s, l, sb, t_new,
        ).reshape(sb * t_new, ATTN_WIDTH)
        xs, st_s = _layer(xs, l, sp, lp, sb, t_new,
                          state_rwkv_shift[l].reshape(sb, 1, RWKV_PROJ), state_rwkv_wkv[l].astype(F32),
                          state_lru_conv[l], state_lru_h[l].reshape(sb, 1, LRU_WIDTH), attend_s)
        new_p.append(st_p)
        new_s.append(st_s)

    stack = lambda sts, i: jnp.stack([s[i] for s in sts])
    k_p = jnp.stack([s[4][:, -keep:] for s in new_p])
    v_p = jnp.stack([s[5][:, -keep:] for s in new_p])
    return (xp.reshape(pb, seq, D_MODEL), xs.reshape(sb, t_new, D_MODEL),
            stack(new_p, 1), stack(new_s, 1), stack(new_p, 0), stack(new_s, 0),
            stack(new_p, 3), stack(new_s, 3), stack(new_p, 2), stack(new_s, 2),
            k_p, stack(new_s, 4), v_p, stack(new_s, 5))
```

```python
import functools
import math

import numpy as np
import jax
import jax.numpy as jnp
from jax import lax
from jax.experimental import pallas as pl
from jax.experimental.pallas import tpu as pltpu

F32 = jnp.float32
BF16 = jnp.bfloat16

D_MODEL = 2048
RWKV_WIDTH = 512
RWKV_HEAD = 64
RWKV_HEADS = RWKV_WIDTH // RWKV_HEAD
DECAY_LORA = 64
ICLR_LORA = 64
GATE_LORA = 128
LORA_WIDTH = DECAY_LORA + ICLR_LORA + GATE_LORA
RWKV_PROJ = 3 * RWKV_WIDTH + LORA_WIDTH
GN_EPS = 64e-5
LRU_WIDTH = 512
LRU_BLOCKS = 8
LRU_BLOCK = LRU_WIDTH // LRU_BLOCKS
CONV_W = 4
RG_C = 8.0
ATTN_WIDTH = 1024
ATTN_HEAD = 128
ATTN_HEADS = ATTN_WIDTH // ATTN_HEAD
DIL_PAIRS = ((128, 1), (512, 4), (2048, 16))
ATTN_SPAN = 128
ATTN_WINDOW = 2048
N_BUCKETS = 32
REL_MAX_DIST = ATTN_WINDOW
D_FF = 4 * D_MODEL
RMS_EPS = 1e-6
NEG = -1e30

LANES = 128
SUBLANES = 8
MXU_TILE = 256
VMEM_LIMIT_BYTES = 56 * 1024 * 1024

ROW_TILE = 1024
IN_COL_TILE = 256
OUT_ROW_TILE = 512
FFN_ROW_TILE = 512
FFN_COL_TILE = 1024
RWKV_CHUNK = 64
RWKV_SEQS_PER_STEP = 4
LRU_CHUNK = 256
ATTN_MERGE_ROWS = 256
ATTN_UNROLL = 8


def _cparams(sem):
    return pltpu.CompilerParams(dimension_semantics=sem, vmem_limit_bytes=VMEM_LIMIT_BYTES)


def _dot(a, b):
    return jnp.dot(a.astype(BF16), b.astype(BF16), preferred_element_type=F32)


def _dot_nt(a, b):
    return lax.dot_general(a.astype(BF16), b.astype(BF16), (((1,), (1,)), ((), ())), preferred_element_type=F32)


def _dot_tn(a, b):
    return lax.dot_general(a.astype(BF16), b.astype(BF16), (((0,), (0,)), ((), ())), preferred_element_type=F32)


def _softplus(z):
    return jnp.maximum(z, 0.0) + jnp.log1p(jnp.exp(-jnp.abs(z)))


def _mix_dtype(rows):
    return BF16 if rows % (2 * SUBLANES) == 0 else F32


def _rms(x, g):
    ms = jnp.mean(x * x, axis=-1, keepdims=True)
    return x * lax.rsqrt(ms + RMS_EPS) * g


PXQ_WIDTH = RWKV_PROJ + 2 * LRU_WIDTH + ATTN_WIDTH
PXQ_TILES = PXQ_WIDTH // IN_COL_TILE
KV_TILES = ATTN_WIDTH // IN_COL_TILE
HEADS_PER_TILE = IN_COL_TILE // ATTN_HEAD


def _in_proj_kernel(x_ref, g_ref, w_ref, k_all_ref, v_all_ref, pxq_ref, k_ref, v_ref, h_scr):
    del k_all_ref, v_all_ref
    j = pl.program_id(1)
    tm = x_ref.shape[0]

    @pl.when(j == 0)
    def _():
        h_scr[...] = _rms(x_ref[...], g_ref[...]).astype(BF16)

    acc = jnp.dot(h_scr[...], w_ref[...], preferred_element_type=F32)

    @pl.when(j < PXQ_TILES)
    def _():
        pxq_ref[...] = acc

    def scatter_heads(o_ref, tile):
        for hh in range(HEADS_PER_TILE):
            head = tile * HEADS_PER_TILE + hh
            o_ref[pl.ds(head, tm, stride=ATTN_HEADS), :] = acc[:, hh * ATTN_HEAD:(hh + 1) * ATTN_HEAD]

    @pl.when((j >= PXQ_TILES) & (j < PXQ_TILES + KV_TILES))
    def _():
        scatter_heads(k_ref, j - PXQ_TILES)

    @pl.when(j >= PXQ_TILES + KV_TILES)
    def _():
        scatter_heads(v_ref, j - PXQ_TILES - KV_TILES)


def _in_proj(x, g, w, layer, k_all, v_all):
    m, d = x.shape
    n = w.shape[2]
    assert n == PXQ_WIDTH + 2 * ATTN_WIDTH
    tm = min(ROW_TILE, m)
    row_tiles = m // tm
    kv_spec = pl.BlockSpec((tm * ATTN_HEADS, ATTN_HEAD), lambda i, j: (layer * row_tiles + i, 0))
    return pl.pallas_call(
        _in_proj_kernel,
        grid=(row_tiles, n // IN_COL_TILE),
        in_specs=[
            pl.BlockSpec((tm, d), lambda i, j: (i, 0)),
            pl.BlockSpec((None, 1, d), lambda i, j: (layer, 0, 0)),
            pl.BlockSpec((None, d, IN_COL_TILE), lambda i, j: (layer, 0, j)),
            pl.BlockSpec(memory_space=pl.ANY),
            pl.BlockSpec(memory_space=pl.ANY),
        ],
        out_specs=[
            pl.BlockSpec((tm, IN_COL_TILE), lambda i, j: (i, jnp.minimum(j, PXQ_TILES - 1))),
            kv_spec, kv_spec,
        ],
        out_shape=[
            jax.ShapeDtypeStruct((m, PXQ_WIDTH), F32),
            jax.ShapeDtypeStruct(k_all.shape, F32),
            jax.ShapeDtypeStruct(v_all.shape, F32),
        ],
        input_output_aliases={3: 1, 4: 2},
        scratch_shapes=[pltpu.VMEM((tm, d), BF16)],
        compiler_params=_cparams(("parallel", "arbitrary")),
        name="in_proj",
    )(x, g, w, k_all, v_all)


def _out_proj_kernel(ya_ref, yb_ref, yc_ref, w_ref, g_ref, x_ref, o_ref):
    acc = jnp.dot(ya_ref[...].astype(BF16), w_ref[0:RWKV_WIDTH, :], preferred_element_type=F32)
    acc = acc + jnp.dot(yb_ref[...].astype(BF16), w_ref[RWKV_WIDTH:RWKV_WIDTH + LRU_WIDTH, :], preferred_element_type=F32)
    acc = acc + jnp.dot(yc_ref[...].astype(BF16), w_ref[RWKV_WIDTH + LRU_WIDTH:, :], preferred_element_type=F32)
    o_ref[...] = x_ref[...] + _rms(acc, g_ref[...])


def _out_proj(ya, yb, yc, w, g, x, layer):
    m, d = x.shape
    tm = min(OUT_ROW_TILE, m)
    return pl.pallas_call(
        _out_proj_kernel,
        grid=(m // tm,),
        in_specs=[
            pl.BlockSpec((tm, RWKV_WIDTH), lambda i: (i, 0)),
            pl.BlockSpec((tm, LRU_WIDTH), lambda i: (i, 0)),
            pl.BlockSpec((tm, ATTN_WIDTH), lambda i: (i, 0)),
            pl.BlockSpec((None, d, d), lambda i: (layer, 0, 0)),
            pl.BlockSpec((None, 1, d), lambda i: (layer, 0, 0)),
            pl.BlockSpec((tm, d), lambda i: (i, 0)),
        ],
        out_specs=pl.BlockSpec((tm, d), lambda i: (i, 0)),
        out_shape=jax.ShapeDtypeStruct((m, d), F32),
        compiler_params=_cparams(("parallel",)),
        name="out_proj",
    )(ya, yb, yc, w, g, x)


def _ffn_kernel(x_ref, g1_ref, w1_ref, w2_ref, g2_ref, o_ref, h_scr, acc_scr):
    j = pl.program_id(1)

    @pl.when(j == 0)
    def _():
        h_scr[...] = _rms(x_ref[...], g1_ref[...]).astype(BF16)
        acc_scr[...] = jnp.zeros_like(acc_scr)

    u = jnp.dot(h_scr[...], w1_ref[...], preferred_element_type=F32)
    u = jnp.square(jnp.maximum(u, 0.0)).astype(BF16)
    acc_scr[...] += jnp.dot(u, w2_ref[...], preferred_element_type=F32)

    @pl.when(j == pl.num_programs(1) - 1)
    def _():
        o_ref[...] = x_ref[...] + _rms(acc_scr[...], g2_ref[...])


def _ffn(x, g1, w1, w2, g2, layer):
    m, d = x.shape
    f = w1.shape[2]
    tm = min(FFN_ROW_TILE, m)
    tf = FFN_COL_TILE
    return pl.pallas_call(
        _ffn_kernel,
        grid=(m // tm, f // tf),
        in_specs=[
            pl.BlockSpec((tm, d), lambda i, j: (i, 0)),
            pl.BlockSpec((None, 1, d), lambda i, j: (layer, 0, 0)),
            pl.BlockSpec((None, d, tf), lambda i, j: (layer, 0, j)),
            pl.BlockSpec((None, tf, d), lambda i, j: (layer, j, 0)),
            pl.BlockSpec((None, 1, d), lambda i, j: (layer, 0, 0)),
        ],
        out_specs=pl.BlockSpec((tm, d), lambda i, j: (i, 0)),
        out_shape=jax.ShapeDtypeStruct((m, d), F32),
        scratch_shapes=[pltpu.VMEM((tm, d), BF16), pltpu.VMEM((tm, d), F32)],
        compiler_params=_cparams(("parallel", "arbitrary")),
        name="ffn",
    )(x, g1, w1, w2, g2)


def _rwkv_groups(chunk):
    return 2 if (RWKV_HEADS * chunk) % (2 * MXU_TILE) == 0 else 1


def _rwkv_kernel(p_ref, shift_ref, s0_ref, mu_ref, w0_ref, wup_ref, a0_ref, aup_ref, gup_ref, kk_ref, ka_ref,
                 rk_ref, lg_ref, lb_ref, ones_ref,
                 y_ref, sout_ref, s_scr, prev_scr, *, chunk, t_valid):
    C = chunk
    W = RWKV_WIDTH
    H = RWKV_HEADS
    N = RWKV_HEAD
    nb = p_ref.shape[0]
    c = pl.program_id(1)

    G = _rwkv_groups(C)
    hpg = H // G
    wl = W // G
    gc = hpg * C

    @pl.when(c == 0)
    def _():
        s_scr[...] = jnp.zeros_like(s_scr)
        for h in range(H):
            o = (h % hpg) * N
            s_scr[:, h // hpg, o:o + N, o:o + N] = s0_ref[:, h]
        prev_scr[...] = shift_ref[...]

    inv_n = 1.0 / RWKV_HEAD
    shared = dict(wup=[wup_ref], aup=[aup_ref], gup=[gup_ref],
                  ones=[ones_ref.at[g * wl:(g + 1) * wl, g * wl:(g + 1) * wl] for g in range(G)])

    def hi_lo(x):
        hi = x.astype(BF16).astype(F32)
        return [hi, x - hi]

    row = lax.broadcasted_iota(jnp.int32, (C, 1), 0)

    def cumsum_rows(z):
        dlt = 1
        while dlt < C:
            z = z + jnp.where(row >= dlt, pltpu.roll(z, dlt, axis=0), 0.0)
            dlt *= 2
        return z

    ti = lax.broadcasted_iota(jnp.int32, (C, H * C), 0)
    si = lax.broadcasted_iota(jnp.int32, (C, H * C), 1) % C
    strict = ti > si
    incl = ti >= si
    eye = (ti == si).astype(F32)
    blk_r = lax.broadcasted_iota(jnp.int32, (gc, 1), 0) // C
    mask_ch = blk_r == lax.broadcasted_iota(jnp.int32, (1, wl), 1) // N
    mask_cc = blk_r == lax.broadcasted_iota(jnp.int32, (1, gc), 1) // C
    mask_ss = (lax.broadcasted_iota(jnp.int32, (wl, 1), 0) // N
               == lax.broadcasted_iota(jnp.int32, (1, wl), 1) // N)

    def block_diag(x, mask, width):
        out = []
        for g in range(G):
            tiled = jnp.concatenate([x[:, g * width:(g + 1) * width]] * hpg, axis=0)
            out.append(jnp.where(mask, tiled, 0.0).astype(BF16))
        return out

    def per_head(a_cat, bds):
        a_b = a_cat.astype(BF16)
        return jnp.concatenate([jnp.dot(a_b[:, g * gc:(g + 1) * gc], bds[g], preferred_element_type=F32)
                                for g in range(G)], axis=1)

    def one_sequence(bi):
        p = p_ref[bi]
        shifted = jnp.where(row == 0, prev_scr[bi], pltpu.roll(p, 1, axis=0))
        prev_scr[bi] = p_ref[bi, C - 1:C, :]
        m = p + (shifted - p) * mu_ref[...]
        r = m[:, 0:W]
        k = m[:, W:2 * W]
        v = m[:, 2 * W:3 * W]
        x = m[:, 3 * W:]
        lw, la, gate = yield [("wup", jnp.tanh(x)), ("aup", x), ("gup", jax.nn.sigmoid(x))]
        w = w0_ref[...] + lw
        a = jax.nn.sigmoid(a0_ref[...] + la)
        loga = -jnp.exp(-_softplus(-w) - 0.5)
        kk = k * kk_ref[...]
        k2 = k * (1.0 + (a - 1.0) * ka_ref[...])
        (ss,) = yield [("ones", jnp.concatenate(hi_lo(kk * kk) + hi_lo(r * k2 * rk_ref[...]), axis=0))]
        kk = kk / jnp.maximum(jnp.sqrt(ss[0:C] + ss[C:2 * C]), 1e-12)
        bonus = (ss[2 * C:3 * C] + ss[3 * C:]) * v
        if t_valid < C:
            live = row < t_valid
            loga = jnp.where(live, loga, 0.0)
            kk = jnp.where(live, kk, 0.0)
            k2 = jnp.where(live, k2, 0.0)
        cl = cumsum_rows(loga)
        cl_last = cl[C - 1:C, :]
        e_neg = jnp.exp(-cl)
        e_rem = jnp.exp(cl_last - cl)
        kka = kk * a
        al = -kk * jnp.exp(cl - loga)
        rt = r * jnp.exp(cl)
        be_bd = block_diag(kka * e_neg, mask_ch, wl)
        kt_bd = block_diag(k2 * e_neg, mask_ch, wl)
        bh = kka * e_rem
        kh = k2 * e_rem
        lhs = jnp.concatenate([al, rt], axis=0).astype(BF16)
        lhs_g = [lhs[:, g * wl:(g + 1) * wl] for g in range(G)]
        g_b = jnp.concatenate([_dot_nt(lhs_g[g], be_bd[g]) for g in range(G)], axis=1)
        g_k = jnp.concatenate([_dot_nt(lhs_g[g], kt_bd[g]) for g in range(G)], axis=1)
        n_cat = jnp.where(strict, g_b[0:C], 0.0)
        a_ak = jnp.where(strict, g_k[0:C], 0.0)
        a_rb = jnp.where(incl, g_b[C:], 0.0)
        a_rk = jnp.where(incl, g_k[C:], 0.0)
        s_prev = [s_scr[bi, g] for g in range(G)]
        proj = jnp.concatenate([_dot_nt(lhs_g[g], s_prev[g]) for g in range(G)], axis=1)
        doublings = max(int(math.log2(C)) - 1, 0)
        t_cat = eye + n_cat
        pw = n_cat
        if doublings:
            pw = per_head(n_cat, block_diag(n_cat, mask_cc, gc))
        yield None
        for it in range(doublings):
            pw_bd = block_diag(pw, mask_cc, gc)
            if it < doublings - 1:
                both = per_head(jnp.concatenate([t_cat, pw], axis=0), pw_bd)
                t_cat = t_cat + both[0:C]
                pw = both[C:]
            else:
                t_cat = t_cat + per_head(t_cat, pw_bd)
            yield None
        v_bd = block_diag(v, mask_ch, wl)
        rhs = proj[0:C] + per_head(a_ak, v_bd)
        yield None
        u = per_head(t_cat, block_diag(rhs, mask_ch, wl))
        yield None
        y = proj[C:] + per_head(a_rb, block_diag(u, mask_ch, wl)) + per_head(a_rk, v_bd)
        pad = LANES - 2 * C
        uv = jnp.concatenate([u, v] + ([jnp.zeros((pad, W), F32)] if pad > 0 else []), axis=0).astype(BF16)
        bk = jnp.concatenate([bh, kh] + ([jnp.zeros((pad, W), F32)] if pad > 0 else []), axis=0).astype(BF16)
        decay = jnp.exp(cl_last)
        for g in range(G):
            lanes = slice(g * wl, (g + 1) * wl)
            upd = _dot_tn(uv[:, lanes], bk[:, lanes])
            s_scr[bi, g] = s_prev[g] * decay[:, lanes] + jnp.where(mask_ss, upd, 0.0)

        @pl.when(c == pl.num_programs(1) - 1)
        def _():
            for h in range(H):
                o = (h % hpg) * N
                sout_ref[bi, h] = s_scr[bi, h // hpg, o:o + N, o:o + N]

        (sy,) = yield [("ones", jnp.concatenate(hi_lo(y), axis=0))]
        d = y - (sy[0:C] + sy[C:]) * inv_n
        (sd,) = yield [("ones", jnp.concatenate(hi_lo(d * d), axis=0))]
        var = (sd[0:C] + sd[C:]) * inv_n
        yn = d * lax.rsqrt(var + GN_EPS) * lg_ref[...] + lb_ref[...]
        y_ref[bi] = ((yn + bonus) * gate).astype(y_ref.dtype)

    seqs = [one_sequence(bi) for bi in range(nb)]
    replies = [None] * nb
    while True:
        asks, finished = [], 0
        for seq, reply in zip(seqs, replies):
            try:
                asks.append(seq.send(reply))
            except StopIteration:
                finished += 1
        if finished:
            assert finished == nb
            break
        if asks[0] is None:
            replies = [None] * nb
            continue
        replies = [[] for _ in range(nb)]
        for qi, (name, _) in enumerate(asks[0]):
            lhs = jnp.concatenate([ask[qi][1] for ask in asks], axis=0).astype(BF16)
            parts = shared[name]
            kw = lhs.shape[1] // len(parts)
            z = jnp.concatenate([jnp.dot(lhs[:, g * kw:(g + 1) * kw], part[...], preferred_element_type=F32)
                                 for g, part in enumerate(parts)], axis=1)
            rows = z.shape[0] // nb
            for si in range(nb):
                replies[si].append(z[si * rows:(si + 1) * rows])


def _rwkv(p, shift_prev, s0, prm, batch, t_len, t_valid, chunk):
    W = RWKV_WIDTH
    nc = t_len // chunk
    nb = _largest_divisor(batch, RWKV_SEQS_PER_STEP)
    vec = lambda n: pl.BlockSpec((1, n), lambda b, c: (0, 0))
    mat = lambda r, n: pl.BlockSpec((r, n), lambda b, c: (0, 0))
    state = pl.BlockSpec((nb, RWKV_HEADS, RWKV_HEAD, RWKV_HEAD), lambda b, c: (b, 0, 0, 0))
    groups = _rwkv_groups(chunk)
    kern = functools.partial(_rwkv_kernel, chunk=chunk, t_valid=t_valid)
    return pl.pallas_call(
        kern,
        grid=(batch // nb, nc),
        in_specs=[
            pl.BlockSpec((nb, chunk, RWKV_PROJ), lambda b, c: (b, c, 0)),
            pl.BlockSpec((nb, 1, RWKV_PROJ), lambda b, c: (b, 0, 0)),
            state,
            vec(RWKV_PROJ), vec(W), mat(LORA_WIDTH, W), vec(W), mat(LORA_WIDTH, W), mat(LORA_WIDTH, W),
            vec(W), vec(W), vec(W), vec(W), vec(W), mat(W, W),
        ],
        out_specs=[pl.BlockSpec((nb, chunk, W), lambda b, c: (b, c, 0)), state],
        out_shape=[
            jax.ShapeDtypeStruct((batch, t_len, W), _mix_dtype(chunk)),
            jax.ShapeDtypeStruct((batch, RWKV_HEADS, RWKV_HEAD, RWKV_HEAD), F32),
        ],
        scratch_shapes=[pltpu.VMEM((nb, groups, W // groups, W // groups), F32),
                        pltpu.VMEM((nb, 1, RWKV_PROJ), F32)],
        compiler_params=_cparams(("parallel", "arbitrary")),
        name="rwkv7",
    )(p, shift_prev, s0, prm["mu"], prm["w0"], prm["w_up"], prm["a0"], prm["a_up"], prm["g_up"],
      prm["k_k"], prm["k_a"], prm["r_k"], prm["lnx_g"], prm["lnx_b"], prm["ones_bd"])


def _lru_kernel(x0_ref, x1_ref, g0_ref, g1_ref, conv_ref, h0_ref, cw_ref, cb_ref, wa_ref, ba_ref, wx_ref, bx_ref,
                lam_ref, y_ref, hout_ref, tail_scr, h_scr, *, chunk, t_valid):
    C = chunk
    c = pl.program_id(1)

    @pl.when(c == 0)
    def _():
        tail_scr[...] = jnp.zeros_like(tail_scr)
        tail_scr[SUBLANES - (CONV_W - 1):, :] = conv_ref[...]
        h_scr[...] = h0_ref[...]

    x = jnp.concatenate([x0_ref[...], x1_ref[...]], axis=1)
    gt = jnp.concatenate([g0_ref[...], g1_ref[...]], axis=1)
    ext = jnp.concatenate([tail_scr[...], x], axis=0)
    tail_scr[...] = x[C - SUBLANES:, :]
    xc = cb_ref[...] + x * cw_ref[CONV_W - 1:CONV_W, :]
    for dlt in range(1, CONV_W):
        sh = pltpu.roll(ext, dlt, axis=0)[SUBLANES:, :]
        xc = xc + sh * cw_ref[CONV_W - 1 - dlt:CONV_W - dlt, :]

    gate_r = jax.nn.sigmoid(_dot(xc, wa_ref[...]) + ba_ref[...])
    gate_i = jax.nn.sigmoid(_dot(xc, wx_ref[...]) + bx_ref[...])
    log_a = -RG_C * gate_r * _softplus(-lam_ref[...])
    a = jnp.exp(log_a)
    th = jnp.tanh(log_a)
    u = jnp.sqrt(-2.0 * th / (1.0 - th)) * (gate_i * xc)

    row = lax.broadcasted_iota(jnp.int32, (C, 1), 0)
    dlt = 1
    while dlt < C:
        keep = row >= dlt
        a_sh = jnp.where(keep, pltpu.roll(a, dlt, axis=0), 1.0)
        u_sh = jnp.where(keep, pltpu.roll(u, dlt, axis=0), 0.0)
        u = a * u_sh + u
        a = a * a_sh
        dlt *= 2
    h = a * h_scr[...] + u
    last = min(t_valid, C) - 1
    h_scr[...] = h[last:last + 1, :]
    hout_ref[...] = h[last:last + 1, :]
    y_ref[...] = (h * jax.nn.gelu(gt)).astype(y_ref.dtype)


def _lru(pxq, conv_prev, h0, prm, batch, t_len, t_valid, chunk):
    W = LRU_WIDTH
    nc = t_len // chunk
    half = W // 2
    col0 = RWKV_PROJ // half
    assert RWKV_PROJ % half == 0
    part = lambda k: pl.BlockSpec((chunk, half), lambda b, c: (b * nc + c, col0 + k))
    vec = lambda n: pl.BlockSpec((1, n), lambda b, c: (0, 0))
    kern = functools.partial(_lru_kernel, chunk=chunk, t_valid=t_valid)
    return pl.pallas_call(
        kern,
        grid=(batch, nc),
        in_specs=[
            part(0), part(1), part(2), part(3),
            pl.BlockSpec((None, CONV_W - 1, W), lambda b, c: (b, 0, 0)),
            pl.BlockSpec((None, 1, W), lambda b, c: (b, 0, 0)),
            pl.BlockSpec((CONV_W, W), lambda b, c: (0, 0)),
            vec(W),
            pl.BlockSpec((W, W), lambda b, c: (0, 0)), vec(W),
            pl.BlockSpec((W, W), lambda b, c: (0, 0)), vec(W),
            vec(W),
        ],
        out_specs=[
            pl.BlockSpec((chunk, W), lambda b, c: (b * nc + c, 0)),
            pl.BlockSpec((None, 1, W), lambda b, c: (b, 0, 0)),
        ],
        out_shape=[
            jax.ShapeDtypeStruct((batch * t_len, W), _mix_dtype(chunk)),
            jax.ShapeDtypeStruct((batch, 1, W), F32),
        ],
        scratch_shapes=[pltpu.VMEM((SUBLANES, W), F32), pltpu.VMEM((1, W), F32)],
        compiler_params=_cparams(("parallel", "arbitrary")),
        name="rglru",
    )(pxq, pxq, pxq, pxq, conv_prev, h0, prm["conv_w"], prm["conv_b"], prm["wa_bd"], prm["ba"], prm["wx_bd"],
      prm["bx"], prm["lam"])


def _attn_blocks(q_ref, k_ref, v_ref, bias_ref, og_scr, lse_scr, gi, dil, blocks):
    scale = ATTN_HEAD ** -0.5
    B = ATTN_SPAN

    def rows(start, first):
        if first is True:
            start, n = start, B
        elif first is False:
            start, n = start - B * dil, 2 * B
        else:
            start, n = jnp.maximum(start - B * dil, 0), 2 * B
        return pl.ds(start, n) if dil == 1 else pl.ds(start, n, stride=dil)

    scores = []
    for qs, first in blocks:
        qb = q_ref[rows(qs, True), :]
        if first is True:
            bias = bias_ref[gi, :, B:]
        elif first is False:
            bias = bias_ref[gi]
        else:
            own_then_masked = jnp.concatenate([bias_ref[gi, :, B:], jnp.full((B, B), NEG, F32)], axis=1)
            bias = jnp.where(first, own_then_masked, bias_ref[gi])
        scores.append(_dot_nt(qb, k_ref[rows(qs, first), :]) * scale + bias)
    probs = []
    for s in scores:
        mx = jnp.max(s, axis=-1, keepdims=True)
        pr = jnp.exp(s - mx)
        den = jnp.sum(pr, axis=-1, keepdims=True)
        probs.append(((pr / den).astype(BF16), mx + jnp.log(den)))
    outs = [_dot(pn, v_ref[rows(qs, first), :]) for (qs, first), (pn, _) in zip(blocks, probs)]
    for (qs, _), o, (_, lse) in zip(blocks, outs, probs):
        og_scr[gi, rows(qs, True), :] = o
        lse_scr[gi, rows(qs, True), :] = jnp.broadcast_to(lse, (B, ATTN_HEAD))


def _largest_divisor(n, cap):
    return max(u for u in range(1, cap + 1) if n % u == 0)


def _attn_prompt_kernel(q_ref, kil_ref, vil_ref, bias_ref, o_ref, k_ref, v_ref, og_scr, lse_scr, *, seq):
    B = ATTN_SPAN
    h = pl.program_id(1)
    R = ATTN_MERGE_ROWS

    def gather_head(i, carry):
        dst = pl.ds(pl.multiple_of(i * R, R), R)
        src = pl.ds(h + i * (R * ATTN_HEADS), R, stride=ATTN_HEADS)
        k_ref[dst, :] = kil_ref[src, :]
        v_ref[dst, :] = vil_ref[src, :]
        return carry

    lax.fori_loop(0, seq // R, gather_head, 0)

    for gi, (win, dil) in enumerate(DIL_PAIRS):
        nblk = seq // (dil * B)
        run = functools.partial(_attn_blocks, q_ref, k_ref, v_ref, bias_ref, og_scr, lse_scr, gi, dil)
        if nblk == 1:
            per = _largest_divisor(dil, ATTN_UNROLL)

            def body(i, carry, run=run, per=per):
                run([(i * per + u, True) for u in range(per)])
                return carry

            lax.fori_loop(0, dil // per, body, 0)
        elif dil == 1:
            per = _largest_divisor(nblk, ATTN_UNROLL)

            def body(i, carry, run=run, per=per):
                run([(pl.multiple_of((i * per + u) * B, B), (i == 0) if u == 0 else False) for u in range(per)])
                return carry

            lax.fori_loop(0, nblk // per, body, 0)
        else:
            per = _largest_divisor(dil, max(ATTN_UNROLL // nblk, 1))

            def body(i, carry, run=run, dil=dil, nblk=nblk, per=per):
                run([(i * per + u + blk * (B * dil), blk == 0) for u in range(per) for blk in range(nblk)])
                return carry

            lax.fori_loop(0, dil // per, body, 0)

    def merge(i, carry):
        sl = pl.ds(pl.multiple_of(i * R, R), R)
        ls = [lse_scr[gi, sl, :] for gi in range(len(DIL_PAIRS))]
        mx = functools.reduce(jnp.maximum, ls)
        ws = [jnp.exp(l - mx) for l in ls]
        tot = functools.reduce(lambda a, b: a + b, ws)
        acc = (ws[0] / tot) * og_scr[0, sl, :]
        for gi in range(1, len(DIL_PAIRS)):
            acc = acc + (ws[gi] / tot) * og_scr[gi, sl, :]
        o_ref[sl, :] = acc.astype(o_ref.dtype)
        return carry

    lax.fori_loop(0, seq // R, merge, 0)


def _attn_prompt(pxq, k_all, v_all, bias, layer, batch, seq):
    H = ATTN_HEADS
    G = len(DIL_PAIRS)
    q_col0 = (PXQ_WIDTH - ATTN_WIDTH) // ATTN_HEAD
    kern = functools.partial(_attn_prompt_kernel, seq=seq)
    kv = pl.BlockSpec((seq * H, ATTN_HEAD), lambda b, h: (layer * batch + b, 0))
    return pl.pallas_call(
        kern,
        grid=(batch, H),
        in_specs=[
            pl.BlockSpec((seq, ATTN_HEAD), lambda b, h: (b, q_col0 + h)),
            kv, kv,
            pl.BlockSpec((G, None, ATTN_SPAN, 2 * ATTN_SPAN), lambda b, h: (0, h, 0, 0)),
        ],
        out_specs=pl.BlockSpec((seq, ATTN_HEAD), lambda b, h: (b, h)),
        out_shape=jax.ShapeDtypeStruct((batch * seq, ATTN_WIDTH), BF16),
        scratch_shapes=[pltpu.VMEM((seq, ATTN_HEAD), F32), pltpu.VMEM((seq, ATTN_HEAD), F32),
                        pltpu.VMEM((G, seq, ATTN_HEAD), F32), pltpu.VMEM((G, seq, ATTN_HEAD), F32)],
        compiler_params=_cparams(("parallel", "arbitrary")),
        name="attn_prompt",
    )(pxq, k_all, v_all, bias)


def _attn_sample_kernel(pxq_ref, kn_ref, vn_ref, kc_ref, vc_ref, bias_ref, o_ref, q_scr, kn_scr, vn_scr, *, t_new, win):
    scale = ATTN_HEAD ** -0.5
    H = ATTN_HEADS
    G = len(DIL_PAIRS)
    TP = q_scr.shape[0]
    q_col0 = PXQ_WIDTH - ATTN_WIDTH
    q_scr[...] = jnp.zeros_like(q_scr)
    kn_scr[...] = jnp.zeros_like(kn_scr)
    vn_scr[...] = jnp.zeros_like(vn_scr)
    for h in range(H):
        lanes = slice(h * ATTN_HEAD, (h + 1) * ATTN_HEAD)
        q_scr[0:t_new, :] = pxq_ref[:, q_col0 + h * ATTN_HEAD:q_col0 + (h + 1) * ATTN_HEAD]
        kn_scr[0:t_new, :] = kn_ref[pl.ds(h, t_new, stride=H), :]
        vn_scr[0:t_new, :] = vn_ref[pl.ds(h, t_new, stride=H), :]
        qb = q_scr[...].astype(BF16)
        kc = kc_ref[pl.ds(h, win, stride=H), :]
        vc = vc_ref[pl.ds(h, win, stride=H), :]
        s = jnp.concatenate([_dot_nt(qb, kc), _dot_nt(qb, kn_scr[...])], axis=1) * scale
        prs, lses = [], []
        for gi in range(G):
            sg = s + bias_ref[gi, h]
            mx = jnp.max(sg, axis=-1, keepdims=True)
            pr = jnp.exp(sg - mx)
            den = jnp.sum(pr, axis=-1, keepdims=True)
            prs.append(pr / den)
            lses.append(mx + jnp.log(den))
        pall = jnp.concatenate(prs, axis=0)
        oall = _dot(pall[:, :win], vc) + _dot(pall[:, win:], vn_scr[...])
        mx = functools.reduce(jnp.maximum, lses)
        ws = [jnp.exp(l - mx) for l in lses]
        tot = functools.reduce(lambda a, b: a + b, ws)
        acc = (ws[0] / tot) * oall[0:TP]
        for gi in range(1, G):
            acc = acc + (ws[gi] / tot) * oall[gi * TP:(gi + 1) * TP]
        o_ref[:, lanes] = acc[0:t_new].astype(o_ref.dtype)


def _attn_sample(pxq, k_new, v_new, k_cache, v_cache, bias, layer, batch, t_new):
    H = ATTN_HEADS
    win = k_cache.shape[2] // H
    tp = bias.shape[2]
    kern = functools.partial(_attn_sample_kernel, t_new=t_new, win=win)
    cache = pl.BlockSpec((None, None, win * H, ATTN_HEAD), lambda b: (layer, b, 0, 0))
    new = pl.BlockSpec((None, t_new * H, ATTN_HEAD), lambda b: (layer * batch + b, 0, 0))
    return pl.pallas_call(
        kern,
        grid=(batch,),
        in_specs=[pl.BlockSpec((None, t_new, PXQ_WIDTH), lambda b: (b, 0, 0)), new, new, cache, cache,
                  pl.BlockSpec(bias.shape, lambda b: (0, 0, 0, 0))],
        out_specs=pl.BlockSpec((None, t_new, ATTN_WIDTH), lambda b: (b, 0, 0)),
        out_shape=jax.ShapeDtypeStruct((batch, t_new, ATTN_WIDTH), F32),
        scratch_shapes=[pltpu.VMEM((tp, ATTN_HEAD), F32), pltpu.VMEM((LANES, ATTN_HEAD), F32),
                        pltpu.VMEM((LANES, ATTN_HEAD), F32)],
        compiler_params=_cparams(("parallel",)),
        name="attn_sample",
    )(pxq, k_new, v_new, k_cache, v_cache, bias)


def _bias_kernel(idx_ref, rb_ref, o_ref):
    h = pl.program_id(0)
    idx = idx_ref[...]
    acc = jnp.full(idx.shape, NEG, F32)
    for b in range(N_BUCKETS):
        acc = jnp.where(idx == b, rb_ref[b, h], acc)
    o_ref[...] = acc


def _bias_table(idx, rel_bias):
    g, r, c = idx.shape
    heads = rel_bias.shape[1]
    return pl.pallas_call(
        _bias_kernel,
        grid=(heads,),
        in_specs=[pl.BlockSpec((g, r, c), lambda h: (0, 0, 0)),
                  pl.BlockSpec(memory_space=pltpu.SMEM)],
        out_specs=pl.BlockSpec((g, None, r, c), lambda h: (0, h, 0, 0)),
        out_shape=jax.ShapeDtypeStruct((g, heads, r, c), F32),
        compiler_params=_cparams(("parallel",)),
        name="bias_table",
    )(jnp.asarray(idx, jnp.int32), rel_bias.astype(F32))


def _t5_bucket_static(dist):
    dist = np.asarray(dist, np.int64)
    exact = N_BUCKETS // 2
    d = np.maximum(dist, 1).astype(np.float64)
    large = exact + (np.log(d / exact) / math.log(REL_MAX_DIST / exact) * (N_BUCKETS - exact)).astype(np.int64)
    return np.where(dist < exact, dist, np.minimum(large, N_BUCKETS - 1))


def _prompt_bias_idx():
    B = ATTN_SPAN
    qi = np.arange(B)[:, None]
    kj = np.arange(2 * B)[None, :]
    delta = qi + B - kj
    valid = (delta >= 0) & (delta <= B)
    return np.stack([np.where(valid, _t5_bucket_static(np.clip(delta, 0, B) * dil), -1) for _, dil in DIL_PAIRS])


def _sample_bias_idx(t_new, t_pad, win, total):
    r = np.arange(total)[None, :]
    t = np.arange(t_pad)[:, None]
    dist = win + t - r
    tabs = []
    for _, dil in DIL_PAIRS:
        assert win >= dil * ATTN_SPAN
        valid = (dist >= 0) & (dist % dil == 0) & (dist <= dil * ATTN_SPAN) & (t < t_new)
        tabs.append(np.where(valid, _t5_bucket_static(np.clip(dist, 0, dil * ATTN_SPAN)), -1))
    return np.stack(tabs)


def _block_diag(blocks):
    n, c, d = blocks.shape
    eye = jnp.eye(n, dtype=blocks.dtype)
    return jnp.einsum("ncd,nm->ncmd", blocks, eye).reshape(n * c, n * d)


def _layer_params(l, a):
    W = RWKV_WIDTH
    row = lambda t: t[l].reshape(1, -1).astype(F32)

    def lora_pad(w, off):
        z = jnp.zeros((LORA_WIDTH, W), F32)
        return z.at[off:off + w.shape[0]].set(w).astype(BF16)

    head_of = np.arange(W) // RWKV_HEAD
    ones_bd = jnp.asarray((head_of[:, None] == head_of[None, :]).astype(np.float32), BF16)
    rwkv = dict(
        mu=row(a["rwkv_mu"]), w0=row(a["rwkv_w0"]), a0=row(a["rwkv_a0"]),
        w_up=lora_pad(a["rwkv_w_up"][l], 0),
        a_up=lora_pad(a["rwkv_a_up"][l], DECAY_LORA),
        g_up=lora_pad(a["rwkv_g_up"][l], DECAY_LORA + ICLR_LORA),
        k_k=row(a["rwkv_k_k"]), k_a=row(a["rwkv_k_a"]), r_k=row(a["rwkv_r_k"]),
        lnx_g=row(a["rwkv_lnx_g"]), lnx_b=row(a["rwkv_lnx_b"]),
        ones_bd=ones_bd,
    )
    lru = dict(
        conv_w=a["lru_conv_w"][l].astype(F32), conv_b=row(a["lru_conv_b"]),
        wa_bd=_block_diag(a["lru_wa"][l]).astype(BF16), ba=row(a["lru_ba"]),
        wx_bd=_block_diag(a["lru_wx"][l]).astype(BF16), bx=row(a["lru_bx"]),
        lam=row(a["lru_lambda"]),
    )
    return dict(rwkv=rwkv, lru=lru)


def _shared_params(a):
    gain = lambda t: t.reshape(t.shape[0], 1, -1).astype(F32)
    return dict(
        w_in=a["w_in"].astype(BF16), w_out=a["w_out"].astype(BF16),
        w1=a["ffn_w1"].astype(BF16), w2=a["ffn_w2"].astype(BF16),
        g_mix_pre=gain(a["norm_mix_pre"]), g_mix_post=gain(a["norm_mix_post"]),
        g_ffn_pre=gain(a["norm_ffn_pre"]), g_ffn_post=gain(a["norm_ffn_post"]),
    )


def _layer(x, layer, sp, lp, batch, t_len, shift_prev, s0, conv_prev, h0, kv_all, attend):
    pxq, k_all, v_all = _in_proj(x, sp["g_mix_pre"], sp["w_in"], layer, *kv_all)
    pxq3 = pxq.reshape(batch, t_len, PXQ_WIDTH)
    if t_len % RWKV_CHUNK == 0:
        y_a, s_fin = _rwkv(pxq3, shift_prev, s0, lp["rwkv"], batch, t_len, t_len, RWKV_CHUNK)
        y_a = y_a.reshape(batch * t_len, -1)
        y_b, h_fin = _lru(pxq, conv_prev, h0, lp["lru"], batch, t_len, t_len, LRU_CHUNK)
    else:
        assert t_len <= SUBLANES
        padded = jnp.pad(pxq3, ((0, 0), (0, SUBLANES - t_len), (0, 0)))
        y_a, s_fin = _rwkv(padded, shift_prev, s0, lp["rwkv"], batch, SUBLANES, t_len, SUBLANES)
        y_a = y_a[:, :t_len].reshape(batch * t_len, -1)
        y_b, h_fin = _lru(padded.reshape(batch * SUBLANES, PXQ_WIDTH), conv_prev, h0, lp["lru"], batch, SUBLANES,
                          t_len, SUBLANES)
        y_b = y_b.reshape(batch, SUBLANES, -1)[:, :t_len].reshape(batch * t_len, -1)
    y_c = attend(pxq, k_all, v_all)

    x = _out_proj(y_a, y_b, y_c, sp["w_out"], sp["g_mix_post"], x, layer)
    x = _ffn(x, sp["g_ffn_pre"], sp["w1"], sp["w2"], sp["g_ffn_post"], layer)

    lru_x = pxq3[:, :, RWKV_PROJ:RWKV_PROJ + LRU_WIDTH]
    if t_len >= CONV_W - 1:
        conv_new = lru_x[:, t_len - (CONV_W - 1):]
    else:
        conv_new = jnp.concatenate([conv_prev, lru_x], axis=1)[:, -(CONV_W - 1):]
    state = (pxq3[:, -1, :RWKV_PROJ], s_fin, conv_new, h_fin.reshape(batch, LRU_WIDTH))
    return x, state, (k_all, v_all)


def kernel(x_prompt, x_sample, state_rwkv_wkv, state_rwkv_shift, state_lru_h, state_lru_conv, cache_attn_k, cache_attn_v, rel_bias, norm_mix_pre, norm_mix_post, norm_ffn_pre, norm_ffn_post, w_in, w_out, rwkv_mu, rwkv_w0, rwkv_w_up, rwkv_a0, rwkv_a_up, rwkv_g_up, rwkv_k_k, rwkv_k_a, rwkv_r_k, rwkv_lnx_g, rwkv_lnx_b, lru_conv_w, lru_conv_b, lru_wa, lru_ba, lru_wx, lru_bx, lru_lambda, ffn_w1, ffn_w2):
    a = dict(norm_mix_pre=norm_mix_pre, norm_mix_post=norm_mix_post, norm_ffn_pre=norm_ffn_pre,
             norm_ffn_post=norm_ffn_post, w_in=w_in, w_out=w_out, rwkv_mu=rwkv_mu, rwkv_w0=rwkv_w0,
             rwkv_w_up=rwkv_w_up, rwkv_a0=rwkv_a0, rwkv_a_up=rwkv_a_up, rwkv_g_up=rwkv_g_up, rwkv_k_k=rwkv_k_k,
             rwkv_k_a=rwkv_k_a, rwkv_r_k=rwkv_r_k, rwkv_lnx_g=rwkv_lnx_g, rwkv_lnx_b=rwkv_lnx_b,
             lru_conv_w=lru_conv_w, lru_conv_b=lru_conv_b, lru_wa=lru_wa, lru_ba=lru_ba, lru_wx=lru_wx,
             lru_bx=lru_bx, lru_lambda=lru_lambda, ffn_w1=ffn_w1, ffn_w2=ffn_w2)
    depth = w_in.shape[0]
    pb, seq, _ = x_prompt.shape
    sb, t_new, _ = x_sample.shape
    win = cache_attn_k.shape[2]
    keep = min(ATTN_WINDOW, seq)
    total = win + LANES
    assert t_new <= LANES and seq % (DIL_PAIRS[-1][1] * ATTN_SPAN) == 0

    bias_p = _bias_table(_prompt_bias_idx(), rel_bias)
    bias_s = _bias_table(_sample_bias_idx(t_new, SUBLANES, win, total), rel_bias)
    k_cache = cache_attn_k.reshape(depth, sb, win * ATTN_HEADS, ATTN_HEAD)
    v_cache = cache_attn_v.reshape(depth, sb, win * ATTN_HEADS, ATTN_HEAD)

    xp = x_prompt.reshape(pb * seq, D_MODEL)
    xs = x_sample.reshape(sb * t_new, D_MODEL)
    new_p, new_s = [], []
    sp = _shared_params(a)
    kv_p = tuple(jnp.zeros((depth * pb * seq * ATTN_HEADS, ATTN_HEAD), F32) for _ in range(2))
    kv_s = tuple(jnp.zeros((depth * sb * t_new * ATTN_HEADS, ATTN_HEAD), F32) for _ in range(2))
    for l in range(depth):
        lp = _layer_params(l, a)
        attend_p = lambda pxq, k_all, v_all, l=l: _attn_prompt(pxq, k_all, v_all, bias_p, l, pb, seq)
        xp, st_p, kv_p = _layer(xp, l, sp, lp, pb, seq,
                                jnp.zeros((pb, 1, RWKV_PROJ), F32),
                                jnp.zeros((pb, RWKV_HEADS, RWKV_HEAD, RWKV_HEAD), F32),
                                jnp.zeros((pb, CONV_W - 1, LRU_WIDTH), F32), jnp.zeros((pb, 1, LRU_WIDTH), F32),
                                kv_p, attend_p)
        attend_s = lambda pxq, k_all, v_all, l=l: _attn_sample(
            pxq.reshape(sb, t_new, PXQ_WIDTH), k_all.reshape(depth * sb, t_new * ATTN_HEADS, ATTN_HEAD),
            v_all.reshape(depth * sb, t_new * ATTN_HEADS, ATTN_HEAD), k_cache, v_cache, bias_s, l, sb, t_new,
        ).reshape(sb * t_new, ATTN_WIDTH)
        xs, st_s, kv_s = _layer(xs, l, sp, lp, sb, t_new,
                                state_rwkv_shift[l].reshape(sb, 1, RWKV_PROJ), state_rwkv_wkv[l].astype(F32),
                                state_lru_conv[l], state_lru_h[l].reshape(sb, 1, LRU_WIDTH), kv_s, attend_s)
        new_p.append(st_p)
        new_s.append(st_s)

    stack = lambda sts, i: jnp.stack([s[i] for s in sts])
    k_p, v_p = (t.reshape(depth, pb, seq, ATTN_HEADS, ATTN_HEAD)[:, :, seq - keep:] for t in kv_p)
    k_s, v_s = (t.reshape(depth, sb, t_new, ATTN_HEADS, ATTN_HEAD) for t in kv_s)
    return (xp.reshape(pb, seq, D_MODEL), xs.reshape(sb, t_new, D_MODEL),
            stack(new_p, 1), stack(new_s, 1), stack(new_p, 0), stack(new_s, 0),
            stack(new_p, 3), stack(new_s, 3), stack(new_p, 2), stack(new_s, 2),
            k_p, k_s, v_p, v_s)
```

```python
import functools
import math

import numpy as np
import jax
import jax.numpy as jnp
from jax import lax
from jax.experimental import pallas as pl
from jax.experimental.pallas import tpu as pltpu

F32 = jnp.float32
BF16 = jnp.bfloat16

D_MODEL = 2048
RWKV_WIDTH = 512
RWKV_HEAD = 64
RWKV_HEADS = RWKV_WIDTH // RWKV_HEAD
DECAY_LORA = 64
ICLR_LORA = 64
GATE_LORA = 128
LORA_WIDTH = DECAY_LORA + ICLR_LORA + GATE_LORA
RWKV_PROJ = 3 * RWKV_WIDTH + LORA_WIDTH
GN_EPS = 64e-5
LRU_WIDTH = 512
LRU_BLOCKS = 8
LRU_BLOCK = LRU_WIDTH // LRU_BLOCKS
CONV_W = 4
RG_C = 8.0
ATTN_WIDTH = 1024
ATTN_HEAD = 128
ATTN_HEADS = ATTN_WIDTH // ATTN_HEAD
DIL_PAIRS = ((128, 1), (512, 4), (2048, 16))
ATTN_SPAN = 128
ATTN_WINDOW = 2048
N_BUCKETS = 32
REL_MAX_DIST = ATTN_WINDOW
D_FF = 4 * D_MODEL
RMS_EPS = 1e-6
NEG = -1e30

LANES = 128
SUBLANES = 8
MXU_TILE = 256
VMEM_LIMIT_BYTES = 56 * 1024 * 1024

ROW_TILE = 1024
IN_COL_TILE = 256
OUT_ROW_TILE = 512
FFN_ROW_TILE = 512
FFN_COL_TILE = 1024
FFN_CAST_COL_TILE = 512
RWKV_CHUNK = 64
RWKV_SEQS_PER_STEP = 4
LRU_CHUNK = 256
ATTN_MERGE_ROWS = 256
ATTN_UNROLL = 8


def _cparams(sem):
    return pltpu.CompilerParams(dimension_semantics=sem, vmem_limit_bytes=VMEM_LIMIT_BYTES)


def _dot(a, b):
    return jnp.dot(a.astype(BF16), b.astype(BF16), preferred_element_type=F32)


def _dot_nt(a, b):
    return lax.dot_general(a.astype(BF16), b.astype(BF16), (((1,), (1,)), ((), ())), preferred_element_type=F32)


def _dot_tn(a, b):
    return lax.dot_general(a.astype(BF16), b.astype(BF16), (((0,), (0,)), ((), ())), preferred_element_type=F32)


def _softplus(z):
    return jnp.maximum(z, 0.0) + jnp.log1p(jnp.exp(-jnp.abs(z)))


def _mix_dtype(rows):
    return BF16 if rows % (2 * SUBLANES) == 0 else F32


def _rms(x, g):
    ms = jnp.mean(x * x, axis=-1, keepdims=True)
    return x * lax.rsqrt(ms + RMS_EPS) * g


PXQ_WIDTH = RWKV_PROJ + 2 * LRU_WIDTH + ATTN_WIDTH
PXQ_TILES = PXQ_WIDTH // IN_COL_TILE
KV_TILES = ATTN_WIDTH // IN_COL_TILE
HEADS_PER_TILE = IN_COL_TILE // ATTN_HEAD


def _in_proj_kernel(x_ref, g_ref, w_ref, k_all_ref, v_all_ref, pxq_ref, k_ref, v_ref, *rest):
    del k_all_ref, v_all_ref
    h_scr = rest[-1]
    j = pl.program_id(1)
    tm = x_ref.shape[0]

    @pl.when(j == 0)
    def _():
        h_scr[...] = _rms(x_ref[...], g_ref[...]).astype(BF16)

    w = w_ref[...].astype(BF16)
    if len(rest) > 1:
        rest[0][...] = w
    acc = jnp.dot(h_scr[...], w, preferred_element_type=F32)

    @pl.when(j < PXQ_TILES)
    def _():
        pxq_ref[...] = acc

    def scatter_heads(o_ref, tile):
        for hh in range(HEADS_PER_TILE):
            head = tile * HEADS_PER_TILE + hh
            o_ref[pl.ds(head, tm, stride=ATTN_HEADS), :] = acc[:, hh * ATTN_HEAD:(hh + 1) * ATTN_HEAD]

    @pl.when((j >= PXQ_TILES) & (j < PXQ_TILES + KV_TILES))
    def _():
        scatter_heads(k_ref, j - PXQ_TILES)

    @pl.when(j >= PXQ_TILES + KV_TILES)
    def _():
        scatter_heads(v_ref, j - PXQ_TILES - KV_TILES)


def _in_proj(x, g, w, layer, w_layer, k_all, v_all, emit_bf16=False):
    m, d = x.shape
    n = w.shape[2]
    assert n == PXQ_WIDTH + 2 * ATTN_WIDTH
    tm = min(ROW_TILE, m)
    row_tiles = m // tm
    assert not emit_bf16 or row_tiles == 1
    kv_spec = pl.BlockSpec((tm * ATTN_HEADS, ATTN_HEAD), lambda i, j: (layer * row_tiles + i, 0))
    out_specs = [
        pl.BlockSpec((tm, IN_COL_TILE), lambda i, j: (i, jnp.minimum(j, PXQ_TILES - 1))),
        kv_spec, kv_spec,
    ]
    out_shape = [
        jax.ShapeDtypeStruct((m, PXQ_WIDTH), F32),
        jax.ShapeDtypeStruct(k_all.shape, F32),
        jax.ShapeDtypeStruct(v_all.shape, F32),
    ]
    if emit_bf16:
        out_specs.append(pl.BlockSpec((None, d, IN_COL_TILE), lambda i, j: (0, 0, j)))
        out_shape.append(jax.ShapeDtypeStruct((1, d, n), BF16))
    return pl.pallas_call(
        _in_proj_kernel,
        grid=(row_tiles, n // IN_COL_TILE),
        in_specs=[
            pl.BlockSpec((tm, d), lambda i, j: (i, 0)),
            pl.BlockSpec((None, 1, d), lambda i, j: (layer, 0, 0)),
            pl.BlockSpec((None, d, IN_COL_TILE), lambda i, j: (w_layer, 0, j)),
            pl.BlockSpec(memory_space=pl.ANY),
            pl.BlockSpec(memory_space=pl.ANY),
        ],
        out_specs=out_specs,
        out_shape=out_shape,
        input_output_aliases={3: 1, 4: 2},
        scratch_shapes=[pltpu.VMEM((tm, d), BF16)],
        compiler_params=_cparams(("parallel", "arbitrary")),
        name="in_proj",
    )(x, g, w, k_all, v_all)


def _out_proj_kernel(ya_ref, yb_ref, yc_ref, w_ref, g_ref, x_ref, o_ref, *w_copy):
    c1, c2 = RWKV_WIDTH, RWKV_WIDTH + LRU_WIDTH
    parts = [(ya_ref, slice(0, c1)), (yb_ref, slice(c1, c2)), (yc_ref, slice(c2, None))]
    acc = None
    for y_ref, rows in parts:
        w = w_ref[rows, :].astype(BF16)
        if w_copy:
            w_copy[0][rows, :] = w
        t = jnp.dot(y_ref[...].astype(BF16), w, preferred_element_type=F32)
        acc = t if acc is None else acc + t
    o_ref[...] = x_ref[...] + _rms(acc, g_ref[...])


def _out_proj(ya, yb, yc, w, g, x, layer, w_layer, emit_bf16=False):
    m, d = x.shape
    tm = min(OUT_ROW_TILE, m)
    assert not emit_bf16 or m == tm
    out_specs = [pl.BlockSpec((tm, d), lambda i: (i, 0))]
    out_shape = [jax.ShapeDtypeStruct((m, d), F32)]
    if emit_bf16:
        out_specs.append(pl.BlockSpec((None, d, d), lambda i: (0, 0, 0)))
        out_shape.append(jax.ShapeDtypeStruct((1, d, d), BF16))
    res = pl.pallas_call(
        _out_proj_kernel,
        grid=(m // tm,),
        in_specs=[
            pl.BlockSpec((tm, RWKV_WIDTH), lambda i: (i, 0)),
            pl.BlockSpec((tm, LRU_WIDTH), lambda i: (i, 0)),
            pl.BlockSpec((tm, ATTN_WIDTH), lambda i: (i, 0)),
            pl.BlockSpec((None, d, d), lambda i: (w_layer, 0, 0)),
            pl.BlockSpec((None, 1, d), lambda i: (layer, 0, 0)),
            pl.BlockSpec((tm, d), lambda i: (i, 0)),
        ],
        out_specs=out_specs,
        out_shape=out_shape,
        compiler_params=_cparams(("parallel",)),
        name="out_proj",
    )(ya, yb, yc, w, g, x)
    return res if emit_bf16 else res[0]


def _ffn_kernel(x_ref, g1_ref, w1_ref, w2_ref, g2_ref, o_ref, *rest):
    h_scr, acc_scr = rest[-2:]
    j = pl.program_id(1)

    @pl.when(j == 0)
    def _():
        h_scr[...] = _rms(x_ref[...], g1_ref[...]).astype(BF16)
        acc_scr[...] = jnp.zeros_like(acc_scr)

    w1 = w1_ref[...].astype(BF16)
    w2 = w2_ref[...].astype(BF16)
    if len(rest) > 2:
        rest[0][...] = w1
        rest[1][...] = w2
    u = jnp.dot(h_scr[...], w1, preferred_element_type=F32)
    u = jnp.square(jnp.maximum(u, 0.0)).astype(BF16)
    acc_scr[...] += jnp.dot(u, w2, preferred_element_type=F32)

    @pl.when(j == pl.num_programs(1) - 1)
    def _():
        o_ref[...] = x_ref[...] + _rms(acc_scr[...], g2_ref[...])


def _ffn(x, g1, w1, w2, g2, layer, w_layer, emit_bf16=False):
    m, d = x.shape
    f = w1.shape[2]
    tm = min(FFN_ROW_TILE, m)
    tf = FFN_CAST_COL_TILE if emit_bf16 else FFN_COL_TILE
    assert not emit_bf16 or m == tm
    out_specs = [pl.BlockSpec((tm, d), lambda i, j: (i, 0))]
    out_shape = [jax.ShapeDtypeStruct((m, d), F32)]
    if emit_bf16:
        out_specs += [pl.BlockSpec((None, d, tf), lambda i, j: (0, 0, j)),
                      pl.BlockSpec((None, tf, d), lambda i, j: (0, j, 0))]
        out_shape += [jax.ShapeDtypeStruct((1, d, f), BF16), jax.ShapeDtypeStruct((1, f, d), BF16)]
    res = pl.pallas_call(
        _ffn_kernel,
        grid=(m // tm, f // tf),
        in_specs=[
            pl.BlockSpec((tm, d), lambda i, j: (i, 0)),
            pl.BlockSpec((None, 1, d), lambda i, j: (layer, 0, 0)),
            pl.BlockSpec((None, d, tf), lambda i, j: (w_layer, 0, j)),
            pl.BlockSpec((None, tf, d), lambda i, j: (w_layer, j, 0)),
            pl.BlockSpec((None, 1, d), lambda i, j: (layer, 0, 0)),
        ],
        out_specs=out_specs,
        out_shape=out_shape,
        scratch_shapes=[pltpu.VMEM((tm, d), BF16), pltpu.VMEM((tm, d), F32)],
        compiler_params=_cparams(("parallel", "arbitrary")),
        name="ffn",
    )(x, g1, w1, w2, g2)
    return res if emit_bf16 else res[0]


def _rwkv_groups(chunk):
    return 2 if (RWKV_HEADS * chunk) % (2 * MXU_TILE) == 0 else 1


def _rwkv_kernel(p_ref, shift_ref, s0_ref, mu_ref, w0_ref, wup_ref, a0_ref, aup_ref, gup_ref, kk_ref, ka_ref,
                 rk_ref, lg_ref, lb_ref, ones_ref,
                 y_ref, sout_ref, s_scr, prev_scr, *, chunk, t_valid):
    C = chunk
    W = RWKV_WIDTH
    H = RWKV_HEADS
    N = RWKV_HEAD
    nb = p_ref.shape[0]
    c = pl.program_id(1)

    G = _rwkv_groups(C)
    hpg = H // G
    wl = W // G
    gc = hpg * C

    @pl.when(c == 0)
    def _():
        s_scr[...] = jnp.zeros_like(s_scr)
        for h in range(H):
            o = (h % hpg) * N
            s_scr[:, h // hpg, o:o + N, o:o + N] = s0_ref[:, h]
        prev_scr[...] = shift_ref[...]

    inv_n = 1.0 / RWKV_HEAD
    shared = dict(wup=[wup_ref], aup=[aup_ref], gup=[gup_ref],
                  ones=[ones_ref.at[g * wl:(g + 1) * wl, g * wl:(g + 1) * wl] for g in range(G)])

    def hi_lo(x):
        hi = x.astype(BF16).astype(F32)
        return [hi, x - hi]

    row = lax.broadcasted_iota(jnp.int32, (C, 1), 0)

    def cumsum_rows(z):
        dlt = 1
        while dlt < C:
            z = z + jnp.where(row >= dlt, pltpu.roll(z, dlt, axis=0), 0.0)
            dlt *= 2
        return z

    ti = lax.broadcasted_iota(jnp.int32, (C, H * C), 0)
    si = lax.broadcasted_iota(jnp.int32, (C, H * C), 1) % C
    strict = ti > si
    incl = ti >= si
    eye = (ti == si).astype(F32)
    blk_r = lax.broadcasted_iota(jnp.int32, (gc, 1), 0) // C
    mask_ch = blk_r == lax.broadcasted_iota(jnp.int32, (1, wl), 1) // N
    mask_cc = blk_r == lax.broadcasted_iota(jnp.int32, (1, gc), 1) // C
    mask_ss = (lax.broadcasted_iota(jnp.int32, (wl, 1), 0) // N
               == lax.broadcasted_iota(jnp.int32, (1, wl), 1) // N)

    def block_diag(x, mask, width):
        out = []
        for g in range(G):
            tiled = jnp.concatenate([x[:, g * width:(g + 1) * width]] * hpg, axis=0)
            out.append(jnp.where(mask, tiled, 0.0).astype(BF16))
        return out

    def per_head(a_cat, bds):
        a_b = a_cat.astype(BF16)
        return jnp.concatenate([jnp.dot(a_b[:, g * gc:(g + 1) * gc], bds[g], preferred_element_type=F32)
                                for g in range(G)], axis=1)

    def one_sequence(bi):
        p = p_ref[bi]
        shifted = jnp.where(row == 0, prev_scr[bi], pltpu.roll(p, 1, axis=0))
        prev_scr[bi] = p_ref[bi, C - 1:C, :]
        m = p + (shifted - p) * mu_ref[...]
        r = m[:, 0:W]
        k = m[:, W:2 * W]
        v = m[:, 2 * W:3 * W]
        x = m[:, 3 * W:]
        lw, la, gate = yield [("wup", jnp.tanh(x)), ("aup", x), ("gup", jax.nn.sigmoid(x))]
        w = w0_ref[...] + lw
        a = jax.nn.sigmoid(a0_ref[...] + la)
        softplus_neg_w = jnp.maximum(-w, 0.0) + jnp.log(1.0 + jnp.exp(-jnp.abs(w)))
        loga = -jnp.exp(-softplus_neg_w - 0.5)
        kk = k * kk_ref[...]
        k2 = k * (1.0 + (a - 1.0) * ka_ref[...])
        (ss,) = yield [("ones", jnp.concatenate(hi_lo(kk * kk) + hi_lo(r * k2 * rk_ref[...]), axis=0))]
        kk = kk / jnp.maximum(jnp.sqrt(ss[0:C] + ss[C:2 * C]), 1e-12)
        bonus = (ss[2 * C:3 * C] + ss[3 * C:]) * v
        if t_valid < C:
            live = row < t_valid
            loga = jnp.where(live, loga, 0.0)
            kk = jnp.where(live, kk, 0.0)
            k2 = jnp.where(live, k2, 0.0)
        cl = cumsum_rows(loga)
        cl_last = cl[C - 1:C, :]
        e_neg = jnp.exp(-cl)
        e_rem = jnp.exp(cl_last - cl)
        kka = kk * a
        al = -kk * jnp.exp(cl - loga)
        rt = r * jnp.exp(cl)
        be_bd = block_diag(kka * e_neg, mask_ch, wl)
        kt_bd = block_diag(k2 * e_neg, mask_ch, wl)
        bh = kka * e_rem
        kh = k2 * e_rem
        lhs = jnp.concatenate([al, rt], axis=0).astype(BF16)
        lhs_g = [lhs[:, g * wl:(g + 1) * wl] for g in range(G)]
        g_b = jnp.concatenate([_dot_nt(lhs_g[g], be_bd[g]) for g in range(G)], axis=1)
        g_k = jnp.concatenate([_dot_nt(lhs_g[g], kt_bd[g]) for g in range(G)], axis=1)
        n_cat = jnp.where(strict, g_b[0:C], 0.0)
        a_ak = jnp.where(strict, g_k[0:C], 0.0)
        a_rb = jnp.where(incl, g_b[C:], 0.0)
        a_rk = jnp.where(incl, g_k[C:], 0.0)
        s_prev = [s_scr[bi, g] for g in range(G)]
        proj = jnp.concatenate([_dot_nt(lhs_g[g], s_prev[g]) for g in range(G)], axis=1)
        doublings = max(int(math.log2(C)) - 1, 0)
        t_cat = eye + n_cat
        pw = n_cat
        if doublings:
            pw = per_head(n_cat, block_diag(n_cat, mask_cc, gc))
        yield None
        for it in range(doublings):
            pw_bd = block_diag(pw, mask_cc, gc)
            if it < doublings - 1:
                both = per_head(jnp.concatenate([t_cat, pw], axis=0), pw_bd)
                t_cat = t_cat + both[0:C]
                pw = both[C:]
            else:
                t_cat = t_cat + per_head(t_cat, pw_bd)
            yield None
        v_bd = block_diag(v, mask_ch, wl)
        rhs = proj[0:C] + per_head(a_ak, v_bd)
        yield None
        u = per_head(t_cat, block_diag(rhs, mask_ch, wl))
        yield None
        y = proj[C:] + per_head(a_rb, block_diag(u, mask_ch, wl)) + per_head(a_rk, v_bd)
        pad = LANES - 2 * C
        uv = jnp.concatenate([u, v] + ([jnp.zeros((pad, W), F32)] if pad > 0 else []), axis=0).astype(BF16)
        bk = jnp.concatenate([bh, kh] + ([jnp.zeros((pad, W), F32)] if pad > 0 else []), axis=0).astype(BF16)
        decay = jnp.exp(cl_last)
        for g in range(G):
            lanes = slice(g * wl, (g + 1) * wl)
            upd = _dot_tn(uv[:, lanes], bk[:, lanes])
            s_scr[bi, g] = s_prev[g] * decay[:, lanes] + jnp.where(mask_ss, upd, 0.0)

        @pl.when(c == pl.num_programs(1) - 1)
        def _():
            for h in range(H):
                o = (h % hpg) * N
                sout_ref[bi, h] = s_scr[bi, h // hpg, o:o + N, o:o + N]

        (sy,) = yield [("ones", jnp.concatenate(hi_lo(y), axis=0))]
        d = y - (sy[0:C] + sy[C:]) * inv_n
        (sd,) = yield [("ones", jnp.concatenate(hi_lo(d * d), axis=0))]
        var = (sd[0:C] + sd[C:]) * inv_n
        yn = d * lax.rsqrt(var + GN_EPS) * lg_ref[...] + lb_ref[...]
        y_ref[bi] = ((yn + bonus) * gate).astype(y_ref.dtype)

    seqs = [one_sequence(bi) for bi in range(nb)]
    replies = [None] * nb
    while True:
        asks, finished = [], 0
        for seq, reply in zip(seqs, replies):
            try:
                asks.append(seq.send(reply))
            except StopIteration:
                finished += 1
        if finished:
            assert finished == nb
            break
        if asks[0] is None:
            replies = [None] * nb
            continue
        replies = [[] for _ in range(nb)]
        for qi, (name, _) in enumerate(asks[0]):
            lhs = jnp.concatenate([ask[qi][1] for ask in asks], axis=0).astype(BF16)
            parts = shared[name]
            kw = lhs.shape[1] // len(parts)
            z = jnp.concatenate([jnp.dot(lhs[:, g * kw:(g + 1) * kw], part[...], preferred_element_type=F32)
                                 for g, part in enumerate(parts)], axis=1)
            rows = z.shape[0] // nb
            for si in range(nb):
                replies[si].append(z[si * rows:(si + 1) * rows])


def _rwkv(p, shift_prev, s0, prm, batch, t_len, t_valid, chunk):
    W = RWKV_WIDTH
    nc = t_len // chunk
    nb = _largest_divisor(batch, RWKV_SEQS_PER_STEP)
    vec = lambda n: pl.BlockSpec((1, n), lambda b, c: (0, 0))
    mat = lambda r, n: pl.BlockSpec((r, n), lambda b, c: (0, 0))
    state = pl.BlockSpec((nb, RWKV_HEADS, RWKV_HEAD, RWKV_HEAD), lambda b, c: (b, 0, 0, 0))
    groups = _rwkv_groups(chunk)
    kern = functools.partial(_rwkv_kernel, chunk=chunk, t_valid=t_valid)
    return pl.pallas_call(
        kern,
        grid=(batch // nb, nc),
        in_specs=[
            pl.BlockSpec((nb, chunk, RWKV_PROJ), lambda b, c: (b, c, 0)),
            pl.BlockSpec((nb, 1, RWKV_PROJ), lambda b, c: (b, 0, 0)),
            state,
            vec(RWKV_PROJ), vec(W), mat(LORA_WIDTH, W), vec(W), mat(LORA_WIDTH, W), mat(LORA_WIDTH, W),
            vec(W), vec(W), vec(W), vec(W), vec(W), mat(W, W),
        ],
        out_specs=[pl.BlockSpec((nb, chunk, W), lambda b, c: (b, c, 0)), state],
        out_shape=[
            jax.ShapeDtypeStruct((batch, t_len, W), _mix_dtype(chunk)),
            jax.ShapeDtypeStruct((batch, RWKV_HEADS, RWKV_HEAD, RWKV_HEAD), F32),
        ],
        scratch_shapes=[pltpu.VMEM((nb, groups, W // groups, W // groups), F32),
                        pltpu.VMEM((nb, 1, RWKV_PROJ), F32)],
        compiler_params=_cparams(("parallel", "arbitrary")),
        name="rwkv7",
    )(p, shift_prev, s0, prm["mu"], prm["w0"], prm["w_up"], prm["a0"], prm["a_up"], prm["g_up"],
      prm["k_k"], prm["k_a"], prm["r_k"], prm["lnx_g"], prm["lnx_b"], prm["ones_bd"])


def _lru_kernel(x0_ref, x1_ref, g0_ref, g1_ref, conv_ref, h0_ref, cw_ref, cb_ref, wa_ref, ba_ref, wx_ref, bx_ref,
                lam_ref, y_ref, hout_ref, tail_scr, h_scr, *, chunk, t_valid):
    C = chunk
    c = pl.program_id(1)

    @pl.when(c == 0)
    def _():
        tail_scr[...] = jnp.zeros_like(tail_scr)
        tail_scr[SUBLANES - (CONV_W - 1):, :] = conv_ref[...]
        h_scr[...] = h0_ref[...]

    x = jnp.concatenate([x0_ref[...], x1_ref[...]], axis=1)
    gt = jnp.concatenate([g0_ref[...], g1_ref[...]], axis=1)
    ext = jnp.concatenate([tail_scr[...], x], axis=0)
    tail_scr[...] = x[C - SUBLANES:, :]
    xc = cb_ref[...] + x * cw_ref[CONV_W - 1:CONV_W, :]
    for dlt in range(1, CONV_W):
        sh = pltpu.roll(ext, dlt, axis=0)[SUBLANES:, :]
        xc = xc + sh * cw_ref[CONV_W - 1 - dlt:CONV_W - dlt, :]

    gate_r = jax.nn.sigmoid(_dot(xc, wa_ref[...]) + ba_ref[...])
    gate_i = jax.nn.sigmoid(_dot(xc, wx_ref[...]) + bx_ref[...])
    log_a = -RG_C * gate_r * _softplus(-lam_ref[...])
    a = jnp.exp(log_a)
    th = jnp.tanh(log_a)
    u = jnp.sqrt(-2.0 * th / (1.0 - th)) * (gate_i * xc)

    row = lax.broadcasted_iota(jnp.int32, (C, 1), 0)
    dlt = 1
    while dlt < C:
        keep = row >= dlt
        a_sh = jnp.where(keep, pltpu.roll(a, dlt, axis=0), 1.0)
        u_sh = jnp.where(keep, pltpu.roll(u, dlt, axis=0), 0.0)
        u = a * u_sh + u
        a = a * a_sh
        dlt *= 2
    h = a * h_scr[...] + u
    last = min(t_valid, C) - 1
    h_scr[...] = h[last:last + 1, :]
    hout_ref[...] = h[last:last + 1, :]
    y_ref[...] = (h * jax.nn.gelu(gt)).astype(y_ref.dtype)


def _lru(pxq, conv_prev, h0, prm, batch, t_len, t_valid, chunk):
    W = LRU_WIDTH
    nc = t_len // chunk
    half = W // 2
    col0 = RWKV_PROJ // half
    assert RWKV_PROJ % half == 0
    part = lambda k: pl.BlockSpec((chunk, half), lambda b, c: (b * nc + c, col0 + k))
    vec = lambda n: pl.BlockSpec((1, n), lambda b, c: (0, 0))
    kern = functools.partial(_lru_kernel, chunk=chunk, t_valid=t_valid)
    return pl.pallas_call(
        kern,
        grid=(batch, nc),
        in_specs=[
            part(0), part(1), part(2), part(3),
            pl.BlockSpec((None, CONV_W - 1, W), lambda b, c: (b, 0, 0)),
            pl.BlockSpec((None, 1, W), lambda b, c: (b, 0, 0)),
            pl.BlockSpec((CONV_W, W), lambda b, c: (0, 0)),
            vec(W),
            pl.BlockSpec((W, W), lambda b, c: (0, 0)), vec(W),
            pl.BlockSpec((W, W), lambda b, c: (0, 0)), vec(W),
            vec(W),
        ],
        out_specs=[
            pl.BlockSpec((chunk, W), lambda b, c: (b * nc + c, 0)),
            pl.BlockSpec((None, 1, W), lambda b, c: (b, 0, 0)),
        ],
        out_shape=[
            jax.ShapeDtypeStruct((batch * t_len, W), _mix_dtype(chunk)),
            jax.ShapeDtypeStruct((batch, 1, W), F32),
        ],
        scratch_shapes=[pltpu.VMEM((SUBLANES, W), F32), pltpu.VMEM((1, W), F32)],
        compiler_params=_cparams(("parallel", "arbitrary")),
        name="rglru",
    )(pxq, pxq, pxq, pxq, conv_prev, h0, prm["conv_w"], prm["conv_b"], prm["wa_bd"], prm["ba"], prm["wx_bd"],
      prm["bx"], prm["lam"])


def _attn_blocks(q_ref, k_ref, v_ref, bias_ref, og_scr, lse_scr, gi, dil, blocks):
    scale = ATTN_HEAD ** -0.5
    B = ATTN_SPAN

    def rows(start, first):
        if first is True:
            start, n = start, B
        elif first is False:
            start, n = start - B * dil, 2 * B
        else:
            start, n = jnp.maximum(start - B * dil, 0), 2 * B
        return pl.ds(start, n) if dil == 1 else pl.ds(start, n, stride=dil)

    scores = []
    for qs, first in blocks:
        qb = q_ref[rows(qs, True), :]
        if first is True:
            bias = bias_ref[gi, :, B:]
        elif first is False:
            bias = bias_ref[gi]
        else:
            own_then_masked = jnp.concatenate([bias_ref[gi, :, B:], jnp.full((B, B), NEG, F32)], axis=1)
            bias = jnp.where(first, own_then_masked, bias_ref[gi])
        scores.append(_dot_nt(qb, k_ref[rows(qs, first), :]) * scale + bias)
    probs = []
    for s in scores:
        mx = jnp.max(s, axis=-1, keepdims=True)
        pr = jnp.exp(s - mx)
        den = jnp.sum(pr, axis=-1, keepdims=True)
        probs.append(((pr / den).astype(BF16), mx + jnp.log(den)))
    outs = [_dot(pn, v_ref[rows(qs, first), :]) for (qs, first), (pn, _) in zip(blocks, probs)]
    for (qs, _), o, (_, lse) in zip(blocks, outs, probs):
        og_scr[gi, rows(qs, True), :] = o
        lse_scr[gi, rows(qs, True), :] = jnp.broadcast_to(lse, (B, ATTN_HEAD))


def _largest_divisor(n, cap):
    return max(u for u in range(1, cap + 1) if n % u == 0)


def _attn_prompt_kernel(q_ref, kil_ref, vil_ref, bias_ref, o_ref, k_ref, v_ref, og_scr, lse_scr, *, seq):
    B = ATTN_SPAN
    h = pl.program_id(1)
    R = ATTN_MERGE_ROWS

    def gather_head(i, carry):
        dst = pl.ds(pl.multiple_of(i * R, R), R)
        src = pl.ds(h + i * (R * ATTN_HEADS), R, stride=ATTN_HEADS)
        k_ref[dst, :] = kil_ref[src, :]
        v_ref[dst, :] = vil_ref[src, :]
        return carry

    lax.fori_loop(0, seq // R, gather_head, 0)

    for gi, (win, dil) in enumerate(DIL_PAIRS):
        nblk = seq // (dil * B)
        run = functools.partial(_attn_blocks, q_ref, k_ref, v_ref, bias_ref, og_scr, lse_scr, gi, dil)
        if nblk == 1:
            per = _largest_divisor(dil, ATTN_UNROLL)

            def body(i, carry, run=run, per=per):
                run([(i * per + u, True) for u in range(per)])
                return carry

            lax.fori_loop(0, dil // per, body, 0)
        elif dil == 1:
            per = _largest_divisor(nblk, ATTN_UNROLL)

            def body(i, carry, run=run, per=per):
                run([(pl.multiple_of((i * per + u) * B, B), (i == 0) if u == 0 else False) for u in range(per)])
                return carry

            lax.fori_loop(0, nblk // per, body, 0)
        else:
            per = _largest_divisor(dil, max(ATTN_UNROLL // nblk, 1))

            def body(i, carry, run=run, dil=dil, nblk=nblk, per=per):
                run([(i * per + u + blk * (B * dil), blk == 0) for u in range(per) for blk in range(nblk)])
                return carry

            lax.fori_loop(0, dil // per, body, 0)

    def merge(i, carry):
        sl = pl.ds(pl.multiple_of(i * R, R), R)
        ls = [lse_scr[gi, sl, :] for gi in range(len(DIL_PAIRS))]
        mx = functools.reduce(jnp.maximum, ls)
        ws = [jnp.exp(l - mx) for l in ls]
        tot = functools.reduce(lambda a, b: a + b, ws)
        acc = (ws[0] / tot) * og_scr[0, sl, :]
        for gi in range(1, len(DIL_PAIRS)):
            acc = acc + (ws[gi] / tot) * og_scr[gi, sl, :]
        o_ref[sl, :] = acc.astype(o_ref.dtype)
        return carry

    lax.fori_loop(0, seq // R, merge, 0)


def _attn_prompt(pxq, k_all, v_all, bias, layer, batch, seq):
    H = ATTN_HEADS
    G = len(DIL_PAIRS)
    q_col0 = (PXQ_WIDTH - ATTN_WIDTH) // ATTN_HEAD
    kern = functools.partial(_attn_prompt_kernel, seq=seq)
    kv = pl.BlockSpec((seq * H, ATTN_HEAD), lambda b, h: (layer * batch + b, 0))
    return pl.pallas_call(
        kern,
        grid=(batch, H),
        in_specs=[
            pl.BlockSpec((seq, ATTN_HEAD), lambda b, h: (b, q_col0 + h)),
            kv, kv,
            pl.BlockSpec((G, None, ATTN_SPAN, 2 * ATTN_SPAN), lambda b, h: (0, h, 0, 0)),
        ],
        out_specs=pl.BlockSpec((seq, ATTN_HEAD), lambda b, h: (b, h)),
        out_shape=jax.ShapeDtypeStruct((batch * seq, ATTN_WIDTH), BF16),
        scratch_shapes=[pltpu.VMEM((seq, ATTN_HEAD), F32), pltpu.VMEM((seq, ATTN_HEAD), F32),
                        pltpu.VMEM((G, seq, ATTN_HEAD), F32), pltpu.VMEM((G, seq, ATTN_HEAD), F32)],
        compiler_params=_cparams(("parallel", "arbitrary")),
        name="attn_prompt",
    )(pxq, k_all, v_all, bias)


def _attn_sample_kernel(pxq_ref, kn_ref, vn_ref, kc_ref, vc_ref, bias_ref, o_ref, q_scr, kn_scr, vn_scr, *, t_new, win):
    scale = ATTN_HEAD ** -0.5
    H = ATTN_HEADS
    G = len(DIL_PAIRS)
    TP = q_scr.shape[0]
    q_col0 = PXQ_WIDTH - ATTN_WIDTH
    q_scr[...] = jnp.zeros_like(q_scr)
    kn_scr[...] = jnp.zeros_like(kn_scr)
    vn_scr[...] = jnp.zeros_like(vn_scr)
    for h in range(H):
        lanes = slice(h * ATTN_HEAD, (h + 1) * ATTN_HEAD)
        q_scr[0:t_new, :] = pxq_ref[:, q_col0 + h * ATTN_HEAD:q_col0 + (h + 1) * ATTN_HEAD]
        kn_scr[0:t_new, :] = kn_ref[pl.ds(h, t_new, stride=H), :]
        vn_scr[0:t_new, :] = vn_ref[pl.ds(h, t_new, stride=H), :]
        qb = q_scr[...].astype(BF16)
        kc = kc_ref[pl.ds(h, win, stride=H), :]
        vc = vc_ref[pl.ds(h, win, stride=H), :]
        s = jnp.concatenate([_dot_nt(qb, kc), _dot_nt(qb, kn_scr[...])], axis=1) * scale
        prs, lses = [], []
        for gi in range(G):
            sg = s + bias_ref[gi, h]
            mx = jnp.max(sg, axis=-1, keepdims=True)
            pr = jnp.exp(sg - mx)
            den = jnp.sum(pr, axis=-1, keepdims=True)
            prs.append(pr / den)
            lses.append(mx + jnp.log(den))
        pall = jnp.concatenate(prs, axis=0)
        oall = _dot(pall[:, :win], vc) + _dot(pall[:, win:], vn_scr[...])
        mx = functools.reduce(jnp.maximum, lses)
        ws = [jnp.exp(l - mx) for l in lses]
        tot = functools.reduce(lambda a, b: a + b, ws)
        acc = (ws[0] / tot) * oall[0:TP]
        for gi in range(1, G):
            acc = acc + (ws[gi] / tot) * oall[gi * TP:(gi + 1) * TP]
        o_ref[:, lanes] = acc[0:t_new].astype(o_ref.dtype)


def _attn_sample(pxq, k_new, v_new, k_cache, v_cache, bias, layer, batch, t_new):
    H = ATTN_HEADS
    win = k_cache.shape[2] // H
    tp = bias.shape[2]
    kern = functools.partial(_attn_sample_kernel, t_new=t_new, win=win)
    cache = pl.BlockSpec((None, None, win * H, ATTN_HEAD), lambda b: (layer, b, 0, 0))
    new = pl.BlockSpec((None, t_new * H, ATTN_HEAD), lambda b: (layer * batch + b, 0, 0))
    return pl.pallas_call(
        kern,
        grid=(batch,),
        in_specs=[pl.BlockSpec((None, t_new, PXQ_WIDTH), lambda b: (b, 0, 0)), new, new, cache, cache,
                  pl.BlockSpec(bias.shape, lambda b: (0, 0, 0, 0))],
        out_specs=pl.BlockSpec((None, t_new, ATTN_WIDTH), lambda b: (b, 0, 0)),
        out_shape=jax.ShapeDtypeStruct((batch, t_new, ATTN_WIDTH), F32),
        scratch_shapes=[pltpu.VMEM((tp, ATTN_HEAD), F32), pltpu.VMEM((LANES, ATTN_HEAD), F32),
                        pltpu.VMEM((LANES, ATTN_HEAD), F32)],
        compiler_params=_cparams(("parallel",)),
        name="attn_sample",
    )(pxq, k_new, v_new, k_cache, v_cache, bias)


def _bias_kernel(idx_ref, rb_ref, o_ref):
    h = pl.program_id(0)
    idx = idx_ref[...]
    acc = jnp.full(idx.shape, NEG, F32)
    for b in range(N_BUCKETS):
        acc = jnp.where(idx == b, rb_ref[b, h], acc)
    o_ref[...] = acc


def _bias_table(idx, rel_bias):
    g, r, c = idx.shape
    heads = rel_bias.shape[1]
    return pl.pallas_call(
        _bias_kernel,
        grid=(heads,),
        in_specs=[pl.BlockSpec((g, r, c), lambda h: (0, 0, 0)),
                  pl.BlockSpec(memory_space=pltpu.SMEM)],
        out_specs=pl.BlockSpec((g, None, r, c), lambda h: (0, h, 0, 0)),
        out_shape=jax.ShapeDtypeStruct((g, heads, r, c), F32),
        compiler_params=_cparams(("parallel",)),
        name="bias_table",
    )(jnp.asarray(idx, jnp.int32), rel_bias.astype(F32))


def _t5_bucket_static(dist):
    dist = np.asarray(dist, np.int64)
    exact = N_BUCKETS // 2
    d = np.maximum(dist, 1).astype(np.float64)
    large = exact + (np.log(d / exact) / math.log(REL_MAX_DIST / exact) * (N_BUCKETS - exact)).astype(np.int64)
    return np.where(dist < exact, dist, np.minimum(large, N_BUCKETS - 1))


def _prompt_bias_idx():
    B = ATTN_SPAN
    qi = np.arange(B)[:, None]
    kj = np.arange(2 * B)[None, :]
    delta = qi + B - kj
    valid = (delta >= 0) & (delta <= B)
    return np.stack([np.where(valid, _t5_bucket_static(np.clip(delta, 0, B) * dil), -1) for _, dil in DIL_PAIRS])


def _sample_bias_idx(t_new, t_pad, win, total):
    r = np.arange(total)[None, :]
    t = np.arange(t_pad)[:, None]
    dist = win + t - r
    tabs = []
    for _, dil in DIL_PAIRS:
        assert win >= dil * ATTN_SPAN
        valid = (dist >= 0) & (dist % dil == 0) & (dist <= dil * ATTN_SPAN) & (t < t_new)
        tabs.append(np.where(valid, _t5_bucket_static(np.clip(dist, 0, dil * ATTN_SPAN)), -1))
    return np.stack(tabs)


def _block_diag(blocks):
    n, c, d = blocks.shape
    eye = jnp.eye(n, dtype=blocks.dtype)
    return jnp.einsum("ncd,nm->ncmd", blocks, eye).reshape(n * c, n * d)


def _layer_params(l, a):
    W = RWKV_WIDTH
    row = lambda t: t[l].reshape(1, -1).astype(F32)

    def lora_pad(w, off):
        z = jnp.zeros((LORA_WIDTH, W), F32)
        return z.at[off:off + w.shape[0]].set(w).astype(BF16)

    head_of = np.arange(W) // RWKV_HEAD
    ones_bd = jnp.asarray((head_of[:, None] == head_of[None, :]).astype(np.float32), BF16)
    rwkv = dict(
        mu=row(a["rwkv_mu"]), w0=row(a["rwkv_w0"]), a0=row(a["rwkv_a0"]),
        w_up=lora_pad(a["rwkv_w_up"][l], 0),
        a_up=lora_pad(a["rwkv_a_up"][l], DECAY_LORA),
        g_up=lora_pad(a["rwkv_g_up"][l], DECAY_LORA + ICLR_LORA),
        k_k=row(a["rwkv_k_k"]), k_a=row(a["rwkv_k_a"]), r_k=row(a["rwkv_r_k"]),
        lnx_g=row(a["rwkv_lnx_g"]), lnx_b=row(a["rwkv_lnx_b"]),
        ones_bd=ones_bd,
    )
    lru = dict(
        conv_w=a["lru_conv_w"][l].astype(F32), conv_b=row(a["lru_conv_b"]),
        wa_bd=_block_diag(a["lru_wa"][l]).astype(BF16), ba=row(a["lru_ba"]),
        wx_bd=_block_diag(a["lru_wx"][l]).astype(BF16), bx=row(a["lru_bx"]),
        lam=row(a["lru_lambda"]),
    )
    return dict(rwkv=rwkv, lru=lru)


def _shared_params(a):
    gain = lambda t: t.reshape(t.shape[0], 1, -1).astype(F32)
    return dict(
        g_mix_pre=gain(a["norm_mix_pre"]), g_mix_post=gain(a["norm_mix_post"]),
        g_ffn_pre=gain(a["norm_ffn_pre"]), g_ffn_post=gain(a["norm_ffn_post"]),
    )


def _layer(x, layer, sp, wts, lp, batch, t_len, shift_prev, s0, conv_prev, h0, kv_all, attend):
    emit = wts is None
    src = sp["f32"] if emit else wts
    wl = layer if emit else 0
    copies = {}
    pxq, k_all, v_all, *w_copy = _in_proj(x, sp["g_mix_pre"], src["w_in"], layer, wl, *kv_all, emit_bf16=emit)
    copies["w_in"] = w_copy[0] if emit else None
    pxq3 = pxq.reshape(batch, t_len, PXQ_WIDTH)
    if t_len % RWKV_CHUNK == 0:
        y_a, s_fin = _rwkv(pxq3, shift_prev, s0, lp["rwkv"], batch, t_len, t_len, RWKV_CHUNK)
        y_a = y_a.reshape(batch * t_len, -1)
        y_b, h_fin = _lru(pxq, conv_prev, h0, lp["lru"], batch, t_len, t_len, LRU_CHUNK)
    else:
        assert t_len <= SUBLANES
        padded = jnp.pad(pxq3, ((0, 0), (0, SUBLANES - t_len), (0, 0)))
        y_a, s_fin = _rwkv(padded, shift_prev, s0, lp["rwkv"], batch, SUBLANES, t_len, SUBLANES)
        y_a = y_a[:, :t_len].reshape(batch * t_len, -1)
        y_b, h_fin = _lru(padded.reshape(batch * SUBLANES, PXQ_WIDTH), conv_prev, h0, lp["lru"], batch, SUBLANES,
                          t_len, SUBLANES)
        y_b = y_b.reshape(batch, SUBLANES, -1)[:, :t_len].reshape(batch * t_len, -1)
    y_c = attend(pxq, k_all, v_all)

    x = _out_proj(y_a, y_b, y_c, src["w_out"], sp["g_mix_post"], x, layer, wl, emit_bf16=emit)
    if emit:
        x, copies["w_out"] = x
    x = _ffn(x, sp["g_ffn_pre"], src["w1"], src["w2"], sp["g_ffn_post"], layer, wl, emit_bf16=emit)
    if emit:
        x, copies["w1"], copies["w2"] = x

    lru_x = pxq3[:, :, RWKV_PROJ:RWKV_PROJ + LRU_WIDTH]
    if t_len >= CONV_W - 1:
        conv_new = lru_x[:, t_len - (CONV_W - 1):]
    else:
        conv_new = jnp.concatenate([conv_prev, lru_x], axis=1)[:, -(CONV_W - 1):]
    state = (pxq3[:, -1, :RWKV_PROJ], s_fin, conv_new, h_fin.reshape(batch, LRU_WIDTH))
    return x, state, (k_all, v_all), copies


def kernel(x_prompt, x_sample, state_rwkv_wkv, state_rwkv_shift, state_lru_h, state_lru_conv, cache_attn_k, cache_attn_v, rel_bias, norm_mix_pre, norm_mix_post, norm_ffn_pre, norm_ffn_post, w_in, w_out, rwkv_mu, rwkv_w0, rwkv_w_up, rwkv_a0, rwkv_a_up, rwkv_g_up, rwkv_k_k, rwkv_k_a, rwkv_r_k, rwkv_lnx_g, rwkv_lnx_b, lru_conv_w, lru_conv_b, lru_wa, lru_ba, lru_wx, lru_bx, lru_lambda, ffn_w1, ffn_w2):
    a = dict(norm_mix_pre=norm_mix_pre, norm_mix_post=norm_mix_post, norm_ffn_pre=norm_ffn_pre,
             norm_ffn_post=norm_ffn_post, w_in=w_in, w_out=w_out, rwkv_mu=rwkv_mu, rwkv_w0=rwkv_w0,
             rwkv_w_up=rwkv_w_up, rwkv_a0=rwkv_a0, rwkv_a_up=rwkv_a_up, rwkv_g_up=rwkv_g_up, rwkv_k_k=rwkv_k_k,
             rwkv_k_a=rwkv_k_a, rwkv_r_k=rwkv_r_k, rwkv_lnx_g=rwkv_lnx_g, rwkv_lnx_b=rwkv_lnx_b,
             lru_conv_w=lru_conv_w, lru_conv_b=lru_conv_b, lru_wa=lru_wa, lru_ba=lru_ba, lru_wx=lru_wx,
             lru_bx=lru_bx, lru_lambda=lru_lambda, ffn_w1=ffn_w1, ffn_w2=ffn_w2)
    depth = w_in.shape[0]
    pb, seq, _ = x_prompt.shape
    sb, t_new, _ = x_sample.shape
    win = cache_attn_k.shape[2]
    keep = min(ATTN_WINDOW, seq)
    total = win + LANES
    assert t_new <= LANES and seq % (DIL_PAIRS[-1][1] * ATTN_SPAN) == 0

    bias_p = _bias_table(_prompt_bias_idx(), rel_bias)
    bias_s = _bias_table(_sample_bias_idx(t_new, SUBLANES, win, total), rel_bias)
    k_cache = cache_attn_k.reshape(depth, sb, win * ATTN_HEADS, ATTN_HEAD)
    v_cache = cache_attn_v.reshape(depth, sb, win * ATTN_HEADS, ATTN_HEAD)

    xp = x_prompt.reshape(pb * seq, D_MODEL)
    xs = x_sample.reshape(sb * t_new, D_MODEL)
    new_p, new_s = [], []
    sp = _shared_params(a)
    kv_p = tuple(jnp.zeros((depth * pb * seq * ATTN_HEADS, ATTN_HEAD), F32) for _ in range(2))
    kv_s = tuple(jnp.zeros((depth * sb * t_new * ATTN_HEADS, ATTN_HEAD), F32) for _ in range(2))
    sp["f32"] = dict(w_in=w_in.astype(F32), w_out=w_out.astype(F32), w1=ffn_w1.astype(F32), w2=ffn_w2.astype(F32))
    for l in range(depth):
        lp = _layer_params(l, a)
        attend_s = lambda pxq, k_all, v_all, l=l: _attn_sample(
            pxq.reshape(sb, t_new, PXQ_WIDTH), k_all.reshape(depth * sb, t_new * ATTN_HEADS, ATTN_HEAD),
            v_all.reshape(depth * sb, t_new * ATTN_HEADS, ATTN_HEAD), k_cache, v_cache, bias_s, l, sb, t_new,
        ).reshape(sb * t_new, ATTN_WIDTH)
        xs, st_s, kv_s, wts = _layer(xs, l, sp, None, lp, sb, t_new,
                                     state_rwkv_shift[l].reshape(sb, 1, RWKV_PROJ), state_rwkv_wkv[l].astype(F32),
                                     state_lru_conv[l], state_lru_h[l].reshape(sb, 1, LRU_WIDTH), kv_s, attend_s)
        attend_p = lambda pxq, k_all, v_all, l=l: _attn_prompt(pxq, k_all, v_all, bias_p, l, pb, seq)
        xp, st_p, kv_p, _ = _layer(xp, l, sp, wts, lp, pb, seq,
                                   jnp.zeros((pb, 1, RWKV_PROJ), F32),
                                   jnp.zeros((pb, RWKV_HEADS, RWKV_HEAD, RWKV_HEAD), F32),
                                   jnp.zeros((pb, CONV_W - 1, LRU_WIDTH), F32), jnp.zeros((pb, 1, LRU_WIDTH), F32),
                                   kv_p, attend_p)
        new_p.append(st_p)
        new_s.append(st_s)

    stack = lambda sts, i: jnp.stack([s[i] for s in sts])
    k_p, v_p = (t.reshape(depth, pb, seq, ATTN_HEADS, ATTN_HEAD)[:, :, seq - keep:] for t in kv_p)
    k_s, v_s = (t.reshape(depth, sb, t_new, ATTN_HEADS, ATTN_HEAD) for t in kv_s)
    return (xp.reshape(pb, seq, D_MODEL), xs.reshape(sb, t_new, D_MODEL),
            stack(new_p, 1), stack(new_s, 1), stack(new_p, 0), stack(new_s, 0),
            stack(new_p, 3), stack(new_s, 3), stack(new_p, 2), stack(new_s, 2),
            k_p, k_s, v_p, v_s)
```

```python
import functools
import math

import numpy as np
import jax
import jax.numpy as jnp
from jax import lax
from jax.experimental import pallas as pl
from jax.experimental.pallas import tpu as pltpu

F32 = jnp.float32
BF16 = jnp.bfloat16

D_MODEL = 2048
RWKV_WIDTH = 512
RWKV_HEAD = 64
RWKV_HEADS = RWKV_WIDTH // RWKV_HEAD
DECAY_LORA = 64
ICLR_LORA = 64
GATE_LORA = 128
LORA_WIDTH = DECAY_LORA + ICLR_LORA + GATE_LORA
RWKV_PROJ = 3 * RWKV_WIDTH + LORA_WIDTH
GN_EPS = 64e-5
LRU_WIDTH = 512
LRU_BLOCKS = 8
LRU_BLOCK = LRU_WIDTH // LRU_BLOCKS
CONV_W = 4
RG_C = 8.0
ATTN_WIDTH = 1024
ATTN_HEAD = 128
ATTN_HEADS = ATTN_WIDTH // ATTN_HEAD
DIL_PAIRS = ((128, 1), (512, 4), (2048, 16))
ATTN_SPAN = 128
ATTN_WINDOW = 2048
N_BUCKETS = 32
REL_MAX_DIST = ATTN_WINDOW
D_FF = 4 * D_MODEL
RMS_EPS = 1e-6
NEG = -1e30

LANES = 128
SUBLANES = 8
MXU_TILE = 256
VMEM_LIMIT_BYTES = 56 * 1024 * 1024

ROW_TILE = 1024
IN_COL_TILE = 256
IN_FUSED_TILES = 2
OUT_ROW_TILE = 512
FFN_ROW_TILE = 512
FFN_COL_TILE = 1024
FFN_CAST_COL_TILE = 512
RWKV_CHUNK = 64
RWKV_SEQS_PER_STEP = 4
LRU_CHUNK = 256
ATTN_MERGE_ROWS = 256
ATTN_UNROLL = 8


def _cparams(sem):
    return pltpu.CompilerParams(dimension_semantics=sem, vmem_limit_bytes=VMEM_LIMIT_BYTES)


def _dot(a, b):
    return jnp.dot(a.astype(BF16), b.astype(BF16), preferred_element_type=F32)


def _dot_nt(a, b):
    return lax.dot_general(a.astype(BF16), b.astype(BF16), (((1,), (1,)), ((), ())), preferred_element_type=F32)


def _dot_tn(a, b):
    return lax.dot_general(a.astype(BF16), b.astype(BF16), (((0,), (0,)), ((), ())), preferred_element_type=F32)


def _softplus(z):
    return jnp.maximum(z, 0.0) + jnp.log1p(jnp.exp(-jnp.abs(z)))


def _mix_dtype(rows):
    return BF16 if rows % (2 * SUBLANES) == 0 else F32


def _rms(x, g):
    ms = jnp.mean(x * x, axis=-1, keepdims=True)
    return x * lax.rsqrt(ms + RMS_EPS) * g


PXQ_WIDTH = RWKV_PROJ + 2 * LRU_WIDTH + ATTN_WIDTH
PXQ_TILES = PXQ_WIDTH // IN_COL_TILE
KV_TILES = ATTN_WIDTH // IN_COL_TILE
HEADS_PER_TILE = IN_COL_TILE // ATTN_HEAD


def _in_proj_kernel(x_ref, g_ref, *refs, fuse):
    w_refs = refs[:fuse]
    pxq_ref, k_ref, v_ref = refs[fuse + 2:fuse + 5]
    rest = refs[fuse + 5:]
    h_scr = rest[-1]
    j = pl.program_id(1)
    tm = x_ref.shape[0]
    pxq_steps = pl.cdiv(PXQ_TILES, fuse)
    kv_steps = KV_TILES // fuse

    @pl.when(j == 0)
    def _():
        h_scr[...] = _rms(x_ref[...], g_ref[...]).astype(BF16)

    ws = [w_ref[...].astype(BF16) for w_ref in w_refs]
    if len(rest) > 1:
        rest[0][...] = ws[0]
    w = ws[0] if fuse == 1 else jnp.concatenate(ws, axis=1)
    acc = jnp.dot(h_scr[...], w, preferred_element_type=F32)

    @pl.when(j < pxq_steps)
    def _():
        pxq_ref[...] = acc

    def scatter_heads(o_ref, step):
        for hh in range(fuse * HEADS_PER_TILE):
            head = step * (fuse * HEADS_PER_TILE) + hh
            o_ref[pl.ds(head, tm, stride=ATTN_HEADS), :] = acc[:, hh * ATTN_HEAD:(hh + 1) * ATTN_HEAD]

    @pl.when((j >= pxq_steps) & (j < pxq_steps + kv_steps))
    def _():
        scatter_heads(k_ref, j - pxq_steps)

    @pl.when(j >= pxq_steps + kv_steps)
    def _():
        scatter_heads(v_ref, j - pxq_steps - kv_steps)


def _in_proj(x, g, w, layer, w_layer, k_all, v_all, emit_bf16=False):
    m, d = x.shape
    n = w.shape[2]
    assert n == PXQ_WIDTH + 2 * ATTN_WIDTH
    tm = min(ROW_TILE, m)
    row_tiles = m // tm
    assert not emit_bf16 or row_tiles == 1
    fuse = 1 if emit_bf16 else IN_FUSED_TILES
    assert KV_TILES % fuse == 0
    pxq_steps = pl.cdiv(PXQ_TILES, fuse)
    spare = pxq_steps * fuse - PXQ_TILES
    width = fuse * IN_COL_TILE

    def w_spec(slot):
        def index(i, j):
            tile = jnp.where(j < pxq_steps, jnp.minimum(fuse * j + slot, PXQ_TILES - 1), fuse * j + slot - spare)
            return (w_layer, 0, tile)
        return pl.BlockSpec((None, d, IN_COL_TILE), index)

    kv_spec = pl.BlockSpec((tm * ATTN_HEADS, ATTN_HEAD), lambda i, j: (layer * row_tiles + i, 0))
    out_specs = [
        pl.BlockSpec((tm, width), lambda i, j: (i, jnp.minimum(j, pxq_steps - 1))),
        kv_spec, kv_spec,
    ]
    out_shape = [
        jax.ShapeDtypeStruct((m, pxq_steps * width), F32),
        jax.ShapeDtypeStruct(k_all.shape, F32),
        jax.ShapeDtypeStruct(v_all.shape, F32),
    ]
    if emit_bf16:
        out_specs.append(pl.BlockSpec((None, d, IN_COL_TILE), lambda i, j: (0, 0, j)))
        out_shape.append(jax.ShapeDtypeStruct((1, d, n), BF16))
    return pl.pallas_call(
        functools.partial(_in_proj_kernel, fuse=fuse),
        grid=(row_tiles, pxq_steps + 2 * (KV_TILES // fuse)),
        in_specs=[
            pl.BlockSpec((tm, d), lambda i, j: (i, 0)),
            pl.BlockSpec((None, 1, d), lambda i, j: (layer, 0, 0)),
            *[w_spec(slot) for slot in range(fuse)],
            pl.BlockSpec(memory_space=pl.ANY),
            pl.BlockSpec(memory_space=pl.ANY),
        ],
        out_specs=out_specs,
        out_shape=out_shape,
        input_output_aliases={2 + fuse: 1, 3 + fuse: 2},
        scratch_shapes=[pltpu.VMEM((tm, d), BF16)],
        compiler_params=_cparams(("parallel", "arbitrary")),
        name="in_proj",
    )(x, g, *([w] * fuse), k_all, v_all)


def _out_proj_kernel(ya_ref, yb_ref, yc_ref, w_ref, g_ref, x_ref, o_ref, *w_copy):
    c1, c2 = RWKV_WIDTH, RWKV_WIDTH + LRU_WIDTH
    parts = [(ya_ref, slice(0, c1)), (yb_ref, slice(c1, c2)), (yc_ref, slice(c2, None))]
    acc = None
    for y_ref, rows in parts:
        w = w_ref[rows, :].astype(BF16)
        if w_copy:
            w_copy[0][rows, :] = w
        t = jnp.dot(y_ref[...].astype(BF16), w, preferred_element_type=F32)
        acc = t if acc is None else acc + t
    o_ref[...] = x_ref[...] + _rms(acc, g_ref[...])


def _out_proj(ya, yb, yc, w, g, x, layer, w_layer, emit_bf16=False):
    m, d = x.shape
    tm = min(OUT_ROW_TILE, m)
    assert not emit_bf16 or m == tm
    out_specs = [pl.BlockSpec((tm, d), lambda i: (i, 0))]
    out_shape = [jax.ShapeDtypeStruct((m, d), F32)]
    if emit_bf16:
        out_specs.append(pl.BlockSpec((None, d, d), lambda i: (0, 0, 0)))
        out_shape.append(jax.ShapeDtypeStruct((1, d, d), BF16))
    res = pl.pallas_call(
        _out_proj_kernel,
        grid=(m // tm,),
        in_specs=[
            pl.BlockSpec((tm, RWKV_WIDTH), lambda i: (i, 0)),
            pl.BlockSpec((tm, LRU_WIDTH), lambda i: (i, 0)),
            pl.BlockSpec((tm, ATTN_WIDTH), lambda i: (i, 0)),
            pl.BlockSpec((None, d, d), lambda i: (w_layer, 0, 0)),
            pl.BlockSpec((None, 1, d), lambda i: (layer, 0, 0)),
            pl.BlockSpec((tm, d), lambda i: (i, 0)),
        ],
        out_specs=out_specs,
        out_shape=out_shape,
        compiler_params=_cparams(("parallel",)),
        name="out_proj",
    )(ya, yb, yc, w, g, x)
    return res if emit_bf16 else res[0]


def _ffn_kernel(x_ref, g1_ref, w1_ref, w2_ref, g2_ref, o_ref, *rest):
    h_scr, acc_scr = rest[-2:]
    j = pl.program_id(1)

    @pl.when(j == 0)
    def _():
        h_scr[...] = _rms(x_ref[...], g1_ref[...]).astype(BF16)
        acc_scr[...] = jnp.zeros_like(acc_scr)

    w1 = w1_ref[...].astype(BF16)
    w2 = w2_ref[...].astype(BF16)
    if len(rest) > 2:
        rest[0][...] = w1
        rest[1][...] = w2
    u = jnp.dot(h_scr[...], w1, preferred_element_type=F32)
    u = jnp.square(jnp.maximum(u, 0.0)).astype(BF16)
    acc_scr[...] += jnp.dot(u, w2, preferred_element_type=F32)

    @pl.when(j == pl.num_programs(1) - 1)
    def _():
        o_ref[...] = x_ref[...] + _rms(acc_scr[...], g2_ref[...])


def _ffn(x, g1, w1, w2, g2, layer, w_layer, emit_bf16=False):
    m, d = x.shape
    f = w1.shape[2]
    tm = min(FFN_ROW_TILE, m)
    tf = FFN_CAST_COL_TILE if emit_bf16 else FFN_COL_TILE
    assert not emit_bf16 or m == tm
    out_specs = [pl.BlockSpec((tm, d), lambda i, j: (i, 0))]
    out_shape = [jax.ShapeDtypeStruct((m, d), F32)]
    if emit_bf16:
        out_specs += [pl.BlockSpec((None, d, tf), lambda i, j: (0, 0, j)),
                      pl.BlockSpec((None, tf, d), lambda i, j: (0, j, 0))]
        out_shape += [jax.ShapeDtypeStruct((1, d, f), BF16), jax.ShapeDtypeStruct((1, f, d), BF16)]
    res = pl.pallas_call(
        _ffn_kernel,
        grid=(m // tm, f // tf),
        in_specs=[
            pl.BlockSpec((tm, d), lambda i, j: (i, 0)),
            pl.BlockSpec((None, 1, d), lambda i, j: (layer, 0, 0)),
            pl.BlockSpec((None, d, tf), lambda i, j: (w_layer, 0, j)),
            pl.BlockSpec((None, tf, d), lambda i, j: (w_layer, j, 0)),
            pl.BlockSpec((None, 1, d), lambda i, j: (layer, 0, 0)),
        ],
        out_specs=out_specs,
        out_shape=out_shape,
        scratch_shapes=[pltpu.VMEM((tm, d), BF16), pltpu.VMEM((tm, d), F32)],
        compiler_params=_cparams(("parallel", "arbitrary")),
        name="ffn",
    )(x, g1, w1, w2, g2)
    return res if emit_bf16 else res[0]


def _rwkv_groups(chunk):
    return 2 if (RWKV_HEADS * chunk) % (2 * MXU_TILE) == 0 else 1


def _rwkv_kernel(p_ref, shift_ref, s0_ref, mu_ref, w0_ref, wup_ref, a0_ref, aup_ref, gup_ref, kk_ref, ka_ref,
                 rk_ref, lg_ref, lb_ref, ones_ref,
                 y_ref, sout_ref, s_scr, prev_scr, *, chunk, t_valid):
    C = chunk
    W = RWKV_WIDTH
    H = RWKV_HEADS
    N = RWKV_HEAD
    nb = p_ref.shape[0]
    c = pl.program_id(1)

    G = _rwkv_groups(C)
    hpg = H // G
    wl = W // G
    gc = hpg * C

    @pl.when(c == 0)
    def _():
        s_scr[...] = jnp.zeros_like(s_scr)
        for h in range(H):
            o = (h % hpg) * N
            s_scr[:, h // hpg, o:o + N, o:o + N] = s0_ref[:, h]
        prev_scr[...] = shift_ref[...]

    inv_n = 1.0 / RWKV_HEAD
    shared = dict(wup=[wup_ref], aup=[aup_ref], gup=[gup_ref],
                  ones=[ones_ref.at[g * wl:(g + 1) * wl, g * wl:(g + 1) * wl] for g in range(G)])

    def hi_lo(x):
        hi = x.astype(BF16).astype(F32)
        return [hi, x - hi]

    row = lax.broadcasted_iota(jnp.int32, (C, 1), 0)

    def cumsum_rows(z):
        dlt = 1
        while dlt < C:
            z = z + jnp.where(row >= dlt, pltpu.roll(z, dlt, axis=0), 0.0)
            dlt *= 2
        return z

    ti = lax.broadcasted_iota(jnp.int32, (C, H * C), 0)
    si = lax.broadcasted_iota(jnp.int32, (C, H * C), 1) % C
    strict = ti > si
    incl = ti >= si
    eye = (ti == si).astype(F32)
    blk_r = lax.broadcasted_iota(jnp.int32, (gc, 1), 0) // C
    mask_ch = blk_r == lax.broadcasted_iota(jnp.int32, (1, wl), 1) // N
    mask_cc = blk_r == lax.broadcasted_iota(jnp.int32, (1, gc), 1) // C
    mask_ss = (lax.broadcasted_iota(jnp.int32, (wl, 1), 0) // N
               == lax.broadcasted_iota(jnp.int32, (1, wl), 1) // N)

    def block_diag(x, mask, width):
        out = []
        for g in range(G):
            tiled = jnp.concatenate([x[:, g * width:(g + 1) * width]] * hpg, axis=0)
            out.append(jnp.where(mask, tiled, 0.0).astype(BF16))
        return out

    def per_head(a_cat, bds):
        a_b = a_cat.astype(BF16)
        return jnp.concatenate([jnp.dot(a_b[:, g * gc:(g + 1) * gc], bds[g], preferred_element_type=F32)
                                for g in range(G)], axis=1)

    def one_sequence(bi):
        p = p_ref[bi]
        shifted = jnp.where(row == 0, prev_scr[bi], pltpu.roll(p, 1, axis=0))
        prev_scr[bi] = p_ref[bi, C - 1:C, :]
        m = p + (shifted - p) * mu_ref[...]
        r = m[:, 0:W]
        k = m[:, W:2 * W]
        v = m[:, 2 * W:3 * W]
        x = m[:, 3 * W:]
        lw, la, gate = yield [("wup", jnp.tanh(x)), ("aup", x), ("gup", jax.nn.sigmoid(x))]
        w = w0_ref[...] + lw
        a = jax.nn.sigmoid(a0_ref[...] + la)
        softplus_neg_w = jnp.maximum(-w, 0.0) + jnp.log(1.0 + jnp.exp(-jnp.abs(w)))
        loga = -jnp.exp(-softplus_neg_w - 0.5)
        kk = k * kk_ref[...]
        k2 = k * (1.0 + (a - 1.0) * ka_ref[...])
        (ss,) = yield [("ones", jnp.concatenate(hi_lo(kk * kk) + hi_lo(r * k2 * rk_ref[...]), axis=0))]
        kk = kk / jnp.maximum(jnp.sqrt(ss[0:C] + ss[C:2 * C]), 1e-12)
        bonus = (ss[2 * C:3 * C] + ss[3 * C:]) * v
        if t_valid < C:
            live = row < t_valid
            loga = jnp.where(live, loga, 0.0)
            kk = jnp.where(live, kk, 0.0)
            k2 = jnp.where(live, k2, 0.0)
        cl = cumsum_rows(loga)
        cl_last = cl[C - 1:C, :]
        e_neg = jnp.exp(-cl)
        e_rem = jnp.exp(cl_last - cl)
        kka = kk * a
        al = -kk * jnp.exp(cl - loga)
        rt = r * jnp.exp(cl)
        be_bd = block_diag(kka * e_neg, mask_ch, wl)
        kt_bd = block_diag(k2 * e_neg, mask_ch, wl)
        bh = kka * e_rem
        kh = k2 * e_rem
        lhs = jnp.concatenate([al, rt], axis=0).astype(BF16)
        lhs_g = [lhs[:, g * wl:(g + 1) * wl] for g in range(G)]
        g_b = jnp.concatenate([_dot_nt(lhs_g[g], be_bd[g]) for g in range(G)], axis=1)
        g_k = jnp.concatenate([_dot_nt(lhs_g[g], kt_bd[g]) for g in range(G)], axis=1)
        n_cat = jnp.where(strict, g_b[0:C], 0.0)
        a_ak = jnp.where(strict, g_k[0:C], 0.0)
        a_rb = jnp.where(incl, g_b[C:], 0.0)
        a_rk = jnp.where(incl, g_k[C:], 0.0)
        s_prev = [s_scr[bi, g] for g in range(G)]
        proj = jnp.concatenate([_dot_nt(lhs_g[g], s_prev[g]) for g in range(G)], axis=1)
        doublings = max(int(math.log2(C)) - 1, 0)
        t_cat = eye + n_cat
        pw = n_cat
        if doublings:
            pw = per_head(n_cat, block_diag(n_cat, mask_cc, gc))
        yield None
        for it in range(doublings):
            pw_bd = block_diag(pw, mask_cc, gc)
            if it < doublings - 1:
                both = per_head(jnp.concatenate([t_cat, pw], axis=0), pw_bd)
                t_cat = t_cat + both[0:C]
                pw = both[C:]
            else:
                t_cat = t_cat + per_head(t_cat, pw_bd)
            yield None
        v_bd = block_diag(v, mask_ch, wl)
        rhs = proj[0:C] + per_head(a_ak, v_bd)
        yield None
        u = per_head(t_cat, block_diag(rhs, mask_ch, wl))
        yield None
        y = proj[C:] + per_head(a_rb, block_diag(u, mask_ch, wl)) + per_head(a_rk, v_bd)
        pad = LANES - 2 * C
        uv = jnp.concatenate([u, v] + ([jnp.zeros((pad, W), F32)] if pad > 0 else []), axis=0).astype(BF16)
        bk = jnp.concatenate([bh, kh] + ([jnp.zeros((pad, W), F32)] if pad > 0 else []), axis=0).astype(BF16)
        decay = jnp.exp(cl_last)
        for g in range(G):
            lanes = slice(g * wl, (g + 1) * wl)
            upd = _dot_tn(uv[:, lanes], bk[:, lanes])
            s_scr[bi, g] = s_prev[g] * decay[:, lanes] + jnp.where(mask_ss, upd, 0.0)

        @pl.when(c == pl.num_programs(1) - 1)
        def _():
            for h in range(H):
                o = (h % hpg) * N
                sout_ref[bi, h] = s_scr[bi, h // hpg, o:o + N, o:o + N]

        (sy,) = yield [("ones", jnp.concatenate(hi_lo(y), axis=0))]
        d = y - (sy[0:C] + sy[C:]) * inv_n
        (sd,) = yield [("ones", jnp.concatenate(hi_lo(d * d), axis=0))]
        var = (sd[0:C] + sd[C:]) * inv_n
        yn = d * lax.rsqrt(var + GN_EPS) * lg_ref[...] + lb_ref[...]
        y_ref[bi] = ((yn + bonus) * gate).astype(y_ref.dtype)

    seqs = [one_sequence(bi) for bi in range(nb)]
    replies = [None] * nb
    while True:
        asks, finished = [], 0
        for seq, reply in zip(seqs, replies):
            try:
                asks.append(seq.send(reply))
            except StopIteration:
                finished += 1
        if finished:
            assert finished == nb
            break
        if asks[0] is None:
            replies = [None] * nb
            continue
        replies = [[] for _ in range(nb)]
        for qi, (name, _) in enumerate(asks[0]):
            lhs = jnp.concatenate([ask[qi][1] for ask in asks], axis=0).astype(BF16)
            parts = shared[name]
            kw = lhs.shape[1] // len(parts)
            z = jnp.concatenate([jnp.dot(lhs[:, g * kw:(g + 1) * kw], part[...], preferred_element_type=F32)
                                 for g, part in enumerate(parts)], axis=1)
            rows = z.shape[0] // nb
            for si in range(nb):
                replies[si].append(z[si * rows:(si + 1) * rows])


def _rwkv(p, shift_prev, s0, prm, batch, t_len, t_valid, chunk):
    W = RWKV_WIDTH
    nc = t_len // chunk
    nb = _largest_divisor(batch, RWKV_SEQS_PER_STEP)
    vec = lambda n: pl.BlockSpec((1, n), lambda b, c: (0, 0))
    mat = lambda r, n: pl.BlockSpec((r, n), lambda b, c: (0, 0))
    state = pl.BlockSpec((nb, RWKV_HEADS, RWKV_HEAD, RWKV_HEAD), lambda b, c: (b, 0, 0, 0))
    groups = _rwkv_groups(chunk)
    kern = functools.partial(_rwkv_kernel, chunk=chunk, t_valid=t_valid)
    return pl.pallas_call(
        kern,
        grid=(batch // nb, nc),
        in_specs=[
            pl.BlockSpec((nb, chunk, RWKV_PROJ), lambda b, c: (b, c, 0)),
            pl.BlockSpec((nb, 1, RWKV_PROJ), lambda b, c: (b, 0, 0)),
            state,
            vec(RWKV_PROJ), vec(W), mat(LORA_WIDTH, W), vec(W), mat(LORA_WIDTH, W), mat(LORA_WIDTH, W),
            vec(W), vec(W), vec(W), vec(W), vec(W), mat(W, W),
        ],
        out_specs=[pl.BlockSpec((nb, chunk, W), lambda b, c: (b, c, 0)), state],
        out_shape=[
            jax.ShapeDtypeStruct((batch, t_len, W), _mix_dtype(chunk)),
            jax.ShapeDtypeStruct((batch, RWKV_HEADS, RWKV_HEAD, RWKV_HEAD), F32),
        ],
        scratch_shapes=[pltpu.VMEM((nb, groups, W // groups, W // groups), F32),
                        pltpu.VMEM((nb, 1, RWKV_PROJ), F32)],
        compiler_params=_cparams(("parallel", "arbitrary")),
        name="rwkv7",
    )(p, shift_prev, s0, prm["mu"], prm["w0"], prm["w_up"], prm["a0"], prm["a_up"], prm["g_up"],
      prm["k_k"], prm["k_a"], prm["r_k"], prm["lnx_g"], prm["lnx_b"], prm["ones_bd"])


def _lru_kernel(x0_ref, x1_ref, g0_ref, g1_ref, conv_ref, h0_ref, cw_ref, cb_ref, wa_ref, ba_ref, wx_ref, bx_ref,
                lam_ref, y_ref, hout_ref, tail_scr, h_scr, *, chunk, t_valid):
    C = chunk
    c = pl.program_id(1)

    @pl.when(c == 0)
    def _():
        tail_scr[...] = jnp.zeros_like(tail_scr)
        tail_scr[SUBLANES - (CONV_W - 1):, :] = conv_ref[...]
        h_scr[...] = h0_ref[...]

    x = jnp.concatenate([x0_ref[...], x1_ref[...]], axis=1)
    gt = jnp.concatenate([g0_ref[...], g1_ref[...]], axis=1)
    row_in_group = lax.broadcasted_iota(jnp.int32, (C, 1), 0) % SUBLANES

    def rotate_in_groups(z, dlt):
        z3 = z.reshape(z.shape[0] // SUBLANES, SUBLANES, z.shape[1])
        return pltpu.roll(z3, dlt, axis=1).reshape(z.shape)

    x_prev = tail_scr[...]
    if C > SUBLANES:
        x_prev = jnp.concatenate([x_prev, x[:C - SUBLANES]], axis=0)
    tail_scr[...] = x[C - SUBLANES:, :]
    xc = cb_ref[...] + x * cw_ref[CONV_W - 1:CONV_W, :]
    for dlt in range(1, CONV_W):
        sh = jnp.where(row_in_group >= dlt, rotate_in_groups(x, dlt), rotate_in_groups(x_prev, dlt))
        xc = xc + sh * cw_ref[CONV_W - 1 - dlt:CONV_W - dlt, :]

    gate_r = jax.nn.sigmoid(_dot(xc, wa_ref[...]) + ba_ref[...])
    gate_i = jax.nn.sigmoid(_dot(xc, wx_ref[...]) + bx_ref[...])
    log_a = -RG_C * gate_r * _softplus(-lam_ref[...])
    a = jnp.exp(log_a)
    th = jnp.tanh(log_a)
    u = jnp.sqrt(-2.0 * th / (1.0 - th)) * (gate_i * xc)

    dlt = 1
    while dlt < SUBLANES:
        keep = row_in_group >= dlt
        a_sh = jnp.where(keep, rotate_in_groups(a, dlt), 1.0)
        u_sh = jnp.where(keep, rotate_in_groups(u, dlt), 0.0)
        u = a * u_sh + u
        a = a * a_sh
        dlt *= 2
    carry = h_scr[...]
    groups = []
    for gi in range(C // SUBLANES):
        rows = slice(gi * SUBLANES, (gi + 1) * SUBLANES)
        groups.append(a[rows] * carry + u[rows])
        carry = groups[-1][SUBLANES - 1:SUBLANES, :]
    h = jnp.concatenate(groups, axis=0)
    last = min(t_valid, C) - 1
    h_scr[...] = h[last:last + 1, :]
    hout_ref[...] = h[last:last + 1, :]
    y_ref[...] = (h * jax.nn.gelu(gt)).astype(y_ref.dtype)


def _lru(pxq, conv_prev, h0, prm, batch, t_len, t_valid, chunk):
    W = LRU_WIDTH
    nc = t_len // chunk
    half = W // 2
    col0 = RWKV_PROJ // half
    assert RWKV_PROJ % half == 0
    part = lambda k: pl.BlockSpec((chunk, half), lambda b, c: (b * nc + c, col0 + k))
    vec = lambda n: pl.BlockSpec((1, n), lambda b, c: (0, 0))
    kern = functools.partial(_lru_kernel, chunk=chunk, t_valid=t_valid)
    return pl.pallas_call(
        kern,
        grid=(batch, nc),
        in_specs=[
            part(0), part(1), part(2), part(3),
            pl.BlockSpec((None, CONV_W - 1, W), lambda b, c: (b, 0, 0)),
            pl.BlockSpec((None, 1, W), lambda b, c: (b, 0, 0)),
            pl.BlockSpec((CONV_W, W), lambda b, c: (0, 0)),
            vec(W),
            pl.BlockSpec((W, W), lambda b, c: (0, 0)), vec(W),
            pl.BlockSpec((W, W), lambda b, c: (0, 0)), vec(W),
            vec(W),
        ],
        out_specs=[
            pl.BlockSpec((chunk, W), lambda b, c: (b * nc + c, 0)),
            pl.BlockSpec((None, 1, W), lambda b, c: (b, 0, 0)),
        ],
        out_shape=[
            jax.ShapeDtypeStruct((batch * t_len, W), _mix_dtype(chunk)),
            jax.ShapeDtypeStruct((batch, 1, W), F32),
        ],
        scratch_shapes=[pltpu.VMEM((SUBLANES, W), F32), pltpu.VMEM((1, W), F32)],
        compiler_params=_cparams(("parallel", "arbitrary")),
        name="rglru",
    )(pxq, pxq, pxq, pxq, conv_prev, h0, prm["conv_w"], prm["conv_b"], prm["wa_bd"], prm["ba"], prm["wx_bd"],
      prm["bx"], prm["lam"])


def _attn_blocks(q_ref, k_ref, v_ref, bias_ref, og_scr, lse_scr, gi, dil, blocks):
    scale = ATTN_HEAD ** -0.5
    B = ATTN_SPAN

    def rows(start, first):
        if first is True:
            start, n = start, B
        elif first is False:
            start, n = start - B * dil, 2 * B
        else:
            start, n = jnp.maximum(start - B * dil, 0), 2 * B
        return pl.ds(start, n) if dil == 1 else pl.ds(start, n, stride=dil)

    scores = []
    for qs, first in blocks:
        qb = q_ref[rows(qs, True), :]
        if first is True:
            bias = bias_ref[gi, :, B:]
        elif first is False:
            bias = bias_ref[gi]
        else:
            own_then_masked = jnp.concatenate([bias_ref[gi, :, B:], jnp.full((B, B), NEG, F32)], axis=1)
            bias = jnp.where(first, own_then_masked, bias_ref[gi])
        scores.append(_dot_nt(qb, k_ref[rows(qs, first), :]) * scale + bias)
    probs = []
    for s in scores:
        mx = jnp.max(s, axis=-1, keepdims=True)
        pr = jnp.exp(s - mx)
        den = jnp.sum(pr, axis=-1, keepdims=True)
        probs.append(((pr / den).astype(BF16), mx + jnp.log(den)))
    outs = [_dot(pn, v_ref[rows(qs, first), :]) for (qs, first), (pn, _) in zip(blocks, probs)]
    for (qs, _), o, (_, lse) in zip(blocks, outs, probs):
        og_scr[gi, rows(qs, True), :] = o
        lse_scr[gi, rows(qs, True), :] = jnp.broadcast_to(lse, (B, ATTN_HEAD))


def _largest_divisor(n, cap):
    return max(u for u in range(1, cap + 1) if n % u == 0)


def _attn_prompt_kernel(q_ref, kil_ref, vil_ref, bias_ref, o_ref, k_ref, v_ref, og_scr, lse_scr, *, seq):
    B = ATTN_SPAN
    h = pl.program_id(1)
    R = ATTN_MERGE_ROWS

    def gather_head(i, carry):
        dst = pl.ds(pl.multiple_of(i * R, R), R)
        src = pl.ds(h + i * (R * ATTN_HEADS), R, stride=ATTN_HEADS)
        k_ref[dst, :] = kil_ref[src, :]
        v_ref[dst, :] = vil_ref[src, :]
        return carry

    lax.fori_loop(0, seq // R, gather_head, 0)

    for gi, (win, dil) in enumerate(DIL_PAIRS):
        nblk = seq // (dil * B)
        run = functools.partial(_attn_blocks, q_ref, k_ref, v_ref, bias_ref, og_scr, lse_scr, gi, dil)
        if nblk == 1:
            per = _largest_divisor(dil, ATTN_UNROLL)

            def body(i, carry, run=run, per=per):
                run([(i * per + u, True) for u in range(per)])
                return carry

            lax.fori_loop(0, dil // per, body, 0)
        elif dil == 1:
            per = _largest_divisor(nblk, ATTN_UNROLL)

            def body(i, carry, run=run, per=per):
                run([(pl.multiple_of((i * per + u) * B, B), (i == 0) if u == 0 else False) for u in range(per)])
                return carry

            lax.fori_loop(0, nblk // per, body, 0)
        else:
            per = _largest_divisor(dil, max(ATTN_UNROLL // nblk, 1))

            def body(i, carry, run=run, dil=dil, nblk=nblk, per=per):
                run([(i * per + u + blk * (B * dil), blk == 0) for u in range(per) for blk in range(nblk)])
                return carry

            lax.fori_loop(0, dil // per, body, 0)

    def merge(i, carry):
        sl = pl.ds(pl.multiple_of(i * R, R), R)
        ls = [lse_scr[gi, sl, :] for gi in range(len(DIL_PAIRS))]
        mx = functools.reduce(jnp.maximum, ls)
        ws = [jnp.exp(l - mx) for l in ls]
        tot = functools.reduce(lambda a, b: a + b, ws)
        acc = (ws[0] / tot) * og_scr[0, sl, :]
        for gi in range(1, len(DIL_PAIRS)):
            acc = acc + (ws[gi] / tot) * og_scr[gi, sl, :]
        o_ref[sl, :] = acc.astype(o_ref.dtype)
        return carry

    lax.fori_loop(0, seq // R, merge, 0)


def _attn_prompt(pxq, k_all, v_all, bias, layer, batch, seq):
    H = ATTN_HEADS
    G = len(DIL_PAIRS)
    q_col0 = (PXQ_WIDTH - ATTN_WIDTH) // ATTN_HEAD
    kern = functools.partial(_attn_prompt_kernel, seq=seq)
    kv = pl.BlockSpec((seq * H, ATTN_HEAD), lambda b, h: (layer * batch + b, 0))
    return pl.pallas_call(
        kern,
        grid=(batch, H),
        in_specs=[
            pl.BlockSpec((seq, ATTN_HEAD), lambda b, h: (b, q_col0 + h)),
            kv, kv,
            pl.BlockSpec((G, None, ATTN_SPAN, 2 * ATTN_SPAN), lambda b, h: (0, h, 0, 0)),
        ],
        out_specs=pl.BlockSpec((seq, ATTN_HEAD), lambda b, h: (b, h)),
        out_shape=jax.ShapeDtypeStruct((batch * seq, ATTN_WIDTH), BF16),
        scratch_shapes=[pltpu.VMEM((seq, ATTN_HEAD), F32), pltpu.VMEM((seq, ATTN_HEAD), F32),
                        pltpu.VMEM((G, seq, ATTN_HEAD), F32), pltpu.VMEM((G, seq, ATTN_HEAD), F32)],
        compiler_params=_cparams(("parallel", "arbitrary")),
        name="attn_prompt",
    )(pxq, k_all, v_all, bias)


def _attn_sample_kernel(pxq_ref, kn_ref, vn_ref, kc_ref, vc_ref, bias_ref, o_ref, q_scr, kn_scr, vn_scr, *, t_new, win):
    scale = ATTN_HEAD ** -0.5
    H = ATTN_HEADS
    G = len(DIL_PAIRS)
    TP = q_scr.shape[0]
    q_col0 = PXQ_WIDTH - ATTN_WIDTH
    q_scr[...] = jnp.zeros_like(q_scr)
    kn_scr[...] = jnp.zeros_like(kn_scr)
    vn_scr[...] = jnp.zeros_like(vn_scr)
    for h in range(H):
        lanes = slice(h * ATTN_HEAD, (h + 1) * ATTN_HEAD)
        q_scr[0:t_new, :] = pxq_ref[:, q_col0 + h * ATTN_HEAD:q_col0 + (h + 1) * ATTN_HEAD]
        kn_scr[0:t_new, :] = kn_ref[pl.ds(h, t_new, stride=H), :]
        vn_scr[0:t_new, :] = vn_ref[pl.ds(h, t_new, stride=H), :]
        qb = q_scr[...].astype(BF16)
        kc = kc_ref[pl.ds(h, win, stride=H), :]
        vc = vc_ref[pl.ds(h, win, stride=H), :]
        s = jnp.concatenate([_dot_nt(qb, kc), _dot_nt(qb, kn_scr[...])], axis=1) * scale
        prs, lses = [], []
        for gi in range(G):
            sg = s + bias_ref[gi, h]
            mx = jnp.max(sg, axis=-1, keepdims=True)
            pr = jnp.exp(sg - mx)
            den = jnp.sum(pr, axis=-1, keepdims=True)
            prs.append(pr / den)
            lses.append(mx + jnp.log(den))
        pall = jnp.concatenate(prs, axis=0)
        oall = _dot(pall[:, :win], vc) + _dot(pall[:, win:], vn_scr[...])
        mx = functools.reduce(jnp.maximum, lses)
        ws = [jnp.exp(l - mx) for l in lses]
        tot = functools.reduce(lambda a, b: a + b, ws)
        acc = (ws[0] / tot) * oall[0:TP]
        for gi in range(1, G):
            acc = acc + (ws[gi] / tot) * oall[gi * TP:(gi + 1) * TP]
        o_ref[:, lanes] = acc[0:t_new].astype(o_ref.dtype)


def _attn_sample(pxq, k_new, v_new, k_cache, v_cache, bias, layer, batch, t_new):
    H = ATTN_HEADS
    win = k_cache.shape[2] // H
    tp = bias.shape[2]
    kern = functools.partial(_attn_sample_kernel, t_new=t_new, win=win)
    cache = pl.BlockSpec((None, None, win * H, ATTN_HEAD), lambda b: (layer, b, 0, 0))
    new = pl.BlockSpec((None, t_new * H, ATTN_HEAD), lambda b: (layer * batch + b, 0, 0))
    return pl.pallas_call(
        kern,
        grid=(batch,),
        in_specs=[pl.BlockSpec((None, t_new, pxq.shape[2]), lambda b: (b, 0, 0)), new, new, cache, cache,
                  pl.BlockSpec(bias.shape, lambda b: (0, 0, 0, 0))],
        out_specs=pl.BlockSpec((None, t_new, ATTN_WIDTH), lambda b: (b, 0, 0)),
        out_shape=jax.ShapeDtypeStruct((batch, t_new, ATTN_WIDTH), F32),
        scratch_shapes=[pltpu.VMEM((tp, ATTN_HEAD), F32), pltpu.VMEM((LANES, ATTN_HEAD), F32),
                        pltpu.VMEM((LANES, ATTN_HEAD), F32)],
        compiler_params=_cparams(("parallel",)),
        name="attn_sample",
    )(pxq, k_new, v_new, k_cache, v_cache, bias)


def _bias_kernel(idx_ref, rb_ref, o_ref):
    h = pl.program_id(0)
    idx = idx_ref[...]
    acc = jnp.full(idx.shape, NEG, F32)
    for b in range(N_BUCKETS):
        acc = jnp.where(idx == b, rb_ref[b, h], acc)
    o_ref[...] = acc


def _bias_table(idx, rel_bias):
    g, r, c = idx.shape
    heads = rel_bias.shape[1]
    return pl.pallas_call(
        _bias_kernel,
        grid=(heads,),
        in_specs=[pl.BlockSpec((g, r, c), lambda h: (0, 0, 0)),
                  pl.BlockSpec(memory_space=pltpu.SMEM)],
        out_specs=pl.BlockSpec((g, None, r, c), lambda h: (0, h, 0, 0)),
        out_shape=jax.ShapeDtypeStruct((g, heads, r, c), F32),
        compiler_params=_cparams(("parallel",)),
        name="bias_table",
    )(jnp.asarray(idx, jnp.int32), rel_bias.astype(F32))


def _t5_bucket_static(dist):
    dist = np.asarray(dist, np.int64)
    exact = N_BUCKETS // 2
    d = np.maximum(dist, 1).astype(np.float64)
    large = exact + (np.log(d / exact) / math.log(REL_MAX_DIST / exact) * (N_BUCKETS - exact)).astype(np.int64)
    return np.where(dist < exact, dist, np.minimum(large, N_BUCKETS - 1))


def _prompt_bias_idx():
    B = ATTN_SPAN
    qi = np.arange(B)[:, None]
    kj = np.arange(2 * B)[None, :]
    delta = qi + B - kj
    valid = (delta >= 0) & (delta <= B)
    return np.stack([np.where(valid, _t5_bucket_static(np.clip(delta, 0, B) * dil), -1) for _, dil in DIL_PAIRS])


def _sample_bias_idx(t_new, t_pad, win, total):
    r = np.arange(total)[None, :]
    t = np.arange(t_pad)[:, None]
    dist = win + t - r
    tabs = []
    for _, dil in DIL_PAIRS:
        assert win >= dil * ATTN_SPAN
        valid = (dist >= 0) & (dist % dil == 0) & (dist <= dil * ATTN_SPAN) & (t < t_new)
        tabs.append(np.where(valid, _t5_bucket_static(np.clip(dist, 0, dil * ATTN_SPAN)), -1))
    return np.stack(tabs)


def _block_diag(blocks):
    n, c, d = blocks.shape
    eye = jnp.eye(n, dtype=blocks.dtype)
    return jnp.einsum("ncd,nm->ncmd", blocks, eye).reshape(n * c, n * d)


def _layer_params(l, a):
    W = RWKV_WIDTH
    row = lambda t: t[l].reshape(1, -1).astype(F32)

    def lora_pad(w, off):
        z = jnp.zeros((LORA_WIDTH, W), F32)
        return z.at[off:off + w.shape[0]].set(w).astype(BF16)

    head_of = np.arange(W) // RWKV_HEAD
    ones_bd = jnp.asarray((head_of[:, None] == head_of[None, :]).astype(np.float32), BF16)
    rwkv = dict(
        mu=row(a["rwkv_mu"]), w0=row(a["rwkv_w0"]), a0=row(a["rwkv_a0"]),
        w_up=lora_pad(a["rwkv_w_up"][l], 0),
        a_up=lora_pad(a["rwkv_a_up"][l], DECAY_LORA),
        g_up=lora_pad(a["rwkv_g_up"][l], DECAY_LORA + ICLR_LORA),
        k_k=row(a["rwkv_k_k"]), k_a=row(a["rwkv_k_a"]), r_k=row(a["rwkv_r_k"]),
        lnx_g=row(a["rwkv_lnx_g"]), lnx_b=row(a["rwkv_lnx_b"]),
        ones_bd=ones_bd,
    )
    lru = dict(
        conv_w=a["lru_conv_w"][l].astype(F32), conv_b=row(a["lru_conv_b"]),
        wa_bd=_block_diag(a["lru_wa"][l]).astype(BF16), ba=row(a["lru_ba"]),
        wx_bd=_block_diag(a["lru_wx"][l]).astype(BF16), bx=row(a["lru_bx"]),
        lam=row(a["lru_lambda"]),
    )
    return dict(rwkv=rwkv, lru=lru)


def _shared_params(a):
    gain = lambda t: t.reshape(t.shape[0], 1, -1).astype(F32)
    return dict(
        g_mix_pre=gain(a["norm_mix_pre"]), g_mix_post=gain(a["norm_mix_post"]),
        g_ffn_pre=gain(a["norm_ffn_pre"]), g_ffn_post=gain(a["norm_ffn_post"]),
    )


def _layer(x, layer, sp, wts, lp, batch, t_len, shift_prev, s0, conv_prev, h0, kv_all, attend):
    emit = wts is None
    src = sp["f32"] if emit else wts
    wl = layer if emit else 0
    copies = {}
    pxq, k_all, v_all, *w_copy = _in_proj(x, sp["g_mix_pre"], src["w_in"], layer, wl, *kv_all, emit_bf16=emit)
    copies["w_in"] = w_copy[0] if emit else None
    pxq3 = pxq.reshape(batch, t_len, pxq.shape[1])
    if t_len % RWKV_CHUNK == 0:
        y_a, s_fin = _rwkv(pxq3, shift_prev, s0, lp["rwkv"], batch, t_len, t_len, RWKV_CHUNK)
        y_a = y_a.reshape(batch * t_len, -1)
        y_b, h_fin = _lru(pxq, conv_prev, h0, lp["lru"], batch, t_len, t_len, LRU_CHUNK)
    else:
        assert t_len <= SUBLANES
        padded = jnp.pad(pxq3, ((0, 0), (0, SUBLANES - t_len), (0, 0)))
        y_a, s_fin = _rwkv(padded, shift_prev, s0, lp["rwkv"], batch, SUBLANES, t_len, SUBLANES)
        y_a = y_a[:, :t_len].reshape(batch * t_len, -1)
        y_b, h_fin = _lru(padded.reshape(batch * SUBLANES, pxq.shape[1]), conv_prev, h0, lp["lru"], batch, SUBLANES,
                          t_len, SUBLANES)
        y_b = y_b.reshape(batch, SUBLANES, -1)[:, :t_len].reshape(batch * t_len, -1)
    y_c = attend(pxq, k_all, v_all)

    x = _out_proj(y_a, y_b, y_c, src["w_out"], sp["g_mix_post"], x, layer, wl, emit_bf16=emit)
    if emit:
        x, copies["w_out"] = x
    x = _ffn(x, sp["g_ffn_pre"], src["w1"], src["w2"], sp["g_ffn_post"], layer, wl, emit_bf16=emit)
    if emit:
        x, copies["w1"], copies["w2"] = x

    lru_x = pxq3[:, :, RWKV_PROJ:RWKV_PROJ + LRU_WIDTH]
    if t_len >= CONV_W - 1:
        conv_new = lru_x[:, t_len - (CONV_W - 1):]
    else:
        conv_new = jnp.concatenate([conv_prev, lru_x], axis=1)[:, -(CONV_W - 1):]
    state = (pxq3[:, -1, :RWKV_PROJ], s_fin, conv_new, h_fin.reshape(batch, LRU_WIDTH))
    return x, state, (k_all, v_all), copies


def kernel(x_prompt, x_sample, state_rwkv_wkv, state_rwkv_shift, state_lru_h, state_lru_conv, cache_attn_k, cache_attn_v, rel_bias, norm_mix_pre, norm_mix_post, norm_ffn_pre, norm_ffn_post, w_in, w_out, rwkv_mu, rwkv_w0, rwkv_w_up, rwkv_a0, rwkv_a_up, rwkv_g_up, rwkv_k_k, rwkv_k_a, rwkv_r_k, rwkv_lnx_g, rwkv_lnx_b, lru_conv_w, lru_conv_b, lru_wa, lru_ba, lru_wx, lru_bx, lru_lambda, ffn_w1, ffn_w2):
    a = dict(norm_mix_pre=norm_mix_pre, norm_mix_post=norm_mix_post, norm_ffn_pre=norm_ffn_pre,
             norm_ffn_post=norm_ffn_post, w_in=w_in, w_out=w_out, rwkv_mu=rwkv_mu, rwkv_w0=rwkv_w0,
             rwkv_w_up=rwkv_w_up, rwkv_a0=rwkv_a0, rwkv_a_up=rwkv_a_up, rwkv_g_up=rwkv_g_up, rwkv_k_k=rwkv_k_k,
             rwkv_k_a=rwkv_k_a, rwkv_r_k=rwkv_r_k, rwkv_lnx_g=rwkv_lnx_g, rwkv_lnx_b=rwkv_lnx_b,
             lru_conv_w=lru_conv_w, lru_conv_b=lru_conv_b, lru_wa=lru_wa, lru_ba=lru_ba, lru_wx=lru_wx,
             lru_bx=lru_bx, lru_lambda=lru_lambda, ffn_w1=ffn_w1, ffn_w2=ffn_w2)
    depth = w_in.shape[0]
    pb, seq, _ = x_prompt.shape
    sb, t_new, _ = x_sample.shape
    win = cache_attn_k.shape[2]
    keep = min(ATTN_WINDOW, seq)
    total = win + LANES
    assert t_new <= LANES and seq % (DIL_PAIRS[-1][1] * ATTN_SPAN) == 0

    bias_p = _bias_table(_prompt_bias_idx(), rel_bias)
    bias_s = _bias_table(_sample_bias_idx(t_new, SUBLANES, win, total), rel_bias)
    k_cache = cache_attn_k.reshape(depth, sb, win * ATTN_HEADS, ATTN_HEAD)
    v_cache = cache_attn_v.reshape(depth, sb, win * ATTN_HEADS, ATTN_HEAD)

    xp = x_prompt.reshape(pb * seq, D_MODEL)
    xs = x_sample.reshape(sb * t_new, D_MODEL)
    new_p, new_s = [], []
    sp = _shared_params(a)
    kv_p = tuple(jnp.zeros((depth * pb * seq * ATTN_HEADS, ATTN_HEAD), F32) for _ in range(2))
    kv_s = tuple(jnp.zeros((depth * sb * t_new * ATTN_HEADS, ATTN_HEAD), F32) for _ in range(2))
    sp["f32"] = dict(w_in=w_in.astype(F32), w_out=w_out.astype(F32), w1=ffn_w1.astype(F32), w2=ffn_w2.astype(F32))
    for l in range(depth):
        lp = _layer_params(l, a)
        attend_s = lambda pxq, k_all, v_all, l=l: _attn_sample(
            pxq.reshape(sb, t_new, pxq.shape[1]), k_all.reshape(depth * sb, t_new * ATTN_HEADS, ATTN_HEAD),
            v_all.reshape(depth * sb, t_new * ATTN_HEADS, ATTN_HEAD), k_cache, v_cache, bias_s, l, sb, t_new,
        ).reshape(sb * t_new, ATTN_WIDTH)
        xs, st_s, kv_s, wts = _layer(xs, l, sp, None, lp, sb, t_new,
                                     state_rwkv_shift[l].reshape(sb, 1, RWKV_PROJ), state_rwkv_wkv[l].astype(F32),
                                     state_lru_conv[l], state_lru_h[l].reshape(sb, 1, LRU_WIDTH), kv_s, attend_s)
        attend_p = lambda pxq, k_all, v_all, l=l: _attn_prompt(pxq, k_all, v_all, bias_p, l, pb, seq)
        xp, st_p, kv_p, _ = _layer(xp, l, sp, wts, lp, pb, seq,
                                   jnp.zeros((pb, 1, RWKV_PROJ), F32),
                                   jnp.zeros((pb, RWKV_HEADS, RWKV_HEAD, RWKV_HEAD), F32),
                                   jnp.zeros((pb, CONV_W - 1, LRU_WIDTH), F32), jnp.zeros((pb, 1, LRU_WIDTH), F32),
                                   kv_p, attend_p)
        new_p.append(st_p)
        new_s.append(st_s)

    stack = lambda sts, i: jnp.stack([s[i] for s in sts])
    k_p, v_p = (t.reshape(depth, pb, seq, ATTN_HEADS, ATTN_HEAD)[:, :, seq - keep:] for t in kv_p)
    k_s, v_s = (t.reshape(depth, sb, t_new, ATTN_HEADS, ATTN_HEAD) for t in kv_s)
    return (xp.reshape(pb, seq, D_MODEL), xs.reshape(sb, t_new, D_MODEL),
            stack(new_p, 1), stack(new_s, 1), stack(new_p, 0), stack(new_s, 0),
            stack(new_p, 3), stack(new_s, 3), stack(new_p, 2), stack(new_s, 2),
            k_p, k_s, v_p, v_s)
```

```python
import functools
import math

import numpy as np
import jax
import jax.numpy as jnp
from jax import lax
from jax.experimental import pallas as pl
from jax.experimental.pallas import tpu as pltpu

F32 = jnp.float32
BF16 = jnp.bfloat16

D_MODEL = 2048
RWKV_WIDTH = 512
RWKV_HEAD = 64
RWKV_HEADS = RWKV_WIDTH // RWKV_HEAD
DECAY_LORA = 64
ICLR_LORA = 64
GATE_LORA = 128
LORA_WIDTH = DECAY_LORA + ICLR_LORA + GATE_LORA
RWKV_PROJ = 3 * RWKV_WIDTH + LORA_WIDTH
GN_EPS = 64e-5
LRU_WIDTH = 512
LRU_BLOCKS = 8
LRU_BLOCK = LRU_WIDTH // LRU_BLOCKS
CONV_W = 4
RG_C = 8.0
ATTN_WIDTH = 1024
ATTN_HEAD = 128
ATTN_HEADS = ATTN_WIDTH // ATTN_HEAD
DIL_PAIRS = ((128, 1), (512, 4), (2048, 16))
ATTN_SPAN = 128
ATTN_WINDOW = 2048
N_BUCKETS = 32
REL_MAX_DIST = ATTN_WINDOW
D_FF = 4 * D_MODEL
RMS_EPS = 1e-6
NEG = -1e30

LANES = 128
SUBLANES = 8
MXU_TILE = 256
VMEM_LIMIT_BYTES = 56 * 1024 * 1024

ROW_TILE = 1024
IN_COL_TILE = 256
IN_FUSED_TILES = 2
OUT_ROW_TILE = 512
FFN_ROW_TILE = 512
FFN_COL_TILE = 1024
FFN_CAST_COL_TILE = 512
RWKV_CHUNK = 64
RWKV_SEQS_PER_STEP = 4
LRU_CHUNK = 256
ATTN_MERGE_ROWS = 256
ATTN_UNROLL = 6


def _cparams(sem):
    return pltpu.CompilerParams(dimension_semantics=sem, vmem_limit_bytes=VMEM_LIMIT_BYTES)


def _dot(a, b):
    return jnp.dot(a.astype(BF16), b.astype(BF16), preferred_element_type=F32)


def _dot_nt(a, b):
    return lax.dot_general(a.astype(BF16), b.astype(BF16), (((1,), (1,)), ((), ())), preferred_element_type=F32)


def _dot_tn(a, b):
    return lax.dot_general(a.astype(BF16), b.astype(BF16), (((0,), (0,)), ((), ())), preferred_element_type=F32)


def _softplus(z):
    return jnp.maximum(z, 0.0) + jnp.log1p(jnp.exp(-jnp.abs(z)))


def _mix_dtype(rows):
    return BF16 if rows % (2 * SUBLANES) == 0 else F32


def _rms(x, g):
    ms = jnp.mean(x * x, axis=-1, keepdims=True)
    return x * lax.rsqrt(ms + RMS_EPS) * g


PXQ_WIDTH = RWKV_PROJ + 2 * LRU_WIDTH + ATTN_WIDTH
PXQ_TILES = PXQ_WIDTH // IN_COL_TILE
KV_TILES = ATTN_WIDTH // IN_COL_TILE
HEADS_PER_TILE = IN_COL_TILE // ATTN_HEAD


def _in_proj_kernel(x_ref, g_ref, *refs, fuse):
    w_refs = refs[:fuse]
    pxq_ref, k_ref, v_ref = refs[fuse + 2:fuse + 5]
    rest = refs[fuse + 5:]
    h_scr = rest[-1]
    j = pl.program_id(1)
    tm = x_ref.shape[0]
    pxq_steps = pl.cdiv(PXQ_TILES, fuse)
    kv_steps = KV_TILES // fuse

    @pl.when(j == 0)
    def _():
        h_scr[...] = _rms(x_ref[...], g_ref[...]).astype(BF16)

    ws = [w_ref[...].astype(BF16) for w_ref in w_refs]
    if len(rest) > 1:
        rest[0][...] = ws[0]
    w = ws[0] if fuse == 1 else jnp.concatenate(ws, axis=1)
    acc = jnp.dot(h_scr[...], w, preferred_element_type=F32)

    @pl.when(j < pxq_steps)
    def _():
        pxq_ref[...] = acc

    def scatter_heads(o_ref, step):
        for hh in range(fuse * HEADS_PER_TILE):
            head = step * (fuse * HEADS_PER_TILE) + hh
            o_ref[pl.ds(head, tm, stride=ATTN_HEADS), :] = acc[:, hh * ATTN_HEAD:(hh + 1) * ATTN_HEAD]

    @pl.when((j >= pxq_steps) & (j < pxq_steps + kv_steps))
    def _():
        scatter_heads(k_ref, j - pxq_steps)

    @pl.when(j >= pxq_steps + kv_steps)
    def _():
        scatter_heads(v_ref, j - pxq_steps - kv_steps)


def _in_proj(x, g, w, layer, w_layer, k_all, v_all, emit_bf16=False):
    m, d = x.shape
    n = w.shape[2]
    assert n == PXQ_WIDTH + 2 * ATTN_WIDTH
    tm = min(ROW_TILE, m)
    row_tiles = m // tm
    assert not emit_bf16 or row_tiles == 1
    fuse = 1 if emit_bf16 else IN_FUSED_TILES
    assert KV_TILES % fuse == 0
    pxq_steps = pl.cdiv(PXQ_TILES, fuse)
    spare = pxq_steps * fuse - PXQ_TILES
    width = fuse * IN_COL_TILE

    def w_spec(slot):
        def index(i, j):
            tile = jnp.where(j < pxq_steps, jnp.minimum(fuse * j + slot, PXQ_TILES - 1), fuse * j + slot - spare)
            return (w_layer, 0, tile)
        return pl.BlockSpec((None, d, IN_COL_TILE), index)

    kv_spec = pl.BlockSpec((tm * ATTN_HEADS, ATTN_HEAD), lambda i, j: (layer * row_tiles + i, 0))
    out_specs = [
        pl.BlockSpec((tm, width), lambda i, j: (i, jnp.minimum(j, pxq_steps - 1))),
        kv_spec, kv_spec,
    ]
    out_shape = [
        jax.ShapeDtypeStruct((m, pxq_steps * width), F32),
        jax.ShapeDtypeStruct(k_all.shape, F32),
        jax.ShapeDtypeStruct(v_all.shape, F32),
    ]
    if emit_bf16:
        out_specs.append(pl.BlockSpec((None, d, IN_COL_TILE), lambda i, j: (0, 0, j)))
        out_shape.append(jax.ShapeDtypeStruct((1, d, n), BF16))
    return pl.pallas_call(
        functools.partial(_in_proj_kernel, fuse=fuse),
        grid=(row_tiles, pxq_steps + 2 * (KV_TILES // fuse)),
        in_specs=[
            pl.BlockSpec((tm, d), lambda i, j: (i, 0)),
            pl.BlockSpec((None, 1, d), lambda i, j: (layer, 0, 0)),
            *[w_spec(slot) for slot in range(fuse)],
            pl.BlockSpec(memory_space=pl.ANY),
            pl.BlockSpec(memory_space=pl.ANY),
        ],
        out_specs=out_specs,
        out_shape=out_shape,
        input_output_aliases={2 + fuse: 1, 3 + fuse: 2},
        scratch_shapes=[pltpu.VMEM((tm, d), BF16)],
        compiler_params=_cparams(("parallel", "arbitrary")),
        name="in_proj",
    )(x, g, *([w] * fuse), k_all, v_all)


def _out_proj_kernel(ya_ref, yb_ref, yc_ref, w_ref, g_ref, x_ref, o_ref, *w_copy):
    c1, c2 = RWKV_WIDTH, RWKV_WIDTH + LRU_WIDTH
    parts = [(ya_ref, slice(0, c1)), (yb_ref, slice(c1, c2)), (yc_ref, slice(c2, None))]
    acc = None
    for y_ref, rows in parts:
        w = w_ref[rows, :].astype(BF16)
        if w_copy:
            w_copy[0][rows, :] = w
        t = jnp.dot(y_ref[...].astype(BF16), w, preferred_element_type=F32)
        acc = t if acc is None else acc + t
    o_ref[...] = x_ref[...] + _rms(acc, g_ref[...])


def _out_proj(ya, yb, yc, w, g, x, layer, w_layer, emit_bf16=False):
    m, d = x.shape
    tm = min(OUT_ROW_TILE, m)
    assert not emit_bf16 or m == tm
    out_specs = [pl.BlockSpec((tm, d), lambda i: (i, 0))]
    out_shape = [jax.ShapeDtypeStruct((m, d), F32)]
    if emit_bf16:
        out_specs.append(pl.BlockSpec((None, d, d), lambda i: (0, 0, 0)))
        out_shape.append(jax.ShapeDtypeStruct((1, d, d), BF16))
    res = pl.pallas_call(
        _out_proj_kernel,
        grid=(m // tm,),
        in_specs=[
            pl.BlockSpec((tm, RWKV_WIDTH), lambda i: (i, 0)),
            pl.BlockSpec((tm, LRU_WIDTH), lambda i: (i, 0)),
            pl.BlockSpec((tm, ATTN_WIDTH), lambda i: (i, 0)),
            pl.BlockSpec((None, d, d), lambda i: (w_layer, 0, 0)),
            pl.BlockSpec((None, 1, d), lambda i: (layer, 0, 0)),
            pl.BlockSpec((tm, d), lambda i: (i, 0)),
        ],
        out_specs=out_specs,
        out_shape=out_shape,
        compiler_params=_cparams(("parallel",)),
        name="out_proj",
    )(ya, yb, yc, w, g, x)
    return res if emit_bf16 else res[0]


def _ffn_kernel(x_ref, g1_ref, w1_ref, w2_ref, g2_ref, o_ref, *rest):
    h_scr, acc_scr = rest[-2:]
    j = pl.program_id(1)

    @pl.when(j == 0)
    def _():
        h_scr[...] = _rms(x_ref[...], g1_ref[...]).astype(BF16)
        acc_scr[...] = jnp.zeros_like(acc_scr)

    w1 = w1_ref[...].astype(BF16)
    w2 = w2_ref[...].astype(BF16)
    if len(rest) > 2:
        rest[0][...] = w1
        rest[1][...] = w2
    u = jnp.dot(h_scr[...], w1, preferred_element_type=F32)
    u = jnp.square(jnp.maximum(u, 0.0)).astype(BF16)
    acc_scr[...] += jnp.dot(u, w2, preferred_element_type=F32)

    @pl.when(j == pl.num_programs(1) - 1)
    def _():
        o_ref[...] = x_ref[...] + _rms(acc_scr[...], g2_ref[...])


def _ffn(x, g1, w1, w2, g2, layer, w_layer, emit_bf16=False):
    m, d = x.shape
    f = w1.shape[2]
    tm = min(FFN_ROW_TILE, m)
    tf = FFN_CAST_COL_TILE if emit_bf16 else FFN_COL_TILE
    assert not emit_bf16 or m == tm
    out_specs = [pl.BlockSpec((tm, d), lambda i, j: (i, 0))]
    out_shape = [jax.ShapeDtypeStruct((m, d), F32)]
    if emit_bf16:
        out_specs += [pl.BlockSpec((None, d, tf), lambda i, j: (0, 0, j)),
                      pl.BlockSpec((None, tf, d), lambda i, j: (0, j, 0))]
        out_shape += [jax.ShapeDtypeStruct((1, d, f), BF16), jax.ShapeDtypeStruct((1, f, d), BF16)]
    res = pl.pallas_call(
        _ffn_kernel,
        grid=(m // tm, f // tf),
        in_specs=[
            pl.BlockSpec((tm, d), lambda i, j: (i, 0)),
            pl.BlockSpec((None, 1, d), lambda i, j: (layer, 0, 0)),
            pl.BlockSpec((None, d, tf), lambda i, j: (w_layer, 0, j)),
            pl.BlockSpec((None, tf, d), lambda i, j: (w_layer, j, 0)),
            pl.BlockSpec((None, 1, d), lambda i, j: (layer, 0, 0)),
        ],
        out_specs=out_specs,
        out_shape=out_shape,
        scratch_shapes=[pltpu.VMEM((tm, d), BF16), pltpu.VMEM((tm, d), F32)],
        compiler_params=_cparams(("parallel", "arbitrary")),
        name="ffn",
    )(x, g1, w1, w2, g2)
    return res if emit_bf16 else res[0]


def _rwkv_groups(chunk):
    return 2 if (RWKV_HEADS * chunk) % (2 * MXU_TILE) == 0 else 1


def _rwkv_kernel(p_ref, shift_ref, s0_ref, mu_ref, w0_ref, wup_ref, a0_ref, aup_ref, gup_ref, kk_ref, ka_ref,
                 rk_ref, lg_ref, lb_ref, ones_ref,
                 y_ref, sout_ref, s_scr, prev_scr, *, chunk, t_valid):
    C = chunk
    W = RWKV_WIDTH
    H = RWKV_HEADS
    N = RWKV_HEAD
    nb = p_ref.shape[0]
    c = pl.program_id(1)

    G = _rwkv_groups(C)
    hpg = H // G
    wl = W // G
    gc = hpg * C

    @pl.when(c == 0)
    def _():
        s_scr[...] = jnp.zeros_like(s_scr)
        for h in range(H):
            o = (h % hpg) * N
            s_scr[:, h // hpg, o:o + N, o:o + N] = s0_ref[:, h]
        prev_scr[...] = shift_ref[...]

    inv_n = 1.0 / RWKV_HEAD
    shared = dict(wup=[wup_ref], aup=[aup_ref], gup=[gup_ref],
                  ones=[ones_ref.at[g * wl:(g + 1) * wl, g * wl:(g + 1) * wl] for g in range(G)])

    def hi_lo(x):
        hi = x.astype(BF16).astype(F32)
        return [hi, x - hi]

    row = lax.broadcasted_iota(jnp.int32, (C, 1), 0)

    def cumsum_rows(z):
        dlt = 1
        while dlt < C:
            z = z + jnp.where(row >= dlt, pltpu.roll(z, dlt, axis=0), 0.0)
            dlt *= 2
        return z

    ti = lax.broadcasted_iota(jnp.int32, (C, H * C), 0)
    si = lax.broadcasted_iota(jnp.int32, (C, H * C), 1) % C
    strict = ti > si
    incl = ti >= si
    eye = (ti == si).astype(F32)
    blk_r = lax.broadcasted_iota(jnp.int32, (gc, 1), 0) // C
    mask_ch = blk_r == lax.broadcasted_iota(jnp.int32, (1, wl), 1) // N
    mask_cc = blk_r == lax.broadcasted_iota(jnp.int32, (1, gc), 1) // C
    mask_ss = (lax.broadcasted_iota(jnp.int32, (wl, 1), 0) // N
               == lax.broadcasted_iota(jnp.int32, (1, wl), 1) // N)

    def block_diag(x, mask, width):
        out = []
        for g in range(G):
            tiled = jnp.concatenate([x[:, g * width:(g + 1) * width]] * hpg, axis=0)
            out.append(jnp.where(mask, tiled, 0.0).astype(BF16))
        return out

    def per_head(a_cat, bds):
        a_b = a_cat.astype(BF16)
        return jnp.concatenate([jnp.dot(a_b[:, g * gc:(g + 1) * gc], bds[g], preferred_element_type=F32)
                                for g in range(G)], axis=1)

    def one_sequence(bi):
        p = p_ref[bi]
        shifted = jnp.where(row == 0, prev_scr[bi], pltpu.roll(p, 1, axis=0))
        prev_scr[bi] = p_ref[bi, C - 1:C, :]
        m = p + (shifted - p) * mu_ref[...]
        r = m[:, 0:W]
        k = m[:, W:2 * W]
        v = m[:, 2 * W:3 * W]
        x = m[:, 3 * W:]
        lw, la, gate = yield [("wup", jnp.tanh(x)), ("aup", x), ("gup", jax.nn.sigmoid(x))]
        w = w0_ref[...] + lw
        a = jax.nn.sigmoid(a0_ref[...] + la)
        softplus_neg_w = jnp.maximum(-w, 0.0) + jnp.log(1.0 + jnp.exp(-jnp.abs(w)))
        loga = -jnp.exp(-softplus_neg_w - 0.5)
        kk = k * kk_ref[...]
        k2 = k * (1.0 + (a - 1.0) * ka_ref[...])
        (ss,) = yield [("ones", jnp.concatenate(hi_lo(kk * kk) + hi_lo(r * k2 * rk_ref[...]), axis=0))]
        kk = kk / jnp.maximum(jnp.sqrt(ss[0:C] + ss[C:2 * C]), 1e-12)
        bonus = (ss[2 * C:3 * C] + ss[3 * C:]) * v
        if t_valid < C:
            live = row < t_valid
            loga = jnp.where(live, loga, 0.0)
            kk = jnp.where(live, kk, 0.0)
            k2 = jnp.where(live, k2, 0.0)
        cl = cumsum_rows(loga)
        cl_last = cl[C - 1:C, :]
        e_neg = jnp.exp(-cl)
        e_rem = jnp.exp(cl_last - cl)
        kka = kk * a
        al = -kk * jnp.exp(cl - loga)
        rt = r * jnp.exp(cl)
        be_bd = block_diag(kka * e_neg, mask_ch, wl)
        kt_bd = block_diag(k2 * e_neg, mask_ch, wl)
        bh = kka * e_rem
        kh = k2 * e_rem
        lhs = jnp.concatenate([al, rt], axis=0).astype(BF16)
        lhs_g = [lhs[:, g * wl:(g + 1) * wl] for g in range(G)]
        g_b = jnp.concatenate([_dot_nt(lhs_g[g], be_bd[g]) for g in range(G)], axis=1)
        g_k = jnp.concatenate([_dot_nt(lhs_g[g], kt_bd[g]) for g in range(G)], axis=1)
        n_cat = jnp.where(strict, g_b[0:C], 0.0)
        a_ak = jnp.where(strict, g_k[0:C], 0.0)
        a_rb = jnp.where(incl, g_b[C:], 0.0)
        a_rk = jnp.where(incl, g_k[C:], 0.0)
        s_prev = [s_scr[bi, g] for g in range(G)]
        proj = jnp.concatenate([_dot_nt(lhs_g[g], s_prev[g]) for g in range(G)], axis=1)
        doublings = max(int(math.log2(C)) - 1, 0)
        t_cat = eye + n_cat
        pw = n_cat
        if doublings:
            pw = per_head(n_cat, block_diag(n_cat, mask_cc, gc))
        yield None
        for it in range(doublings):
            pw_bd = block_diag(pw, mask_cc, gc)
            if it < doublings - 1:
                both = per_head(jnp.concatenate([t_cat, pw], axis=0), pw_bd)
                t_cat = t_cat + both[0:C]
                pw = both[C:]
            else:
                t_cat = t_cat + per_head(t_cat, pw_bd)
            yield None
        v_bd = block_diag(v, mask_ch, wl)
        rhs = proj[0:C] + per_head(a_ak, v_bd)
        yield None
        u = per_head(t_cat, block_diag(rhs, mask_ch, wl))
        yield None
        y = proj[C:] + per_head(a_rb, block_diag(u, mask_ch, wl)) + per_head(a_rk, v_bd)
        pad = LANES - 2 * C
        uv = jnp.concatenate([u, v] + ([jnp.zeros((pad, W), F32)] if pad > 0 else []), axis=0).astype(BF16)
        bk = jnp.concatenate([bh, kh] + ([jnp.zeros((pad, W), F32)] if pad > 0 else []), axis=0).astype(BF16)
        decay = jnp.exp(cl_last)
        for g in range(G):
            lanes = slice(g * wl, (g + 1) * wl)
            upd = _dot_tn(uv[:, lanes], bk[:, lanes])
            s_scr[bi, g] = s_prev[g] * decay[:, lanes] + jnp.where(mask_ss, upd, 0.0)

        @pl.when(c == pl.num_programs(1) - 1)
        def _():
            for h in range(H):
                o = (h % hpg) * N
                sout_ref[bi, h] = s_scr[bi, h // hpg, o:o + N, o:o + N]

        (sy,) = yield [("ones", jnp.concatenate(hi_lo(y), axis=0))]
        d = y - (sy[0:C] + sy[C:]) * inv_n
        (sd,) = yield [("ones", jnp.concatenate(hi_lo(d * d), axis=0))]
        var = (sd[0:C] + sd[C:]) * inv_n
        yn = d * lax.rsqrt(var + GN_EPS) * lg_ref[...] + lb_ref[...]
        y_ref[bi] = ((yn + bonus) * gate).astype(y_ref.dtype)

    seqs = [one_sequence(bi) for bi in range(nb)]
    replies = [None] * nb
    while True:
        asks, finished = [], 0
        for seq, reply in zip(seqs, replies):
            try:
                asks.append(seq.send(reply))
            except StopIteration:
                finished += 1
        if finished:
            assert finished == nb
            break
        if asks[0] is None:
            replies = [None] * nb
            continue
        replies = [[] for _ in range(nb)]
        for qi, (name, _) in enumerate(asks[0]):
            lhs = jnp.concatenate([ask[qi][1] for ask in asks], axis=0).astype(BF16)
            parts = shared[name]
            kw = lhs.shape[1] // len(parts)
            z = jnp.concatenate([jnp.dot(lhs[:, g * kw:(g + 1) * kw], part[...], preferred_element_type=F32)
                                 for g, part in enumerate(parts)], axis=1)
            rows = z.shape[0] // nb
            for si in range(nb):
                replies[si].append(z[si * rows:(si + 1) * rows])


def _rwkv(p, shift_prev, s0, prm, batch, t_len, t_valid, chunk):
    W = RWKV_WIDTH
    nc = t_len // chunk
    nb = _largest_divisor(batch, RWKV_SEQS_PER_STEP)
    vec = lambda n: pl.BlockSpec((1, n), lambda b, c: (0, 0))
    mat = lambda r, n: pl.BlockSpec((r, n), lambda b, c: (0, 0))
    state = pl.BlockSpec((nb, RWKV_HEADS, RWKV_HEAD, RWKV_HEAD), lambda b, c: (b, 0, 0, 0))
    groups = _rwkv_groups(chunk)
    kern = functools.partial(_rwkv_kernel, chunk=chunk, t_valid=t_valid)
    return pl.pallas_call(
        kern,
        grid=(batch // nb, nc),
        in_specs=[
            pl.BlockSpec((nb, chunk, RWKV_PROJ), lambda b, c: (b, c, 0)),
            pl.BlockSpec((nb, 1, RWKV_PROJ), lambda b, c: (b, 0, 0)),
            state,
            vec(RWKV_PROJ), vec(W), mat(LORA_WIDTH, W), vec(W), mat(LORA_WIDTH, W), mat(LORA_WIDTH, W),
            vec(W), vec(W), vec(W), vec(W), vec(W), mat(W, W),
        ],
        out_specs=[pl.BlockSpec((nb, chunk, W), lambda b, c: (b, c, 0)), state],
        out_shape=[
            jax.ShapeDtypeStruct((batch, t_len, W), _mix_dtype(chunk)),
            jax.ShapeDtypeStruct((batch, RWKV_HEADS, RWKV_HEAD, RWKV_HEAD), F32),
        ],
        scratch_shapes=[pltpu.VMEM((nb, groups, W // groups, W // groups), F32),
                        pltpu.VMEM((nb, 1, RWKV_PROJ), F32)],
        compiler_params=_cparams(("parallel", "arbitrary")),
        name="rwkv7",
    )(p, shift_prev, s0, prm["mu"], prm["w0"], prm["w_up"], prm["a0"], prm["a_up"], prm["g_up"],
      prm["k_k"], prm["k_a"], prm["r_k"], prm["lnx_g"], prm["lnx_b"], prm["ones_bd"])


def _lru_kernel(x0_ref, x1_ref, g0_ref, g1_ref, conv_ref, h0_ref, cw_ref, cb_ref, wa_ref, ba_ref, wx_ref, bx_ref,
                lam_ref, y_ref, hout_ref, tail_scr, h_scr, *, chunk, t_valid):
    C = chunk
    c = pl.program_id(1)

    @pl.when(c == 0)
    def _():
        tail_scr[...] = jnp.zeros_like(tail_scr)
        tail_scr[SUBLANES - (CONV_W - 1):, :] = conv_ref[...]
        h_scr[...] = h0_ref[...]

    x = jnp.concatenate([x0_ref[...], x1_ref[...]], axis=1)
    gt = jnp.concatenate([g0_ref[...], g1_ref[...]], axis=1)
    row_in_group = lax.broadcasted_iota(jnp.int32, (C, 1), 0) % SUBLANES

    def rotate_in_groups(z, dlt):
        z3 = z.reshape(z.shape[0] // SUBLANES, SUBLANES, z.shape[1])
        return pltpu.roll(z3, dlt, axis=1).reshape(z.shape)

    x_prev = tail_scr[...]
    if C > SUBLANES:
        x_prev = jnp.concatenate([x_prev, x[:C - SUBLANES]], axis=0)
    tail_scr[...] = x[C - SUBLANES:, :]
    xc = cb_ref[...] + x * cw_ref[CONV_W - 1:CONV_W, :]
    for dlt in range(1, CONV_W):
        sh = jnp.where(row_in_group >= dlt, rotate_in_groups(x, dlt), rotate_in_groups(x_prev, dlt))
        xc = xc + sh * cw_ref[CONV_W - 1 - dlt:CONV_W - dlt, :]

    gate_r = jax.nn.sigmoid(_dot(xc, wa_ref[...]) + ba_ref[...])
    gate_i = jax.nn.sigmoid(_dot(xc, wx_ref[...]) + bx_ref[...])
    log_a = -RG_C * gate_r * _softplus(-lam_ref[...])
    a = jnp.exp(log_a)
    th = jnp.tanh(log_a)
    u = jnp.sqrt(-2.0 * th / (1.0 - th)) * (gate_i * xc)

    dlt = 1
    while dlt < SUBLANES:
        keep = row_in_group >= dlt
        a_sh = jnp.where(keep, rotate_in_groups(a, dlt), 1.0)
        u_sh = jnp.where(keep, rotate_in_groups(u, dlt), 0.0)
        u = a * u_sh + u
        a = a * a_sh
        dlt *= 2
    carry = h_scr[...]
    groups = []
    for gi in range(C // SUBLANES):
        rows = slice(gi * SUBLANES, (gi + 1) * SUBLANES)
        groups.append(a[rows] * carry + u[rows])
        carry = groups[-1][SUBLANES - 1:SUBLANES, :]
    h = jnp.concatenate(groups, axis=0)
    last = min(t_valid, C) - 1
    h_scr[...] = h[last:last + 1, :]
    hout_ref[...] = h[last:last + 1, :]
    y_ref[...] = (h * jax.nn.gelu(gt)).astype(y_ref.dtype)


def _lru(pxq, conv_prev, h0, prm, batch, t_len, t_valid, chunk):
    W = LRU_WIDTH
    nc = t_len // chunk
    half = W // 2
    col0 = RWKV_PROJ // half
    assert RWKV_PROJ % half == 0
    part = lambda k: pl.BlockSpec((chunk, half), lambda b, c: (b * nc + c, col0 + k))
    vec = lambda n: pl.BlockSpec((1, n), lambda b, c: (0, 0))
    kern = functools.partial(_lru_kernel, chunk=chunk, t_valid=t_valid)
    return pl.pallas_call(
        kern,
        grid=(batch, nc),
        in_specs=[
            part(0), part(1), part(2), part(3),
            pl.BlockSpec((None, CONV_W - 1, W), lambda b, c: (b, 0, 0)),
            pl.BlockSpec((None, 1, W), lambda b, c: (b, 0, 0)),
            pl.BlockSpec((CONV_W, W), lambda b, c: (0, 0)),
            vec(W),
            pl.BlockSpec((W, W), lambda b, c: (0, 0)), vec(W),
            pl.BlockSpec((W, W), lambda b, c: (0, 0)), vec(W),
            vec(W),
        ],
        out_specs=[
            pl.BlockSpec((chunk, W), lambda b, c: (b * nc + c, 0)),
            pl.BlockSpec((None, 1, W), lambda b, c: (b, 0, 0)),
        ],
        out_shape=[
            jax.ShapeDtypeStruct((batch * t_len, W), _mix_dtype(chunk)),
            jax.ShapeDtypeStruct((batch, 1, W), F32),
        ],
        scratch_shapes=[pltpu.VMEM((SUBLANES, W), F32), pltpu.VMEM((1, W), F32)],
        compiler_params=_cparams(("parallel", "arbitrary")),
        name="rglru",
    )(pxq, pxq, pxq, pxq, conv_prev, h0, prm["conv_w"], prm["conv_b"], prm["wa_bd"], prm["ba"], prm["wx_bd"],
      prm["bx"], prm["lam"])


def _largest_divisor(n, cap):
    return max(u for u in range(1, cap + 1) if n % u == 0)


def _attn_prompt_kernel(q_ref, kil_ref, vil_ref, bias_ref, o_ref, k_ref, v_ref, og_scr, lse_scr, *, seq):
    B = ATTN_SPAN
    scale = ATTN_HEAD ** -0.5
    h = pl.program_id(1)
    R = ATTN_MERGE_ROWS

    def gather_head(i, carry):
        dst = pl.ds(pl.multiple_of(i * R, R), R)
        src = pl.ds(h + i * (R * ATTN_HEADS), R, stride=ATTN_HEADS)
        k_ref[dst, :] = kil_ref[src, :]
        v_ref[dst, :] = vil_ref[src, :]
        return carry

    lax.fori_loop(0, seq // R, gather_head, 0)

    blocks = [(gi, dil, cls + blk * (B * dil), blk == 0)
              for gi, (win, dil) in enumerate(DIL_PAIRS)
              for cls in range(dil) for blk in range(seq // (dil * B))]
    batches = [blocks[i:i + ATTN_UNROLL] for i in range(0, len(blocks), ATTN_UNROLL)]

    def rows(dil, start, first):
        start, n = (start, B) if first else (start - B * dil, 2 * B)
        return pl.ds(start, n) if dil == 1 else pl.ds(start, n, stride=dil)

    def scores(batch):
        return [_dot_nt(q_ref[rows(dil, qs, True), :], k_ref[rows(dil, qs, first), :]) * scale
                + (bias_ref[gi, :, B:] if first else bias_ref[gi]) for gi, dil, qs, first in batch]

    def softmax(ss):
        out = []
        for s in ss:
            mx = jnp.max(s, axis=-1, keepdims=True)
            pr = jnp.exp(s - mx)
            den = jnp.sum(pr, axis=-1, keepdims=True)
            out.append(((pr / den).astype(BF16), mx + jnp.log(den)))
        return out

    def values_and_store(batch, probs):
        outs = [_dot(pn, v_ref[rows(dil, qs, first), :]) for (gi, dil, qs, first), (pn, _) in zip(batch, probs)]
        for (gi, dil, qs, _), o, (_, lse) in zip(batch, outs, probs):
            og_scr[gi, rows(dil, qs, True), :] = o
            lse_scr[gi, rows(dil, qs, True), :] = jnp.broadcast_to(lse, (B, ATTN_HEAD))

    pending = scores(batches[0])
    for bi, batch in enumerate(batches):
        upcoming = scores(batches[bi + 1]) if bi + 1 < len(batches) else None
        values_and_store(batch, softmax(pending))
        pending = upcoming

    def merge(i, carry):
        sl = pl.ds(pl.multiple_of(i * R, R), R)
        ls = [lse_scr[gi, sl, :] for gi in range(len(DIL_PAIRS))]
        mx = functools.reduce(jnp.maximum, ls)
        ws = [jnp.exp(l - mx) for l in ls]
        tot = functools.reduce(lambda a, b: a + b, ws)
        acc = (ws[0] / tot) * og_scr[0, sl, :]
        for gi in range(1, len(DIL_PAIRS)):
            acc = acc + (ws[gi] / tot) * og_scr[gi, sl, :]
        o_ref[sl, :] = acc.astype(o_ref.dtype)
        return carry

    lax.fori_loop(0, seq // R, merge, 0)


def _attn_prompt(pxq, k_all, v_all, bias, layer, batch, seq):
    H = ATTN_HEADS
    G = len(DIL_PAIRS)
    q_col0 = (PXQ_WIDTH - ATTN_WIDTH) // ATTN_HEAD
    kern = functools.partial(_attn_prompt_kernel, seq=seq)
    kv = pl.BlockSpec((seq * H, ATTN_HEAD), lambda b, h: (layer * batch + b, 0))
    return pl.pallas_call(
        kern,
        grid=(batch, H),
        in_specs=[
            pl.BlockSpec((seq, ATTN_HEAD), lambda b, h: (b, q_col0 + h)),
            kv, kv,
            pl.BlockSpec((G, None, ATTN_SPAN, 2 * ATTN_SPAN), lambda b, h: (0, h, 0, 0)),
        ],
        out_specs=pl.BlockSpec((seq, ATTN_HEAD), lambda b, h: (b, h)),
        out_shape=jax.ShapeDtypeStruct((batch * seq, ATTN_WIDTH), BF16),
        scratch_shapes=[pltpu.VMEM((seq, ATTN_HEAD), F32), pltpu.VMEM((seq, ATTN_HEAD), F32),
                        pltpu.VMEM((G, seq, ATTN_HEAD), F32), pltpu.VMEM((G, seq, ATTN_HEAD), F32)],
        compiler_params=_cparams(("parallel", "arbitrary")),
        name="attn_prompt",
    )(pxq, k_all, v_all, bias)


def _attn_sample_kernel(pxq_ref, kn_ref, vn_ref, kc_ref, vc_ref, bias_ref, o_ref, q_scr, kn_scr, vn_scr, *, t_new, win):
    scale = ATTN_HEAD ** -0.5
    H = ATTN_HEADS
    G = len(DIL_PAIRS)
    TP = q_scr.shape[0]
    q_col0 = PXQ_WIDTH - ATTN_WIDTH
    q_scr[...] = jnp.zeros_like(q_scr)
    kn_scr[...] = jnp.zeros_like(kn_scr)
    vn_scr[...] = jnp.zeros_like(vn_scr)
    for h in range(H):
        lanes = slice(h * ATTN_HEAD, (h + 1) * ATTN_HEAD)
        q_scr[0:t_new, :] = pxq_ref[:, q_col0 + h * ATTN_HEAD:q_col0 + (h + 1) * ATTN_HEAD]
        kn_scr[0:t_new, :] = kn_ref[pl.ds(h, t_new, stride=H), :]
        vn_scr[0:t_new, :] = vn_ref[pl.ds(h, t_new, stride=H), :]
        qb = q_scr[...].astype(BF16)
        kc = kc_ref[pl.ds(h, win, stride=H), :]
        vc = vc_ref[pl.ds(h, win, stride=H), :]
        s = jnp.concatenate([_dot_nt(qb, kc), _dot_nt(qb, kn_scr[...])], axis=1) * scale
        prs, lses = [], []
        for gi in range(G):
            sg = s + bias_ref[gi, h]
            mx = jnp.max(sg, axis=-1, keepdims=True)
            pr = jnp.exp(sg - mx)
            den = jnp.sum(pr, axis=-1, keepdims=True)
            prs.append(pr / den)
            lses.append(mx + jnp.log(den))
        pall = jnp.concatenate(prs, axis=0)
        oall = _dot(pall[:, :win], vc) + _dot(pall[:, win:], vn_scr[...])
        mx = functools.reduce(jnp.maximum, lses)
        ws = [jnp.exp(l - mx) for l in lses]
        tot = functools.reduce(lambda a, b: a + b, ws)
        acc = (ws[0] / tot) * oall[0:TP]
        for gi in range(1, G):
            acc = acc + (ws[gi] / tot) * oall[gi * TP:(gi + 1) * TP]
        o_ref[:, lanes] = acc[0:t_new].astype(o_ref.dtype)


def _attn_sample(pxq, k_new, v_new, k_cache, v_cache, bias, layer, batch, t_new):
    H = ATTN_HEADS
    win = k_cache.shape[2] // H
    tp = bias.shape[2]
    kern = functools.partial(_attn_sample_kernel, t_new=t_new, win=win)
    cache = pl.BlockSpec((None, None, win * H, ATTN_HEAD), lambda b: (layer, b, 0, 0))
    new = pl.BlockSpec((None, t_new * H, ATTN_HEAD), lambda b: (layer * batch + b, 0, 0))
    return pl.pallas_call(
        kern,
        grid=(batch,),
        in_specs=[pl.BlockSpec((None, t_new, pxq.shape[2]), lambda b: (b, 0, 0)), new, new, cache, cache,
                  pl.BlockSpec(bias.shape, lambda b: (0, 0, 0, 0))],
        out_specs=pl.BlockSpec((None, t_new, ATTN_WIDTH), lambda b: (b, 0, 0)),
        out_shape=jax.ShapeDtypeStruct((batch, t_new, ATTN_WIDTH), F32),
        scratch_shapes=[pltpu.VMEM((tp, ATTN_HEAD), F32), pltpu.VMEM((LANES, ATTN_HEAD), F32),
                        pltpu.VMEM((LANES, ATTN_HEAD), F32)],
        compiler_params=_cparams(("parallel",)),
        name="attn_sample",
    )(pxq, k_new, v_new, k_cache, v_cache, bias)


def _bias_kernel(idx_ref, rb_ref, o_ref):
    h = pl.program_id(0)
    idx = idx_ref[...]
    acc = jnp.full(idx.shape, NEG, F32)
    for b in range(N_BUCKETS):
        acc = jnp.where(idx == b, rb_ref[b, h], acc)
    o_ref[...] = acc


def _bias_table(idx, rel_bias):
    g, r, c = idx.shape
    heads = rel_bias.shape[1]
    return pl.pallas_call(
        _bias_kernel,
        grid=(heads,),
        in_specs=[pl.BlockSpec((g, r, c), lambda h: (0, 0, 0)),
                  pl.BlockSpec(memory_space=pltpu.SMEM)],
        out_specs=pl.BlockSpec((g, None, r, c), lambda h: (0, h, 0, 0)),
        out_shape=jax.ShapeDtypeStruct((g, heads, r, c), F32),
        compiler_params=_cparams(("parallel",)),
        name="bias_table",
    )(jnp.asarray(idx, jnp.int32), rel_bias.astype(F32))


def _t5_bucket_static(dist):
    dist = np.asarray(dist, np.int64)
    exact = N_BUCKETS // 2
    d = np.maximum(dist, 1).astype(np.float64)
    large = exact + (np.log(d / exact) / math.log(REL_MAX_DIST / exact) * (N_BUCKETS - exact)).astype(np.int64)
    return np.where(dist < exact, dist, np.minimum(large, N_BUCKETS - 1))


def _prompt_bias_idx():
    B = ATTN_SPAN
    qi = np.arange(B)[:, None]
    kj = np.arange(2 * B)[None, :]
    delta = qi + B - kj
    valid = (delta >= 0) & (delta <= B)
    return np.stack([np.where(valid, _t5_bucket_static(np.clip(delta, 0, B) * dil), -1) for _, dil in DIL_PAIRS])


def _sample_bias_idx(t_new, t_pad, win, total):
    r = np.arange(total)[None, :]
    t = np.arange(t_pad)[:, None]
    dist = win + t - r
    tabs = []
    for _, dil in DIL_PAIRS:
        assert win >= dil * ATTN_SPAN
        valid = (dist >= 0) & (dist % dil == 0) & (dist <= dil * ATTN_SPAN) & (t < t_new)
        tabs.append(np.where(valid, _t5_bucket_static(np.clip(dist, 0, dil * ATTN_SPAN)), -1))
    return np.stack(tabs)


def _block_diag(blocks):
    n, c, d = blocks.shape
    eye = jnp.eye(n, dtype=blocks.dtype)
    return jnp.einsum("ncd,nm->ncmd", blocks, eye).reshape(n * c, n * d)


def _layer_params(l, a):
    W = RWKV_WIDTH
    row = lambda t: t[l].reshape(1, -1).astype(F32)

    def lora_pad(w, off):
        z = jnp.zeros((LORA_WIDTH, W), F32)
        return z.at[off:off + w.shape[0]].set(w).astype(BF16)

    head_of = np.arange(W) // RWKV_HEAD
    ones_bd = jnp.asarray((head_of[:, None] == head_of[None, :]).astype(np.float32), BF16)
    rwkv = dict(
        mu=row(a["rwkv_mu"]), w0=row(a["rwkv_w0"]), a0=row(a["rwkv_a0"]),
        w_up=lora_pad(a["rwkv_w_up"][l], 0),
        a_up=lora_pad(a["rwkv_a_up"][l], DECAY_LORA),
        g_up=lora_pad(a["rwkv_g_up"][l], DECAY_LORA + ICLR_LORA),
        k_k=row(a["rwkv_k_k"]), k_a=row(a["rwkv_k_a"]), r_k=row(a["rwkv_r_k"]),
        lnx_g=row(a["rwkv_lnx_g"]), lnx_b=row(a["rwkv_lnx_b"]),
        ones_bd=ones_bd,
    )
    lru = dict(
        conv_w=a["lru_conv_w"][l].astype(F32), conv_b=row(a["lru_conv_b"]),
        wa_bd=_block_diag(a["lru_wa"][l]).astype(BF16), ba=row(a["lru_ba"]),
        wx_bd=_block_diag(a["lru_wx"][l]).astype(BF16), bx=row(a["lru_bx"]),
        lam=row(a["lru_lambda"]),
    )
    return dict(rwkv=rwkv, lru=lru)


def _shared_params(a):
    gain = lambda t: t.reshape(t.shape[0], 1, -1).astype(F32)
    return dict(
        g_mix_pre=gain(a["norm_mix_pre"]), g_mix_post=gain(a["norm_mix_post"]),
        g_ffn_pre=gain(a["norm_ffn_pre"]), g_ffn_post=gain(a["norm_ffn_post"]),
    )


def _layer(x, layer, sp, wts, lp, batch, t_len, shift_prev, s0, conv_prev, h0, kv_all, attend):
    emit = wts is None
    src = sp["f32"] if emit else wts
    wl = layer if emit else 0
    copies = {}
    pxq, k_all, v_all, *w_copy = _in_proj(x, sp["g_mix_pre"], src["w_in"], layer, wl, *kv_all, emit_bf16=emit)
    copies["w_in"] = w_copy[0] if emit else None
    pxq3 = pxq.reshape(batch, t_len, pxq.shape[1])
    if t_len % RWKV_CHUNK == 0:
        y_a, s_fin = _rwkv(pxq3, shift_prev, s0, lp["rwkv"], batch, t_len, t_len, RWKV_CHUNK)
        y_a = y_a.reshape(batch * t_len, -1)
        y_b, h_fin = _lru(pxq, conv_prev, h0, lp["lru"], batch, t_len, t_len, LRU_CHUNK)
    else:
        assert t_len <= SUBLANES
        padded = jnp.pad(pxq3, ((0, 0), (0, SUBLANES - t_len), (0, 0)))
        y_a, s_fin = _rwkv(padded, shift_prev, s0, lp["rwkv"], batch, SUBLANES, t_len, SUBLANES)
        y_a = y_a[:, :t_len].reshape(batch * t_len, -1)
        y_b, h_fin = _lru(padded.reshape(batch * SUBLANES, pxq.shape[1]), conv_prev, h0, lp["lru"], batch, SUBLANES,
                          t_len, SUBLANES)
        y_b = y_b.reshape(batch, SUBLANES, -1)[:, :t_len].reshape(batch * t_len, -1)
    y_c = attend(pxq, k_all, v_all)

    x = _out_proj(y_a, y_b, y_c, src["w_out"], sp["g_mix_post"], x, layer, wl, emit_bf16=emit)
    if emit:
        x, copies["w_out"] = x
    x = _ffn(x, sp["g_ffn_pre"], src["w1"], src["w2"], sp["g_ffn_post"], layer, wl, emit_bf16=emit)
    if emit:
        x, copies["w1"], copies["w2"] = x

    lru_x = pxq3[:, :, RWKV_PROJ:RWKV_PROJ + LRU_WIDTH]
    if t_len >= CONV_W - 1:
        conv_new = lru_x[:, t_len - (CONV_W - 1):]
    else:
        conv_new = jnp.concatenate([conv_prev, lru_x], axis=1)[:, -(CONV_W - 1):]
    state = (pxq3[:, -1, :RWKV_PROJ], s_fin, conv_new, h_fin.reshape(batch, LRU_WIDTH))
    return x, state, (k_all, v_all), copies


def kernel(x_prompt, x_sample, state_rwkv_wkv, state_rwkv_shift, state_lru_h, state_lru_conv, cache_attn_k, cache_attn_v, rel_bias, norm_mix_pre, norm_mix_post, norm_ffn_pre, norm_ffn_post, w_in, w_out, rwkv_mu, rwkv_w0, rwkv_w_up, rwkv_a0, rwkv_a_up, rwkv_g_up, rwkv_k_k, rwkv_k_a, rwkv_r_k, rwkv_lnx_g, rwkv_lnx_b, lru_conv_w, lru_conv_b, lru_wa, lru_ba, lru_wx, lru_bx, lru_lambda, ffn_w1, ffn_w2):
    a = dict(norm_mix_pre=norm_mix_pre, norm_mix_post=norm_mix_post, norm_ffn_pre=norm_ffn_pre,
             norm_ffn_post=norm_ffn_post, w_in=w_in, w_out=w_out, rwkv_mu=rwkv_mu, rwkv_w0=rwkv_w0,
             rwkv_w_up=rwkv_w_up, rwkv_a0=rwkv_a0, rwkv_a_up=rwkv_a_up, rwkv_g_up=rwkv_g_up, rwkv_k_k=rwkv_k_k,
             rwkv_k_a=rwkv_k_a, rwkv_r_k=rwkv_r_k, rwkv_lnx_g=rwkv_lnx_g, rwkv_lnx_b=rwkv_lnx_b,
             lru_conv_w=lru_conv_w, lru_conv_b=lru_conv_b, lru_wa=lru_wa, lru_ba=lru_ba, lru_wx=lru_wx,
             lru_bx=lru_bx, lru_lambda=lru_lambda, ffn_w1=ffn_w1, ffn_w2=ffn_w2)
    depth = w_in.shape[0]
    pb, seq, _ = x_prompt.shape
    sb, t_new, _ = x_sample.shape
    win = cache_attn_k.shape[2]
    keep = min(ATTN_WINDOW, seq)
    total = win + LANES
    assert t_new <= LANES and seq % (DIL_PAIRS[-1][1] * ATTN_SPAN) == 0

    bias_p = _bias_table(_prompt_bias_idx(), rel_bias)
    bias_s = _bias_table(_sample_bias_idx(t_new, SUBLANES, win, total), rel_bias)
    k_cache = cache_attn_k.reshape(depth, sb, win * ATTN_HEADS, ATTN_HEAD)
    v_cache = cache_attn_v.reshape(depth, sb, win * ATTN_HEADS, ATTN_HEAD)

    xp = x_prompt.reshape(pb * seq, D_MODEL)
    xs = x_sample.reshape(sb * t_new, D_MODEL)
    new_p, new_s = [], []
    sp = _shared_params(a)
    kv_p = tuple(jnp.zeros((depth * pb * seq * ATTN_HEADS, ATTN_HEAD), F32) for _ in range(2))
    kv_s = tuple(jnp.zeros((depth * sb * t_new * ATTN_HEADS, ATTN_HEAD), F32) for _ in range(2))
    sp["f32"] = dict(w_in=w_in.astype(F32), w_out=w_out.astype(F32), w1=ffn_w1.astype(F32), w2=ffn_w2.astype(F32))
    for l in range(depth):
        lp = _layer_params(l, a)
        attend_s = lambda pxq, k_all, v_all, l=l: _attn_sample(
            pxq.reshape(sb, t_new, pxq.shape[1]), k_all.reshape(depth * sb, t_new * ATTN_HEADS, ATTN_HEAD),
            v_all.reshape(depth * sb, t_new * ATTN_HEADS, ATTN_HEAD), k_cache, v_cache, bias_s, l, sb, t_new,
        ).reshape(sb * t_new, ATTN_WIDTH)
        xs, st_s, kv_s, wts = _layer(xs, l, sp, None, lp, sb, t_new,
                                     state_rwkv_shift[l].reshape(sb, 1, RWKV_PROJ), state_rwkv_wkv[l].astype(F32),
                                     state_lru_conv[l], state_lru_h[l].reshape(sb, 1, LRU_WIDTH), kv_s, attend_s)
        attend_p = lambda pxq, k_all, v_all, l=l: _attn_prompt(pxq, k_all, v_all, bias_p, l, pb, seq)
        xp, st_p, kv_p, _ = _layer(xp, l, sp, wts, lp, pb, seq,
                                   jnp.zeros((pb, 1, RWKV_PROJ), F32),
                                   jnp.zeros((pb, RWKV_HEADS, RWKV_HEAD, RWKV_HEAD), F32),
                                   jnp.zeros((pb, CONV_W - 1, LRU_WIDTH), F32), jnp.zeros((pb, 1, LRU_WIDTH), F32),
                                   kv_p, attend_p)
        new_p.append(st_p)
        new_s.append(st_s)

    stack = lambda sts, i: jnp.stack([s[i] for s in sts])
    k_p, v_p = (t.reshape(depth, pb, seq, ATTN_HEADS, ATTN_HEAD)[:, :, seq - keep:] for t in kv_p)
    k_s, v_s = (t.reshape(depth, sb, t_new, ATTN_HEADS, ATTN_HEAD) for t in kv_s)
    return (xp.reshape(pb, seq, D_MODEL), xs.reshape(sb, t_new, D_MODEL),
            stack(new_p, 1), stack(new_s, 1), stack(new_p, 0), stack(new_s, 0),
            stack(new_p, 3), stack(new_s, 3), stack(new_p, 2), stack(new_s, 2),
            k_p, k_s, v_p, v_s)
```

```python
import functools
import math

import numpy as np
import jax
import jax.numpy as jnp
from jax import lax
from jax.experimental import pallas as pl
from jax.experimental.pallas import tpu as pltpu

F32 = jnp.float32
BF16 = jnp.bfloat16

D_MODEL = 2048
RWKV_WIDTH = 512
RWKV_HEAD = 64
RWKV_HEADS = RWKV_WIDTH // RWKV_HEAD
DECAY_LORA = 64
ICLR_LORA = 64
GATE_LORA = 128
LORA_WIDTH = DECAY_LORA + ICLR_LORA + GATE_LORA
RWKV_PROJ = 3 * RWKV_WIDTH + LORA_WIDTH
GN_EPS = 64e-5
LRU_WIDTH = 512
LRU_BLOCKS = 8
LRU_BLOCK = LRU_WIDTH // LRU_BLOCKS
CONV_W = 4
RG_C = 8.0
ATTN_WIDTH = 1024
ATTN_HEAD = 128
ATTN_HEADS = ATTN_WIDTH // ATTN_HEAD
DIL_PAIRS = ((128, 1), (512, 4), (2048, 16))
ATTN_SPAN = 128
ATTN_WINDOW = 2048
N_BUCKETS = 32
REL_MAX_DIST = ATTN_WINDOW
D_FF = 4 * D_MODEL
RMS_EPS = 1e-6
NEG = -1e30

LANES = 128
SUBLANES = 8
MXU_TILE = 256
VMEM_LIMIT_BYTES = 56 * 1024 * 1024

ROW_TILE = 1024
IN_COL_TILE = 256
IN_FUSED_TILES = 2
OUT_ROW_TILE = 512
FFN_ROW_TILE = 512
FFN_COL_TILE = 1024
FFN_CAST_COL_TILE = 512
RWKV_CHUNK = 64
RWKV_SEQS_PER_STEP = 4
LRU_CHUNK = 256
ATTN_MERGE_ROWS = 256
ATTN_UNROLL = 6


def _cparams(sem):
    return pltpu.CompilerParams(dimension_semantics=sem, vmem_limit_bytes=VMEM_LIMIT_BYTES)


def _dot(a, b):
    return jnp.dot(a.astype(BF16), b.astype(BF16), preferred_element_type=F32)


def _dot_nt(a, b):
    return lax.dot_general(a.astype(BF16), b.astype(BF16), (((1,), (1,)), ((), ())), preferred_element_type=F32)


def _dot_tn(a, b):
    return lax.dot_general(a.astype(BF16), b.astype(BF16), (((0,), (0,)), ((), ())), preferred_element_type=F32)


def _softplus(z):
    return jnp.maximum(z, 0.0) + jnp.log1p(jnp.exp(-jnp.abs(z)))


def _mix_dtype(rows):
    return BF16 if rows % (2 * SUBLANES) == 0 else F32


def _rms(x, g):
    ms = jnp.mean(x * x, axis=-1, keepdims=True)
    return x * lax.rsqrt(ms + RMS_EPS) * g


PXQ_WIDTH = RWKV_PROJ + 2 * LRU_WIDTH + ATTN_WIDTH
PXQ_TILES = PXQ_WIDTH // IN_COL_TILE
KV_TILES = ATTN_WIDTH // IN_COL_TILE
HEADS_PER_TILE = IN_COL_TILE // ATTN_HEAD


def _in_proj_kernel(x_ref, g_ref, *refs, fuse):
    w_refs = refs[:fuse]
    pxq_ref, k_ref, v_ref = refs[fuse + 2:fuse + 5]
    rest = refs[fuse + 5:]
    h_scr = rest[-1]
    j = pl.program_id(1)
    tm = x_ref.shape[0]
    pxq_steps = pl.cdiv(PXQ_TILES, fuse)
    kv_steps = KV_TILES // fuse

    @pl.when(j == 0)
    def _():
        h_scr[...] = _rms(x_ref[...], g_ref[...]).astype(BF16)

    ws = [w_ref[...].astype(BF16) for w_ref in w_refs]
    if len(rest) > 1:
        rest[0][...] = ws[0]
    w = ws[0] if fuse == 1 else jnp.concatenate(ws, axis=1)
    acc = jnp.dot(h_scr[...], w, preferred_element_type=F32)

    @pl.when(j < pxq_steps)
    def _():
        pxq_ref[...] = acc

    def scatter_heads(o_ref, step):
        for hh in range(fuse * HEADS_PER_TILE):
            head = step * (fuse * HEADS_PER_TILE) + hh
            o_ref[pl.ds(head, tm, stride=ATTN_HEADS), :] = acc[:, hh * ATTN_HEAD:(hh + 1) * ATTN_HEAD]

    @pl.when((j >= pxq_steps) & (j < pxq_steps + kv_steps))
    def _():
        scatter_heads(k_ref, j - pxq_steps)

    @pl.when(j >= pxq_steps + kv_steps)
    def _():
        scatter_heads(v_ref, j - pxq_steps - kv_steps)


def _in_proj(x, g, w, layer, w_layer, k_all, v_all, emit_bf16=False):
    m, d = x.shape
    n = w.shape[2]
    assert n == PXQ_WIDTH + 2 * ATTN_WIDTH
    tm = min(ROW_TILE, m)
    row_tiles = m // tm
    assert not emit_bf16 or row_tiles == 1
    fuse = 1 if emit_bf16 else IN_FUSED_TILES
    assert KV_TILES % fuse == 0
    pxq_steps = pl.cdiv(PXQ_TILES, fuse)
    spare = pxq_steps * fuse - PXQ_TILES
    width = fuse * IN_COL_TILE

    def w_spec(slot):
        def index(i, j):
            tile = jnp.where(j < pxq_steps, jnp.minimum(fuse * j + slot, PXQ_TILES - 1), fuse * j + slot - spare)
            return (w_layer, 0, tile)
        return pl.BlockSpec((None, d, IN_COL_TILE), index)

    kv_spec = pl.BlockSpec((tm * ATTN_HEADS, ATTN_HEAD), lambda i, j: (layer * row_tiles + i, 0))
    out_specs = [
        pl.BlockSpec((tm, width), lambda i, j: (i, jnp.minimum(j, pxq_steps - 1))),
        kv_spec, kv_spec,
    ]
    out_shape = [
        jax.ShapeDtypeStruct((m, pxq_steps * width), F32),
        jax.ShapeDtypeStruct(k_all.shape, F32),
        jax.ShapeDtypeStruct(v_all.shape, F32),
    ]
    if emit_bf16:
        out_specs.append(pl.BlockSpec((None, d, IN_COL_TILE), lambda i, j: (0, 0, j)))
        out_shape.append(jax.ShapeDtypeStruct((1, d, n), BF16))
    return pl.pallas_call(
        functools.partial(_in_proj_kernel, fuse=fuse),
        grid=(row_tiles, pxq_steps + 2 * (KV_TILES // fuse)),
        in_specs=[
            pl.BlockSpec((tm, d), lambda i, j: (i, 0)),
            pl.BlockSpec((None, 1, d), lambda i, j: (layer, 0, 0)),
            *[w_spec(slot) for slot in range(fuse)],
            pl.BlockSpec(memory_space=pl.ANY),
            pl.BlockSpec(memory_space=pl.ANY),
        ],
        out_specs=out_specs,
        out_shape=out_shape,
        input_output_aliases={2 + fuse: 1, 3 + fuse: 2},
        scratch_shapes=[pltpu.VMEM((tm, d), BF16)],
        compiler_params=_cparams(("parallel", "arbitrary")),
        name="in_proj",
    )(x, g, *([w] * fuse), k_all, v_all)


def _out_proj_kernel(ya_ref, yb_ref, yc_ref, w_ref, g_ref, x_ref, o_ref, *w_copy):
    c1, c2 = RWKV_WIDTH, RWKV_WIDTH + LRU_WIDTH
    if w_copy:
        acc = None
        for y_ref, rows in [(ya_ref, slice(0, c1)), (yb_ref, slice(c1, c2)), (yc_ref, slice(c2, None))]:
            w = w_ref[rows, :].astype(BF16)
            w_copy[0][rows, :] = w
            t = jnp.dot(y_ref[...].astype(BF16), w, preferred_element_type=F32)
            acc = t if acc is None else acc + t
    else:
        y = jnp.concatenate([r[...].astype(BF16) for r in (ya_ref, yb_ref, yc_ref)], axis=1)
        acc = jnp.dot(y, w_ref[...].astype(BF16), preferred_element_type=F32)
    o_ref[...] = x_ref[...] + _rms(acc, g_ref[...])


def _out_proj(ya, yb, yc, w, g, x, layer, w_layer, emit_bf16=False):
    m, d = x.shape
    tm = min(OUT_ROW_TILE, m)
    assert not emit_bf16 or m == tm
    out_specs = [pl.BlockSpec((tm, d), lambda i: (i, 0))]
    out_shape = [jax.ShapeDtypeStruct((m, d), F32)]
    if emit_bf16:
        out_specs.append(pl.BlockSpec((None, d, d), lambda i: (0, 0, 0)))
        out_shape.append(jax.ShapeDtypeStruct((1, d, d), BF16))
    res = pl.pallas_call(
        _out_proj_kernel,
        grid=(m // tm,),
        in_specs=[
            pl.BlockSpec((tm, RWKV_WIDTH), lambda i: (i, 0)),
            pl.BlockSpec((tm, LRU_WIDTH), lambda i: (i, 0)),
            pl.BlockSpec((tm, ATTN_WIDTH), lambda i: (i, 0)),
            pl.BlockSpec((None, d, d), lambda i: (w_layer, 0, 0)),
            pl.BlockSpec((None, 1, d), lambda i: (layer, 0, 0)),
            pl.BlockSpec((tm, d), lambda i: (i, 0)),
        ],
        out_specs=out_specs,
        out_shape=out_shape,
        compiler_params=_cparams(("parallel",)),
        name="out_proj",
    )(ya, yb, yc, w, g, x)
    return res if emit_bf16 else res[0]


def _ffn_kernel(x_ref, g1_ref, w1_ref, w2_ref, g2_ref, o_ref, *rest):
    h_scr, acc_scr = rest[-2:]
    j = pl.program_id(1)

    @pl.when(j == 0)
    def _():
        h_scr[...] = _rms(x_ref[...], g1_ref[...]).astype(BF16)
        acc_scr[...] = jnp.zeros_like(acc_scr)

    w1 = w1_ref[...].astype(BF16)
    w2 = w2_ref[...].astype(BF16)
    if len(rest) > 2:
        rest[0][...] = w1
        rest[1][...] = w2
    u = jnp.dot(h_scr[...], w1, preferred_element_type=F32)
    u = jnp.square(jnp.maximum(u, 0.0)).astype(BF16)
    acc_scr[...] += jnp.dot(u, w2, preferred_element_type=F32)

    @pl.when(j == pl.num_programs(1) - 1)
    def _():
        o_ref[...] = x_ref[...] + _rms(acc_scr[...], g2_ref[...])


def _ffn(x, g1, w1, w2, g2, layer, w_layer, emit_bf16=False):
    m, d = x.shape
    f = w1.shape[2]
    tm = min(FFN_ROW_TILE, m)
    tf = FFN_CAST_COL_TILE if emit_bf16 else FFN_COL_TILE
    assert not emit_bf16 or m == tm
    out_specs = [pl.BlockSpec((tm, d), lambda i, j: (i, 0))]
    out_shape = [jax.ShapeDtypeStruct((m, d), F32)]
    if emit_bf16:
        out_specs += [pl.BlockSpec((None, d, tf), lambda i, j: (0, 0, j)),
                      pl.BlockSpec((None, tf, d), lambda i, j: (0, j, 0))]
        out_shape += [jax.ShapeDtypeStruct((1, d, f), BF16), jax.ShapeDtypeStruct((1, f, d), BF16)]
    res = pl.pallas_call(
        _ffn_kernel,
        grid=(m // tm, f // tf),
        in_specs=[
            pl.BlockSpec((tm, d), lambda i, j: (i, 0)),
            pl.BlockSpec((None, 1, d), lambda i, j: (layer, 0, 0)),
            pl.BlockSpec((None, d, tf), lambda i, j: (w_layer, 0, j)),
            pl.BlockSpec((None, tf, d), lambda i, j: (w_layer, j, 0)),
            pl.BlockSpec((None, 1, d), lambda i, j: (layer, 0, 0)),
        ],
        out_specs=out_specs,
        out_shape=out_shape,
        scratch_shapes=[pltpu.VMEM((tm, d), BF16), pltpu.VMEM((tm, d), F32)],
        compiler_params=_cparams(("parallel", "arbitrary")),
        name="ffn",
    )(x, g1, w1, w2, g2)
    return res if emit_bf16 else res[0]


def _rwkv_groups(chunk):
    lanes = RWKV_HEADS * chunk
    return lanes // MXU_TILE if lanes % (2 * MXU_TILE) == 0 else 1


def _rwkv_kernel(p_ref, shift_ref, s0_ref, mu_ref, w0_ref, wup_ref, a0_ref, aup_ref, gup_ref, kk_ref, ka_ref,
                 rk_ref, lg_ref, lb_ref, ones_ref,
                 y_ref, sout_ref, s_scr, prev_scr, *, chunk, t_valid):
    C = chunk
    W = RWKV_WIDTH
    H = RWKV_HEADS
    N = RWKV_HEAD
    nb = p_ref.shape[0]
    c = pl.program_id(1)

    G = _rwkv_groups(C)
    hpg = H // G
    wl = W // G
    gc = hpg * C

    @pl.when(c == 0)
    def _():
        s_scr[...] = jnp.zeros_like(s_scr)
        for h in range(H):
            o = (h % hpg) * N
            s_scr[:, h // hpg, o:o + N, o:o + N] = s0_ref[:, h]
        prev_scr[...] = shift_ref[...]

    inv_n = 1.0 / RWKV_HEAD
    shared = dict(wup=[wup_ref], aup=[aup_ref], gup=[gup_ref],
                  ones=[ones_ref.at[g * wl:(g + 1) * wl, g * wl:(g + 1) * wl] for g in range(G)])

    def hi_lo(x):
        hi = x.astype(BF16).astype(F32)
        return [hi, x - hi]

    row = lax.broadcasted_iota(jnp.int32, (C, 1), 0)

    def cumsum_rows(z):
        dlt = 1
        while dlt < C:
            z = z + jnp.where(row >= dlt, pltpu.roll(z, dlt, axis=0), 0.0)
            dlt *= 2
        return z

    ti = lax.broadcasted_iota(jnp.int32, (C, H * C), 0)
    si = lax.broadcasted_iota(jnp.int32, (C, H * C), 1) % C
    strict = ti > si
    incl = ti >= si
    eye = (ti == si).astype(F32)
    blk_r = lax.broadcasted_iota(jnp.int32, (gc, 1), 0) // C
    mask_ch = blk_r == lax.broadcasted_iota(jnp.int32, (1, wl), 1) // N
    mask_cc = blk_r == lax.broadcasted_iota(jnp.int32, (1, gc), 1) // C
    mask_ss = (lax.broadcasted_iota(jnp.int32, (wl, 1), 0) // N
               == lax.broadcasted_iota(jnp.int32, (1, wl), 1) // N)

    def block_diag(x, mask, width):
        out = []
        for g in range(G):
            tiled = jnp.concatenate([x[:, g * width:(g + 1) * width]] * hpg, axis=0)
            out.append(jnp.where(mask, tiled, 0.0).astype(BF16))
        return out

    def per_head(a_cat, bds):
        a_b = a_cat.astype(BF16)
        return jnp.concatenate([jnp.dot(a_b[:, g * gc:(g + 1) * gc], bds[g], preferred_element_type=F32)
                                for g in range(G)], axis=1)

    def one_sequence(bi):
        p = p_ref[bi]
        shifted = jnp.where(row == 0, prev_scr[bi], pltpu.roll(p, 1, axis=0))
        prev_scr[bi] = p_ref[bi, C - 1:C, :]
        m = p + (shifted - p) * mu_ref[...]
        r = m[:, 0:W]
        k = m[:, W:2 * W]
        v = m[:, 2 * W:3 * W]
        x = m[:, 3 * W:]
        lw, la, gate = yield [("wup", jnp.tanh(x)), ("aup", x), ("gup", jax.nn.sigmoid(x))]
        w = w0_ref[...] + lw
        a = jax.nn.sigmoid(a0_ref[...] + la)
        softplus_neg_w = jnp.maximum(-w, 0.0) + jnp.log(1.0 + jnp.exp(-jnp.abs(w)))
        loga = -jnp.exp(-softplus_neg_w - 0.5)
        kk = k * kk_ref[...]
        k2 = k * (1.0 + (a - 1.0) * ka_ref[...])
        (ss,) = yield [("ones", jnp.concatenate(hi_lo(kk * kk) + hi_lo(r * k2 * rk_ref[...]), axis=0))]
        kk = kk / jnp.maximum(jnp.sqrt(ss[0:C] + ss[C:2 * C]), 1e-12)
        bonus = (ss[2 * C:3 * C] + ss[3 * C:]) * v
        if t_valid < C:
            live = row < t_valid
            loga = jnp.where(live, loga, 0.0)
            kk = jnp.where(live, kk, 0.0)
            k2 = jnp.where(live, k2, 0.0)
        cl = cumsum_rows(loga)
        cl_last = cl[C - 1:C, :]
        e_neg = jnp.exp(-cl)
        e_rem = jnp.exp(cl_last - cl)
        kka = kk * a
        al = -kk * jnp.exp(cl - loga)
        rt = r * jnp.exp(cl)
        be_bd = block_diag(kka * e_neg, mask_ch, wl)
        kt_bd = block_diag(k2 * e_neg, mask_ch, wl)
        bh = kka * e_rem
        kh = k2 * e_rem
        lhs = jnp.concatenate([al, rt], axis=0).astype(BF16)
        lhs_g = [lhs[:, g * wl:(g + 1) * wl] for g in range(G)]
        g_b = jnp.concatenate([_dot_nt(lhs_g[g], be_bd[g]) for g in range(G)], axis=1)
        g_k = jnp.concatenate([_dot_nt(lhs_g[g], kt_bd[g]) for g in range(G)], axis=1)
        n_cat = jnp.where(strict, g_b[0:C], 0.0)
        a_ak = jnp.where(strict, g_k[0:C], 0.0)
        a_rb = jnp.where(incl, g_b[C:], 0.0)
        a_rk = jnp.where(incl, g_k[C:], 0.0)
        s_prev = [s_scr[bi, g] for g in range(G)]
        proj = jnp.concatenate([_dot_nt(lhs_g[g], s_prev[g]) for g in range(G)], axis=1)
        doublings = max(int(math.log2(C)) - 1, 0)
        t_cat = eye + n_cat
        pw = n_cat
        if doublings:
            pw = per_head(n_cat, block_diag(n_cat, mask_cc, gc))
        yield None
        for it in range(doublings):
            pw_bd = block_diag(pw, mask_cc, gc)
            if it < doublings - 1:
                both = per_head(jnp.concatenate([t_cat, pw], axis=0), pw_bd)
                t_cat = t_cat + both[0:C]
                pw = both[C:]
            else:
                t_cat = t_cat + per_head(t_cat, pw_bd)
            yield None
        v_bd = block_diag(v, mask_ch, wl)
        rhs = proj[0:C] + per_head(a_ak, v_bd)
        yield None
        u = per_head(t_cat, block_diag(rhs, mask_ch, wl))
        yield None
        y = proj[C:] + per_head(a_rb, block_diag(u, mask_ch, wl)) + per_head(a_rk, v_bd)
        pad = LANES - 2 * C
        uv = jnp.concatenate([u, v] + ([jnp.zeros((pad, W), F32)] if pad > 0 else []), axis=0).astype(BF16)
        bk = jnp.concatenate([bh, kh] + ([jnp.zeros((pad, W), F32)] if pad > 0 else []), axis=0).astype(BF16)
        decay = jnp.exp(cl_last)
        for g in range(G):
            lanes = slice(g * wl, (g + 1) * wl)
            upd = _dot_tn(uv[:, lanes], bk[:, lanes])
            s_scr[bi, g] = s_prev[g] * decay[:, lanes] + jnp.where(mask_ss, upd, 0.0)

        @pl.when(c == pl.num_programs(1) - 1)
        def _():
            for h in range(H):
                o = (h % hpg) * N
                sout_ref[bi, h] = s_scr[bi, h // hpg, o:o + N, o:o + N]

        (sy,) = yield [("ones", jnp.concatenate(hi_lo(y), axis=0))]
        d = y - (sy[0:C] + sy[C:]) * inv_n
        (sd,) = yield [("ones", jnp.concatenate(hi_lo(d * d), axis=0))]
        var = (sd[0:C] + sd[C:]) * inv_n
        yn = d * lax.rsqrt(var + GN_EPS) * lg_ref[...] + lb_ref[...]
        y_ref[bi] = ((yn + bonus) * gate).astype(y_ref.dtype)

    seqs = [one_sequence(bi) for bi in range(nb)]
    replies = [None] * nb
    while True:
        asks, finished = [], 0
        for seq, reply in zip(seqs, replies):
            try:
                asks.append(seq.send(reply))
            except StopIteration:
                finished += 1
        if finished:
            assert finished == nb
            break
        if asks[0] is None:
            replies = [None] * nb
            continue
        replies = [[] for _ in range(nb)]
        for qi, (name, _) in enumerate(asks[0]):
            lhs = jnp.concatenate([ask[qi][1] for ask in asks], axis=0).astype(BF16)
            parts = shared[name]
            kw = lhs.shape[1] // len(parts)
            z = jnp.concatenate([jnp.dot(lhs[:, g * kw:(g + 1) * kw], part[...], preferred_element_type=F32)
                                 for g, part in enumerate(parts)], axis=1)
            rows = z.shape[0] // nb
            for si in range(nb):
                replies[si].append(z[si * rows:(si + 1) * rows])


def _rwkv(p, shift_prev, s0, prm, batch, t_len, t_valid, chunk):
    W = RWKV_WIDTH
    nc = t_len // chunk
    nb = _largest_divisor(batch, RWKV_SEQS_PER_STEP)
    vec = lambda n: pl.BlockSpec((1, n), lambda b, c: (0, 0))
    mat = lambda r, n: pl.BlockSpec((r, n), lambda b, c: (0, 0))
    state = pl.BlockSpec((nb, RWKV_HEADS, RWKV_HEAD, RWKV_HEAD), lambda b, c: (b, 0, 0, 0))
    groups = _rwkv_groups(chunk)
    kern = functools.partial(_rwkv_kernel, chunk=chunk, t_valid=t_valid)
    return pl.pallas_call(
        kern,
        grid=(batch // nb, nc),
        in_specs=[
            pl.BlockSpec((nb, chunk, RWKV_PROJ), lambda b, c: (b, c, 0)),
            pl.BlockSpec((nb, 1, RWKV_PROJ), lambda b, c: (b, 0, 0)),
            state,
            vec(RWKV_PROJ), vec(W), mat(LORA_WIDTH, W), vec(W), mat(LORA_WIDTH, W), mat(LORA_WIDTH, W),
            vec(W), vec(W), vec(W), vec(W), vec(W), mat(W, W),
        ],
        out_specs=[pl.BlockSpec((nb, chunk, W), lambda b, c: (b, c, 0)), state],
        out_shape=[
            jax.ShapeDtypeStruct((batch, t_len, W), _mix_dtype(chunk)),
            jax.ShapeDtypeStruct((batch, RWKV_HEADS, RWKV_HEAD, RWKV_HEAD), F32),
        ],
        scratch_shapes=[pltpu.VMEM((nb, groups, W // groups, W // groups), F32),
                        pltpu.VMEM((nb, 1, RWKV_PROJ), F32)],
        compiler_params=_cparams(("parallel", "arbitrary")),
        name="rwkv7",
    )(p, shift_prev, s0, prm["mu"], prm["w0"], prm["w_up"], prm["a0"], prm["a_up"], prm["g_up"],
      prm["k_k"], prm["k_a"], prm["r_k"], prm["lnx_g"], prm["lnx_b"], prm["ones_bd"])


def _lru_kernel(x0_ref, x1_ref, g0_ref, g1_ref, conv_ref, h0_ref, cw_ref, cb_ref, wa_ref, ba_ref, wx_ref, bx_ref,
                lam_ref, y_ref, hout_ref, tail_scr, h_scr, *, chunk, t_valid):
    C = chunk
    c = pl.program_id(1)

    @pl.when(c == 0)
    def _():
        tail_scr[...] = jnp.zeros_like(tail_scr)
        tail_scr[SUBLANES - (CONV_W - 1):, :] = conv_ref[...]
        h_scr[...] = h0_ref[...]

    x = jnp.concatenate([x0_ref[...], x1_ref[...]], axis=1)
    gt = jnp.concatenate([g0_ref[...], g1_ref[...]], axis=1)
    row_in_group = lax.broadcasted_iota(jnp.int32, (C, 1), 0) % SUBLANES

    def rotate_in_groups(z, dlt):
        z3 = z.reshape(z.shape[0] // SUBLANES, SUBLANES, z.shape[1])
        return pltpu.roll(z3, dlt, axis=1).reshape(z.shape)

    x_prev = tail_scr[...]
    if C > SUBLANES:
        x_prev = jnp.concatenate([x_prev, x[:C - SUBLANES]], axis=0)
    tail_scr[...] = x[C - SUBLANES:, :]
    xc = cb_ref[...] + x * cw_ref[CONV_W - 1:CONV_W, :]
    for dlt in range(1, CONV_W):
        sh = jnp.where(row_in_group >= dlt, rotate_in_groups(x, dlt), rotate_in_groups(x_prev, dlt))
        xc = xc + sh * cw_ref[CONV_W - 1 - dlt:CONV_W - dlt, :]

    gate_r = jax.nn.sigmoid(_dot(xc, wa_ref[...]) + ba_ref[...])
    gate_i = jax.nn.sigmoid(_dot(xc, wx_ref[...]) + bx_ref[...])
    log_a = -RG_C * gate_r * _softplus(-lam_ref[...])
    a = jnp.exp(log_a)
    th = jnp.tanh(log_a)
    u = jnp.sqrt(-2.0 * th / (1.0 - th)) * (gate_i * xc)

    dlt = 1
    while dlt < SUBLANES:
        keep = row_in_group >= dlt
        a_sh = jnp.where(keep, rotate_in_groups(a, dlt), 1.0)
        u_sh = jnp.where(keep, rotate_in_groups(u, dlt), 0.0)
        u = a * u_sh + u
        a = a * a_sh
        dlt *= 2
    carry = h_scr[...]
    groups = []
    for gi in range(C // SUBLANES):
        rows = slice(gi * SUBLANES, (gi + 1) * SUBLANES)
        groups.append(a[rows] * carry + u[rows])
        carry = groups[-1][SUBLANES - 1:SUBLANES, :]
    h = jnp.concatenate(groups, axis=0)
    last = min(t_valid, C) - 1
    h_scr[...] = h[last:last + 1, :]
    hout_ref[...] = h[last:last + 1, :]
    y_ref[...] = (h * jax.nn.gelu(gt)).astype(y_ref.dtype)


def _lru(pxq, conv_prev, h0, prm, batch, t_len, t_valid, chunk):
    W = LRU_WIDTH
    nc = t_len // chunk
    half = W // 2
    col0 = RWKV_PROJ // half
    assert RWKV_PROJ % half == 0
    part = lambda k: pl.BlockSpec((chunk, half), lambda b, c: (b * nc + c, col0 + k))
    vec = lambda n: pl.BlockSpec((1, n), lambda b, c: (0, 0))
    kern = functools.partial(_lru_kernel, chunk=chunk, t_valid=t_valid)
    return pl.pallas_call(
        kern,
        grid=(batch, nc),
        in_specs=[
            part(0), part(1), part(2), part(3),
            pl.BlockSpec((None, CONV_W - 1, W), lambda b, c: (b, 0, 0)),
            pl.BlockSpec((None, 1, W), lambda b, c: (b, 0, 0)),
            pl.BlockSpec((CONV_W, W), lambda b, c: (0, 0)),
            vec(W),
            pl.BlockSpec((W, W), lambda b, c: (0, 0)), vec(W),
            pl.BlockSpec((W, W), lambda b, c: (0, 0)), vec(W),
            vec(W),
        ],
        out_specs=[
            pl.BlockSpec((chunk, W), lambda b, c: (b * nc + c, 0)),
            pl.BlockSpec((None, 1, W), lambda b, c: (b, 0, 0)),
        ],
        out_shape=[
            jax.ShapeDtypeStruct((batch * t_len, W), _mix_dtype(chunk)),
            jax.ShapeDtypeStruct((batch, 1, W), F32),
        ],
        scratch_shapes=[pltpu.VMEM((SUBLANES, W), F32), pltpu.VMEM((1, W), F32)],
        compiler_params=_cparams(("parallel", "arbitrary")),
        name="rglru",
    )(pxq, pxq, pxq, pxq, conv_prev, h0, prm["conv_w"], prm["conv_b"], prm["wa_bd"], prm["ba"], prm["wx_bd"],
      prm["bx"], prm["lam"])


def _largest_divisor(n, cap):
    return max(u for u in range(1, cap + 1) if n % u == 0)


def _attn_prompt_kernel(q_ref, kil_ref, vil_ref, bias_ref, o_ref, k_ref, v_ref, og_scr, lse_scr, *, seq):
    B = ATTN_SPAN
    scale = ATTN_HEAD ** -0.5
    h = pl.program_id(1)
    R = ATTN_MERGE_ROWS

    def gather_head(i, carry):
        dst = pl.ds(pl.multiple_of(i * R, R), R)
        src = pl.ds(h + i * (R * ATTN_HEADS), R, stride=ATTN_HEADS)
        k_ref[dst, :] = kil_ref[src, :]
        v_ref[dst, :] = vil_ref[src, :]
        return carry

    lax.fori_loop(0, seq // R, gather_head, 0)

    blocks = [(gi, dil, cls + blk * (B * dil), blk == 0)
              for gi, (win, dil) in enumerate(DIL_PAIRS)
              for cls in range(dil) for blk in range(seq // (dil * B))]
    batches = [blocks[i:i + ATTN_UNROLL] for i in range(0, len(blocks), ATTN_UNROLL)]

    def rows(dil, start, first):
        start, n = (start, B) if first else (start - B * dil, 2 * B)
        return pl.ds(start, n) if dil == 1 else pl.ds(start, n, stride=dil)

    def scores(batch):
        return [_dot_nt(q_ref[rows(dil, qs, True), :], k_ref[rows(dil, qs, first), :]) * scale
                + (bias_ref[gi, :, B:] if first else bias_ref[gi]) for gi, dil, qs, first in batch]

    def softmax(ss):
        out = []
        for s in ss:
            mx = jnp.max(s, axis=-1, keepdims=True)
            pr = jnp.exp(s - mx)
            den = jnp.sum(pr, axis=-1, keepdims=True)
            out.append(((pr / den).astype(BF16), mx + jnp.log(den)))
        return out

    def values_and_store(batch, probs):
        outs = [_dot(pn, v_ref[rows(dil, qs, first), :]) for (gi, dil, qs, first), (pn, _) in zip(batch, probs)]
        for (gi, dil, qs, _), o, (_, lse) in zip(batch, outs, probs):
            og_scr[gi, rows(dil, qs, True), :] = o
            lse_scr[gi, rows(dil, qs, True), :] = jnp.broadcast_to(lse, (B, ATTN_HEAD))

    pending = scores(batches[0])
    for bi, batch in enumerate(batches):
        upcoming = scores(batches[bi + 1]) if bi + 1 < len(batches) else None
        values_and_store(batch, softmax(pending))
        pending = upcoming

    def merge(i, carry):
        sl = pl.ds(pl.multiple_of(i * R, R), R)
        ls = [lse_scr[gi, sl, :] for gi in range(len(DIL_PAIRS))]
        mx = functools.reduce(jnp.maximum, ls)
        ws = [jnp.exp(l - mx) for l in ls]
        tot = functools.reduce(lambda a, b: a + b, ws)
        acc = (ws[0] / tot) * og_scr[0, sl, :]
        for gi in range(1, len(DIL_PAIRS)):
            acc = acc + (ws[gi] / tot) * og_scr[gi, sl, :]
        o_ref[sl, :] = acc.astype(o_ref.dtype)
        return carry

    lax.fori_loop(0, seq // R, merge, 0)


def _attn_prompt(pxq, k_all, v_all, bias, layer, batch, seq):
    H = ATTN_HEADS
    G = len(DIL_PAIRS)
    q_col0 = (PXQ_WIDTH - ATTN_WIDTH) // ATTN_HEAD
    kern = functools.partial(_attn_prompt_kernel, seq=seq)
    kv = pl.BlockSpec((seq * H, ATTN_HEAD), lambda b, h: (layer * batch + b, 0))
    return pl.pallas_call(
        kern,
        grid=(batch, H),
        in_specs=[
            pl.BlockSpec((seq, ATTN_HEAD), lambda b, h: (b, q_col0 + h)),
            kv, kv,
            pl.BlockSpec((G, None, ATTN_SPAN, 2 * ATTN_SPAN), lambda b, h: (0, h, 0, 0)),
        ],
        out_specs=pl.BlockSpec((seq, ATTN_HEAD), lambda b, h: (b, h)),
        out_shape=jax.ShapeDtypeStruct((batch * seq, ATTN_WIDTH), BF16),
        scratch_shapes=[pltpu.VMEM((seq, ATTN_HEAD), F32), pltpu.VMEM((seq, ATTN_HEAD), F32),
                        pltpu.VMEM((G, seq, ATTN_HEAD), F32), pltpu.VMEM((G, seq, ATTN_HEAD), F32)],
        compiler_params=_cparams(("parallel", "arbitrary")),
        name="attn_prompt",
    )(pxq, k_all, v_all, bias)


def _attn_sample_kernel(pxq_ref, kn_ref, vn_ref, kc_ref, vc_ref, bias_ref, o_ref, q_scr, kn_scr, vn_scr, *, t_new, win):
    scale = ATTN_HEAD ** -0.5
    H = ATTN_HEADS
    G = len(DIL_PAIRS)
    TP = q_scr.shape[0]
    q_col0 = PXQ_WIDTH - ATTN_WIDTH
    q_scr[...] = jnp.zeros_like(q_scr)
    kn_scr[...] = jnp.zeros_like(kn_scr)
    vn_scr[...] = jnp.zeros_like(vn_scr)
    for h in range(H):
        lanes = slice(h * ATTN_HEAD, (h + 1) * ATTN_HEAD)
        q_scr[0:t_new, :] = pxq_ref[:, q_col0 + h * ATTN_HEAD:q_col0 + (h + 1) * ATTN_HEAD]
        kn_scr[0:t_new, :] = kn_ref[pl.ds(h, t_new, stride=H), :]
        vn_scr[0:t_new, :] = vn_ref[pl.ds(h, t_new, stride=H), :]
        qb = q_scr[...].astype(BF16)
        kc = kc_ref[pl.ds(h, win, stride=H), :]
        vc = vc_ref[pl.ds(h, win, stride=H), :]
        s = jnp.concatenate([_dot_nt(qb, kc), _dot_nt(qb, kn_scr[...])], axis=1) * scale
        prs, lses = [], []
        for gi in range(G):
            sg = s + bias_ref[gi, h]
            mx = jnp.max(sg, axis=-1, keepdims=True)
            pr = jnp.exp(sg - mx)
            den = jnp.sum(pr, axis=-1, keepdims=True)
            prs.append(pr / den)
            lses.append(mx + jnp.log(den))
        pall = jnp.concatenate(prs, axis=0)
        oall = _dot(pall[:, :win], vc) + _dot(pall[:, win:], vn_scr[...])
        mx = functools.reduce(jnp.maximum, lses)
        ws = [jnp.exp(l - mx) for l in lses]
        tot = functools.reduce(lambda a, b: a + b, ws)
        acc = (ws[0] / tot) * oall[0:TP]
        for gi in range(1, G):
            acc = acc + (ws[gi] / tot) * oall[gi * TP:(gi + 1) * TP]
        o_ref[:, lanes] = acc[0:t_new].astype(o_ref.dtype)


def _attn_sample(pxq, k_new, v_new, k_cache, v_cache, bias, layer, batch, t_new):
    H = ATTN_HEADS
    win = k_cache.shape[2] // H
    tp = bias.shape[2]
    kern = functools.partial(_attn_sample_kernel, t_new=t_new, win=win)
    cache = pl.BlockSpec((None, None, win * H, ATTN_HEAD), lambda b: (layer, b, 0, 0))
    new = pl.BlockSpec((None, t_new * H, ATTN_HEAD), lambda b: (layer * batch + b, 0, 0))
    return pl.pallas_call(
        kern,
        grid=(batch,),
        in_specs=[pl.BlockSpec((None, t_new, pxq.shape[2]), lambda b: (b, 0, 0)), new, new, cache, cache,
                  pl.BlockSpec(bias.shape, lambda b: (0, 0, 0, 0))],
        out_specs=pl.BlockSpec((None, t_new, ATTN_WIDTH), lambda b: (b, 0, 0)),
        out_shape=jax.ShapeDtypeStruct((batch, t_new, ATTN_WIDTH), F32),
        scratch_shapes=[pltpu.VMEM((tp, ATTN_HEAD), F32), pltpu.VMEM((LANES, ATTN_HEAD), F32),
                        pltpu.VMEM((LANES, ATTN_HEAD), F32)],
        compiler_params=_cparams(("parallel",)),
        name="attn_sample",
    )(pxq, k_new, v_new, k_cache, v_cache, bias)


def _bias_kernel(idx_ref, rb_ref, o_ref):
    h = pl.program_id(0)
    idx = idx_ref[...]
    acc = jnp.full(idx.shape, NEG, F32)
    for b in range(N_BUCKETS):
        acc = jnp.where(idx == b, rb_ref[b, h], acc)
    o_ref[...] = acc


def _bias_table(idx, rel_bias):
    g, r, c = idx.shape
    heads = rel_bias.shape[1]
    return pl.pallas_call(
        _bias_kernel,
        grid=(heads,),
        in_specs=[pl.BlockSpec((g, r, c), lambda h: (0, 0, 0)),
                  pl.BlockSpec(memory_space=pltpu.SMEM)],
        out_specs=pl.BlockSpec((g, None, r, c), lambda h: (0, h, 0, 0)),
        out_shape=jax.ShapeDtypeStruct((g, heads, r, c), F32),
        compiler_params=_cparams(("parallel",)),
        name="bias_table",
    )(jnp.asarray(idx, jnp.int32), rel_bias.astype(F32))


def _t5_bucket_static(dist):
    dist = np.asarray(dist, np.int64)
    exact = N_BUCKETS // 2
    d = np.maximum(dist, 1).astype(np.float64)
    large = exact + (np.log(d / exact) / math.log(REL_MAX_DIST / exact) * (N_BUCKETS - exact)).astype(np.int64)
    return np.where(dist < exact, dist, np.minimum(large, N_BUCKETS - 1))


def _prompt_bias_idx():
    B = ATTN_SPAN
    qi = np.arange(B)[:, None]
    kj = np.arange(2 * B)[None, :]
    delta = qi + B - kj
    valid = (delta >= 0) & (delta <= B)
    return np.stack([np.where(valid, _t5_bucket_static(np.clip(delta, 0, B) * dil), -1) for _, dil in DIL_PAIRS])


def _sample_bias_idx(t_new, t_pad, win, total):
    r = np.arange(total)[None, :]
    t = np.arange(t_pad)[:, None]
    dist = win + t - r
    tabs = []
    for _, dil in DIL_PAIRS:
        assert win >= dil * ATTN_SPAN
        valid = (dist >= 0) & (dist % dil == 0) & (dist <= dil * ATTN_SPAN) & (t < t_new)
        tabs.append(np.where(valid, _t5_bucket_static(np.clip(dist, 0, dil * ATTN_SPAN)), -1))
    return np.stack(tabs)


def _block_diag(blocks):
    n, c, d = blocks.shape
    eye = jnp.eye(n, dtype=blocks.dtype)
    return jnp.einsum("ncd,nm->ncmd", blocks, eye).reshape(n * c, n * d)


def _layer_params(l, a):
    W = RWKV_WIDTH
    row = lambda t: t[l].reshape(1, -1).astype(F32)

    def lora_pad(w, off):
        z = jnp.zeros((LORA_WIDTH, W), F32)
        return z.at[off:off + w.shape[0]].set(w).astype(BF16)

    head_of = np.arange(W) // RWKV_HEAD
    ones_bd = jnp.asarray((head_of[:, None] == head_of[None, :]).astype(np.float32), BF16)
    rwkv = dict(
        mu=row(a["rwkv_mu"]), w0=row(a["rwkv_w0"]), a0=row(a["rwkv_a0"]),
        w_up=lora_pad(a["rwkv_w_up"][l], 0),
        a_up=lora_pad(a["rwkv_a_up"][l], DECAY_LORA),
        g_up=lora_pad(a["rwkv_g_up"][l], DECAY_LORA + ICLR_LORA),
        k_k=row(a["rwkv_k_k"]), k_a=row(a["rwkv_k_a"]), r_k=row(a["rwkv_r_k"]),
        lnx_g=row(a["rwkv_lnx_g"]), lnx_b=row(a["rwkv_lnx_b"]),
        ones_bd=ones_bd,
    )
    lru = dict(
        conv_w=a["lru_conv_w"][l].astype(F32), conv_b=row(a["lru_conv_b"]),
        wa_bd=_block_diag(a["lru_wa"][l]).astype(BF16), ba=row(a["lru_ba"]),
        wx_bd=_block_diag(a["lru_wx"][l]).astype(BF16), bx=row(a["lru_bx"]),
        lam=row(a["lru_lambda"]),
    )
    return dict(rwkv=rwkv, lru=lru)


def _shared_params(a):
    gain = lambda t: t.reshape(t.shape[0], 1, -1).astype(F32)
    return dict(
        g_mix_pre=gain(a["norm_mix_pre"]), g_mix_post=gain(a["norm_mix_post"]),
        g_ffn_pre=gain(a["norm_ffn_pre"]), g_ffn_post=gain(a["norm_ffn_post"]),
    )


def _layer(x, layer, sp, wts, lp, batch, t_len, shift_prev, s0, conv_prev, h0, kv_all, attend):
    emit = wts is None
    src = sp["f32"] if emit else wts
    wl = layer if emit else 0
    copies = {}
    pxq, k_all, v_all, *w_copy = _in_proj(x, sp["g_mix_pre"], src["w_in"], layer, wl, *kv_all, emit_bf16=emit)
    copies["w_in"] = w_copy[0] if emit else None
    pxq3 = pxq.reshape(batch, t_len, pxq.shape[1])
    if t_len % RWKV_CHUNK == 0:
        y_a, s_fin = _rwkv(pxq3, shift_prev, s0, lp["rwkv"], batch, t_len, t_len, RWKV_CHUNK)
        y_a = y_a.reshape(batch * t_len, -1)
        y_b, h_fin = _lru(pxq, conv_prev, h0, lp["lru"], batch, t_len, t_len, LRU_CHUNK)
    else:
        assert t_len <= SUBLANES
        padded = jnp.pad(pxq3, ((0, 0), (0, SUBLANES - t_len), (0, 0)))
        y_a, s_fin = _rwkv(padded, shift_prev, s0, lp["rwkv"], batch, SUBLANES, t_len, SUBLANES)
        y_a = y_a[:, :t_len].reshape(batch * t_len, -1)
        y_b, h_fin = _lru(padded.reshape(batch * SUBLANES, pxq.shape[1]), conv_prev, h0, lp["lru"], batch, SUBLANES,
                          t_len, SUBLANES)
        y_b = y_b.reshape(batch, SUBLANES, -1)[:, :t_len].reshape(batch * t_len, -1)
    y_c = attend(pxq, k_all, v_all)

    x = _out_proj(y_a, y_b, y_c, src["w_out"], sp["g_mix_post"], x, layer, wl, emit_bf16=emit)
    if emit:
        x, copies["w_out"] = x
    x = _ffn(x, sp["g_ffn_pre"], src["w1"], src["w2"], sp["g_ffn_post"], layer, wl, emit_bf16=emit)
    if emit:
        x, copies["w1"], copies["w2"] = x

    lru_x = pxq3[:, :, RWKV_PROJ:RWKV_PROJ + LRU_WIDTH]
    if t_len >= CONV_W - 1:
        conv_new = lru_x[:, t_len - (CONV_W - 1):]
    else:
        conv_new = jnp.concatenate([conv_prev, lru_x], axis=1)[:, -(CONV_W - 1):]
    state = (pxq3[:, -1, :RWKV_PROJ], s_fin, conv_new, h_fin.reshape(batch, LRU_WIDTH))
    return x, state, (k_all, v_all), copies


def kernel(x_prompt, x_sample, state_rwkv_wkv, state_rwkv_shift, state_lru_h, state_lru_conv, cache_attn_k, cache_attn_v, rel_bias, norm_mix_pre, norm_mix_post, norm_ffn_pre, norm_ffn_post, w_in, w_out, rwkv_mu, rwkv_w0, rwkv_w_up, rwkv_a0, rwkv_a_up, rwkv_g_up, rwkv_k_k, rwkv_k_a, rwkv_r_k, rwkv_lnx_g, rwkv_lnx_b, lru_conv_w, lru_conv_b, lru_wa, lru_ba, lru_wx, lru_bx, lru_lambda, ffn_w1, ffn_w2):
    a = dict(norm_mix_pre=norm_mix_pre, norm_mix_post=norm_mix_post, norm_ffn_pre=norm_ffn_pre,
             norm_ffn_post=norm_ffn_post, w_in=w_in, w_out=w_out, rwkv_mu=rwkv_mu, rwkv_w0=rwkv_w0,
             rwkv_w_up=rwkv_w_up, rwkv_a0=rwkv_a0, rwkv_a_up=rwkv_a_up, rwkv_g_up=rwkv_g_up, rwkv_k_k=rwkv_k_k,
             rwkv_k_a=rwkv_k_a, rwkv_r_k=rwkv_r_k, rwkv_lnx_g=rwkv_lnx_g, rwkv_lnx_b=rwkv_lnx_b,
             lru_conv_w=lru_conv_w, lru_conv_b=lru_conv_b, lru_wa=lru_wa, lru_ba=lru_ba, lru_wx=lru_wx,
             lru_bx=lru_bx, lru_lambda=lru_lambda, ffn_w1=ffn_w1, ffn_w2=ffn_w2)
    depth = w_in.shape[0]
    pb, seq, _ = x_prompt.shape
    sb, t_new, _ = x_sample.shape
    win = cache_attn_k.shape[2]
    keep = min(ATTN_WINDOW, seq)
    total = win + LANES
    assert t_new <= LANES and seq % (DIL_PAIRS[-1][1] * ATTN_SPAN) == 0

    bias_p = _bias_table(_prompt_bias_idx(), rel_bias)
    bias_s = _bias_table(_sample_bias_idx(t_new, SUBLANES, win, total), rel_bias)
    k_cache = cache_attn_k.reshape(depth, sb, win * ATTN_HEADS, ATTN_HEAD)
    v_cache = cache_attn_v.reshape(depth, sb, win * ATTN_HEADS, ATTN_HEAD)

    xp = x_prompt.reshape(pb * seq, D_MODEL)
    xs = x_sample.reshape(sb * t_new, D_MODEL)
    new_p, new_s = [], []
    sp = _shared_params(a)
    kv_p = tuple(jnp.zeros((depth * pb * seq * ATTN_HEADS, ATTN_HEAD), F32) for _ in range(2))
    kv_s = tuple(jnp.zeros((depth * sb * t_new * ATTN_HEADS, ATTN_HEAD), F32) for _ in range(2))
    sp["f32"] = dict(w_in=w_in.astype(F32), w_out=w_out.astype(F32), w1=ffn_w1.astype(F32), w2=ffn_w2.astype(F32))
    for l in range(depth):
        lp = _layer_params(l, a)
        attend_s = lambda pxq, k_all, v_all, l=l: _attn_sample(
            pxq.reshape(sb, t_new, pxq.shape[1]), k_all.reshape(depth * sb, t_new * ATTN_HEADS, ATTN_HEAD),
            v_all.reshape(depth * sb, t_new * ATTN_HEADS, ATTN_HEAD), k_cache, v_cache, bias_s, l, sb, t_new,
        ).reshape(sb * t_new, ATTN_WIDTH)
        xs, st_s, kv_s, wts = _layer(xs, l, sp, None, lp, sb, t_new,
                                     state_rwkv_shift[l].reshape(sb, 1, RWKV_PROJ), state_rwkv_wkv[l].astype(F32),
                                     state_lru_conv[l], state_lru_h[l].reshape(sb, 1, LRU_WIDTH), kv_s, attend_s)
        attend_p = lambda pxq, k_all, v_all, l=l: _attn_prompt(pxq, k_all, v_all, bias_p, l, pb, seq)
        xp, st_p, kv_p, _ = _layer(xp, l, sp, wts, lp, pb, seq,
                                   jnp.zeros((pb, 1, RWKV_PROJ), F32),
                                   jnp.zeros((pb, RWKV_HEADS, RWKV_HEAD, RWKV_HEAD), F32),
                                   jnp.zeros((pb, CONV_W - 1, LRU_WIDTH), F32), jnp.zeros((pb, 1, LRU_WIDTH), F32),
                                   kv_p, attend_p)
        new_p.append(st_p)
        new_s.append(st_s)

    stack = lambda sts, i: jnp.stack([s[i] for s in sts])
    k_p, v_p = (t.reshape(depth, pb, seq, ATTN_HEADS, ATTN_HEAD)[:, :, seq - keep:] for t in kv_p)
    k_s, v_s = (t.reshape(depth, sb, t_new, ATTN_HEADS, ATTN_HEAD) for t in kv_s)
    return (xp.reshape(pb, seq, D_MODEL), xs.reshape(sb, t_new, D_MODEL),
            stack(new_p, 1), stack(new_s, 1), stack(new_p, 0), stack(new_s, 0),
            stack(new_p, 3), stack(new_s, 3), stack(new_p, 2), stack(new_s, 2),
            k_p, k_s, v_p, v_s)
```

```python
import functools
import math

import numpy as np
import jax
import jax.numpy as jnp
from jax import lax
from jax.experimental import pallas as pl
from jax.experimental.pallas import tpu as pltpu

F32 = jnp.float32
BF16 = jnp.bfloat16

D_MODEL = 2048
RWKV_WIDTH = 512
RWKV_HEAD = 64
RWKV_HEADS = RWKV_WIDTH // RWKV_HEAD
DECAY_LORA = 64
ICLR_LORA = 64
GATE_LORA = 128
LORA_WIDTH = DECAY_LORA + ICLR_LORA + GATE_LORA
RWKV_PROJ = 3 * RWKV_WIDTH + LORA_WIDTH
GN_EPS = 64e-5
LRU_WIDTH = 512
LRU_BLOCKS = 8
LRU_BLOCK = LRU_WIDTH // LRU_BLOCKS
CONV_W = 4
RG_C = 8.0
ATTN_WIDTH = 1024
ATTN_HEAD = 128
ATTN_HEADS = ATTN_WIDTH // ATTN_HEAD
DIL_PAIRS = ((128, 1), (512, 4), (2048, 16))
ATTN_SPAN = 128
ATTN_WINDOW = 2048
N_BUCKETS = 32
REL_MAX_DIST = ATTN_WINDOW
D_FF = 4 * D_MODEL
RMS_EPS = 1e-6
NEG = -1e30

LANES = 128
SUBLANES = 8
MXU_TILE = 256
VMEM_LIMIT_BYTES = 56 * 1024 * 1024

ROW_TILE = 1024
IN_COL_TILE = 256
IN_FUSED_TILES = 2
OUT_ROW_TILE = 512
FFN_ROW_TILE = 512
FFN_COL_TILE = 1024
FFN_CAST_COL_TILE = 512
RWKV_CHUNK = 64
RWKV_SEQS_PER_STEP = 4
LRU_CHUNK = 256
ATTN_MERGE_ROWS = 256
ATTN_UNROLL = 6


def _cparams(sem):
    return pltpu.CompilerParams(dimension_semantics=sem, vmem_limit_bytes=VMEM_LIMIT_BYTES)


def _dot(a, b):
    return jnp.dot(a.astype(BF16), b.astype(BF16), preferred_element_type=F32)


def _dot_nt(a, b):
    return lax.dot_general(a.astype(BF16), b.astype(BF16), (((1,), (1,)), ((), ())), preferred_element_type=F32)


def _dot_tn(a, b):
    return lax.dot_general(a.astype(BF16), b.astype(BF16), (((0,), (0,)), ((), ())), preferred_element_type=F32)


def _softplus(z):
    return jnp.maximum(z, 0.0) + jnp.log1p(jnp.exp(-jnp.abs(z)))


def _mix_dtype(rows):
    return BF16 if rows % (2 * SUBLANES) == 0 else F32


def _rms(x, g):
    ms = jnp.mean(x * x, axis=-1, keepdims=True)
    return x * lax.rsqrt(ms + RMS_EPS) * g


PXQ_WIDTH = RWKV_PROJ + 2 * LRU_WIDTH + ATTN_WIDTH
PXQ_TILES = PXQ_WIDTH // IN_COL_TILE
KV_TILES = ATTN_WIDTH // IN_COL_TILE
HEADS_PER_TILE = IN_COL_TILE // ATTN_HEAD


def _in_proj_kernel(x_ref, g_ref, *refs, fuse):
    w_refs = refs[:fuse]
    pxq_ref, k_ref, v_ref = refs[fuse + 2:fuse + 5]
    rest = refs[fuse + 5:]
    h_scr = rest[-1]
    j = pl.program_id(1)
    tm = x_ref.shape[0]
    pxq_steps = pl.cdiv(PXQ_TILES, fuse)
    kv_steps = KV_TILES // fuse

    @pl.when(j == 0)
    def _():
        h_scr[...] = _rms(x_ref[...], g_ref[...]).astype(BF16)

    ws = [w_ref[...].astype(BF16) for w_ref in w_refs]
    if len(rest) > 1:
        rest[0][...] = ws[0]
    w = ws[0] if fuse == 1 else jnp.concatenate(ws, axis=1)
    acc = jnp.dot(h_scr[...], w, preferred_element_type=F32)

    @pl.when(j < pxq_steps)
    def _():
        pxq_ref[...] = acc

    def scatter_heads(o_ref, step):
        for hh in range(fuse * HEADS_PER_TILE):
            head = step * (fuse * HEADS_PER_TILE) + hh
            o_ref[pl.ds(head, tm, stride=ATTN_HEADS), :] = acc[:, hh * ATTN_HEAD:(hh + 1) * ATTN_HEAD]

    @pl.when((j >= pxq_steps) & (j < pxq_steps + kv_steps))
    def _():
        scatter_heads(k_ref, j - pxq_steps)

    @pl.when(j >= pxq_steps + kv_steps)
    def _():
        scatter_heads(v_ref, j - pxq_steps - kv_steps)


def _in_proj(x, g, w, layer, w_layer, k_all, v_all, emit_bf16=False):
    m, d = x.shape
    n = w.shape[2]
    assert n == PXQ_WIDTH + 2 * ATTN_WIDTH
    tm = min(ROW_TILE, m)
    row_tiles = m // tm
    assert not emit_bf16 or row_tiles == 1
    fuse = 1 if emit_bf16 else IN_FUSED_TILES
    assert KV_TILES % fuse == 0
    pxq_steps = pl.cdiv(PXQ_TILES, fuse)
    spare = pxq_steps * fuse - PXQ_TILES
    width = fuse * IN_COL_TILE

    def w_spec(slot):
        def index(i, j):
            tile = jnp.where(j < pxq_steps, jnp.minimum(fuse * j + slot, PXQ_TILES - 1), fuse * j + slot - spare)
            return (w_layer, 0, tile)
        return pl.BlockSpec((None, d, IN_COL_TILE), index)

    kv_spec = pl.BlockSpec((tm * ATTN_HEADS, ATTN_HEAD), lambda i, j: (layer * row_tiles + i, 0))
    out_specs = [
        pl.BlockSpec((tm, width), lambda i, j: (i, jnp.minimum(j, pxq_steps - 1))),
        kv_spec, kv_spec,
    ]
    out_shape = [
        jax.ShapeDtypeStruct((m, pxq_steps * width), F32),
        jax.ShapeDtypeStruct(k_all.shape, F32),
        jax.ShapeDtypeStruct(v_all.shape, F32),
    ]
    if emit_bf16:
        out_specs.append(pl.BlockSpec((None, d, IN_COL_TILE), lambda i, j: (0, 0, j)))
        out_shape.append(jax.ShapeDtypeStruct((1, d, n), BF16))
    return pl.pallas_call(
        functools.partial(_in_proj_kernel, fuse=fuse),
        grid=(row_tiles, pxq_steps + 2 * (KV_TILES // fuse)),
        in_specs=[
            pl.BlockSpec((tm, d), lambda i, j: (i, 0)),
            pl.BlockSpec((None, 1, d), lambda i, j: (layer, 0, 0)),
            *[w_spec(slot) for slot in range(fuse)],
            pl.BlockSpec(memory_space=pl.ANY),
            pl.BlockSpec(memory_space=pl.ANY),
        ],
        out_specs=out_specs,
        out_shape=out_shape,
        input_output_aliases={2 + fuse: 1, 3 + fuse: 2},
        scratch_shapes=[pltpu.VMEM((tm, d), BF16)],
        compiler_params=_cparams(("parallel", "arbitrary")),
        name="in_proj",
    )(x, g, *([w] * fuse), k_all, v_all)


def _out_proj_kernel(ya_ref, yb_ref, yc_ref, w_ref, g_ref, x_ref, o_ref, *w_copy):
    c1, c2 = RWKV_WIDTH, RWKV_WIDTH + LRU_WIDTH
    if w_copy:
        acc = None
        for y_ref, rows in [(ya_ref, slice(0, c1)), (yb_ref, slice(c1, c2)), (yc_ref, slice(c2, None))]:
            w = w_ref[rows, :].astype(BF16)
            w_copy[0][rows, :] = w
            t = jnp.dot(y_ref[...].astype(BF16), w, preferred_element_type=F32)
            acc = t if acc is None else acc + t
    else:
        y = jnp.concatenate([r[...].astype(BF16) for r in (ya_ref, yb_ref, yc_ref)], axis=1)
        acc = jnp.dot(y, w_ref[...].astype(BF16), preferred_element_type=F32)
    o_ref[...] = x_ref[...] + _rms(acc, g_ref[...])


def _out_proj(ya, yb, yc, w, g, x, layer, w_layer, emit_bf16=False):
    m, d = x.shape
    tm = min(OUT_ROW_TILE, m)
    assert not emit_bf16 or m == tm
    out_specs = [pl.BlockSpec((tm, d), lambda i: (i, 0))]
    out_shape = [jax.ShapeDtypeStruct((m, d), F32)]
    if emit_bf16:
        out_specs.append(pl.BlockSpec((None, d, d), lambda i: (0, 0, 0)))
        out_shape.append(jax.ShapeDtypeStruct((1, d, d), BF16))
    res = pl.pallas_call(
        _out_proj_kernel,
        grid=(m // tm,),
        in_specs=[
            pl.BlockSpec((tm, RWKV_WIDTH), lambda i: (i, 0)),
            pl.BlockSpec((tm, LRU_WIDTH), lambda i: (i, 0)),
            pl.BlockSpec((tm, ATTN_WIDTH), lambda i: (i, 0)),
            pl.BlockSpec((None, d, d), lambda i: (w_layer, 0, 0)),
            pl.BlockSpec((None, 1, d), lambda i: (layer, 0, 0)),
            pl.BlockSpec((tm, d), lambda i: (i, 0)),
        ],
        out_specs=out_specs,
        out_shape=out_shape,
        compiler_params=_cparams(("parallel",)),
        name="out_proj",
    )(ya, yb, yc, w, g, x)
    return res if emit_bf16 else res[0]


def _ffn_kernel(x_ref, g1_ref, w1_ref, w2_ref, g2_ref, o_ref, *rest):
    h_scr, acc_scr = rest[-2:]
    j = pl.program_id(1)

    @pl.when(j == 0)
    def _():
        h_scr[...] = _rms(x_ref[...], g1_ref[...]).astype(BF16)
        acc_scr[...] = jnp.zeros_like(acc_scr)

    w1 = w1_ref[...].astype(BF16)
    w2 = w2_ref[...].astype(BF16)
    if len(rest) > 2:
        rest[0][...] = w1
        rest[1][...] = w2
    u = jnp.dot(h_scr[...], w1, preferred_element_type=F32)
    u = jnp.square(jnp.maximum(u, 0.0)).astype(BF16)
    acc_scr[...] += jnp.dot(u, w2, preferred_element_type=F32)

    @pl.when(j == pl.num_programs(1) - 1)
    def _():
        o_ref[...] = x_ref[...] + _rms(acc_scr[...], g2_ref[...])


def _ffn(x, g1, w1, w2, g2, layer, w_layer, emit_bf16=False):
    m, d = x.shape
    f = w1.shape[2]
    tm = min(FFN_ROW_TILE, m)
    tf = FFN_CAST_COL_TILE if emit_bf16 else FFN_COL_TILE
    assert not emit_bf16 or m == tm
    out_specs = [pl.BlockSpec((tm, d), lambda i, j: (i, 0))]
    out_shape = [jax.ShapeDtypeStruct((m, d), F32)]
    if emit_bf16:
        out_specs += [pl.BlockSpec((None, d, tf), lambda i, j: (0, 0, j)),
                      pl.BlockSpec((None, tf, d), lambda i, j: (0, j, 0))]
        out_shape += [jax.ShapeDtypeStruct((1, d, f), BF16), jax.ShapeDtypeStruct((1, f, d), BF16)]
    res = pl.pallas_call(
        _ffn_kernel,
        grid=(m // tm, f // tf),
        in_specs=[
            pl.BlockSpec((tm, d), lambda i, j: (i, 0)),
            pl.BlockSpec((None, 1, d), lambda i, j: (layer, 0, 0)),
            pl.BlockSpec((None, d, tf), lambda i, j: (w_layer, 0, j)),
            pl.BlockSpec((None, tf, d), lambda i, j: (w_layer, j, 0)),
            pl.BlockSpec((None, 1, d), lambda i, j: (layer, 0, 0)),
        ],
        out_specs=out_specs,
        out_shape=out_shape,
        scratch_shapes=[pltpu.VMEM((tm, d), BF16), pltpu.VMEM((tm, d), F32)],
        compiler_params=_cparams(("parallel", "arbitrary")),
        name="ffn",
    )(x, g1, w1, w2, g2)
    return res if emit_bf16 else res[0]


def _rwkv_groups(chunk):
    lanes = RWKV_HEADS * chunk
    return lanes // MXU_TILE if lanes % (2 * MXU_TILE) == 0 else 1


def _rwkv_kernel(p_ref, shift_ref, s0_ref, mu_ref, w0_ref, wup_ref, a0_ref, aup_ref, gup_ref, kk_ref, ka_ref,
                 rk_ref, lg_ref, lb_ref, ones_ref,
                 y_ref, sout_ref, s_scr, prev_scr, *, chunk, t_valid):
    C = chunk
    W = RWKV_WIDTH
    H = RWKV_HEADS
    N = RWKV_HEAD
    nb = p_ref.shape[0]
    c = pl.program_id(1)

    G = _rwkv_groups(C)
    hpg = H // G
    wl = W // G
    gc = hpg * C

    @pl.when(c == 0)
    def _():
        s_scr[...] = jnp.zeros_like(s_scr)
        for h in range(H):
            o = (h % hpg) * N
            s_scr[:, h // hpg, o:o + N, o:o + N] = s0_ref[:, h]
        prev_scr[...] = shift_ref[...]

    inv_n = 1.0 / RWKV_HEAD
    shared = dict(wup=[wup_ref], aup=[aup_ref], gup=[gup_ref],
                  ones=[ones_ref.at[g * wl:(g + 1) * wl, g * wl:(g + 1) * wl] for g in range(G)])

    def hi_lo(x):
        hi = x.astype(BF16).astype(F32)
        return [hi, x - hi]

    row = lax.broadcasted_iota(jnp.int32, (C, 1), 0)

    def cumsum_rows(z):
        dlt = 1
        while dlt < C:
            z = z + jnp.where(row >= dlt, pltpu.roll(z, dlt, axis=0), 0.0)
            dlt *= 2
        return z

    ti = lax.broadcasted_iota(jnp.int32, (C, H * C), 0)
    si = lax.broadcasted_iota(jnp.int32, (C, H * C), 1) % C
    strict = ti > si
    incl = ti >= si
    eye = (ti == si).astype(F32)
    blk_r = lax.broadcasted_iota(jnp.int32, (gc, 1), 0) // C
    mask_ch = blk_r == lax.broadcasted_iota(jnp.int32, (1, wl), 1) // N
    mask_cc = blk_r == lax.broadcasted_iota(jnp.int32, (1, gc), 1) // C
    mask_ss = (lax.broadcasted_iota(jnp.int32, (wl, 1), 0) // N
               == lax.broadcasted_iota(jnp.int32, (1, wl), 1) // N)

    def block_diag(x, mask, width):
        out = []
        for g in range(G):
            tiled = jnp.concatenate([x[:, g * width:(g + 1) * width]] * hpg, axis=0)
            out.append(jnp.where(mask, tiled, 0.0).astype(BF16))
        return out

    def per_head(a_cat, bds):
        a_b = a_cat.astype(BF16)
        return jnp.concatenate([jnp.dot(a_b[:, g * gc:(g + 1) * gc], bds[g], preferred_element_type=F32)
                                for g in range(G)], axis=1)

    def one_sequence(bi):
        p = p_ref[bi]
        shifted = jnp.where(row == 0, prev_scr[bi], pltpu.roll(p, 1, axis=0))
        prev_scr[bi] = p_ref[bi, C - 1:C, :]
        m = p + (shifted - p) * mu_ref[...]
        r = m[:, 0:W]
        k = m[:, W:2 * W]
        v = m[:, 2 * W:3 * W]
        x = m[:, 3 * W:]
        lw, la, gate = yield [("wup", jnp.tanh(x)), ("aup", x), ("gup", jax.nn.sigmoid(x))]
        w = w0_ref[...] + lw
        a = jax.nn.sigmoid(a0_ref[...] + la)
        softplus_neg_w = jnp.maximum(-w, 0.0) + jnp.log(1.0 + jnp.exp(-jnp.abs(w)))
        loga = -jnp.exp(-softplus_neg_w - 0.5)
        kk = k * kk_ref[...]
        k2 = k * (1.0 + (a - 1.0) * ka_ref[...])
        (ss,) = yield [("ones", jnp.concatenate(hi_lo(kk * kk) + hi_lo(r * k2 * rk_ref[...]), axis=0))]
        kk = kk / jnp.maximum(jnp.sqrt(ss[0:C] + ss[C:2 * C]), 1e-12)
        bonus = (ss[2 * C:3 * C] + ss[3 * C:]) * v
        if t_valid < C:
            live = row < t_valid
            loga = jnp.where(live, loga, 0.0)
            kk = jnp.where(live, kk, 0.0)
            k2 = jnp.where(live, k2, 0.0)
        cl = cumsum_rows(loga)
        cl_last = cl[C - 1:C, :]
        e_neg = jnp.exp(-cl)
        e_rem = jnp.exp(cl_last - cl)
        kka = kk * a
        al = -kk * jnp.exp(cl - loga)
        rt = r * jnp.exp(cl)
        be_bd = block_diag(kka * e_neg, mask_ch, wl)
        kt_bd = block_diag(k2 * e_neg, mask_ch, wl)
        bh = kka * e_rem
        kh = k2 * e_rem
        lhs = jnp.concatenate([al, rt], axis=0).astype(BF16)
        lhs_g = [lhs[:, g * wl:(g + 1) * wl] for g in range(G)]
        g_b = jnp.concatenate([_dot_nt(lhs_g[g], be_bd[g]) for g in range(G)], axis=1)
        g_k = jnp.concatenate([_dot_nt(lhs_g[g], kt_bd[g]) for g in range(G)], axis=1)
        n_cat = jnp.where(strict, g_b[0:C], 0.0)
        a_ak = jnp.where(strict, g_k[0:C], 0.0)
        a_rb = jnp.where(incl, g_b[C:], 0.0)
        a_rk = jnp.where(incl, g_k[C:], 0.0)
        s_prev = [s_scr[bi, g] for g in range(G)]
        proj = jnp.concatenate([_dot_nt(lhs_g[g], s_prev[g]) for g in range(G)], axis=1)
        doublings = max(int(math.log2(C)) - 1, 0)
        t_cat = eye + n_cat
        pw = n_cat
        if doublings:
            pw = per_head(n_cat, block_diag(n_cat, mask_cc, gc))
        yield None
        for it in range(doublings):
            pw_bd = block_diag(pw, mask_cc, gc)
            if it < doublings - 1:
                both = per_head(jnp.concatenate([t_cat, pw], axis=0), pw_bd)
                t_cat = t_cat + both[0:C]
                pw = both[C:]
            else:
                t_cat = t_cat + per_head(t_cat, pw_bd)
            yield None
        v_bd = block_diag(v, mask_ch, wl)
        rhs = proj[0:C] + per_head(a_ak, v_bd)
        yield None
        u = per_head(t_cat, block_diag(rhs, mask_ch, wl))
        yield None
        y = proj[C:] + per_head(a_rb, block_diag(u, mask_ch, wl)) + per_head(a_rk, v_bd)
        pad = LANES - 2 * C
        uv = jnp.concatenate([u, v] + ([jnp.zeros((pad, W), F32)] if pad > 0 else []), axis=0).astype(BF16)
        bk = jnp.concatenate([bh, kh] + ([jnp.zeros((pad, W), F32)] if pad > 0 else []), axis=0).astype(BF16)
        decay = jnp.exp(cl_last)
        for g in range(G):
            lanes = slice(g * wl, (g + 1) * wl)
            upd = _dot_tn(uv[:, lanes], bk[:, lanes])
            s_scr[bi, g] = s_prev[g] * decay[:, lanes] + jnp.where(mask_ss, upd, 0.0)

        @pl.when(c == pl.num_programs(1) - 1)
        def _():
            for h in range(H):
                o = (h % hpg) * N
                sout_ref[bi, h] = s_scr[bi, h // hpg, o:o + N, o:o + N]

        (sy,) = yield [("ones", jnp.concatenate(hi_lo(y), axis=0))]
        d = y - (sy[0:C] + sy[C:]) * inv_n
        (sd,) = yield [("ones", jnp.concatenate(hi_lo(d * d), axis=0))]
        var = (sd[0:C] + sd[C:]) * inv_n
        yn = d * lax.rsqrt(var + GN_EPS) * lg_ref[...] + lb_ref[...]
        y_ref[bi] = ((yn + bonus) * gate).astype(y_ref.dtype)

    seqs = [one_sequence(bi) for bi in range(nb)]
    replies = [None] * nb
    while True:
        asks, finished = [], 0
        for seq, reply in zip(seqs, replies):
            try:
                asks.append(seq.send(reply))
            except StopIteration:
                finished += 1
        if finished:
            assert finished == nb
            break
        if asks[0] is None:
            replies = [None] * nb
            continue
        replies = [[] for _ in range(nb)]
        for qi, (name, _) in enumerate(asks[0]):
            lhs = jnp.concatenate([ask[qi][1] for ask in asks], axis=0).astype(BF16)
            parts = shared[name]
            kw = lhs.shape[1] // len(parts)
            z = jnp.concatenate([jnp.dot(lhs[:, g * kw:(g + 1) * kw], part[...], preferred_element_type=F32)
                                 for g, part in enumerate(parts)], axis=1)
            rows = z.shape[0] // nb
            for si in range(nb):
                replies[si].append(z[si * rows:(si + 1) * rows])


def _rwkv(p, shift_prev, s0, prm, batch, t_len, t_valid, chunk):
    W = RWKV_WIDTH
    nc = t_len // chunk
    nb = _largest_divisor(batch, RWKV_SEQS_PER_STEP)
    vec = lambda n: pl.BlockSpec((1, n), lambda b, c: (0, 0))
    mat = lambda r, n: pl.BlockSpec((r, n), lambda b, c: (0, 0))
    state = pl.BlockSpec((nb, RWKV_HEADS, RWKV_HEAD, RWKV_HEAD), lambda b, c: (b, 0, 0, 0))
    groups = _rwkv_groups(chunk)
    kern = functools.partial(_rwkv_kernel, chunk=chunk, t_valid=t_valid)
    return pl.pallas_call(
        kern,
        grid=(batch // nb, nc),
        in_specs=[
            pl.BlockSpec((nb, chunk, RWKV_PROJ), lambda b, c: (b, c, 0)),
            pl.BlockSpec((nb, 1, RWKV_PROJ), lambda b, c: (b, 0, 0)),
            state,
            vec(RWKV_PROJ), vec(W), mat(LORA_WIDTH, W), vec(W), mat(LORA_WIDTH, W), mat(LORA_WIDTH, W),
            vec(W), vec(W), vec(W), vec(W), vec(W), mat(W, W),
        ],
        out_specs=[pl.BlockSpec((nb, chunk, W), lambda b, c: (b, c, 0)), state],
        out_shape=[
            jax.ShapeDtypeStruct((batch, t_len, W), _mix_dtype(chunk)),
            jax.ShapeDtypeStruct((batch, RWKV_HEADS, RWKV_HEAD, RWKV_HEAD), F32),
        ],
        scratch_shapes=[pltpu.VMEM((nb, groups, W // groups, W // groups), F32),
                        pltpu.VMEM((nb, 1, RWKV_PROJ), F32)],
        compiler_params=_cparams(("parallel", "arbitrary")),
        name="rwkv7",
    )(p, shift_prev, s0, prm["mu"], prm["w0"], prm["w_up"], prm["a0"], prm["a_up"], prm["g_up"],
      prm["k_k"], prm["k_a"], prm["r_k"], prm["lnx_g"], prm["lnx_b"], prm["ones_bd"])


def _lru_kernel(x0_ref, x1_ref, g0_ref, g1_ref, conv_ref, h0_ref, cw_ref, cb_ref, wa_ref, ba_ref, wx_ref, bx_ref,
                lam_ref, y_ref, hout_ref, tail_scr, h_scr, *, chunk, t_valid):
    C = chunk
    c = pl.program_id(1)

    @pl.when(c == 0)
    def _():
        tail_scr[...] = jnp.zeros_like(tail_scr)
        tail_scr[SUBLANES - (CONV_W - 1):, :] = conv_ref[...]
        h_scr[...] = h0_ref[...]

    x = jnp.concatenate([x0_ref[...], x1_ref[...]], axis=1)
    gt = jnp.concatenate([g0_ref[...], g1_ref[...]], axis=1)
    row_in_group = lax.broadcasted_iota(jnp.int32, (C, 1), 0) % SUBLANES

    def rotate_in_groups(z, dlt):
        z3 = z.reshape(z.shape[0] // SUBLANES, SUBLANES, z.shape[1])
        return pltpu.roll(z3, dlt, axis=1).reshape(z.shape)

    x_prev = tail_scr[...]
    if C > SUBLANES:
        x_prev = jnp.concatenate([x_prev, x[:C - SUBLANES]], axis=0)
    tail_scr[...] = x[C - SUBLANES:, :]
    xc = cb_ref[...] + x * cw_ref[CONV_W - 1:CONV_W, :]
    for dlt in range(1, CONV_W):
        sh = jnp.where(row_in_group >= dlt, rotate_in_groups(x, dlt), rotate_in_groups(x_prev, dlt))
        xc = xc + sh * cw_ref[CONV_W - 1 - dlt:CONV_W - dlt, :]

    gate_r = jax.nn.sigmoid(_dot(xc, wa_ref[...]) + ba_ref[...])
    gate_i = jax.nn.sigmoid(_dot(xc, wx_ref[...]) + bx_ref[...])
    log_a = -RG_C * gate_r * _softplus(-lam_ref[...])
    a = jnp.exp(log_a)
    th = jnp.tanh(log_a)
    u = jnp.sqrt(-2.0 * th / (1.0 - th)) * (gate_i * xc)

    dlt = 1
    while dlt < SUBLANES:
        keep = row_in_group >= dlt
        a_sh = jnp.where(keep, rotate_in_groups(a, dlt), 1.0)
        u_sh = jnp.where(keep, rotate_in_groups(u, dlt), 0.0)
        u = a * u_sh + u
        a = a * a_sh
        dlt *= 2
    carry = h_scr[...]
    groups = []
    for gi in range(C // SUBLANES):
        rows = slice(gi * SUBLANES, (gi + 1) * SUBLANES)
        groups.append(a[rows] * carry + u[rows])
        carry = groups[-1][SUBLANES - 1:SUBLANES, :]
    h = jnp.concatenate(groups, axis=0)
    last = min(t_valid, C) - 1
    h_scr[...] = h[last:last + 1, :]
    hout_ref[...] = h[last:last + 1, :]
    y_ref[...] = (h * jax.nn.gelu(gt)).astype(y_ref.dtype)


def _lru(pxq, conv_prev, h0, prm, batch, t_len, t_valid, chunk):
    W = LRU_WIDTH
    nc = t_len // chunk
    half = W // 2
    col0 = RWKV_PROJ // half
    assert RWKV_PROJ % half == 0
    part = lambda k: pl.BlockSpec((chunk, half), lambda b, c: (b * nc + c, col0 + k))
    vec = lambda n: pl.BlockSpec((1, n), lambda b, c: (0, 0))
    kern = functools.partial(_lru_kernel, chunk=chunk, t_valid=t_valid)
    return pl.pallas_call(
        kern,
        grid=(batch, nc),
        in_specs=[
            part(0), part(1), part(2), part(3),
            pl.BlockSpec((None, CONV_W - 1, W), lambda b, c: (b, 0, 0)),
            pl.BlockSpec((None, 1, W), lambda b, c: (b, 0, 0)),
            pl.BlockSpec((CONV_W, W), lambda b, c: (0, 0)),
            vec(W),
            pl.BlockSpec((W, W), lambda b, c: (0, 0)), vec(W),
            pl.BlockSpec((W, W), lambda b, c: (0, 0)), vec(W),
            vec(W),
        ],
        out_specs=[
            pl.BlockSpec((chunk, W), lambda b, c: (b * nc + c, 0)),
            pl.BlockSpec((None, 1, W), lambda b, c: (b, 0, 0)),
        ],
        out_shape=[
            jax.ShapeDtypeStruct((batch * t_len, W), _mix_dtype(chunk)),
            jax.ShapeDtypeStruct((batch, 1, W), F32),
        ],
        scratch_shapes=[pltpu.VMEM((SUBLANES, W), F32), pltpu.VMEM((1, W), F32)],
        compiler_params=_cparams(("parallel", "arbitrary")),
        name="rglru",
    )(pxq, pxq, pxq, pxq, conv_prev, h0, prm["conv_w"], prm["conv_b"], prm["wa_bd"], prm["ba"], prm["wx_bd"],
      prm["bx"], prm["lam"])


def _largest_divisor(n, cap):
    return max(u for u in range(1, cap + 1) if n % u == 0)


def _attn_prompt_kernel(q_ref, kil_ref, vil_ref, bias_ref, o_ref, k_ref, v_ref, og_scr, lse_scr, *, seq):
    B = ATTN_SPAN
    scale = ATTN_HEAD ** -0.5
    h = pl.program_id(1)
    R = ATTN_MERGE_ROWS

    def gather_head(i, carry):
        dst = pl.ds(pl.multiple_of(i * R, R), R)
        src = pl.ds(h + i * (R * ATTN_HEADS), R, stride=ATTN_HEADS)
        k_ref[dst, :] = kil_ref[src, :]
        v_ref[dst, :] = vil_ref[src, :]
        return carry

    lax.fori_loop(0, seq // R, gather_head, 0)

    blocks = [(gi, dil, cls + blk * (B * dil), blk == 0)
              for gi, (win, dil) in enumerate(DIL_PAIRS)
              for cls in range(dil) for blk in range(seq // (dil * B))]
    batches = [blocks[i:i + ATTN_UNROLL] for i in range(0, len(blocks), ATTN_UNROLL)]

    def rows(dil, start, first):
        start, n = (start, B) if first else (start - B * dil, 2 * B)
        return pl.ds(start, n) if dil == 1 else pl.ds(start, n, stride=dil)

    def scores(batch):
        return [_dot_nt(q_ref[rows(dil, qs, True), :], k_ref[rows(dil, qs, first), :]) * scale
                + (bias_ref[gi, :, B:] if first else bias_ref[gi]) for gi, dil, qs, first in batch]

    def softmax(ss):
        out = []
        for s in ss:
            mx = jnp.max(s, axis=-1, keepdims=True)
            pr = jnp.exp(s - mx)
            den = jnp.sum(pr, axis=-1, keepdims=True)
            out.append(((pr / den).astype(BF16), mx + jnp.log(den)))
        return out

    def values_and_store(batch, probs):
        outs = [_dot(pn, v_ref[rows(dil, qs, first), :]) for (gi, dil, qs, first), (pn, _) in zip(batch, probs)]
        for (gi, dil, qs, _), o, (_, lse) in zip(batch, outs, probs):
            og_scr[gi, rows(dil, qs, True), :] = o
            lse_scr[gi, rows(dil, qs, True), :] = jnp.broadcast_to(lse, (B, ATTN_HEAD))

    pending = scores(batches[0])
    for bi, batch in enumerate(batches):
        upcoming = scores(batches[bi + 1]) if bi + 1 < len(batches) else None
        values_and_store(batch, softmax(pending))
        pending = upcoming

    def merge(i, carry):
        sl = pl.ds(pl.multiple_of(i * R, R), R)
        ls = [lse_scr[gi, sl, :] for gi in range(len(DIL_PAIRS))]
        mx = functools.reduce(jnp.maximum, ls)
        ws = [jnp.exp(l - mx) for l in ls]
        tot = functools.reduce(lambda a, b: a + b, ws)
        acc = (ws[0] / tot) * og_scr[0, sl, :]
        for gi in range(1, len(DIL_PAIRS)):
            acc = acc + (ws[gi] / tot) * og_scr[gi, sl, :]
        o_ref[sl, :] = acc.astype(o_ref.dtype)
        return carry

    lax.fori_loop(0, seq // R, merge, 0)


def _attn_prompt(pxq, k_all, v_all, bias, layer, batch, seq):
    H = ATTN_HEADS
    G = len(DIL_PAIRS)
    q_col0 = (PXQ_WIDTH - ATTN_WIDTH) // ATTN_HEAD
    kern = functools.partial(_attn_prompt_kernel, seq=seq)
    kv = pl.BlockSpec((seq * H, ATTN_HEAD), lambda b, h: (layer * batch + b, 0))
    return pl.pallas_call(
        kern,
        grid=(batch, H),
        in_specs=[
            pl.BlockSpec((seq, ATTN_HEAD), lambda b, h: (b, q_col0 + h)),
            kv, kv,
            pl.BlockSpec((G, None, ATTN_SPAN, 2 * ATTN_SPAN), lambda b, h: (0, h, 0, 0)),
        ],
        out_specs=pl.BlockSpec((seq, ATTN_HEAD), lambda b, h: (b, h)),
        out_shape=jax.ShapeDtypeStruct((batch * seq, ATTN_WIDTH), BF16),
        scratch_shapes=[pltpu.VMEM((seq, ATTN_HEAD), F32), pltpu.VMEM((seq, ATTN_HEAD), F32),
                        pltpu.VMEM((G, seq, ATTN_HEAD), F32), pltpu.VMEM((G, seq, ATTN_HEAD), F32)],
        compiler_params=_cparams(("parallel", "arbitrary")),
        name="attn_prompt",
    )(pxq, k_all, v_all, bias)


def _attn_sample_kernel(pxq_ref, kn_ref, vn_ref, kc_ref, vc_ref, bias_ref, o_ref, q_scr, kn_scr, vn_scr, *, t_new, win):
    scale = ATTN_HEAD ** -0.5
    H = ATTN_HEADS
    G = len(DIL_PAIRS)
    TP = q_scr.shape[1]
    q_col0 = PXQ_WIDTH - ATTN_WIDTH
    q_scr[...] = jnp.zeros_like(q_scr)
    kn_scr[...] = jnp.zeros_like(kn_scr)
    vn_scr[...] = jnp.zeros_like(vn_scr)
    scores = []
    for h in range(H):
        q_scr[h, 0:t_new, :] = pxq_ref[:, q_col0 + h * ATTN_HEAD:q_col0 + (h + 1) * ATTN_HEAD]
        kn_scr[h, 0:t_new, :] = kn_ref[pl.ds(h, t_new, stride=H), :]
        vn_scr[h, 0:t_new, :] = vn_ref[pl.ds(h, t_new, stride=H), :]
        qb = q_scr[h].astype(BF16)
        kc = kc_ref[pl.ds(h, win, stride=H), :]
        scores.append(jnp.concatenate([_dot_nt(qb, kc), _dot_nt(qb, kn_scr[h])], axis=1) * scale)
    probs = []
    for h, s in enumerate(scores):
        prs, lses = [], []
        for gi in range(G):
            sg = s + bias_ref[gi, h]
            mx = jnp.max(sg, axis=-1, keepdims=True)
            pr = jnp.exp(sg - mx)
            den = jnp.sum(pr, axis=-1, keepdims=True)
            prs.append(pr / den)
            lses.append(mx + jnp.log(den))
        probs.append((jnp.concatenate(prs, axis=0).astype(BF16), lses))
    outs = []
    for h, (pall, _) in enumerate(probs):
        vc = vc_ref[pl.ds(h, win, stride=H), :]
        outs.append(_dot(pall[:, :win], vc) + _dot(pall[:, win:], vn_scr[h]))
    for h, (oall, (_, lses)) in enumerate(zip(outs, probs)):
        mx = functools.reduce(jnp.maximum, lses)
        ws = [jnp.exp(l - mx) for l in lses]
        tot = functools.reduce(lambda a, b: a + b, ws)
        acc = (ws[0] / tot) * oall[0:TP]
        for gi in range(1, G):
            acc = acc + (ws[gi] / tot) * oall[gi * TP:(gi + 1) * TP]
        o_ref[:, h * ATTN_HEAD:(h + 1) * ATTN_HEAD] = acc[0:t_new].astype(o_ref.dtype)


def _attn_sample(pxq, k_new, v_new, k_cache, v_cache, bias, layer, batch, t_new):
    H = ATTN_HEADS
    win = k_cache.shape[2] // H
    tp = bias.shape[2]
    kern = functools.partial(_attn_sample_kernel, t_new=t_new, win=win)
    cache = pl.BlockSpec((None, None, win * H, ATTN_HEAD), lambda b: (layer, b, 0, 0))
    new = pl.BlockSpec((None, t_new * H, ATTN_HEAD), lambda b: (layer * batch + b, 0, 0))
    return pl.pallas_call(
        kern,
        grid=(batch,),
        in_specs=[pl.BlockSpec((None, t_new, pxq.shape[2]), lambda b: (b, 0, 0)), new, new, cache, cache,
                  pl.BlockSpec(bias.shape, lambda b: (0, 0, 0, 0))],
        out_specs=pl.BlockSpec((None, t_new, ATTN_WIDTH), lambda b: (b, 0, 0)),
        out_shape=jax.ShapeDtypeStruct((batch, t_new, ATTN_WIDTH), F32),
        scratch_shapes=[pltpu.VMEM((H, tp, ATTN_HEAD), F32), pltpu.VMEM((H, LANES, ATTN_HEAD), F32),
                        pltpu.VMEM((H, LANES, ATTN_HEAD), F32)],
        compiler_params=_cparams(("parallel",)),
        name="attn_sample",
    )(pxq, k_new, v_new, k_cache, v_cache, bias)


def _bias_kernel(idx_ref, rb_ref, o_ref):
    h = pl.program_id(0)
    idx = idx_ref[...]
    acc = jnp.full(idx.shape, NEG, F32)
    for b in range(N_BUCKETS):
        acc = jnp.where(idx == b, rb_ref[b, h], acc)
    o_ref[...] = acc


def _bias_table(idx, rel_bias):
    g, r, c = idx.shape
    heads = rel_bias.shape[1]
    return pl.pallas_call(
        _bias_kernel,
        grid=(heads,),
        in_specs=[pl.BlockSpec((g, r, c), lambda h: (0, 0, 0)),
                  pl.BlockSpec(memory_space=pltpu.SMEM)],
        out_specs=pl.BlockSpec((g, None, r, c), lambda h: (0, h, 0, 0)),
        out_shape=jax.ShapeDtypeStruct((g, heads, r, c), F32),
        compiler_params=_cparams(("parallel",)),
        name="bias_table",
    )(jnp.asarray(idx, jnp.int32), rel_bias.astype(F32))


def _t5_bucket_static(dist):
    dist = np.asarray(dist, np.int64)
    exact = N_BUCKETS // 2
    d = np.maximum(dist, 1).astype(np.float64)
    large = exact + (np.log(d / exact) / math.log(REL_MAX_DIST / exact) * (N_BUCKETS - exact)).astype(np.int64)
    return np.where(dist < exact, dist, np.minimum(large, N_BUCKETS - 1))


def _prompt_bias_idx():
    B = ATTN_SPAN
    qi = np.arange(B)[:, None]
    kj = np.arange(2 * B)[None, :]
    delta = qi + B - kj
    valid = (delta >= 0) & (delta <= B)
    return np.stack([np.where(valid, _t5_bucket_static(np.clip(delta, 0, B) * dil), -1) for _, dil in DIL_PAIRS])


def _sample_bias_idx(t_new, t_pad, win, total):
    r = np.arange(total)[None, :]
    t = np.arange(t_pad)[:, None]
    dist = win + t - r
    tabs = []
    for _, dil in DIL_PAIRS:
        assert win >= dil * ATTN_SPAN
        valid = (dist >= 0) & (dist % dil == 0) & (dist <= dil * ATTN_SPAN) & (t < t_new)
        tabs.append(np.where(valid, _t5_bucket_static(np.clip(dist, 0, dil * ATTN_SPAN)), -1))
    return np.stack(tabs)


def _block_diag(blocks):
    n, c, d = blocks.shape
    eye = jnp.eye(n, dtype=blocks.dtype)
    return jnp.einsum("ncd,nm->ncmd", blocks, eye).reshape(n * c, n * d)


def _layer_params(l, a):
    W = RWKV_WIDTH
    row = lambda t: t[l].reshape(1, -1).astype(F32)

    def lora_pad(w, off):
        z = jnp.zeros((LORA_WIDTH, W), F32)
        return z.at[off:off + w.shape[0]].set(w).astype(BF16)

    head_of = np.arange(W) // RWKV_HEAD
    ones_bd = jnp.asarray((head_of[:, None] == head_of[None, :]).astype(np.float32), BF16)
    rwkv = dict(
        mu=row(a["rwkv_mu"]), w0=row(a["rwkv_w0"]), a0=row(a["rwkv_a0"]),
        w_up=lora_pad(a["rwkv_w_up"][l], 0),
        a_up=lora_pad(a["rwkv_a_up"][l], DECAY_LORA),
        g_up=lora_pad(a["rwkv_g_up"][l], DECAY_LORA + ICLR_LORA),
        k_k=row(a["rwkv_k_k"]), k_a=row(a["rwkv_k_a"]), r_k=row(a["rwkv_r_k"]),
        lnx_g=row(a["rwkv_lnx_g"]), lnx_b=row(a["rwkv_lnx_b"]),
        ones_bd=ones_bd,
    )
    lru = dict(
        conv_w=a["lru_conv_w"][l].astype(F32), conv_b=row(a["lru_conv_b"]),
        wa_bd=_block_diag(a["lru_wa"][l]).astype(BF16), ba=row(a["lru_ba"]),
        wx_bd=_block_diag(a["lru_wx"][l]).astype(BF16), bx=row(a["lru_bx"]),
        lam=row(a["lru_lambda"]),
    )
    return dict(rwkv=rwkv, lru=lru)


def _shared_params(a):
    gain = lambda t: t.reshape(t.shape[0], 1, -1).astype(F32)
    return dict(
        g_mix_pre=gain(a["norm_mix_pre"]), g_mix_post=gain(a["norm_mix_post"]),
        g_ffn_pre=gain(a["norm_ffn_pre"]), g_ffn_post=gain(a["norm_ffn_post"]),
    )


def _layer(x, layer, sp, wts, lp, batch, t_len, shift_prev, s0, conv_prev, h0, kv_all, attend):
    emit = wts is None
    src = sp["f32"] if emit else wts
    wl = layer if emit else 0
    copies = {}
    pxq, k_all, v_all, *w_copy = _in_proj(x, sp["g_mix_pre"], src["w_in"], layer, wl, *kv_all, emit_bf16=emit)
    copies["w_in"] = w_copy[0] if emit else None
    pxq3 = pxq.reshape(batch, t_len, pxq.shape[1])
    if t_len % RWKV_CHUNK == 0:
        y_a, s_fin = _rwkv(pxq3, shift_prev, s0, lp["rwkv"], batch, t_len, t_len, RWKV_CHUNK)
        y_a = y_a.reshape(batch * t_len, -1)
        y_b, h_fin = _lru(pxq, conv_prev, h0, lp["lru"], batch, t_len, t_len, LRU_CHUNK)
    else:
        assert t_len <= SUBLANES
        padded = jnp.pad(pxq3, ((0, 0), (0, SUBLANES - t_len), (0, 0)))
        y_a, s_fin = _rwkv(padded, shift_prev, s0, lp["rwkv"], batch, SUBLANES, t_len, SUBLANES)
        y_a = y_a[:, :t_len].reshape(batch * t_len, -1)
        y_b, h_fin = _lru(padded.reshape(batch * SUBLANES, pxq.shape[1]), conv_prev, h0, lp["lru"], batch, SUBLANES,
                          t_len, SUBLANES)
        y_b = y_b.reshape(batch, SUBLANES, -1)[:, :t_len].reshape(batch * t_len, -1)
    y_c = attend(pxq, k_all, v_all)

    x = _out_proj(y_a, y_b, y_c, src["w_out"], sp["g_mix_post"], x, layer, wl, emit_bf16=emit)
    if emit:
        x, copies["w_out"] = x
    x = _ffn(x, sp["g_ffn_pre"], src["w1"], src["w2"], sp["g_ffn_post"], layer, wl, emit_bf16=emit)
    if emit:
        x, copies["w1"], copies["w2"] = x

    lru_x = pxq3[:, :, RWKV_PROJ:RWKV_PROJ + LRU_WIDTH]
    if t_len >= CONV_W - 1:
        conv_new = lru_x[:, t_len - (CONV_W - 1):]
    else:
        conv_new = jnp.concatenate([conv_prev, lru_x], axis=1)[:, -(CONV_W - 1):]
    state = (pxq3[:, -1, :RWKV_PROJ], s_fin, conv_new, h_fin.reshape(batch, LRU_WIDTH))
    return x, state, (k_all, v_all), copies


def kernel(x_prompt, x_sample, state_rwkv_wkv, state_rwkv_shift, state_lru_h, state_lru_conv, cache_attn_k, cache_attn_v, rel_bias, norm_mix_pre, norm_mix_post, norm_ffn_pre, norm_ffn_post, w_in, w_out, rwkv_mu, rwkv_w0, rwkv_w_up, rwkv_a0, rwkv_a_up, rwkv_g_up, rwkv_k_k, rwkv_k_a, rwkv_r_k, rwkv_lnx_g, rwkv_lnx_b, lru_conv_w, lru_conv_b, lru_wa, lru_ba, lru_wx, lru_bx, lru_lambda, ffn_w1, ffn_w2):
    a = dict(norm_mix_pre=norm_mix_pre, norm_mix_post=norm_mix_post, norm_ffn_pre=norm_ffn_pre,
             norm_ffn_post=norm_ffn_post, w_in=w_in, w_out=w_out, rwkv_mu=rwkv_mu, rwkv_w0=rwkv_w0,
             rwkv_w_up=rwkv_w_up, rwkv_a0=rwkv_a0, rwkv_a_up=rwkv_a_up, rwkv_g_up=rwkv_g_up, rwkv_k_k=rwkv_k_k,
             rwkv_k_a=rwkv_k_a, rwkv_r_k=rwkv_r_k, rwkv_lnx_g=rwkv_lnx_g, rwkv_lnx_b=rwkv_lnx_b,
             lru_conv_w=lru_conv_w, lru_conv_b=lru_conv_b, lru_wa=lru_wa, lru_ba=lru_ba, lru_wx=lru_wx,
             lru_bx=lru_bx, lru_lambda=lru_lambda, ffn_w1=ffn_w1, ffn_w2=ffn_w2)
    depth = w_in.shape[0]
    pb, seq, _ = x_prompt.shape
    sb, t_new, _ = x_sample.shape
    win = cache_attn_k.shape[2]
    keep = min(ATTN_WINDOW, seq)
    total = win + LANES
    assert t_new <= LANES and seq % (DIL_PAIRS[-1][1] * ATTN_SPAN) == 0

    bias_p = _bias_table(_prompt_bias_idx(), rel_bias)
    bias_s = _bias_table(_sample_bias_idx(t_new, SUBLANES, win, total), rel_bias)
    k_cache = cache_attn_k.reshape(depth, sb, win * ATTN_HEADS, ATTN_HEAD)
    v_cache = cache_attn_v.reshape(depth, sb, win * ATTN_HEADS, ATTN_HEAD)

    xp = x_prompt.reshape(pb * seq, D_MODEL)
    xs = x_sample.reshape(sb * t_new, D_MODEL)
    new_p, new_s = [], []
    sp = _shared_params(a)
    kv_p = tuple(jnp.zeros((depth * pb * seq * ATTN_HEADS, ATTN_HEAD), F32) for _ in range(2))
    kv_s = tuple(jnp.zeros((depth * sb * t_new * ATTN_HEADS, ATTN_HEAD), F32) for _ in range(2))
    sp["f32"] = dict(w_in=w_in.astype(F32), w_out=w_out.astype(F32), w1=ffn_w1.astype(F32), w2=ffn_w2.astype(F32))
    for l in range(depth):
        lp = _layer_params(l, a)
        attend_s = lambda pxq, k_all, v_all, l=l: _attn_sample(
            pxq.reshape(sb, t_new, pxq.shape[1]), k_all.reshape(depth * sb, t_new * ATTN_HEADS, ATTN_HEAD),
            v_all.reshape(depth * sb, t_new * ATTN_HEADS, ATTN_HEAD), k_cache, v_cache, bias_s, l, sb, t_new,
        ).reshape(sb * t_new, ATTN_WIDTH)
        xs, st_s, kv_s, wts = _layer(xs, l, sp, None, lp, sb, t_new,
                                     state_rwkv_shift[l].reshape(sb, 1, RWKV_PROJ), state_rwkv_wkv[l].astype(F32),
                                     state_lru_conv[l], state_lru_h[l].reshape(sb, 1, LRU_WIDTH), kv_s, attend_s)
        attend_p = lambda pxq, k_all, v_all, l=l: _attn_prompt(pxq, k_all, v_all, bias_p, l, pb, seq)
        xp, st_p, kv_p, _ = _layer(xp, l, sp, wts, lp, pb, seq,
                                   jnp.zeros((pb, 1, RWKV_PROJ), F32),
                                   jnp.zeros((pb, RWKV_HEADS, RWKV_HEAD, RWKV_HEAD), F32),
                                   jnp.zeros((pb, CONV_W - 1, LRU_WIDTH), F32), jnp.zeros((pb, 1, LRU_WIDTH), F32),
                                   kv_p, attend_p)
        new_p.append(st_p)
        new_s.append(st_s)

    stack = lambda sts, i: jnp.stack([s[i] for s in sts])
    k_p, v_p = (t.reshape(depth, pb, seq, ATTN_HEADS, ATTN_HEAD)[:, :, seq - keep:] for t in kv_p)
    k_s, v_s = (t.reshape(depth, sb, t_new, ATTN_HEADS, ATTN_HEAD) for t in kv_s)
    return (xp.reshape(pb, seq, D_MODEL), xs.reshape(sb, t_new, D_MODEL),
            stack(new_p, 1), stack(new_s, 1), stack(new_p, 0), stack(new_s, 0),
            stack(new_p, 3), stack(new_s, 3), stack(new_p, 2), stack(new_s, 2),
            k_p, k_s, v_p, v_s)
```

```python
import functools
import math

import numpy as np
import jax
import jax.numpy as jnp
from jax import lax
from jax.experimental import pallas as pl
from jax.experimental.pallas import tpu as pltpu

F32 = jnp.float32
BF16 = jnp.bfloat16

D_MODEL = 2048
RWKV_WIDTH = 512
RWKV_HEAD = 64
RWKV_HEADS = RWKV_WIDTH // RWKV_HEAD
DECAY_LORA = 64
ICLR_LORA = 64
GATE_LORA = 128
LORA_WIDTH = DECAY_LORA + ICLR_LORA + GATE_LORA
RWKV_PROJ = 3 * RWKV_WIDTH + LORA_WIDTH
GN_EPS = 64e-5
LRU_WIDTH = 512
LRU_BLOCKS = 8
LRU_BLOCK = LRU_WIDTH // LRU_BLOCKS
CONV_W = 4
RG_C = 8.0
ATTN_WIDTH = 1024
ATTN_HEAD = 128
ATTN_HEADS = ATTN_WIDTH // ATTN_HEAD
DIL_PAIRS = ((128, 1), (512, 4), (2048, 16))
ATTN_SPAN = 128
ATTN_WINDOW = 2048
N_BUCKETS = 32
REL_MAX_DIST = ATTN_WINDOW
D_FF = 4 * D_MODEL
RMS_EPS = 1e-6
NEG = -1e30

LANES = 128
SUBLANES = 8
MXU_TILE = 256
VMEM_LIMIT_BYTES = 56 * 1024 * 1024

ROW_TILE = 1024
IN_COL_TILE = 256
IN_FUSED_TILES = 2
OUT_ROW_TILE = 512
FFN_ROW_TILE = 512
FFN_COL_TILE = 1024
FFN_CAST_COL_TILE = 512
RWKV_CHUNK = 64
RWKV_SEQS_PER_STEP = 4
LRU_CHUNK = 256
ATTN_MERGE_ROWS = 256
ATTN_UNROLL = 6


def _cparams(sem):
    return pltpu.CompilerParams(dimension_semantics=sem, vmem_limit_bytes=VMEM_LIMIT_BYTES)


def _dot(a, b):
    return jnp.dot(a.astype(BF16), b.astype(BF16), preferred_element_type=F32)


def _dot_nt(a, b):
    return lax.dot_general(a.astype(BF16), b.astype(BF16), (((1,), (1,)), ((), ())), preferred_element_type=F32)


def _dot_tn(a, b):
    return lax.dot_general(a.astype(BF16), b.astype(BF16), (((0,), (0,)), ((), ())), preferred_element_type=F32)


def _softplus(z):
    return jnp.maximum(z, 0.0) + jnp.log1p(jnp.exp(-jnp.abs(z)))


def _mix_dtype(rows):
    return BF16 if rows % (2 * SUBLANES) == 0 else F32


def _rms(x, g):
    ms = jnp.mean(x * x, axis=-1, keepdims=True)
    return x * lax.rsqrt(ms + RMS_EPS) * g


PXQ_WIDTH = RWKV_PROJ + 2 * LRU_WIDTH + ATTN_WIDTH
PXQ_TILES = PXQ_WIDTH // IN_COL_TILE
KV_TILES = ATTN_WIDTH // IN_COL_TILE
HEADS_PER_TILE = IN_COL_TILE // ATTN_HEAD


def _in_proj_kernel(x_ref, g_ref, *refs, fuse):
    w_refs = refs[:fuse]
    pxq_ref, k_ref, v_ref = refs[fuse + 2:fuse + 5]
    rest = refs[fuse + 5:]
    h_scr = rest[-1]
    j = pl.program_id(1)
    tm = x_ref.shape[0]
    pxq_steps = pl.cdiv(PXQ_TILES, fuse)
    kv_steps = KV_TILES // fuse

    @pl.when(j == 0)
    def _():
        h_scr[...] = _rms(x_ref[...], g_ref[...]).astype(BF16)

    ws = [w_ref[...].astype(BF16) for w_ref in w_refs]
    if len(rest) > 1:
        rest[0][...] = ws[0]
    w = ws[0] if fuse == 1 else jnp.concatenate(ws, axis=1)
    acc = jnp.dot(h_scr[...], w, preferred_element_type=F32)

    @pl.when(j < pxq_steps)
    def _():
        pxq_ref[...] = acc

    def scatter_heads(o_ref, step):
        for hh in range(fuse * HEADS_PER_TILE):
            head = step * (fuse * HEADS_PER_TILE) + hh
            o_ref[pl.ds(head, tm, stride=ATTN_HEADS), :] = acc[:, hh * ATTN_HEAD:(hh + 1) * ATTN_HEAD]

    @pl.when((j >= pxq_steps) & (j < pxq_steps + kv_steps))
    def _():
        scatter_heads(k_ref, j - pxq_steps)

    @pl.when(j >= pxq_steps + kv_steps)
    def _():
        scatter_heads(v_ref, j - pxq_steps - kv_steps)


def _in_proj(x, g, w, layer, w_layer, k_all, v_all, emit_bf16=False):
    m, d = x.shape
    n = w.shape[2]
    assert n == PXQ_WIDTH + 2 * ATTN_WIDTH
    tm = min(ROW_TILE, m)
    row_tiles = m // tm
    assert not emit_bf16 or row_tiles == 1
    fuse = 1 if emit_bf16 else IN_FUSED_TILES
    assert KV_TILES % fuse == 0
    pxq_steps = pl.cdiv(PXQ_TILES, fuse)
    spare = pxq_steps * fuse - PXQ_TILES
    width = fuse * IN_COL_TILE

    def w_spec(slot):
        def index(i, j):
            tile = jnp.where(j < pxq_steps, jnp.minimum(fuse * j + slot, PXQ_TILES - 1), fuse * j + slot - spare)
            return (w_layer, 0, tile)
        return pl.BlockSpec((None, d, IN_COL_TILE), index)

    kv_spec = pl.BlockSpec((tm * ATTN_HEADS, ATTN_HEAD), lambda i, j: (layer * row_tiles + i, 0))
    out_specs = [
        pl.BlockSpec((tm, width), lambda i, j: (i, jnp.minimum(j, pxq_steps - 1))),
        kv_spec, kv_spec,
    ]
    out_shape = [
        jax.ShapeDtypeStruct((m, pxq_steps * width), F32),
        jax.ShapeDtypeStruct(k_all.shape, F32),
        jax.ShapeDtypeStruct(v_all.shape, F32),
    ]
    if emit_bf16:
        out_specs.append(pl.BlockSpec((None, d, IN_COL_TILE), lambda i, j: (0, 0, j)))
        out_shape.append(jax.ShapeDtypeStruct((1, d, n), BF16))
    return pl.pallas_call(
        functools.partial(_in_proj_kernel, fuse=fuse),
        grid=(row_tiles, pxq_steps + 2 * (KV_TILES // fuse)),
        in_specs=[
            pl.BlockSpec((tm, d), lambda i, j: (i, 0)),
            pl.BlockSpec((None, 1, d), lambda i, j: (layer, 0, 0)),
            *[w_spec(slot) for slot in range(fuse)],
            pl.BlockSpec(memory_space=pl.ANY),
            pl.BlockSpec(memory_space=pl.ANY),
        ],
        out_specs=out_specs,
        out_shape=out_shape,
        input_output_aliases={2 + fuse: 1, 3 + fuse: 2},
        scratch_shapes=[pltpu.VMEM((tm, d), BF16)],
        compiler_params=_cparams(("parallel", "arbitrary")),
        name="in_proj",
    )(x, g, *([w] * fuse), k_all, v_all)


def _out_proj_kernel(ya_ref, yb_ref, yc_ref, w_ref, g_ref, x_ref, o_ref, *w_copy):
    c1, c2 = RWKV_WIDTH, RWKV_WIDTH + LRU_WIDTH
    if w_copy:
        acc = None
        for y_ref, rows in [(ya_ref, slice(0, c1)), (yb_ref, slice(c1, c2)), (yc_ref, slice(c2, None))]:
            w = w_ref[rows, :].astype(BF16)
            w_copy[0][rows, :] = w
            t = jnp.dot(y_ref[...].astype(BF16), w, preferred_element_type=F32)
            acc = t if acc is None else acc + t
    else:
        y = jnp.concatenate([r[...].astype(BF16) for r in (ya_ref, yb_ref, yc_ref)], axis=1)
        acc = jnp.dot(y, w_ref[...].astype(BF16), preferred_element_type=F32)
    o_ref[...] = x_ref[...] + _rms(acc, g_ref[...])


def _out_proj(ya, yb, yc, w, g, x, layer, w_layer, emit_bf16=False):
    m, d = x.shape
    tm = min(OUT_ROW_TILE, m)
    assert not emit_bf16 or m == tm
    out_specs = [pl.BlockSpec((tm, d), lambda i: (i, 0))]
    out_shape = [jax.ShapeDtypeStruct((m, d), F32)]
    if emit_bf16:
        out_specs.append(pl.BlockSpec((None, d, d), lambda i: (0, 0, 0)))
        out_shape.append(jax.ShapeDtypeStruct((1, d, d), BF16))
    res = pl.pallas_call(
        _out_proj_kernel,
        grid=(m // tm,),
        in_specs=[
            pl.BlockSpec((tm, RWKV_WIDTH), lambda i: (i, 0)),
            pl.BlockSpec((tm, LRU_WIDTH), lambda i: (i, 0)),
            pl.BlockSpec((tm, ATTN_WIDTH), lambda i: (i, 0)),
            pl.BlockSpec((None, d, d), lambda i: (w_layer, 0, 0)),
            pl.BlockSpec((None, 1, d), lambda i: (layer, 0, 0)),
            pl.BlockSpec((tm, d), lambda i: (i, 0)),
        ],
        out_specs=out_specs,
        out_shape=out_shape,
        compiler_params=_cparams(("parallel",)),
        name="out_proj",
    )(ya, yb, yc, w, g, x)
    return res if emit_bf16 else res[0]


def _ffn_kernel(x_ref, g1_ref, w1_ref, w2_ref, g2_ref, o_ref, *rest):
    h_scr, acc_scr = rest[-2:]
    j = pl.program_id(1)

    @pl.when(j == 0)
    def _():
        h_scr[...] = _rms(x_ref[...], g1_ref[...]).astype(BF16)
        acc_scr[...] = jnp.zeros_like(acc_scr)

    w1 = w1_ref[...].astype(BF16)
    w2 = w2_ref[...].astype(BF16)
    if len(rest) > 2:
        rest[0][...] = w1
        rest[1][...] = w2
    u = jnp.dot(h_scr[...], w1, preferred_element_type=F32)
    u = jnp.square(jnp.maximum(u, 0.0)).astype(BF16)
    acc_scr[...] += jnp.dot(u, w2, preferred_element_type=F32)

    @pl.when(j == pl.num_programs(1) - 1)
    def _():
        o_ref[...] = x_ref[...] + _rms(acc_scr[...], g2_ref[...])


def _ffn(x, g1, w1, w2, g2, layer, w_layer, emit_bf16=False):
    m, d = x.shape
    f = w1.shape[2]
    tm = min(FFN_ROW_TILE, m)
    tf = FFN_CAST_COL_TILE if emit_bf16 else FFN_COL_TILE
    assert not emit_bf16 or m == tm
    out_specs = [pl.BlockSpec((tm, d), lambda i, j: (i, 0))]
    out_shape = [jax.ShapeDtypeStruct((m, d), F32)]
    if emit_bf16:
        out_specs += [pl.BlockSpec((None, d, tf), lambda i, j: (0, 0, j)),
                      pl.BlockSpec((None, tf, d), lambda i, j: (0, j, 0))]
        out_shape += [jax.ShapeDtypeStruct((1, d, f), BF16), jax.ShapeDtypeStruct((1, f, d), BF16)]
    res = pl.pallas_call(
        _ffn_kernel,
        grid=(m // tm, f // tf),
        in_specs=[
            pl.BlockSpec((tm, d), lambda i, j: (i, 0)),
            pl.BlockSpec((None, 1, d), lambda i, j: (layer, 0, 0)),
            pl.BlockSpec((None, d, tf), lambda i, j: (w_layer, 0, j)),
            pl.BlockSpec((None, tf, d), lambda i, j: (w_layer, j, 0)),
            pl.BlockSpec((None, 1, d), lambda i, j: (layer, 0, 0)),
        ],
        out_specs=out_specs,
        out_shape=out_shape,
        scratch_shapes=[pltpu.VMEM((tm, d), BF16), pltpu.VMEM((tm, d), F32)],
        compiler_params=_cparams(("parallel", "arbitrary")),
        name="ffn",
    )(x, g1, w1, w2, g2)
    return res if emit_bf16 else res[0]


def _rwkv_groups(chunk):
    lanes = RWKV_HEADS * chunk
    return lanes // MXU_TILE if lanes % (2 * MXU_TILE) == 0 else 1


def _rwkv_kernel(p_ref, shift_ref, s0_ref, mu_ref, w0_ref, wup_ref, a0_ref, aup_ref, gup_ref, kk_ref, ka_ref,
                 rk_ref, lg_ref, lb_ref, ones_ref,
                 y_ref, sout_ref, s_scr, prev_scr, *, chunk, t_valid):
    C = chunk
    W = RWKV_WIDTH
    H = RWKV_HEADS
    N = RWKV_HEAD
    nb = p_ref.shape[0]
    c = pl.program_id(1)

    G = _rwkv_groups(C)
    hpg = H // G
    wl = W // G
    gc = hpg * C

    @pl.when(c == 0)
    def _():
        s_scr[...] = jnp.zeros_like(s_scr)
        for h in range(H):
            o = (h % hpg) * N
            s_scr[:, h // hpg, o:o + N, o:o + N] = s0_ref[:, h]
        prev_scr[...] = shift_ref[...]

    inv_n = 1.0 / RWKV_HEAD
    shared = dict(wup=[wup_ref], aup=[aup_ref], gup=[gup_ref],
                  ones=[ones_ref.at[g * wl:(g + 1) * wl, g * wl:(g + 1) * wl] for g in range(G)])

    def hi_lo(x):
        hi = x.astype(BF16).astype(F32)
        return [hi, x - hi]

    row = lax.broadcasted_iota(jnp.int32, (C, 1), 0)

    def cumsum_rows(z):
        dlt = 1
        while dlt < C:
            z = z + jnp.where(row >= dlt, pltpu.roll(z, dlt, axis=0), 0.0)
            dlt *= 2
        return z

    ti = lax.broadcasted_iota(jnp.int32, (C, H * C), 0)
    si = lax.broadcasted_iota(jnp.int32, (C, H * C), 1) % C
    strict = ti > si
    incl = ti >= si
    eye = (ti == si).astype(F32)
    blk_r = lax.broadcasted_iota(jnp.int32, (gc, 1), 0) // C
    mask_ch = blk_r == lax.broadcasted_iota(jnp.int32, (1, wl), 1) // N
    mask_cc = blk_r == lax.broadcasted_iota(jnp.int32, (1, gc), 1) // C
    mask_ss = (lax.broadcasted_iota(jnp.int32, (wl, 1), 0) // N
               == lax.broadcasted_iota(jnp.int32, (1, wl), 1) // N)

    def block_diag(x, mask, width):
        out = []
        for g in range(G):
            tiled = jnp.concatenate([x[:, g * width:(g + 1) * width]] * hpg, axis=0)
            out.append(jnp.where(mask, tiled, 0.0).astype(BF16))
        return out

    def per_head(a_cat, bds):
        a_b = a_cat.astype(BF16)
        return jnp.concatenate([jnp.dot(a_b[:, g * gc:(g + 1) * gc], bds[g], preferred_element_type=F32)
                                for g in range(G)], axis=1)

    def one_sequence(bi):
        p = p_ref[bi]
        shifted = jnp.where(row == 0, prev_scr[bi], pltpu.roll(p, 1, axis=0))
        prev_scr[bi] = p_ref[bi, C - 1:C, :]
        m = p + (shifted - p) * mu_ref[...]
        r = m[:, 0:W]
        k = m[:, W:2 * W]
        v = m[:, 2 * W:3 * W]
        x = m[:, 3 * W:]
        lw, la, gate = yield [("wup", jnp.tanh(x)), ("aup", x), ("gup", jax.nn.sigmoid(x))]
        w = w0_ref[...] + lw
        a = jax.nn.sigmoid(a0_ref[...] + la)
        softplus_neg_w = jnp.maximum(-w, 0.0) + jnp.log(1.0 + jnp.exp(-jnp.abs(w)))
        loga = -jnp.exp(-softplus_neg_w - 0.5)
        kk = k * kk_ref[...]
        k2 = k * (1.0 + (a - 1.0) * ka_ref[...])
        (ss,) = yield [("ones", jnp.concatenate(hi_lo(kk * kk) + hi_lo(r * k2 * rk_ref[...]), axis=0))]
        kk = kk / jnp.maximum(jnp.sqrt(ss[0:C] + ss[C:2 * C]), 1e-12)
        bonus = (ss[2 * C:3 * C] + ss[3 * C:]) * v
        if t_valid < C:
            live = row < t_valid
            loga = jnp.where(live, loga, 0.0)
            kk = jnp.where(live, kk, 0.0)
            k2 = jnp.where(live, k2, 0.0)
        cl = cumsum_rows(loga)
        cl_last = cl[C - 1:C, :]
        e_neg = jnp.exp(-cl)
        e_rem = jnp.exp(cl_last - cl)
        kka = kk * a
        al = -kk * jnp.exp(cl - loga)
        rt = r * jnp.exp(cl)
        be_bd = block_diag(kka * e_neg, mask_ch, wl)
        kt_bd = block_diag(k2 * e_neg, mask_ch, wl)
        bh = kka * e_rem
        kh = k2 * e_rem
        lhs = jnp.concatenate([al, rt], axis=0).astype(BF16)
        lhs_g = [lhs[:, g * wl:(g + 1) * wl] for g in range(G)]
        g_b = jnp.concatenate([_dot_nt(lhs_g[g], be_bd[g]) for g in range(G)], axis=1)
        g_k = jnp.concatenate([_dot_nt(lhs_g[g], kt_bd[g]) for g in range(G)], axis=1)
        n_cat = jnp.where(strict, g_b[0:C], 0.0)
        a_ak = jnp.where(strict, g_k[0:C], 0.0)
        a_rb = jnp.where(incl, g_b[C:], 0.0)
        a_rk = jnp.where(incl, g_k[C:], 0.0)
        s_prev = [s_scr[bi, g] for g in range(G)]
        proj = jnp.concatenate([_dot_nt(lhs_g[g], s_prev[g]) for g in range(G)], axis=1)
        doublings = max(int(math.log2(C)) - 1, 0)
        t_cat = eye + n_cat
        pw = n_cat
        if doublings:
            pw = per_head(n_cat, block_diag(n_cat, mask_cc, gc))
        yield None
        for it in range(doublings):
            pw_bd = block_diag(pw, mask_cc, gc)
            if it < doublings - 1:
                both = per_head(jnp.concatenate([t_cat, pw], axis=0), pw_bd)
                t_cat = t_cat + both[0:C]
                pw = both[C:]
            else:
                t_cat = t_cat + per_head(t_cat, pw_bd)
            yield None
        v_bd = block_diag(v, mask_ch, wl)
        rhs = proj[0:C] + per_head(a_ak, v_bd)
        yield None
        u = per_head(t_cat, block_diag(rhs, mask_ch, wl))
        yield None
        y = proj[C:] + per_head(a_rb, block_diag(u, mask_ch, wl)) + per_head(a_rk, v_bd)
        pad = LANES - 2 * C
        uv = jnp.concatenate([u, v] + ([jnp.zeros((pad, W), F32)] if pad > 0 else []), axis=0).astype(BF16)
        bk = jnp.concatenate([bh, kh] + ([jnp.zeros((pad, W), F32)] if pad > 0 else []), axis=0).astype(BF16)
        decay = jnp.exp(cl_last)
        for g in range(G):
            lanes = slice(g * wl, (g + 1) * wl)
            upd = _dot_tn(uv[:, lanes], bk[:, lanes])
            s_scr[bi, g] = s_prev[g] * decay[:, lanes] + jnp.where(mask_ss, upd, 0.0)

        @pl.when(c == pl.num_programs(1) - 1)
        def _():
            for h in range(H):
                o = (h % hpg) * N
                sout_ref[bi, h] = s_scr[bi, h // hpg, o:o + N, o:o + N]

        (sy,) = yield [("ones", jnp.concatenate(hi_lo(y), axis=0))]
        d = y - (sy[0:C] + sy[C:]) * inv_n
        (sd,) = yield [("ones", jnp.concatenate(hi_lo(d * d), axis=0))]
        var = (sd[0:C] + sd[C:]) * inv_n
        yn = d * lax.rsqrt(var + GN_EPS) * lg_ref[...] + lb_ref[...]
        y_ref[bi] = ((yn + bonus) * gate).astype(y_ref.dtype)

    seqs = [one_sequence(bi) for bi in range(nb)]
    replies = [None] * nb
    while True:
        asks, finished = [], 0
        for seq, reply in zip(seqs, replies):
            try:
                asks.append(seq.send(reply))
            except StopIteration:
                finished += 1
        if finished:
            assert finished == nb
            break
        if asks[0] is None:
            replies = [None] * nb
            continue
        replies = [[] for _ in range(nb)]
        for qi, (name, _) in enumerate(asks[0]):
            lhs = jnp.concatenate([ask[qi][1] for ask in asks], axis=0).astype(BF16)
            parts = shared[name]
            kw = lhs.shape[1] // len(parts)
            z = jnp.concatenate([jnp.dot(lhs[:, g * kw:(g + 1) * kw], part[...], preferred_element_type=F32)
                                 for g, part in enumerate(parts)], axis=1)
            rows = z.shape[0] // nb
            for si in range(nb):
                replies[si].append(z[si * rows:(si + 1) * rows])


def _rwkv(p, shift_prev, s0, prm, layer, batch, t_len, t_valid, chunk):
    W = RWKV_WIDTH
    nc = t_len // chunk
    nb = _largest_divisor(batch, RWKV_SEQS_PER_STEP)
    vec = lambda n: pl.BlockSpec((None, 1, n), lambda b, c: (layer, 0, 0))
    mat = lambda r, n: pl.BlockSpec((None, r, n), lambda b, c: (layer, 0, 0))
    state = pl.BlockSpec((nb, RWKV_HEADS, RWKV_HEAD, RWKV_HEAD), lambda b, c: (b, 0, 0, 0))
    groups = _rwkv_groups(chunk)
    kern = functools.partial(_rwkv_kernel, chunk=chunk, t_valid=t_valid)
    return pl.pallas_call(
        kern,
        grid=(batch // nb, nc),
        in_specs=[
            pl.BlockSpec((nb, chunk, RWKV_PROJ), lambda b, c: (b, c, 0)),
            pl.BlockSpec((nb, 1, RWKV_PROJ), lambda b, c: (b, 0, 0)),
            state,
            vec(RWKV_PROJ), vec(W), mat(LORA_WIDTH, W), vec(W), mat(LORA_WIDTH, W), mat(LORA_WIDTH, W),
            vec(W), vec(W), vec(W), vec(W), vec(W), pl.BlockSpec((W, W), lambda b, c: (0, 0)),
        ],
        out_specs=[pl.BlockSpec((nb, chunk, W), lambda b, c: (b, c, 0)), state],
        out_shape=[
            jax.ShapeDtypeStruct((batch, t_len, W), _mix_dtype(chunk)),
            jax.ShapeDtypeStruct((batch, RWKV_HEADS, RWKV_HEAD, RWKV_HEAD), F32),
        ],
        scratch_shapes=[pltpu.VMEM((nb, groups, W // groups, W // groups), F32),
                        pltpu.VMEM((nb, 1, RWKV_PROJ), F32)],
        compiler_params=_cparams(("parallel", "arbitrary")),
        name="rwkv7",
    )(p, shift_prev, s0, prm["mu"], prm["w0"], prm["w_up"], prm["a0"], prm["a_up"], prm["g_up"],
      prm["k_k"], prm["k_a"], prm["r_k"], prm["lnx_g"], prm["lnx_b"], prm["ones_bd"])


def _lru_kernel(x0_ref, x1_ref, g0_ref, g1_ref, conv_ref, h0_ref, cw_ref, cb_ref, wa_ref, ba_ref, wx_ref, bx_ref,
                lam_ref, y_ref, hout_ref, tail_scr, h_scr, *, chunk, t_valid):
    C = chunk
    c = pl.program_id(1)

    @pl.when(c == 0)
    def _():
        tail_scr[...] = jnp.zeros_like(tail_scr)
        tail_scr[SUBLANES - (CONV_W - 1):, :] = conv_ref[...]
        h_scr[...] = h0_ref[...]

    x = jnp.concatenate([x0_ref[...], x1_ref[...]], axis=1)
    gt = jnp.concatenate([g0_ref[...], g1_ref[...]], axis=1)
    row_in_group = lax.broadcasted_iota(jnp.int32, (C, 1), 0) % SUBLANES

    def rotate_in_groups(z, dlt):
        z3 = z.reshape(z.shape[0] // SUBLANES, SUBLANES, z.shape[1])
        return pltpu.roll(z3, dlt, axis=1).reshape(z.shape)

    x_prev = tail_scr[...]
    if C > SUBLANES:
        x_prev = jnp.concatenate([x_prev, x[:C - SUBLANES]], axis=0)
    tail_scr[...] = x[C - SUBLANES:, :]
    xc = cb_ref[...] + x * cw_ref[CONV_W - 1:CONV_W, :]
    for dlt in range(1, CONV_W):
        sh = jnp.where(row_in_group >= dlt, rotate_in_groups(x, dlt), rotate_in_groups(x_prev, dlt))
        xc = xc + sh * cw_ref[CONV_W - 1 - dlt:CONV_W - dlt, :]

    gate_r = jax.nn.sigmoid(_dot(xc, wa_ref[...]) + ba_ref[...])
    gate_i = jax.nn.sigmoid(_dot(xc, wx_ref[...]) + bx_ref[...])
    log_a = -RG_C * gate_r * _softplus(-lam_ref[...])
    a = jnp.exp(log_a)
    th = jnp.tanh(log_a)
    u = jnp.sqrt(-2.0 * th / (1.0 - th)) * (gate_i * xc)

    dlt = 1
    while dlt < SUBLANES:
        keep = row_in_group >= dlt
        a_sh = jnp.where(keep, rotate_in_groups(a, dlt), 1.0)
        u_sh = jnp.where(keep, rotate_in_groups(u, dlt), 0.0)
        u = a * u_sh + u
        a = a * a_sh
        dlt *= 2
    carry = h_scr[...]
    groups = []
    for gi in range(C // SUBLANES):
        rows = slice(gi * SUBLANES, (gi + 1) * SUBLANES)
        groups.append(a[rows] * carry + u[rows])
        carry = groups[-1][SUBLANES - 1:SUBLANES, :]
    h = jnp.concatenate(groups, axis=0)
    last = min(t_valid, C) - 1
    h_scr[...] = h[last:last + 1, :]
    hout_ref[...] = h[last:last + 1, :]
    y_ref[...] = (h * jax.nn.gelu(gt)).astype(y_ref.dtype)


def _lru(pxq, conv_prev, h0, prm, layer, batch, t_len, t_valid, chunk):
    W = LRU_WIDTH
    nc = t_len // chunk
    half = W // 2
    col0 = RWKV_PROJ // half
    assert RWKV_PROJ % half == 0
    part = lambda k: pl.BlockSpec((chunk, half), lambda b, c: (b * nc + c, col0 + k))
    vec = lambda n: pl.BlockSpec((None, 1, n), lambda b, c: (layer, 0, 0))
    mat = lambda r, n: pl.BlockSpec((None, r, n), lambda b, c: (layer, 0, 0))
    kern = functools.partial(_lru_kernel, chunk=chunk, t_valid=t_valid)
    return pl.pallas_call(
        kern,
        grid=(batch, nc),
        in_specs=[
            part(0), part(1), part(2), part(3),
            pl.BlockSpec((None, CONV_W - 1, W), lambda b, c: (b, 0, 0)),
            pl.BlockSpec((None, 1, W), lambda b, c: (b, 0, 0)),
            mat(CONV_W, W),
            vec(W),
            mat(W, W), vec(W),
            mat(W, W), vec(W),
            vec(W),
        ],
        out_specs=[
            pl.BlockSpec((chunk, W), lambda b, c: (b * nc + c, 0)),
            pl.BlockSpec((None, 1, W), lambda b, c: (b, 0, 0)),
        ],
        out_shape=[
            jax.ShapeDtypeStruct((batch * t_len, W), _mix_dtype(chunk)),
            jax.ShapeDtypeStruct((batch, 1, W), F32),
        ],
        scratch_shapes=[pltpu.VMEM((SUBLANES, W), F32), pltpu.VMEM((1, W), F32)],
        compiler_params=_cparams(("parallel", "arbitrary")),
        name="rglru",
    )(pxq, pxq, pxq, pxq, conv_prev, h0, prm["conv_w"], prm["conv_b"], prm["wa_bd"], prm["ba"], prm["wx_bd"],
      prm["bx"], prm["lam"])


def _largest_divisor(n, cap):
    return max(u for u in range(1, cap + 1) if n % u == 0)


def _attn_prompt_kernel(q_ref, kil_ref, vil_ref, bias_ref, o_ref, k_ref, v_ref, og_scr, lse_scr, *, seq):
    B = ATTN_SPAN
    scale = ATTN_HEAD ** -0.5
    h = pl.program_id(1)
    R = ATTN_MERGE_ROWS

    def gather_head(i, carry):
        dst = pl.ds(pl.multiple_of(i * R, R), R)
        src = pl.ds(h + i * (R * ATTN_HEADS), R, stride=ATTN_HEADS)
        k_ref[dst, :] = kil_ref[src, :]
        v_ref[dst, :] = vil_ref[src, :]
        return carry

    lax.fori_loop(0, seq // R, gather_head, 0)

    blocks = [(gi, dil, cls + blk * (B * dil), blk == 0)
              for gi, (win, dil) in enumerate(DIL_PAIRS)
              for cls in range(dil) for blk in range(seq // (dil * B))]
    batches = [blocks[i:i + ATTN_UNROLL] for i in range(0, len(blocks), ATTN_UNROLL)]

    def rows(dil, start, first):
        start, n = (start, B) if first else (start - B * dil, 2 * B)
        return pl.ds(start, n) if dil == 1 else pl.ds(start, n, stride=dil)

    def scores(batch):
        return [_dot_nt(q_ref[rows(dil, qs, True), :], k_ref[rows(dil, qs, first), :]) * scale
                + (bias_ref[gi, :, B:] if first else bias_ref[gi]) for gi, dil, qs, first in batch]

    def softmax(ss):
        out = []
        for s in ss:
            mx = jnp.max(s, axis=-1, keepdims=True)
            pr = jnp.exp(s - mx)
            den = jnp.sum(pr, axis=-1, keepdims=True)
            out.append(((pr / den).astype(BF16), mx + jnp.log(den)))
        return out

    def values_and_store(batch, probs):
        outs = [_dot(pn, v_ref[rows(dil, qs, first), :]) for (gi, dil, qs, first), (pn, _) in zip(batch, probs)]
        for (gi, dil, qs, _), o, (_, lse) in zip(batch, outs, probs):
            og_scr[gi, rows(dil, qs, True), :] = o
            lse_scr[gi, rows(dil, qs, True), :] = jnp.broadcast_to(lse, (B, ATTN_HEAD))

    pending = scores(batches[0])
    for bi, batch in enumerate(batches):
        upcoming = scores(batches[bi + 1]) if bi + 1 < len(batches) else None
        values_and_store(batch, softmax(pending))
        pending = upcoming

    def merge(i, carry):
        sl = pl.ds(pl.multiple_of(i * R, R), R)
        ls = [lse_scr[gi, sl, :] for gi in range(len(DIL_PAIRS))]
        mx = functools.reduce(jnp.maximum, ls)
        ws = [jnp.exp(l - mx) for l in ls]
        tot = functools.reduce(lambda a, b: a + b, ws)
        acc = (ws[0] / tot) * og_scr[0, sl, :]
        for gi in range(1, len(DIL_PAIRS)):
            acc = acc + (ws[gi] / tot) * og_scr[gi, sl, :]
        o_ref[sl, :] = acc.astype(o_ref.dtype)
        return carry

    lax.fori_loop(0, seq // R, merge, 0)


def _attn_prompt(pxq, k_all, v_all, bias, layer, batch, seq):
    H = ATTN_HEADS
    G = len(DIL_PAIRS)
    q_col0 = (PXQ_WIDTH - ATTN_WIDTH) // ATTN_HEAD
    kern = functools.partial(_attn_prompt_kernel, seq=seq)
    kv = pl.BlockSpec((seq * H, ATTN_HEAD), lambda b, h: (layer * batch + b, 0))
    return pl.pallas_call(
        kern,
        grid=(batch, H),
        in_specs=[
            pl.BlockSpec((seq, ATTN_HEAD), lambda b, h: (b, q_col0 + h)),
            kv, kv,
            pl.BlockSpec((G, None, ATTN_SPAN, 2 * ATTN_SPAN), lambda b, h: (0, h, 0, 0)),
        ],
        out_specs=pl.BlockSpec((seq, ATTN_HEAD), lambda b, h: (b, h)),
        out_shape=jax.ShapeDtypeStruct((batch * seq, ATTN_WIDTH), BF16),
        scratch_shapes=[pltpu.VMEM((seq, ATTN_HEAD), F32), pltpu.VMEM((seq, ATTN_HEAD), F32),
                        pltpu.VMEM((G, seq, ATTN_HEAD), F32), pltpu.VMEM((G, seq, ATTN_HEAD), F32)],
        compiler_params=_cparams(("parallel", "arbitrary")),
        name="attn_prompt",
    )(pxq, k_all, v_all, bias)


def _attn_sample_kernel(pxq_ref, kn_ref, vn_ref, kc_ref, vc_ref, bias_ref, o_ref, q_scr, kn_scr, vn_scr, *, t_new, win):
    scale = ATTN_HEAD ** -0.5
    H = ATTN_HEADS
    G = len(DIL_PAIRS)
    TP = q_scr.shape[1]
    q_col0 = PXQ_WIDTH - ATTN_WIDTH
    q_scr[...] = jnp.zeros_like(q_scr)
    kn_scr[...] = jnp.zeros_like(kn_scr)
    vn_scr[...] = jnp.zeros_like(vn_scr)
    scores = []
    for h in range(H):
        q_scr[h, 0:t_new, :] = pxq_ref[:, q_col0 + h * ATTN_HEAD:q_col0 + (h + 1) * ATTN_HEAD]
        kn_scr[h, 0:t_new, :] = kn_ref[pl.ds(h, t_new, stride=H), :]
        vn_scr[h, 0:t_new, :] = vn_ref[pl.ds(h, t_new, stride=H), :]
        qb = q_scr[h].astype(BF16)
        kc = kc_ref[pl.ds(h, win, stride=H), :]
        scores.append(jnp.concatenate([_dot_nt(qb, kc), _dot_nt(qb, kn_scr[h])], axis=1) * scale)
    probs = []
    for h, s in enumerate(scores):
        prs, lses = [], []
        for gi in range(G):
            sg = s + bias_ref[gi, h]
            mx = jnp.max(sg, axis=-1, keepdims=True)
            pr = jnp.exp(sg - mx)
            den = jnp.sum(pr, axis=-1, keepdims=True)
            prs.append(pr / den)
            lses.append(mx + jnp.log(den))
        probs.append((jnp.concatenate(prs, axis=0).astype(BF16), lses))
    outs = []
    for h, (pall, _) in enumerate(probs):
        vc = vc_ref[pl.ds(h, win, stride=H), :]
        outs.append(_dot(pall[:, :win], vc) + _dot(pall[:, win:], vn_scr[h]))
    for h, (oall, (_, lses)) in enumerate(zip(outs, probs)):
        mx = functools.reduce(jnp.maximum, lses)
        ws = [jnp.exp(l - mx) for l in lses]
        tot = functools.reduce(lambda a, b: a + b, ws)
        acc = (ws[0] / tot) * oall[0:TP]
        for gi in range(1, G):
            acc = acc + (ws[gi] / tot) * oall[gi * TP:(gi + 1) * TP]
        o_ref[:, h * ATTN_HEAD:(h + 1) * ATTN_HEAD] = acc[0:t_new].astype(o_ref.dtype)


def _attn_sample(pxq, k_new, v_new, k_cache, v_cache, bias, layer, batch, t_new):
    H = ATTN_HEADS
    win = k_cache.shape[2] // H
    tp = bias.shape[2]
    kern = functools.partial(_attn_sample_kernel, t_new=t_new, win=win)
    cache = pl.BlockSpec((None, None, win * H, ATTN_HEAD), lambda b: (layer, b, 0, 0))
    new = pl.BlockSpec((None, t_new * H, ATTN_HEAD), lambda b: (layer * batch + b, 0, 0))
    return pl.pallas_call(
        kern,
        grid=(batch,),
        in_specs=[pl.BlockSpec((None, t_new, pxq.shape[2]), lambda b: (b, 0, 0)), new, new, cache, cache,
                  pl.BlockSpec(bias.shape, lambda b: (0, 0, 0, 0))],
        out_specs=pl.BlockSpec((None, t_new, ATTN_WIDTH), lambda b: (b, 0, 0)),
        out_shape=jax.ShapeDtypeStruct((batch, t_new, ATTN_WIDTH), F32),
        scratch_shapes=[pltpu.VMEM((H, tp, ATTN_HEAD), F32), pltpu.VMEM((H, LANES, ATTN_HEAD), F32),
                        pltpu.VMEM((H, LANES, ATTN_HEAD), F32)],
        compiler_params=_cparams(("parallel",)),
        name="attn_sample",
    )(pxq, k_new, v_new, k_cache, v_cache, bias)


def _bias_kernel(idx_ref, rb_ref, o_ref):
    h = pl.program_id(0)
    idx = idx_ref[...]
    acc = jnp.full(idx.shape, NEG, F32)
    for b in range(N_BUCKETS):
        acc = jnp.where(idx == b, rb_ref[b, h], acc)
    o_ref[...] = acc


def _bias_table(idx, rel_bias):
    g, r, c = idx.shape
    heads = rel_bias.shape[1]
    return pl.pallas_call(
        _bias_kernel,
        grid=(heads,),
        in_specs=[pl.BlockSpec((g, r, c), lambda h: (0, 0, 0)),
                  pl.BlockSpec(memory_space=pltpu.SMEM)],
        out_specs=pl.BlockSpec((g, None, r, c), lambda h: (0, h, 0, 0)),
        out_shape=jax.ShapeDtypeStruct((g, heads, r, c), F32),
        compiler_params=_cparams(("parallel",)),
        name="bias_table",
    )(jnp.asarray(idx, jnp.int32), rel_bias.astype(F32))


def _t5_bucket_static(dist):
    dist = np.asarray(dist, np.int64)
    exact = N_BUCKETS // 2
    d = np.maximum(dist, 1).astype(np.float64)
    large = exact + (np.log(d / exact) / math.log(REL_MAX_DIST / exact) * (N_BUCKETS - exact)).astype(np.int64)
    return np.where(dist < exact, dist, np.minimum(large, N_BUCKETS - 1))


def _prompt_bias_idx():
    B = ATTN_SPAN
    qi = np.arange(B)[:, None]
    kj = np.arange(2 * B)[None, :]
    delta = qi + B - kj
    valid = (delta >= 0) & (delta <= B)
    return np.stack([np.where(valid, _t5_bucket_static(np.clip(delta, 0, B) * dil), -1) for _, dil in DIL_PAIRS])


def _sample_bias_idx(t_new, t_pad, win, total):
    r = np.arange(total)[None, :]
    t = np.arange(t_pad)[:, None]
    dist = win + t - r
    tabs = []
    for _, dil in DIL_PAIRS:
        assert win >= dil * ATTN_SPAN
        valid = (dist >= 0) & (dist % dil == 0) & (dist <= dil * ATTN_SPAN) & (t < t_new)
        tabs.append(np.where(valid, _t5_bucket_static(np.clip(dist, 0, dil * ATTN_SPAN)), -1))
    return np.stack(tabs)


def _block_diag(blocks):
    n, c, d = blocks.shape
    eye = jnp.eye(n, dtype=blocks.dtype)
    return jnp.einsum("ncd,nm->ncmd", blocks, eye).reshape(n * c, n * d)


def _mixer_params(a):
    W = RWKV_WIDTH
    depth = a["rwkv_mu"].shape[0]
    row = lambda t: t.reshape(depth, 1, -1).astype(F32)

    def lora_pad(w, off):
        z = jnp.zeros((depth, LORA_WIDTH, W), F32)
        return z.at[:, off:off + w.shape[1]].set(w).astype(BF16)

    head_of = np.arange(W) // RWKV_HEAD
    ones_bd = jnp.asarray((head_of[:, None] == head_of[None, :]).astype(np.float32), BF16)
    rwkv = dict(
        mu=row(a["rwkv_mu"]), w0=row(a["rwkv_w0"]), a0=row(a["rwkv_a0"]),
        w_up=lora_pad(a["rwkv_w_up"], 0),
        a_up=lora_pad(a["rwkv_a_up"], DECAY_LORA),
        g_up=lora_pad(a["rwkv_g_up"], DECAY_LORA + ICLR_LORA),
        k_k=row(a["rwkv_k_k"]), k_a=row(a["rwkv_k_a"]), r_k=row(a["rwkv_r_k"]),
        lnx_g=row(a["rwkv_lnx_g"]), lnx_b=row(a["rwkv_lnx_b"]),
        ones_bd=ones_bd,
    )
    lru = dict(
        conv_w=a["lru_conv_w"].astype(F32), conv_b=row(a["lru_conv_b"]),
        wa_bd=jax.vmap(_block_diag)(a["lru_wa"]).astype(BF16), ba=row(a["lru_ba"]),
        wx_bd=jax.vmap(_block_diag)(a["lru_wx"]).astype(BF16), bx=row(a["lru_bx"]),
        lam=row(a["lru_lambda"]),
    )
    return dict(rwkv=rwkv, lru=lru)


def _shared_params(a):
    gain = lambda t: t.reshape(t.shape[0], 1, -1).astype(F32)
    return dict(
        g_mix_pre=gain(a["norm_mix_pre"]), g_mix_post=gain(a["norm_mix_post"]),
        g_ffn_pre=gain(a["norm_ffn_pre"]), g_ffn_post=gain(a["norm_ffn_post"]),
    )


def _layer(x, layer, sp, wts, lp, batch, t_len, shift_prev, s0, conv_prev, h0, kv_all, attend):
    emit = wts is None
    src = sp["f32"] if emit else wts
    wl = layer if emit else 0
    copies = {}
    pxq, k_all, v_all, *w_copy = _in_proj(x, sp["g_mix_pre"], src["w_in"], layer, wl, *kv_all, emit_bf16=emit)
    copies["w_in"] = w_copy[0] if emit else None
    pxq3 = pxq.reshape(batch, t_len, pxq.shape[1])
    if t_len % RWKV_CHUNK == 0:
        y_a, s_fin = _rwkv(pxq3, shift_prev, s0, lp["rwkv"], layer, batch, t_len, t_len, RWKV_CHUNK)
        y_a = y_a.reshape(batch * t_len, -1)
        y_b, h_fin = _lru(pxq, conv_prev, h0, lp["lru"], layer, batch, t_len, t_len, LRU_CHUNK)
    else:
        assert t_len <= SUBLANES
        padded = jnp.pad(pxq3, ((0, 0), (0, SUBLANES - t_len), (0, 0)))
        y_a, s_fin = _rwkv(padded, shift_prev, s0, lp["rwkv"], layer, batch, SUBLANES, t_len, SUBLANES)
        y_a = y_a[:, :t_len].reshape(batch * t_len, -1)
        y_b, h_fin = _lru(padded.reshape(batch * SUBLANES, pxq.shape[1]), conv_prev, h0, lp["lru"], layer, batch, SUBLANES,
                          t_len, SUBLANES)
        y_b = y_b.reshape(batch, SUBLANES, -1)[:, :t_len].reshape(batch * t_len, -1)
    y_c = attend(pxq, k_all, v_all)

    x = _out_proj(y_a, y_b, y_c, src["w_out"], sp["g_mix_post"], x, layer, wl, emit_bf16=emit)
    if emit:
        x, copies["w_out"] = x
    x = _ffn(x, sp["g_ffn_pre"], src["w1"], src["w2"], sp["g_ffn_post"], layer, wl, emit_bf16=emit)
    if emit:
        x, copies["w1"], copies["w2"] = x

    lru_x = pxq3[:, :, RWKV_PROJ:RWKV_PROJ + LRU_WIDTH]
    if t_len >= CONV_W - 1:
        conv_new = lru_x[:, t_len - (CONV_W - 1):]
    else:
        conv_new = jnp.concatenate([conv_prev, lru_x], axis=1)[:, -(CONV_W - 1):]
    state = (pxq3[:, -1, :RWKV_PROJ], s_fin, conv_new, h_fin.reshape(batch, LRU_WIDTH))
    return x, state, (k_all, v_all), copies


def kernel(x_prompt, x_sample, state_rwkv_wkv, state_rwkv_shift, state_lru_h, state_lru_conv, cache_attn_k, cache_attn_v, rel_bias, norm_mix_pre, norm_mix_post, norm_ffn_pre, norm_ffn_post, w_in, w_out, rwkv_mu, rwkv_w0, rwkv_w_up, rwkv_a0, rwkv_a_up, rwkv_g_up, rwkv_k_k, rwkv_k_a, rwkv_r_k, rwkv_lnx_g, rwkv_lnx_b, lru_conv_w, lru_conv_b, lru_wa, lru_ba, lru_wx, lru_bx, lru_lambda, ffn_w1, ffn_w2):
    a = dict(norm_mix_pre=norm_mix_pre, norm_mix_post=norm_mix_post, norm_ffn_pre=norm_ffn_pre,
             norm_ffn_post=norm_ffn_post, w_in=w_in, w_out=w_out, rwkv_mu=rwkv_mu, rwkv_w0=rwkv_w0,
             rwkv_w_up=rwkv_w_up, rwkv_a0=rwkv_a0, rwkv_a_up=rwkv_a_up, rwkv_g_up=rwkv_g_up, rwkv_k_k=rwkv_k_k,
             rwkv_k_a=rwkv_k_a, rwkv_r_k=rwkv_r_k, rwkv_lnx_g=rwkv_lnx_g, rwkv_lnx_b=rwkv_lnx_b,
             lru_conv_w=lru_conv_w, lru_conv_b=lru_conv_b, lru_wa=lru_wa, lru_ba=lru_ba, lru_wx=lru_wx,
             lru_bx=lru_bx, lru_lambda=lru_lambda, ffn_w1=ffn_w1, ffn_w2=ffn_w2)
    depth = w_in.shape[0]
    pb, seq, _ = x_prompt.shape
    sb, t_new, _ = x_sample.shape
    win = cache_attn_k.shape[2]
    keep = min(ATTN_WINDOW, seq)
    total = win + LANES
    assert t_new <= LANES and seq % (DIL_PAIRS[-1][1] * ATTN_SPAN) == 0

    bias_p = _bias_table(_prompt_bias_idx(), rel_bias)
    bias_s = _bias_table(_sample_bias_idx(t_new, SUBLANES, win, total), rel_bias)
    k_cache = cache_attn_k.reshape(depth, sb, win * ATTN_HEADS, ATTN_HEAD)
    v_cache = cache_attn_v.reshape(depth, sb, win * ATTN_HEADS, ATTN_HEAD)

    xp = x_prompt.reshape(pb * seq, D_MODEL)
    xs = x_sample.reshape(sb * t_new, D_MODEL)
    new_p, new_s = [], []
    sp = _shared_params(a)
    kv_p = tuple(jnp.zeros((depth * pb * seq * ATTN_HEADS, ATTN_HEAD), F32) for _ in range(2))
    kv_s = tuple(jnp.zeros((depth * sb * t_new * ATTN_HEADS, ATTN_HEAD), F32) for _ in range(2))
    sp["f32"] = dict(w_in=w_in.astype(F32), w_out=w_out.astype(F32), w1=ffn_w1.astype(F32), w2=ffn_w2.astype(F32))
    lp = _mixer_params(a)
    for l in range(depth):
        attend_s = lambda pxq, k_all, v_all, l=l: _attn_sample(
            pxq.reshape(sb, t_new, pxq.shape[1]), k_all.reshape(depth * sb, t_new * ATTN_HEADS, ATTN_HEAD),
            v_all.reshape(depth * sb, t_new * ATTN_HEADS, ATTN_HEAD), k_cache, v_cache, bias_s, l, sb, t_new,
        ).reshape(sb * t_new, ATTN_WIDTH)
        xs, st_s, kv_s, wts = _layer(xs, l, sp, None, lp, sb, t_new,
                                     state_rwkv_shift[l].reshape(sb, 1, RWKV_PROJ), state_rwkv_wkv[l].astype(F32),
                                     state_lru_conv[l], state_lru_h[l].reshape(sb, 1, LRU_WIDTH), kv_s, attend_s)
        attend_p = lambda pxq, k_all, v_all, l=l: _attn_prompt(pxq, k_all, v_all, bias_p, l, pb, seq)
        xp, st_p, kv_p, _ = _layer(xp, l, sp, wts, lp, pb, seq,
                                   jnp.zeros((pb, 1, RWKV_PROJ), F32),
                                   jnp.zeros((pb, RWKV_HEADS, RWKV_HEAD, RWKV_HEAD), F32),
                                   jnp.zeros((pb, CONV_W - 1, LRU_WIDTH), F32), jnp.zeros((pb, 1, LRU_WIDTH), F32),
                                   kv_p, attend_p)
        new_p.append(st_p)
        new_s.append(st_s)

    stack = lambda sts, i: jnp.stack([s[i] for s in sts])
    k_p, v_p = (t.reshape(depth, pb, seq, ATTN_HEADS, ATTN_HEAD)[:, :, seq - keep:] for t in kv_p)
    k_s, v_s = (t.reshape(depth, sb, t_new, ATTN_HEADS, ATTN_HEAD) for t in kv_s)
    return (xp.reshape(pb, seq, D_MODEL), xs.reshape(sb, t_new, D_MODEL),
            stack(new_p, 1), stack(new_s, 1), stack(new_p, 0), stack(new_s, 0),
            stack(new_p, 3), stack(new_s, 3), stack(new_p, 2), stack(new_s, 2),
            k_p, k_s, v_p, v_s)
```

```python
import functools
import math

import numpy as np
import jax
import jax.numpy as jnp
from jax import lax
from jax.experimental import pallas as pl
from jax.experimental.pallas import tpu as pltpu

F32 = jnp.float32
BF16 = jnp.bfloat16

D_MODEL = 2048
RWKV_WIDTH = 512
RWKV_HEAD = 64
RWKV_HEADS = RWKV_WIDTH // RWKV_HEAD
DECAY_LORA = 64
ICLR_LORA = 64
GATE_LORA = 128
LORA_WIDTH = DECAY_LORA + ICLR_LORA + GATE_LORA
RWKV_PROJ = 3 * RWKV_WIDTH + LORA_WIDTH
GN_EPS = 64e-5
LRU_WIDTH = 512
LRU_BLOCKS = 8
LRU_BLOCK = LRU_WIDTH // LRU_BLOCKS
CONV_W = 4
RG_C = 8.0
ATTN_WIDTH = 1024
ATTN_HEAD = 128
ATTN_HEADS = ATTN_WIDTH // ATTN_HEAD
DIL_PAIRS = ((128, 1), (512, 4), (2048, 16))
ATTN_SPAN = 128
ATTN_WINDOW = 2048
N_BUCKETS = 32
REL_MAX_DIST = ATTN_WINDOW
D_FF = 4 * D_MODEL
RMS_EPS = 1e-6
NEG = -1e30

LANES = 128
SUBLANES = 8
MXU_TILE = 256
VMEM_LIMIT_BYTES = 56 * 1024 * 1024

ROW_TILE = 1024
IN_COL_TILE = 256
IN_FUSED_TILES = 2
OUT_ROW_TILE = 512
FFN_ROW_TILE = 512
FFN_COL_TILE = 1024
FFN_CAST_COL_TILE = 512
RWKV_CHUNK = 64
RWKV_SEQS_PER_STEP = 4
LRU_CHUNK = 256
ATTN_MERGE_ROWS = 256
ATTN_UNROLL = 4


def _cparams(sem):
    return pltpu.CompilerParams(dimension_semantics=sem, vmem_limit_bytes=VMEM_LIMIT_BYTES)


def _dot(a, b):
    return jnp.dot(a.astype(BF16), b.astype(BF16), preferred_element_type=F32)


def _dot_nt(a, b):
    return lax.dot_general(a.astype(BF16), b.astype(BF16), (((1,), (1,)), ((), ())), preferred_element_type=F32)


def _dot_tn(a, b):
    return lax.dot_general(a.astype(BF16), b.astype(BF16), (((0,), (0,)), ((), ())), preferred_element_type=F32)


def _softplus(z):
    return jnp.maximum(z, 0.0) + jnp.log1p(jnp.exp(-jnp.abs(z)))


def _mix_dtype(rows):
    return BF16 if rows % (2 * SUBLANES) == 0 else F32


def _rms(x, g):
    ms = jnp.mean(x * x, axis=-1, keepdims=True)
    return x * lax.rsqrt(ms + RMS_EPS) * g


PXQ_WIDTH = RWKV_PROJ + 2 * LRU_WIDTH + ATTN_WIDTH
PXQ_TILES = PXQ_WIDTH // IN_COL_TILE
KV_TILES = ATTN_WIDTH // IN_COL_TILE
HEADS_PER_TILE = IN_COL_TILE // ATTN_HEAD


def _in_proj_kernel(x_ref, g_ref, *refs, fuse):
    w_refs = refs[:fuse]
    pxq_ref, k_ref, v_ref = refs[fuse + 2:fuse + 5]
    rest = refs[fuse + 5:]
    h_scr = rest[-1]
    j = pl.program_id(1)
    tm = x_ref.shape[0]
    pxq_steps = pl.cdiv(PXQ_TILES, fuse)
    kv_steps = KV_TILES // fuse

    @pl.when(j == 0)
    def _():
        h_scr[...] = _rms(x_ref[...], g_ref[...]).astype(BF16)

    ws = [w_ref[...].astype(BF16) for w_ref in w_refs]
    if len(rest) > 1:
        rest[0][...] = ws[0]
    w = ws[0] if fuse == 1 else jnp.concatenate(ws, axis=1)
    acc = jnp.dot(h_scr[...], w, preferred_element_type=F32)

    @pl.when(j < pxq_steps)
    def _():
        pxq_ref[...] = acc

    def scatter_heads(o_ref, step):
        for hh in range(fuse * HEADS_PER_TILE):
            head = step * (fuse * HEADS_PER_TILE) + hh
            o_ref[pl.ds(head, tm, stride=ATTN_HEADS), :] = acc[:, hh * ATTN_HEAD:(hh + 1) * ATTN_HEAD]

    @pl.when((j >= pxq_steps) & (j < pxq_steps + kv_steps))
    def _():
        scatter_heads(k_ref, j - pxq_steps)

    @pl.when(j >= pxq_steps + kv_steps)
    def _():
        scatter_heads(v_ref, j - pxq_steps - kv_steps)


def _in_proj(x, g, w, layer, w_layer, k_all, v_all, emit_bf16=False):
    m, d = x.shape
    n = w.shape[2]
    assert n == PXQ_WIDTH + 2 * ATTN_WIDTH
    tm = min(ROW_TILE, m)
    row_tiles = m // tm
    assert not emit_bf16 or row_tiles == 1
    fuse = 1 if emit_bf16 else IN_FUSED_TILES
    assert KV_TILES % fuse == 0
    pxq_steps = pl.cdiv(PXQ_TILES, fuse)
    spare = pxq_steps * fuse - PXQ_TILES
    width = fuse * IN_COL_TILE

    def w_spec(slot):
        def index(i, j):
            tile = jnp.where(j < pxq_steps, jnp.minimum(fuse * j + slot, PXQ_TILES - 1), fuse * j + slot - spare)
            return (w_layer, 0, tile)
        return pl.BlockSpec((None, d, IN_COL_TILE), index)

    kv_spec = pl.BlockSpec((tm * ATTN_HEADS, ATTN_HEAD), lambda i, j: (layer * row_tiles + i, 0))
    out_specs = [
        pl.BlockSpec((tm, width), lambda i, j: (i, jnp.minimum(j, pxq_steps - 1))),
        kv_spec, kv_spec,
    ]
    out_shape = [
        jax.ShapeDtypeStruct((m, pxq_steps * width), F32),
        jax.ShapeDtypeStruct(k_all.shape, F32),
        jax.ShapeDtypeStruct(v_all.shape, F32),
    ]
    if emit_bf16:
        out_specs.append(pl.BlockSpec((None, d, IN_COL_TILE), lambda i, j: (0, 0, j)))
        out_shape.append(jax.ShapeDtypeStruct((1, d, n), BF16))
    return pl.pallas_call(
        functools.partial(_in_proj_kernel, fuse=fuse),
        grid=(row_tiles, pxq_steps + 2 * (KV_TILES // fuse)),
        in_specs=[
            pl.BlockSpec((tm, d), lambda i, j: (i, 0)),
            pl.BlockSpec((None, 1, d), lambda i, j: (layer, 0, 0)),
            *[w_spec(slot) for slot in range(fuse)],
            pl.BlockSpec(memory_space=pl.ANY),
            pl.BlockSpec(memory_space=pl.ANY),
        ],
        out_specs=out_specs,
        out_shape=out_shape,
        input_output_aliases={2 + fuse: 1, 3 + fuse: 2},
        scratch_shapes=[pltpu.VMEM((tm, d), BF16)],
        compiler_params=_cparams(("parallel", "arbitrary")),
        name="in_proj",
    )(x, g, *([w] * fuse), k_all, v_all)


def _out_proj_kernel(ya_ref, yb_ref, yc_ref, w_ref, g_ref, x_ref, o_ref, *w_copy):
    c1, c2 = RWKV_WIDTH, RWKV_WIDTH + LRU_WIDTH
    if w_copy:
        acc = None
        for y_ref, rows in [(ya_ref, slice(0, c1)), (yb_ref, slice(c1, c2)), (yc_ref, slice(c2, None))]:
            w = w_ref[rows, :].astype(BF16)
            w_copy[0][rows, :] = w
            t = jnp.dot(y_ref[...].astype(BF16), w, preferred_element_type=F32)
            acc = t if acc is None else acc + t
    else:
        y = jnp.concatenate([r[...].astype(BF16) for r in (ya_ref, yb_ref, yc_ref)], axis=1)
        acc = jnp.dot(y, w_ref[...].astype(BF16), preferred_element_type=F32)
    o_ref[...] = x_ref[...] + _rms(acc, g_ref[...])


def _out_proj(ya, yb, yc, w, g, x, layer, w_layer, emit_bf16=False):
    m, d = x.shape
    tm = min(OUT_ROW_TILE, m)
    assert not emit_bf16 or m == tm
    out_specs = [pl.BlockSpec((tm, d), lambda i: (i, 0))]
    out_shape = [jax.ShapeDtypeStruct((m, d), F32)]
    if emit_bf16:
        out_specs.append(pl.BlockSpec((None, d, d), lambda i: (0, 0, 0)))
        out_shape.append(jax.ShapeDtypeStruct((1, d, d), BF16))
    res = pl.pallas_call(
        _out_proj_kernel,
        grid=(m // tm,),
        in_specs=[
            pl.BlockSpec((tm, RWKV_WIDTH), lambda i: (i, 0)),
            pl.BlockSpec((tm, LRU_WIDTH), lambda i: (i, 0)),
            pl.BlockSpec((tm, ATTN_WIDTH), lambda i: (i, 0)),
            pl.BlockSpec((None, d, d), lambda i: (w_layer, 0, 0)),
            pl.BlockSpec((None, 1, d), lambda i: (layer, 0, 0)),
            pl.BlockSpec((tm, d), lambda i: (i, 0)),
        ],
        out_specs=out_specs,
        out_shape=out_shape,
        compiler_params=_cparams(("parallel",)),
        name="out_proj",
    )(ya, yb, yc, w, g, x)
    return res if emit_bf16 else res[0]


def _ffn_kernel(x_ref, g1_ref, w1_ref, w2_ref, g2_ref, o_ref, *rest):
    h_scr, acc_scr = rest[-2:]
    j = pl.program_id(1)

    @pl.when(j == 0)
    def _():
        h_scr[...] = _rms(x_ref[...], g1_ref[...]).astype(BF16)
        acc_scr[...] = jnp.zeros_like(acc_scr)

    w1 = w1_ref[...].astype(BF16)
    w2 = w2_ref[...].astype(BF16)
    if len(rest) > 2:
        rest[0][...] = w1
        rest[1][...] = w2
    u = jnp.dot(h_scr[...], w1, preferred_element_type=F32)
    u = jnp.square(jnp.maximum(u, 0.0)).astype(BF16)
    acc_scr[...] += jnp.dot(u, w2, preferred_element_type=F32)

    @pl.when(j == pl.num_programs(1) - 1)
    def _():
        o_ref[...] = x_ref[...] + _rms(acc_scr[...], g2_ref[...])


def _ffn(x, g1, w1, w2, g2, layer, w_layer, emit_bf16=False):
    m, d = x.shape
    f = w1.shape[2]
    tm = min(FFN_ROW_TILE, m)
    tf = FFN_CAST_COL_TILE if emit_bf16 else FFN_COL_TILE
    assert not emit_bf16 or m == tm
    out_specs = [pl.BlockSpec((tm, d), lambda i, j: (i, 0))]
    out_shape = [jax.ShapeDtypeStruct((m, d), F32)]
    if emit_bf16:
        out_specs += [pl.BlockSpec((None, d, tf), lambda i, j: (0, 0, j)),
                      pl.BlockSpec((None, tf, d), lambda i, j: (0, j, 0))]
        out_shape += [jax.ShapeDtypeStruct((1, d, f), BF16), jax.ShapeDtypeStruct((1, f, d), BF16)]
    res = pl.pallas_call(
        _ffn_kernel,
        grid=(m // tm, f // tf),
        in_specs=[
            pl.BlockSpec((tm, d), lambda i, j: (i, 0)),
            pl.BlockSpec((None, 1, d), lambda i, j: (layer, 0, 0)),
            pl.BlockSpec((None, d, tf), lambda i, j: (w_layer, 0, j)),
            pl.BlockSpec((None, tf, d), lambda i, j: (w_layer, j, 0)),
            pl.BlockSpec((None, 1, d), lambda i, j: (layer, 0, 0)),
        ],
        out_specs=out_specs,
        out_shape=out_shape,
        scratch_shapes=[pltpu.VMEM((tm, d), BF16), pltpu.VMEM((tm, d), F32)],
        compiler_params=_cparams(("parallel", "arbitrary")),
        name="ffn",
    )(x, g1, w1, w2, g2)
    return res if emit_bf16 else res[0]


def _rwkv_groups(chunk):
    lanes = RWKV_HEADS * chunk
    return lanes // MXU_TILE if lanes % (2 * MXU_TILE) == 0 else 1


def _rwkv_kernel(p_ref, shift_ref, s0_ref, mu_ref, w0_ref, wup_ref, a0_ref, aup_ref, gup_ref, kk_ref, ka_ref,
                 rk_ref, lg_ref, lb_ref, ones_ref,
                 y_ref, sout_ref, s_scr, prev_scr, *, chunk, t_valid):
    C = chunk
    W = RWKV_WIDTH
    H = RWKV_HEADS
    N = RWKV_HEAD
    nb = p_ref.shape[0]
    c = pl.program_id(1)

    G = _rwkv_groups(C)
    hpg = H // G
    wl = W // G
    gc = hpg * C

    @pl.when(c == 0)
    def _():
        s_scr[...] = jnp.zeros_like(s_scr)
        for h in range(H):
            o = (h % hpg) * N
            s_scr[:, h // hpg, o:o + N, o:o + N] = s0_ref[:, h]
        prev_scr[...] = shift_ref[...]

    inv_n = 1.0 / RWKV_HEAD
    shared = dict(wup=[wup_ref], aup=[aup_ref], gup=[gup_ref],
                  ones=[ones_ref.at[g * wl:(g + 1) * wl, g * wl:(g + 1) * wl] for g in range(G)])

    def hi_lo(x):
        hi = x.astype(BF16).astype(F32)
        return [hi, x - hi]

    row = lax.broadcasted_iota(jnp.int32, (C, 1), 0)

    def cumsum_rows(z):
        dlt = 1
        while dlt < C:
            z = z + jnp.where(row >= dlt, pltpu.roll(z, dlt, axis=0), 0.0)
            dlt *= 2
        return z

    ti = lax.broadcasted_iota(jnp.int32, (C, H * C), 0)
    si = lax.broadcasted_iota(jnp.int32, (C, H * C), 1) % C
    strict = ti > si
    incl = ti >= si
    eye = (ti == si).astype(F32)
    blk_r = lax.broadcasted_iota(jnp.int32, (gc, 1), 0) // C
    mask_ch = blk_r == lax.broadcasted_iota(jnp.int32, (1, wl), 1) // N
    mask_cc = blk_r == lax.broadcasted_iota(jnp.int32, (1, gc), 1) // C
    mask_ss = (lax.broadcasted_iota(jnp.int32, (wl, 1), 0) // N
               == lax.broadcasted_iota(jnp.int32, (1, wl), 1) // N)

    def block_diag(x, mask, width):
        out = []
        for g in range(G):
            tiled = jnp.concatenate([x[:, g * width:(g + 1) * width]] * hpg, axis=0)
            out.append(jnp.where(mask, tiled, 0.0).astype(BF16))
        return out

    def per_head(a_cat, bds):
        a_b = a_cat.astype(BF16)
        return jnp.concatenate([jnp.dot(a_b[:, g * gc:(g + 1) * gc], bds[g], preferred_element_type=F32)
                                for g in range(G)], axis=1)

    def one_sequence(bi):
        p = p_ref[bi]
        shifted = jnp.where(row == 0, prev_scr[bi], pltpu.roll(p, 1, axis=0))
        prev_scr[bi] = p_ref[bi, C - 1:C, :]
        m = p + (shifted - p) * mu_ref[...]
        r = m[:, 0:W]
        k = m[:, W:2 * W]
        v = m[:, 2 * W:3 * W]
        x = m[:, 3 * W:]
        lw, la, gate = yield [("wup", jnp.tanh(x)), ("aup", x), ("gup", jax.nn.sigmoid(x))]
        w = w0_ref[...] + lw
        a = jax.nn.sigmoid(a0_ref[...] + la)
        softplus_neg_w = jnp.maximum(-w, 0.0) + jnp.log(1.0 + jnp.exp(-jnp.abs(w)))
        loga = -jnp.exp(-softplus_neg_w - 0.5)
        kk = k * kk_ref[...]
        k2 = k * (1.0 + (a - 1.0) * ka_ref[...])
        (ss,) = yield [("ones", jnp.concatenate(hi_lo(kk * kk) + hi_lo(r * k2 * rk_ref[...]), axis=0))]
        kk = kk / jnp.maximum(jnp.sqrt(ss[0:C] + ss[C:2 * C]), 1e-12)
        bonus = (ss[2 * C:3 * C] + ss[3 * C:]) * v
        if t_valid < C:
            live = row < t_valid
            loga = jnp.where(live, loga, 0.0)
            kk = jnp.where(live, kk, 0.0)
            k2 = jnp.where(live, k2, 0.0)
        cl = cumsum_rows(loga)
        cl_last = cl[C - 1:C, :]
        e_neg = jnp.exp(-cl)
        e_rem = jnp.exp(cl_last - cl)
        kka = kk * a
        al = -kk * jnp.exp(cl - loga)
        rt = r * jnp.exp(cl)
        be_bd = block_diag(kka * e_neg, mask_ch, wl)
        kt_bd = block_diag(k2 * e_neg, mask_ch, wl)
        bh = kka * e_rem
        kh = k2 * e_rem
        lhs = jnp.concatenate([al, rt], axis=0).astype(BF16)
        lhs_g = [lhs[:, g * wl:(g + 1) * wl] for g in range(G)]
        g_b = jnp.concatenate([_dot_nt(lhs_g[g], be_bd[g]) for g in range(G)], axis=1)
        g_k = jnp.concatenate([_dot_nt(lhs_g[g], kt_bd[g]) for g in range(G)], axis=1)
        n_cat = jnp.where(strict, g_b[0:C], 0.0)
        a_ak = jnp.where(strict, g_k[0:C], 0.0)
        a_rb = jnp.where(incl, g_b[C:], 0.0)
        a_rk = jnp.where(incl, g_k[C:], 0.0)
        s_prev = [s_scr[bi, g] for g in range(G)]
        proj = jnp.concatenate([_dot_nt(lhs_g[g], s_prev[g]) for g in range(G)], axis=1)
        doublings = max(int(math.log2(C)) - 1, 0)
        t_cat = eye + n_cat
        pw = n_cat
        if doublings:
            pw = per_head(n_cat, block_diag(n_cat, mask_cc, gc))
        yield None
        for it in range(doublings):
            pw_bd = block_diag(pw, mask_cc, gc)
            if it < doublings - 1:
                both = per_head(jnp.concatenate([t_cat, pw], axis=0), pw_bd)
                t_cat = t_cat + both[0:C]
                pw = both[C:]
            else:
                t_cat = t_cat + per_head(t_cat, pw_bd)
            yield None
        v_bd = block_diag(v, mask_ch, wl)
        rhs = proj[0:C] + per_head(a_ak, v_bd)
        yield None
        u = per_head(t_cat, block_diag(rhs, mask_ch, wl))
        yield None
        y = proj[C:] + per_head(a_rb, block_diag(u, mask_ch, wl)) + per_head(a_rk, v_bd)
        pad = LANES - 2 * C
        uv = jnp.concatenate([u, v] + ([jnp.zeros((pad, W), F32)] if pad > 0 else []), axis=0).astype(BF16)
        bk = jnp.concatenate([bh, kh] + ([jnp.zeros((pad, W), F32)] if pad > 0 else []), axis=0).astype(BF16)
        decay = jnp.exp(cl_last)
        for g in range(G):
            lanes = slice(g * wl, (g + 1) * wl)
            upd = _dot_tn(uv[:, lanes], bk[:, lanes])
            s_scr[bi, g] = s_prev[g] * decay[:, lanes] + jnp.where(mask_ss, upd, 0.0)

        @pl.when(c == pl.num_programs(1) - 1)
        def _():
            for h in range(H):
                o = (h % hpg) * N
                sout_ref[bi, h] = s_scr[bi, h // hpg, o:o + N, o:o + N]

        (sy,) = yield [("ones", jnp.concatenate(hi_lo(y), axis=0))]
        d = y - (sy[0:C] + sy[C:]) * inv_n
        (sd,) = yield [("ones", jnp.concatenate(hi_lo(d * d), axis=0))]
        var = (sd[0:C] + sd[C:]) * inv_n
        yn = d * lax.rsqrt(var + GN_EPS) * lg_ref[...] + lb_ref[...]
        y_ref[bi] = ((yn + bonus) * gate).astype(y_ref.dtype)

    seqs = [one_sequence(bi) for bi in range(nb)]
    replies = [None] * nb
    while True:
        asks, finished = [], 0
        for seq, reply in zip(seqs, replies):
            try:
                asks.append(seq.send(reply))
            except StopIteration:
                finished += 1
        if finished:
            assert finished == nb
            break
        if asks[0] is None:
            replies = [None] * nb
            continue
        replies = [[] for _ in range(nb)]
        for qi, (name, _) in enumerate(asks[0]):
            lhs = jnp.concatenate([ask[qi][1] for ask in asks], axis=0).astype(BF16)
            parts = shared[name]
            kw = lhs.shape[1] // len(parts)
            z = jnp.concatenate([jnp.dot(lhs[:, g * kw:(g + 1) * kw], part[...], preferred_element_type=F32)
                                 for g, part in enumerate(parts)], axis=1)
            rows = z.shape[0] // nb
            for si in range(nb):
                replies[si].append(z[si * rows:(si + 1) * rows])


def _rwkv(p, shift_prev, s0, prm, layer, batch, t_len, t_valid, chunk):
    W = RWKV_WIDTH
    nc = t_len // chunk
    nb = _largest_divisor(batch, RWKV_SEQS_PER_STEP)
    vec = lambda n: pl.BlockSpec((None, 1, n), lambda b, c: (layer, 0, 0))
    mat = lambda r, n: pl.BlockSpec((None, r, n), lambda b, c: (layer, 0, 0))
    state = pl.BlockSpec((nb, RWKV_HEADS, RWKV_HEAD, RWKV_HEAD), lambda b, c: (b, 0, 0, 0))
    groups = _rwkv_groups(chunk)
    kern = functools.partial(_rwkv_kernel, chunk=chunk, t_valid=t_valid)
    return pl.pallas_call(
        kern,
        grid=(batch // nb, nc),
        in_specs=[
            pl.BlockSpec((nb, chunk, RWKV_PROJ), lambda b, c: (b, c, 0)),
            pl.BlockSpec((nb, 1, RWKV_PROJ), lambda b, c: (b, 0, 0)),
            state,
            vec(RWKV_PROJ), vec(W), mat(LORA_WIDTH, W), vec(W), mat(LORA_WIDTH, W), mat(LORA_WIDTH, W),
            vec(W), vec(W), vec(W), vec(W), vec(W), pl.BlockSpec((W, W), lambda b, c: (0, 0)),
        ],
        out_specs=[pl.BlockSpec((nb, chunk, W), lambda b, c: (b, c, 0)), state],
        out_shape=[
            jax.ShapeDtypeStruct((batch, t_len, W), _mix_dtype(chunk)),
            jax.ShapeDtypeStruct((batch, RWKV_HEADS, RWKV_HEAD, RWKV_HEAD), F32),
        ],
        scratch_shapes=[pltpu.VMEM((nb, groups, W // groups, W // groups), F32),
                        pltpu.VMEM((nb, 1, RWKV_PROJ), F32)],
        compiler_params=_cparams(("parallel", "arbitrary")),
        name="rwkv7",
    )(p, shift_prev, s0, prm["mu"], prm["w0"], prm["w_up"], prm["a0"], prm["a_up"], prm["g_up"],
      prm["k_k"], prm["k_a"], prm["r_k"], prm["lnx_g"], prm["lnx_b"], prm["ones_bd"])


def _lru_kernel(x0_ref, x1_ref, g0_ref, g1_ref, conv_ref, h0_ref, cw_ref, cb_ref, wa_ref, ba_ref, wx_ref, bx_ref,
                lam_ref, y_ref, hout_ref, tail_scr, h_scr, *, chunk, t_valid):
    C = chunk
    c = pl.program_id(1)

    @pl.when(c == 0)
    def _():
        tail_scr[...] = jnp.zeros_like(tail_scr)
        tail_scr[SUBLANES - (CONV_W - 1):, :] = conv_ref[...]
        h_scr[...] = h0_ref[...]

    x = jnp.concatenate([x0_ref[...], x1_ref[...]], axis=1)
    gt = jnp.concatenate([g0_ref[...], g1_ref[...]], axis=1)
    row_in_group = lax.broadcasted_iota(jnp.int32, (C, 1), 0) % SUBLANES

    def rotate_in_groups(z, dlt):
        z3 = z.reshape(z.shape[0] // SUBLANES, SUBLANES, z.shape[1])
        return pltpu.roll(z3, dlt, axis=1).reshape(z.shape)

    x_prev = tail_scr[...]
    if C > SUBLANES:
        x_prev = jnp.concatenate([x_prev, x[:C - SUBLANES]], axis=0)
    tail_scr[...] = x[C - SUBLANES:, :]
    xc = cb_ref[...] + x * cw_ref[CONV_W - 1:CONV_W, :]
    for dlt in range(1, CONV_W):
        sh = jnp.where(row_in_group >= dlt, rotate_in_groups(x, dlt), rotate_in_groups(x_prev, dlt))
        xc = xc + sh * cw_ref[CONV_W - 1 - dlt:CONV_W - dlt, :]

    gate_r = jax.nn.sigmoid(_dot(xc, wa_ref[...]) + ba_ref[...])
    gate_i = jax.nn.sigmoid(_dot(xc, wx_ref[...]) + bx_ref[...])
    log_a = -RG_C * gate_r * _softplus(-lam_ref[...])
    a = jnp.exp(log_a)
    th = jnp.tanh(log_a)
    u = jnp.sqrt(-2.0 * th / (1.0 - th)) * (gate_i * xc)

    dlt = 1
    while dlt < SUBLANES:
        keep = row_in_group >= dlt
        a_sh = jnp.where(keep, rotate_in_groups(a, dlt), 1.0)
        u_sh = jnp.where(keep, rotate_in_groups(u, dlt), 0.0)
        u = a * u_sh + u
        a = a * a_sh
        dlt *= 2
    carry = h_scr[...]
    groups = []
    for gi in range(C // SUBLANES):
        rows = slice(gi * SUBLANES, (gi + 1) * SUBLANES)
        groups.append(a[rows] * carry + u[rows])
        carry = groups[-1][SUBLANES - 1:SUBLANES, :]
    h = jnp.concatenate(groups, axis=0)
    last = min(t_valid, C) - 1
    h_scr[...] = h[last:last + 1, :]
    hout_ref[...] = h[last:last + 1, :]
    y_ref[...] = (h * jax.nn.gelu(gt)).astype(y_ref.dtype)


def _lru(pxq, conv_prev, h0, prm, layer, batch, t_len, t_valid, chunk):
    W = LRU_WIDTH
    nc = t_len // chunk
    half = W // 2
    col0 = RWKV_PROJ // half
    assert RWKV_PROJ % half == 0
    part = lambda k: pl.BlockSpec((chunk, half), lambda b, c: (b * nc + c, col0 + k))
    vec = lambda n: pl.BlockSpec((None, 1, n), lambda b, c: (layer, 0, 0))
    mat = lambda r, n: pl.BlockSpec((None, r, n), lambda b, c: (layer, 0, 0))
    kern = functools.partial(_lru_kernel, chunk=chunk, t_valid=t_valid)
    return pl.pallas_call(
        kern,
        grid=(batch, nc),
        in_specs=[
            part(0), part(1), part(2), part(3),
            pl.BlockSpec((None, CONV_W - 1, W), lambda b, c: (b, 0, 0)),
            pl.BlockSpec((None, 1, W), lambda b, c: (b, 0, 0)),
            mat(CONV_W, W),
            vec(W),
            mat(W, W), vec(W),
            mat(W, W), vec(W),
            vec(W),
        ],
        out_specs=[
            pl.BlockSpec((chunk, W), lambda b, c: (b * nc + c, 0)),
            pl.BlockSpec((None, 1, W), lambda b, c: (b, 0, 0)),
        ],
        out_shape=[
            jax.ShapeDtypeStruct((batch * t_len, W), _mix_dtype(chunk)),
            jax.ShapeDtypeStruct((batch, 1, W), F32),
        ],
        scratch_shapes=[pltpu.VMEM((SUBLANES, W), F32), pltpu.VMEM((1, W), F32)],
        compiler_params=_cparams(("parallel", "arbitrary")),
        name="rglru",
    )(pxq, pxq, pxq, pxq, conv_prev, h0, prm["conv_w"], prm["conv_b"], prm["wa_bd"], prm["ba"], prm["wx_bd"],
      prm["bx"], prm["lam"])


def _largest_divisor(n, cap):
    return max(u for u in range(1, cap + 1) if n % u == 0)


def _attn_prompt_kernel(q_ref, kil_ref, vil_ref, bias_ref, o_ref, k_ref, v_ref, og_scr, lse_scr, *, seq):
    B = ATTN_SPAN
    scale = ATTN_HEAD ** -0.5
    h = pl.program_id(1)
    R = ATTN_MERGE_ROWS

    def gather_head(src_ref, dst_ref, i):
        dst_ref[pl.ds(i * R, R), :] = src_ref[pl.ds(h + i * (R * ATTN_HEADS), R, stride=ATTN_HEADS), :]

    for i in range(seq // R):
        gather_head(kil_ref, k_ref, i)

    blocks = [(gi, dil, cls + blk * (B * dil), blk == 0)
              for gi, (win, dil) in sorted(enumerate(DIL_PAIRS), key=lambda e: -e[1][1])
              for cls in range(dil) for blk in range(seq // (dil * B))]
    batches = [blocks[i:i + ATTN_UNROLL] for i in range(0, len(blocks), ATTN_UNROLL)]
    assert blocks[-1][1] == 1

    def rows(dil, start, first):
        start, n = (start, B) if first else (start - B * dil, 2 * B)
        return pl.ds(start, n) if dil == 1 else pl.ds(start, n, stride=dil)

    def scores(batch):
        return [_dot_nt(q_ref[rows(dil, qs, True), :], k_ref[rows(dil, qs, first), :]) * scale
                + (bias_ref[gi, :, B:] if first else bias_ref[gi]) for gi, dil, qs, first in batch]

    def softmax(ss):
        out = []
        for s in ss:
            mx = jnp.max(s, axis=-1, keepdims=True)
            pr = jnp.exp(s - mx)
            den = jnp.sum(pr, axis=-1, keepdims=True)
            out.append(((pr / den).astype(BF16), mx + jnp.log(den)))
        return out

    def values_and_store(batch, probs):
        outs = [_dot(pn, v_ref[rows(dil, qs, first), :]) for (gi, dil, qs, first), (pn, _) in zip(batch, probs)]
        for (gi, dil, qs, _), o, (_, lse) in zip(batch, outs, probs):
            og_scr[gi, rows(dil, qs, True), :] = o
            lse_scr[gi, rows(dil, qs, True), :] = jnp.broadcast_to(lse, (B, ATTN_HEAD))

    def merge(i):
        sl = pl.ds(i * R, R)
        ls = [lse_scr[gi, sl, :] for gi in range(len(DIL_PAIRS))]
        mx = functools.reduce(jnp.maximum, ls)
        ws = [jnp.exp(l - mx) for l in ls]
        tot = functools.reduce(lambda a, b: a + b, ws)
        acc = (ws[0] / tot) * og_scr[0, sl, :]
        for gi in range(1, len(DIL_PAIRS)):
            acc = acc + (ws[gi] / tot) * og_scr[gi, sl, :]
        o_ref[sl, :] = acc.astype(o_ref.dtype)

    pending = scores(batches[0])
    for i in range(seq // R):
        gather_head(vil_ref, v_ref, i)
    merged = 0
    contiguous_done = 0
    for bi, batch in enumerate(batches):
        upcoming = scores(batches[bi + 1]) if bi + 1 < len(batches) else None
        values_and_store(batch, softmax(pending))
        pending = upcoming
        contiguous_done += sum(1 for blk in batch if blk[1] == 1)
        while (merged + 1) * R <= contiguous_done * B:
            merge(merged)
            merged += 1
    assert merged == seq // R


def _attn_prompt(pxq, k_all, v_all, bias, layer, batch, seq):
    H = ATTN_HEADS
    G = len(DIL_PAIRS)
    q_col0 = (PXQ_WIDTH - ATTN_WIDTH) // ATTN_HEAD
    kern = functools.partial(_attn_prompt_kernel, seq=seq)
    kv = pl.BlockSpec((seq * H, ATTN_HEAD), lambda b, h: (layer * batch + b, 0))
    return pl.pallas_call(
        kern,
        grid=(batch, H),
        in_specs=[
            pl.BlockSpec((seq, ATTN_HEAD), lambda b, h: (b, q_col0 + h)),
            kv, kv,
            pl.BlockSpec((G, None, ATTN_SPAN, 2 * ATTN_SPAN), lambda b, h: (0, h, 0, 0)),
        ],
        out_specs=pl.BlockSpec((seq, ATTN_HEAD), lambda b, h: (b, h)),
        out_shape=jax.ShapeDtypeStruct((batch * seq, ATTN_WIDTH), BF16),
        scratch_shapes=[pltpu.VMEM((seq, ATTN_HEAD), F32), pltpu.VMEM((seq, ATTN_HEAD), F32),
                        pltpu.VMEM((G, seq, ATTN_HEAD), F32), pltpu.VMEM((G, seq, ATTN_HEAD), F32)],
        compiler_params=_cparams(("parallel", "arbitrary")),
        name="attn_prompt",
    )(pxq, k_all, v_all, bias)


def _attn_sample_kernel(pxq_ref, kn_ref, vn_ref, kc_ref, vc_ref, bias_ref, o_ref, q_scr, kn_scr, vn_scr, *, t_new, win):
    scale = ATTN_HEAD ** -0.5
    H = ATTN_HEADS
    G = len(DIL_PAIRS)
    TP = q_scr.shape[1]
    q_col0 = PXQ_WIDTH - ATTN_WIDTH
    q_scr[...] = jnp.zeros_like(q_scr)
    kn_scr[...] = jnp.zeros_like(kn_scr)
    vn_scr[...] = jnp.zeros_like(vn_scr)
    scores = []
    for h in range(H):
        q_scr[h, 0:t_new, :] = pxq_ref[:, q_col0 + h * ATTN_HEAD:q_col0 + (h + 1) * ATTN_HEAD]
        kn_scr[h, 0:t_new, :] = kn_ref[pl.ds(h, t_new, stride=H), :]
        vn_scr[h, 0:t_new, :] = vn_ref[pl.ds(h, t_new, stride=H), :]
        qb = q_scr[h].astype(BF16)
        kc = kc_ref[pl.ds(h, win, stride=H), :]
        scores.append(jnp.concatenate([_dot_nt(qb, kc), _dot_nt(qb, kn_scr[h])], axis=1) * scale)
    probs = []
    for h, s in enumerate(scores):
        prs, lses = [], []
        for gi in range(G):
            sg = s + bias_ref[gi, h]
            mx = jnp.max(sg, axis=-1, keepdims=True)
            pr = jnp.exp(sg - mx)
            den = jnp.sum(pr, axis=-1, keepdims=True)
            prs.append(pr / den)
            lses.append(mx + jnp.log(den))
        probs.append((jnp.concatenate(prs, axis=0).astype(BF16), lses))
    outs = []
    for h, (pall, _) in enumerate(probs):
        vc = vc_ref[pl.ds(h, win, stride=H), :]
        outs.append(_dot(pall[:, :win], vc) + _dot(pall[:, win:], vn_scr[h]))
    for h, (oall, (_, lses)) in enumerate(zip(outs, probs)):
        mx = functools.reduce(jnp.maximum, lses)
        ws = [jnp.exp(l - mx) for l in lses]
        tot = functools.reduce(lambda a, b: a + b, ws)
        acc = (ws[0] / tot) * oall[0:TP]
        for gi in range(1, G):
            acc = acc + (ws[gi] / tot) * oall[gi * TP:(gi + 1) * TP]
        o_ref[:, h * ATTN_HEAD:(h + 1) * ATTN_HEAD] = acc[0:t_new].astype(o_ref.dtype)


def _attn_sample(pxq, k_new, v_new, k_cache, v_cache, bias, layer, batch, t_new):
    H = ATTN_HEADS
    win = k_cache.shape[2] // H
    tp = bias.shape[2]
    kern = functools.partial(_attn_sample_kernel, t_new=t_new, win=win)
    cache = pl.BlockSpec((None, None, win * H, ATTN_HEAD), lambda b: (layer, b, 0, 0))
    new = pl.BlockSpec((None, t_new * H, ATTN_HEAD), lambda b: (layer * batch + b, 0, 0))
    return pl.pallas_call(
        kern,
        grid=(batch,),
        in_specs=[pl.BlockSpec((None, t_new, pxq.shape[2]), lambda b: (b, 0, 0)), new, new, cache, cache,
                  pl.BlockSpec(bias.shape, lambda b: (0, 0, 0, 0))],
        out_specs=pl.BlockSpec((None, t_new, ATTN_WIDTH), lambda b: (b, 0, 0)),
        out_shape=jax.ShapeDtypeStruct((batch, t_new, ATTN_WIDTH), F32),
        scratch_shapes=[pltpu.VMEM((H, tp, ATTN_HEAD), F32), pltpu.VMEM((H, LANES, ATTN_HEAD), F32),
                        pltpu.VMEM((H, LANES, ATTN_HEAD), F32)],
        compiler_params=_cparams(("parallel",)),
        name="attn_sample",
    )(pxq, k_new, v_new, k_cache, v_cache, bias)


def _bias_kernel(idx_ref, rb_ref, o_ref):
    h = pl.program_id(0)
    idx = idx_ref[...]
    acc = jnp.full(idx.shape, NEG, F32)
    for b in range(N_BUCKETS):
        acc = jnp.where(idx == b, rb_ref[b, h], acc)
    o_ref[...] = acc


def _bias_table(idx, rel_bias):
    g, r, c = idx.shape
    heads = rel_bias.shape[1]
    return pl.pallas_call(
        _bias_kernel,
        grid=(heads,),
        in_specs=[pl.BlockSpec((g, r, c), lambda h: (0, 0, 0)),
                  pl.BlockSpec(memory_space=pltpu.SMEM)],
        out_specs=pl.BlockSpec((g, None, r, c), lambda h: (0, h, 0, 0)),
        out_shape=jax.ShapeDtypeStruct((g, heads, r, c), F32),
        compiler_params=_cparams(("parallel",)),
        name="bias_table",
    )(jnp.asarray(idx, jnp.int32), rel_bias.astype(F32))


def _t5_bucket_static(dist):
    dist = np.asarray(dist, np.int64)
    exact = N_BUCKETS // 2
    d = np.maximum(dist, 1).astype(np.float64)
    large = exact + (np.log(d / exact) / math.log(REL_MAX_DIST / exact) * (N_BUCKETS - exact)).astype(np.int64)
    return np.where(dist < exact, dist, np.minimum(large, N_BUCKETS - 1))


def _prompt_bias_idx():
    B = ATTN_SPAN
    qi = np.arange(B)[:, None]
    kj = np.arange(2 * B)[None, :]
    delta = qi + B - kj
    valid = (delta >= 0) & (delta <= B)
    return np.stack([np.where(valid, _t5_bucket_static(np.clip(delta, 0, B) * dil), -1) for _, dil in DIL_PAIRS])


def _sample_bias_idx(t_new, t_pad, win, total):
    r = np.arange(total)[None, :]
    t = np.arange(t_pad)[:, None]
    dist = win + t - r
    tabs = []
    for _, dil in DIL_PAIRS:
        assert win >= dil * ATTN_SPAN
        valid = (dist >= 0) & (dist % dil == 0) & (dist <= dil * ATTN_SPAN) & (t < t_new)
        tabs.append(np.where(valid, _t5_bucket_static(np.clip(dist, 0, dil * ATTN_SPAN)), -1))
    return np.stack(tabs)


def _block_diag(blocks):
    n, c, d = blocks.shape
    eye = jnp.eye(n, dtype=blocks.dtype)
    return jnp.einsum("ncd,nm->ncmd", blocks, eye).reshape(n * c, n * d)


def _mixer_params(a):
    W = RWKV_WIDTH
    depth = a["rwkv_mu"].shape[0]
    row = lambda t: t.reshape(depth, 1, -1).astype(F32)

    def lora_pad(w, off):
        z = jnp.zeros((depth, LORA_WIDTH, W), F32)
        return z.at[:, off:off + w.shape[1]].set(w).astype(BF16)

    head_of = np.arange(W) // RWKV_HEAD
    ones_bd = jnp.asarray((head_of[:, None] == head_of[None, :]).astype(np.float32), BF16)
    rwkv = dict(
        mu=row(a["rwkv_mu"]), w0=row(a["rwkv_w0"]), a0=row(a["rwkv_a0"]),
        w_up=lora_pad(a["rwkv_w_up"], 0),
        a_up=lora_pad(a["rwkv_a_up"], DECAY_LORA),
        g_up=lora_pad(a["rwkv_g_up"], DECAY_LORA + ICLR_LORA),
        k_k=row(a["rwkv_k_k"]), k_a=row(a["rwkv_k_a"]), r_k=row(a["rwkv_r_k"]),
        lnx_g=row(a["rwkv_lnx_g"]), lnx_b=row(a["rwkv_lnx_b"]),
        ones_bd=ones_bd,
    )
    lru = dict(
        conv_w=a["lru_conv_w"].astype(F32), conv_b=row(a["lru_conv_b"]),
        wa_bd=jax.vmap(_block_diag)(a["lru_wa"]).astype(BF16), ba=row(a["lru_ba"]),
        wx_bd=jax.vmap(_block_diag)(a["lru_wx"]).astype(BF16), bx=row(a["lru_bx"]),
        lam=row(a["lru_lambda"]),
    )
    return dict(rwkv=rwkv, lru=lru)


def _shared_params(a):
    gain = lambda t: t.reshape(t.shape[0], 1, -1).astype(F32)
    return dict(
        g_mix_pre=gain(a["norm_mix_pre"]), g_mix_post=gain(a["norm_mix_post"]),
        g_ffn_pre=gain(a["norm_ffn_pre"]), g_ffn_post=gain(a["norm_ffn_post"]),
    )


def _layer(x, layer, sp, wts, lp, batch, t_len, shift_prev, s0, conv_prev, h0, kv_all, attend):
    emit = wts is None
    src = sp["f32"] if emit else wts
    wl = layer if emit else 0
    copies = {}
    pxq, k_all, v_all, *w_copy = _in_proj(x, sp["g_mix_pre"], src["w_in"], layer, wl, *kv_all, emit_bf16=emit)
    copies["w_in"] = w_copy[0] if emit else None
    pxq3 = pxq.reshape(batch, t_len, pxq.shape[1])
    if t_len % RWKV_CHUNK == 0:
        y_a, s_fin = _rwkv(pxq3, shift_prev, s0, lp["rwkv"], layer, batch, t_len, t_len, RWKV_CHUNK)
        y_a = y_a.reshape(batch * t_len, -1)
        y_b, h_fin = _lru(pxq, conv_prev, h0, lp["lru"], layer, batch, t_len, t_len, LRU_CHUNK)
    else:
        assert t_len <= SUBLANES
        padded = jnp.pad(pxq3, ((0, 0), (0, SUBLANES - t_len), (0, 0)))
        y_a, s_fin = _rwkv(padded, shift_prev, s0, lp["rwkv"], layer, batch, SUBLANES, t_len, SUBLANES)
        y_a = y_a[:, :t_len].reshape(batch * t_len, -1)
        y_b, h_fin = _lru(padded.reshape(batch * SUBLANES, pxq.shape[1]), conv_prev, h0, lp["lru"], layer, batch, SUBLANES,
                          t_len, SUBLANES)
        y_b = y_b.reshape(batch, SUBLANES, -1)[:, :t_len].reshape(batch * t_len, -1)
    y_c = attend(pxq, k_all, v_all)

    x = _out_proj(y_a, y_b, y_c, src["w_out"], sp["g_mix_post"], x, layer, wl, emit_bf16=emit)
    if emit:
        x, copies["w_out"] = x
    x = _ffn(x, sp["g_ffn_pre"], src["w1"], src["w2"], sp["g_ffn_post"], layer, wl, emit_bf16=emit)
    if emit:
        x, copies["w1"], copies["w2"] = x

    lru_x = pxq3[:, :, RWKV_PROJ:RWKV_PROJ + LRU_WIDTH]
    if t_len >= CONV_W - 1:
        conv_new = lru_x[:, t_len - (CONV_W - 1):]
    else:
        conv_new = jnp.concatenate([conv_prev, lru_x], axis=1)[:, -(CONV_W - 1):]
    state = (pxq3[:, -1, :RWKV_PROJ], s_fin, conv_new, h_fin.reshape(batch, LRU_WIDTH))
    return x, state, (k_all, v_all), copies


def kernel(x_prompt, x_sample, state_rwkv_wkv, state_rwkv_shift, state_lru_h, state_lru_conv, cache_attn_k, cache_attn_v, rel_bias, norm_mix_pre, norm_mix_post, norm_ffn_pre, norm_ffn_post, w_in, w_out, rwkv_mu, rwkv_w0, rwkv_w_up, rwkv_a0, rwkv_a_up, rwkv_g_up, rwkv_k_k, rwkv_k_a, rwkv_r_k, rwkv_lnx_g, rwkv_lnx_b, lru_conv_w, lru_conv_b, lru_wa, lru_ba, lru_wx, lru_bx, lru_lambda, ffn_w1, ffn_w2):
    a = dict(norm_mix_pre=norm_mix_pre, norm_mix_post=norm_mix_post, norm_ffn_pre=norm_ffn_pre,
             norm_ffn_post=norm_ffn_post, w_in=w_in, w_out=w_out, rwkv_mu=rwkv_mu, rwkv_w0=rwkv_w0,
             rwkv_w_up=rwkv_w_up, rwkv_a0=rwkv_a0, rwkv_a_up=rwkv_a_up, rwkv_g_up=rwkv_g_up, rwkv_k_k=rwkv_k_k,
             rwkv_k_a=rwkv_k_a, rwkv_r_k=rwkv_r_k, rwkv_lnx_g=rwkv_lnx_g, rwkv_lnx_b=rwkv_lnx_b,
             lru_conv_w=lru_conv_w, lru_conv_b=lru_conv_b, lru_wa=lru_wa, lru_ba=lru_ba, lru_wx=lru_wx,
             lru_bx=lru_bx, lru_lambda=lru_lambda, ffn_w1=ffn_w1, ffn_w2=ffn_w2)
    depth = w_in.shape[0]
    pb, seq, _ = x_prompt.shape
    sb, t_new, _ = x_sample.shape
    win = cache_attn_k.shape[2]
    keep = min(ATTN_WINDOW, seq)
    total = win + LANES
    assert t_new <= LANES and seq % (DIL_PAIRS[-1][1] * ATTN_SPAN) == 0

    bias_p = _bias_table(_prompt_bias_idx(), rel_bias)
    bias_s = _bias_table(_sample_bias_idx(t_new, SUBLANES, win, total), rel_bias)
    k_cache = cache_attn_k.reshape(depth, sb, win * ATTN_HEADS, ATTN_HEAD)
    v_cache = cache_attn_v.reshape(depth, sb, win * ATTN_HEADS, ATTN_HEAD)

    xp = x_prompt.reshape(pb * seq, D_MODEL)
    xs = x_sample.reshape(sb * t_new, D_MODEL)
    new_p, new_s = [], []
    sp = _shared_params(a)
    kv_p = tuple(jnp.zeros((depth * pb * seq * ATTN_HEADS, ATTN_HEAD), F32) for _ in range(2))
    kv_s = tuple(jnp.zeros((depth * sb * t_new * ATTN_HEADS, ATTN_HEAD), F32) for _ in range(2))
    sp["f32"] = dict(w_in=w_in.astype(F32), w_out=w_out.astype(F32), w1=ffn_w1.astype(F32), w2=ffn_w2.astype(F32))
    lp = _mixer_params(a)
    for l in range(depth):
        attend_s = lambda pxq, k_all, v_all, l=l: _attn_sample(
            pxq.reshape(sb, t_new, pxq.shape[1]), k_all.reshape(depth * sb, t_new * ATTN_HEADS, ATTN_HEAD),
            v_all.reshape(depth * sb, t_new * ATTN_HEADS, ATTN_HEAD), k_cache, v_cache, bias_s, l, sb, t_new,
        ).reshape(sb * t_new, ATTN_WIDTH)
        xs, st_s, kv_s, wts = _layer(xs, l, sp, None, lp, sb, t_new,
                                     state_rwkv_shift[l].reshape(sb, 1, RWKV_PROJ), state_rwkv_wkv[l].astype(F32),
                                     state_lru_conv[l], state_lru_h[l].reshape(sb, 1, LRU_WIDTH), kv_s, attend_s)
        attend_p = lambda pxq, k_all, v_all, l=l: _attn_prompt(pxq, k_all, v_all, bias_p, l, pb, seq)
        xp, st_p, kv_p, _ = _layer(xp, l, sp, wts, lp, pb, seq,
                                   jnp.zeros((pb, 1, RWKV_PROJ), F32),
                                   jnp.zeros((pb, RWKV_HEADS, RWKV_HEAD, RWKV_HEAD), F32),
                                   jnp.zeros((pb, CONV_W - 1, LRU_WIDTH), F32), jnp.zeros((pb, 1, LRU_WIDTH), F32),
                                   kv_p, attend_p)
        new_p.append(st_p)
        new_s.append(st_s)

    stack = lambda sts, i: jnp.stack([s[i] for s in sts])
    k_p, v_p = (t.reshape(depth, pb, seq, ATTN_HEADS, ATTN_HEAD)[:, :, seq - keep:] for t in kv_p)
    k_s, v_s = (t.reshape(depth, sb, t_new, ATTN_HEADS, ATTN_HEAD) for t in kv_s)
    return (xp.reshape(pb, seq, D_MODEL), xs.reshape(sb, t_new, D_MODEL),
            stack(new_p, 1), stack(new_s, 1), stack(new_p, 0), stack(new_s, 0),
            stack(new_p, 3), stack(new_s, 3), stack(new_p, 2), stack(new_s, 2),
            k_p, k_s, v_p, v_s)
```

```python
import functools
import math

import numpy as np
import jax
import jax.numpy as jnp
from jax import lax
from jax.experimental import pallas as pl
from jax.experimental.pallas import tpu as pltpu

F32 = jnp.float32
BF16 = jnp.bfloat16

D_MODEL = 2048
RWKV_WIDTH = 512
RWKV_HEAD = 64
RWKV_HEADS = RWKV_WIDTH // RWKV_HEAD
DECAY_LORA = 64
ICLR_LORA = 64
GATE_LORA = 128
LORA_WIDTH = DECAY_LORA + ICLR_LORA + GATE_LORA
RWKV_PROJ = 3 * RWKV_WIDTH + LORA_WIDTH
GN_EPS = 64e-5
LRU_WIDTH = 512
LRU_BLOCKS = 8
LRU_BLOCK = LRU_WIDTH // LRU_BLOCKS
CONV_W = 4
RG_C = 8.0
ATTN_WIDTH = 1024
ATTN_HEAD = 128
ATTN_HEADS = ATTN_WIDTH // ATTN_HEAD
DIL_PAIRS = ((128, 1), (512, 4), (2048, 16))
ATTN_SPAN = 128
ATTN_WINDOW = 2048
N_BUCKETS = 32
REL_MAX_DIST = ATTN_WINDOW
D_FF = 4 * D_MODEL
RMS_EPS = 1e-6
NEG = -1e30

LANES = 128
SUBLANES = 8
MXU_TILE = 256
VMEM_LIMIT_BYTES = 56 * 1024 * 1024
IN_PROJ_VMEM_LIMIT_BYTES = 60 * 1024 * 1024

ROW_TILE = 1024
IN_COL_TILE = 256
IN_FUSED_TILES = 4
OUT_ROW_TILE = 512
FFN_ROW_TILE = 512
FFN_COL_TILE = 1024
FFN_CAST_COL_TILE = 512
RWKV_CHUNK = 64
RWKV_SEQS_PER_STEP = 4
LRU_CHUNK = 256
ATTN_MERGE_ROWS = 256
ATTN_UNROLL = 4


def _cparams(sem, vmem_limit_bytes=VMEM_LIMIT_BYTES):
    return pltpu.CompilerParams(dimension_semantics=sem, vmem_limit_bytes=vmem_limit_bytes)


def _dot(a, b):
    return jnp.dot(a.astype(BF16), b.astype(BF16), preferred_element_type=F32)


def _dot_nt(a, b):
    return lax.dot_general(a.astype(BF16), b.astype(BF16), (((1,), (1,)), ((), ())), preferred_element_type=F32)


def _dot_tn(a, b):
    return lax.dot_general(a.astype(BF16), b.astype(BF16), (((0,), (0,)), ((), ())), preferred_element_type=F32)


def _softplus(z):
    return jnp.maximum(z, 0.0) + jnp.log1p(jnp.exp(-jnp.abs(z)))


def _mix_dtype(rows):
    return BF16 if rows % (2 * SUBLANES) == 0 else F32


def _rms(x, g):
    ms = jnp.mean(x * x, axis=-1, keepdims=True)
    return x * lax.rsqrt(ms + RMS_EPS) * g


PXQ_WIDTH = RWKV_PROJ + 2 * LRU_WIDTH + ATTN_WIDTH
PXQ_TILES = PXQ_WIDTH // IN_COL_TILE
KV_TILES = ATTN_WIDTH // IN_COL_TILE
HEADS_PER_TILE = IN_COL_TILE // ATTN_HEAD


def _in_proj_kernel(x_ref, g_ref, *refs, fuse):
    w_refs = refs[:fuse]
    pxq_ref, k_ref, v_ref = refs[fuse + 2:fuse + 5]
    rest = refs[fuse + 5:]
    h_scr = rest[-1]
    j = pl.program_id(1)
    tm = x_ref.shape[0]
    pxq_steps = pl.cdiv(PXQ_TILES, fuse)
    kv_steps = KV_TILES // fuse

    @pl.when(j == 0)
    def _():
        h_scr[...] = _rms(x_ref[...], g_ref[...]).astype(BF16)

    ws = [w_ref[...].astype(BF16) for w_ref in w_refs]
    if len(rest) > 1:
        rest[0][...] = ws[0]
    w = ws[0] if fuse == 1 else jnp.concatenate(ws, axis=1)
    acc = jnp.dot(h_scr[...], w, preferred_element_type=F32)

    @pl.when(j < pxq_steps)
    def _():
        pxq_ref[...] = acc

    def scatter_heads(o_ref, step):
        for hh in range(fuse * HEADS_PER_TILE):
            head = step * (fuse * HEADS_PER_TILE) + hh
            o_ref[pl.ds(head, tm, stride=ATTN_HEADS), :] = acc[:, hh * ATTN_HEAD:(hh + 1) * ATTN_HEAD]

    @pl.when((j >= pxq_steps) & (j < pxq_steps + kv_steps))
    def _():
        scatter_heads(k_ref, j - pxq_steps)

    @pl.when(j >= pxq_steps + kv_steps)
    def _():
        scatter_heads(v_ref, j - pxq_steps - kv_steps)


def _in_proj(x, g, w, layer, w_layer, k_all, v_all, emit_bf16=False):
    m, d = x.shape
    n = w.shape[2]
    assert n == PXQ_WIDTH + 2 * ATTN_WIDTH
    tm = min(ROW_TILE, m)
    row_tiles = m // tm
    assert not emit_bf16 or row_tiles == 1
    fuse = 1 if emit_bf16 else IN_FUSED_TILES
    assert KV_TILES % fuse == 0
    pxq_steps = pl.cdiv(PXQ_TILES, fuse)
    spare = pxq_steps * fuse - PXQ_TILES
    width = fuse * IN_COL_TILE

    def w_spec(slot):
        def index(i, j):
            tile = jnp.where(j < pxq_steps, jnp.minimum(fuse * j + slot, PXQ_TILES - 1), fuse * j + slot - spare)
            return (w_layer, 0, tile)
        return pl.BlockSpec((None, d, IN_COL_TILE), index)

    kv_spec = pl.BlockSpec((tm * ATTN_HEADS, ATTN_HEAD), lambda i, j: (layer * row_tiles + i, 0))
    out_specs = [
        pl.BlockSpec((tm, width), lambda i, j: (i, jnp.minimum(j, pxq_steps - 1))),
        kv_spec, kv_spec,
    ]
    out_shape = [
        jax.ShapeDtypeStruct((m, pxq_steps * width), F32),
        jax.ShapeDtypeStruct(k_all.shape, F32),
        jax.ShapeDtypeStruct(v_all.shape, F32),
    ]
    if emit_bf16:
        out_specs.append(pl.BlockSpec((None, d, IN_COL_TILE), lambda i, j: (0, 0, j)))
        out_shape.append(jax.ShapeDtypeStruct((1, d, n), BF16))
    return pl.pallas_call(
        functools.partial(_in_proj_kernel, fuse=fuse),
        grid=(row_tiles, pxq_steps + 2 * (KV_TILES // fuse)),
        in_specs=[
            pl.BlockSpec((tm, d), lambda i, j: (i, 0)),
            pl.BlockSpec((None, 1, d), lambda i, j: (layer, 0, 0)),
            *[w_spec(slot) for slot in range(fuse)],
            pl.BlockSpec(memory_space=pl.ANY),
            pl.BlockSpec(memory_space=pl.ANY),
        ],
        out_specs=out_specs,
        out_shape=out_shape,
        input_output_aliases={2 + fuse: 1, 3 + fuse: 2},
        scratch_shapes=[pltpu.VMEM((tm, d), BF16)],
        compiler_params=_cparams(("parallel", "arbitrary"), IN_PROJ_VMEM_LIMIT_BYTES),
        name="in_proj",
    )(x, g, *([w] * fuse), k_all, v_all)


def _out_proj_kernel(ya_ref, yb_ref, yc_ref, w_ref, g_ref, x_ref, o_ref, *w_copy):
    c1, c2 = RWKV_WIDTH, RWKV_WIDTH + LRU_WIDTH
    if w_copy:
        acc = None
        for y_ref, rows in [(ya_ref, slice(0, c1)), (yb_ref, slice(c1, c2)), (yc_ref, slice(c2, None))]:
            w = w_ref[rows, :].astype(BF16)
            w_copy[0][rows, :] = w
            t = jnp.dot(y_ref[...].astype(BF16), w, preferred_element_type=F32)
            acc = t if acc is None else acc + t
    else:
        y = jnp.concatenate([r[...].astype(BF16) for r in (ya_ref, yb_ref, yc_ref)], axis=1)
        acc = jnp.dot(y, w_ref[...].astype(BF16), preferred_element_type=F32)
    o_ref[...] = x_ref[...] + _rms(acc, g_ref[...])


def _out_proj(ya, yb, yc, w, g, x, layer, w_layer, emit_bf16=False):
    m, d = x.shape
    tm = min(OUT_ROW_TILE, m)
    assert not emit_bf16 or m == tm
    out_specs = [pl.BlockSpec((tm, d), lambda i: (i, 0))]
    out_shape = [jax.ShapeDtypeStruct((m, d), F32)]
    if emit_bf16:
        out_specs.append(pl.BlockSpec((None, d, d), lambda i: (0, 0, 0)))
        out_shape.append(jax.ShapeDtypeStruct((1, d, d), BF16))
    res = pl.pallas_call(
        _out_proj_kernel,
        grid=(m // tm,),
        in_specs=[
            pl.BlockSpec((tm, RWKV_WIDTH), lambda i: (i, 0)),
            pl.BlockSpec((tm, LRU_WIDTH), lambda i: (i, 0)),
            pl.BlockSpec((tm, ATTN_WIDTH), lambda i: (i, 0)),
            pl.BlockSpec((None, d, d), lambda i: (w_layer, 0, 0)),
            pl.BlockSpec((None, 1, d), lambda i: (layer, 0, 0)),
            pl.BlockSpec((tm, d), lambda i: (i, 0)),
        ],
        out_specs=out_specs,
        out_shape=out_shape,
        compiler_params=_cparams(("parallel",)),
        name="out_proj",
    )(ya, yb, yc, w, g, x)
    return res if emit_bf16 else res[0]


def _ffn_kernel(x_ref, g1_ref, w1_ref, w2_ref, g2_ref, o_ref, *rest):
    h_scr, acc_scr = rest[-2:]
    j = pl.program_id(1)

    @pl.when(j == 0)
    def _():
        h_scr[...] = _rms(x_ref[...], g1_ref[...]).astype(BF16)
        acc_scr[...] = jnp.zeros_like(acc_scr)

    w1 = w1_ref[...].astype(BF16)
    w2 = w2_ref[...].astype(BF16)
    if len(rest) > 2:
        rest[0][...] = w1
        rest[1][...] = w2
    u = jnp.dot(h_scr[...], w1, preferred_element_type=F32)
    u = jnp.square(jnp.maximum(u, 0.0)).astype(BF16)
    acc_scr[...] += jnp.dot(u, w2, preferred_element_type=F32)

    @pl.when(j == pl.num_programs(1) - 1)
    def _():
        o_ref[...] = x_ref[...] + _rms(acc_scr[...], g2_ref[...])


def _ffn(x, g1, w1, w2, g2, layer, w_layer, emit_bf16=False):
    m, d = x.shape
    f = w1.shape[2]
    tm = min(FFN_ROW_TILE, m)
    tf = FFN_CAST_COL_TILE if emit_bf16 else FFN_COL_TILE
    assert not emit_bf16 or m == tm
    out_specs = [pl.BlockSpec((tm, d), lambda i, j: (i, 0))]
    out_shape = [jax.ShapeDtypeStruct((m, d), F32)]
    if emit_bf16:
        out_specs += [pl.BlockSpec((None, d, tf), lambda i, j: (0, 0, j)),
                      pl.BlockSpec((None, tf, d), lambda i, j: (0, j, 0))]
        out_shape += [jax.ShapeDtypeStruct((1, d, f), BF16), jax.ShapeDtypeStruct((1, f, d), BF16)]
    res = pl.pallas_call(
        _ffn_kernel,
        grid=(m // tm, f // tf),
        in_specs=[
            pl.BlockSpec((tm, d), lambda i, j: (i, 0)),
            pl.BlockSpec((None, 1, d), lambda i, j: (layer, 0, 0)),
            pl.BlockSpec((None, d, tf), lambda i, j: (w_layer, 0, j)),
            pl.BlockSpec((None, tf, d), lambda i, j: (w_layer, j, 0)),
            pl.BlockSpec((None, 1, d), lambda i, j: (layer, 0, 0)),
        ],
        out_specs=out_specs,
        out_shape=out_shape,
        scratch_shapes=[pltpu.VMEM((tm, d), BF16), pltpu.VMEM((tm, d), F32)],
        compiler_params=_cparams(("parallel", "arbitrary")),
        name="ffn",
    )(x, g1, w1, w2, g2)
    return res if emit_bf16 else res[0]


def _rwkv_groups(chunk):
    lanes = RWKV_HEADS * chunk
    return lanes // MXU_TILE if lanes % (2 * MXU_TILE) == 0 else 1


def _rwkv_kernel(p_ref, shift_ref, s0_ref, mu_ref, w0_ref, wup_ref, a0_ref, aup_ref, gup_ref, kk_ref, ka_ref,
                 rk_ref, lg_ref, lb_ref, ones_ref,
                 y_ref, sout_ref, s_scr, prev_scr, *, chunk, t_valid):
    C = chunk
    W = RWKV_WIDTH
    H = RWKV_HEADS
    N = RWKV_HEAD
    nb = p_ref.shape[0]
    c = pl.program_id(1)

    G = _rwkv_groups(C)
    hpg = H // G
    wl = W // G
    gc = hpg * C

    @pl.when(c == 0)
    def _():
        s_scr[...] = jnp.zeros_like(s_scr)
        for h in range(H):
            o = (h % hpg) * N
            s_scr[:, h // hpg, o:o + N, o:o + N] = s0_ref[:, h]
        prev_scr[...] = shift_ref[...]

    inv_n = 1.0 / RWKV_HEAD
    shared = dict(wup=[wup_ref], aup=[aup_ref], gup=[gup_ref],
                  ones=[ones_ref.at[g * wl:(g + 1) * wl, g * wl:(g + 1) * wl] for g in range(G)])

    def hi_lo(x):
        hi = x.astype(BF16).astype(F32)
        return [hi, x - hi]

    row = lax.broadcasted_iota(jnp.int32, (C, 1), 0)

    def cumsum_rows(z):
        dlt = 1
        while dlt < C:
            z = z + jnp.where(row >= dlt, pltpu.roll(z, dlt, axis=0), 0.0)
            dlt *= 2
        return z

    ti = lax.broadcasted_iota(jnp.int32, (C, H * C), 0)
    si = lax.broadcasted_iota(jnp.int32, (C, H * C), 1) % C
    strict = ti > si
    incl = ti >= si
    eye = (ti == si).astype(F32)
    blk_r = lax.broadcasted_iota(jnp.int32, (gc, 1), 0) // C
    mask_ch = blk_r == lax.broadcasted_iota(jnp.int32, (1, wl), 1) // N
    mask_cc = blk_r == lax.broadcasted_iota(jnp.int32, (1, gc), 1) // C
    mask_ss = (lax.broadcasted_iota(jnp.int32, (wl, 1), 0) // N
               == lax.broadcasted_iota(jnp.int32, (1, wl), 1) // N)

    def block_diag(x, mask, width):
        out = []
        for g in range(G):
            tiled = jnp.concatenate([x[:, g * width:(g + 1) * width]] * hpg, axis=0)
            out.append(jnp.where(mask, tiled, 0.0).astype(BF16))
        return out

    def per_head(a_cat, bds):
        a_b = a_cat.astype(BF16)
        return jnp.concatenate([jnp.dot(a_b[:, g * gc:(g + 1) * gc], bds[g], preferred_element_type=F32)
                                for g in range(G)], axis=1)

    def one_sequence(bi):
        p = p_ref[bi]
        shifted = jnp.where(row == 0, prev_scr[bi], pltpu.roll(p, 1, axis=0))
        prev_scr[bi] = p_ref[bi, C - 1:C, :]
        m = p + (shifted - p) * mu_ref[...]
        r = m[:, 0:W]
        k = m[:, W:2 * W]
        v = m[:, 2 * W:3 * W]
        x = m[:, 3 * W:]
        lw, la, gate = yield [("wup", jnp.tanh(x)), ("aup", x), ("gup", jax.nn.sigmoid(x))]
        w = w0_ref[...] + lw
        a = jax.nn.sigmoid(a0_ref[...] + la)
        softplus_neg_w = jnp.maximum(-w, 0.0) + jnp.log(1.0 + jnp.exp(-jnp.abs(w)))
        loga = -jnp.exp(-softplus_neg_w - 0.5)
        kk = k * kk_ref[...]
        k2 = k * (1.0 + (a - 1.0) * ka_ref[...])
        (ss,) = yield [("ones", jnp.concatenate(hi_lo(kk * kk) + hi_lo(r * k2 * rk_ref[...]), axis=0))]
        kk = kk / jnp.maximum(jnp.sqrt(ss[0:C] + ss[C:2 * C]), 1e-12)
        bonus = (ss[2 * C:3 * C] + ss[3 * C:]) * v
        if t_valid < C:
            live = row < t_valid
            loga = jnp.where(live, loga, 0.0)
            kk = jnp.where(live, kk, 0.0)
            k2 = jnp.where(live, k2, 0.0)
        cl = cumsum_rows(loga)
        cl_last = cl[C - 1:C, :]
        e_neg = jnp.exp(-cl)
        e_rem = jnp.exp(cl_last - cl)
        kka = kk * a
        al = -kk * jnp.exp(cl - loga)
        rt = r * jnp.exp(cl)
        be_bd = block_diag(kka * e_neg, mask_ch, wl)
        kt_bd = block_diag(k2 * e_neg, mask_ch, wl)
        bh = kka * e_rem
        kh = k2 * e_rem
        lhs = jnp.concatenate([al, rt], axis=0).astype(BF16)
        lhs_g = [lhs[:, g * wl:(g + 1) * wl] for g in range(G)]
        g_b = jnp.concatenate([_dot_nt(lhs_g[g], be_bd[g]) for g in range(G)], axis=1)
        g_k = jnp.concatenate([_dot_nt(lhs_g[g], kt_bd[g]) for g in range(G)], axis=1)
        n_cat = jnp.where(strict, g_b[0:C], 0.0)
        a_ak = jnp.where(strict, g_k[0:C], 0.0)
        a_rb = jnp.where(incl, g_b[C:], 0.0)
        a_rk = jnp.where(incl, g_k[C:], 0.0)
        s_prev = [s_scr[bi, g] for g in range(G)]
        proj = jnp.concatenate([_dot_nt(lhs_g[g], s_prev[g]) for g in range(G)], axis=1)
        doublings = max(int(math.log2(C)) - 1, 0)
        t_cat = eye + n_cat
        pw = n_cat
        if doublings:
            pw = per_head(n_cat, block_diag(n_cat, mask_cc, gc))
        yield None
        for it in range(doublings):
            pw_bd = block_diag(pw, mask_cc, gc)
            if it < doublings - 1:
                both = per_head(jnp.concatenate([t_cat, pw], axis=0), pw_bd)
                t_cat = t_cat + both[0:C]
                pw = both[C:]
            else:
                t_cat = t_cat + per_head(t_cat, pw_bd)
            yield None
        v_bd = block_diag(v, mask_ch, wl)
        rhs = proj[0:C] + per_head(a_ak, v_bd)
        yield None
        u = per_head(t_cat, block_diag(rhs, mask_ch, wl))
        yield None
        y = proj[C:] + per_head(a_rb, block_diag(u, mask_ch, wl)) + per_head(a_rk, v_bd)
        pad = LANES - 2 * C
        uv = jnp.concatenate([u, v] + ([jnp.zeros((pad, W), F32)] if pad > 0 else []), axis=0).astype(BF16)
        bk = jnp.concatenate([bh, kh] + ([jnp.zeros((pad, W), F32)] if pad > 0 else []), axis=0).astype(BF16)
        decay = jnp.exp(cl_last)
        for g in range(G):
            lanes = slice(g * wl, (g + 1) * wl)
            upd = _dot_tn(uv[:, lanes], bk[:, lanes])
            s_scr[bi, g] = s_prev[g] * decay[:, lanes] + jnp.where(mask_ss, upd, 0.0)

        @pl.when(c == pl.num_programs(1) - 1)
        def _():
            for h in range(H):
                o = (h % hpg) * N
                sout_ref[bi, h] = s_scr[bi, h // hpg, o:o + N, o:o + N]

        (sy,) = yield [("ones", jnp.concatenate(hi_lo(y), axis=0))]
        d = y - (sy[0:C] + sy[C:]) * inv_n
        (sd,) = yield [("ones", jnp.concatenate(hi_lo(d * d), axis=0))]
        var = (sd[0:C] + sd[C:]) * inv_n
        yn = d * lax.rsqrt(var + GN_EPS) * lg_ref[...] + lb_ref[...]
        y_ref[bi] = ((yn + bonus) * gate).astype(y_ref.dtype)

    seqs = [one_sequence(bi) for bi in range(nb)]
    replies = [None] * nb
    while True:
        asks, finished = [], 0
        for seq, reply in zip(seqs, replies):
            try:
                asks.append(seq.send(reply))
            except StopIteration:
                finished += 1
        if finished:
            assert finished == nb
            break
        if asks[0] is None:
            replies = [None] * nb
            continue
        replies = [[] for _ in range(nb)]
        for qi, (name, _) in enumerate(asks[0]):
            lhs = jnp.concatenate([ask[qi][1] for ask in asks], axis=0).astype(BF16)
            parts = shared[name]
            kw = lhs.shape[1] // len(parts)
            z = jnp.concatenate([jnp.dot(lhs[:, g * kw:(g + 1) * kw], part[...], preferred_element_type=F32)
                                 for g, part in enumerate(parts)], axis=1)
            rows = z.shape[0] // nb
            for si in range(nb):
                replies[si].append(z[si * rows:(si + 1) * rows])


def _rwkv(p, shift_prev, s0, prm, layer, batch, t_len, t_valid, chunk):
    W = RWKV_WIDTH
    nc = t_len // chunk
    nb = _largest_divisor(batch, RWKV_SEQS_PER_STEP)
    vec = lambda n: pl.BlockSpec((None, 1, n), lambda b, c: (layer, 0, 0))
    mat = lambda r, n: pl.BlockSpec((None, r, n), lambda b, c: (layer, 0, 0))
    state = pl.BlockSpec((nb, RWKV_HEADS, RWKV_HEAD, RWKV_HEAD), lambda b, c: (b, 0, 0, 0))
    groups = _rwkv_groups(chunk)
    kern = functools.partial(_rwkv_kernel, chunk=chunk, t_valid=t_valid)
    return pl.pallas_call(
        kern,
        grid=(batch // nb, nc),
        in_specs=[
            pl.BlockSpec((nb, chunk, RWKV_PROJ), lambda b, c: (b, c, 0)),
            pl.BlockSpec((nb, 1, RWKV_PROJ), lambda b, c: (b, 0, 0)),
            state,
            vec(RWKV_PROJ), vec(W), mat(LORA_WIDTH, W), vec(W), mat(LORA_WIDTH, W), mat(LORA_WIDTH, W),
            vec(W), vec(W), vec(W), vec(W), vec(W), pl.BlockSpec((W, W), lambda b, c: (0, 0)),
        ],
        out_specs=[pl.BlockSpec((nb, chunk, W), lambda b, c: (b, c, 0)), state],
        out_shape=[
            jax.ShapeDtypeStruct((batch, t_len, W), _mix_dtype(chunk)),
            jax.ShapeDtypeStruct((batch, RWKV_HEADS, RWKV_HEAD, RWKV_HEAD), F32),
        ],
        scratch_shapes=[pltpu.VMEM((nb, groups, W // groups, W // groups), F32),
                        pltpu.VMEM((nb, 1, RWKV_PROJ), F32)],
        compiler_params=_cparams(("parallel", "arbitrary")),
        name="rwkv7",
    )(p, shift_prev, s0, prm["mu"], prm["w0"], prm["w_up"], prm["a0"], prm["a_up"], prm["g_up"],
      prm["k_k"], prm["k_a"], prm["r_k"], prm["lnx_g"], prm["lnx_b"], prm["ones_bd"])


def _lru_kernel(x0_ref, x1_ref, g0_ref, g1_ref, conv_ref, h0_ref, cw_ref, cb_ref, wa_ref, ba_ref, wx_ref, bx_ref,
                lam_ref, y_ref, hout_ref, tail_scr, h_scr, *, chunk, t_valid):
    C = chunk
    c = pl.program_id(1)

    @pl.when(c == 0)
    def _():
        tail_scr[...] = jnp.zeros_like(tail_scr)
        tail_scr[SUBLANES - (CONV_W - 1):, :] = conv_ref[...]
        h_scr[...] = h0_ref[...]

    x = jnp.concatenate([x0_ref[...], x1_ref[...]], axis=1)
    gt = jnp.concatenate([g0_ref[...], g1_ref[...]], axis=1)
    row_in_group = lax.broadcasted_iota(jnp.int32, (C, 1), 0) % SUBLANES

    def rotate_in_groups(z, dlt):
        z3 = z.reshape(z.shape[0] // SUBLANES, SUBLANES, z.shape[1])
        return pltpu.roll(z3, dlt, axis=1).reshape(z.shape)

    x_prev = tail_scr[...]
    if C > SUBLANES:
        x_prev = jnp.concatenate([x_prev, x[:C - SUBLANES]], axis=0)
    tail_scr[...] = x[C - SUBLANES:, :]
    xc = cb_ref[...] + x * cw_ref[CONV_W - 1:CONV_W, :]
    for dlt in range(1, CONV_W):
        sh = jnp.where(row_in_group >= dlt, rotate_in_groups(x, dlt), rotate_in_groups(x_prev, dlt))
        xc = xc + sh * cw_ref[CONV_W - 1 - dlt:CONV_W - dlt, :]

    gate_r = jax.nn.sigmoid(_dot(xc, wa_ref[...]) + ba_ref[...])
    gate_i = jax.nn.sigmoid(_dot(xc, wx_ref[...]) + bx_ref[...])
    log_a = -RG_C * gate_r * _softplus(-lam_ref[...])
    a = jnp.exp(log_a)
    th = jnp.tanh(log_a)
    u = jnp.sqrt(-2.0 * th / (1.0 - th)) * (gate_i * xc)

    dlt = 1
    while dlt < SUBLANES:
        keep = row_in_group >= dlt
        a_sh = jnp.where(keep, rotate_in_groups(a, dlt), 1.0)
        u_sh = jnp.where(keep, rotate_in_groups(u, dlt), 0.0)
        u = a * u_sh + u
        a = a * a_sh
        dlt *= 2
    carry = h_scr[...]
    groups = []
    for gi in range(C // SUBLANES):
        rows = slice(gi * SUBLANES, (gi + 1) * SUBLANES)
        groups.append(a[rows] * carry + u[rows])
        carry = groups[-1][SUBLANES - 1:SUBLANES, :]
    h = jnp.concatenate(groups, axis=0)
    last = min(t_valid, C) - 1
    h_scr[...] = h[last:last + 1, :]
    hout_ref[...] = h[last:last + 1, :]
    y_ref[...] = (h * jax.nn.gelu(gt)).astype(y_ref.dtype)


def _lru(pxq, conv_prev, h0, prm, layer, batch, t_len, t_valid, chunk):
    W = LRU_WIDTH
    nc = t_len // chunk
    half = W // 2
    col0 = RWKV_PROJ // half
    assert RWKV_PROJ % half == 0
    part = lambda k: pl.BlockSpec((chunk, half), lambda b, c: (b * nc + c, col0 + k))
    vec = lambda n: pl.BlockSpec((None, 1, n), lambda b, c: (layer, 0, 0))
    mat = lambda r, n: pl.BlockSpec((None, r, n), lambda b, c: (layer, 0, 0))
    kern = functools.partial(_lru_kernel, chunk=chunk, t_valid=t_valid)
    return pl.pallas_call(
        kern,
        grid=(batch, nc),
        in_specs=[
            part(0), part(1), part(2), part(3),
            pl.BlockSpec((None, CONV_W - 1, W), lambda b, c: (b, 0, 0)),
            pl.BlockSpec((None, 1, W), lambda b, c: (b, 0, 0)),
            mat(CONV_W, W),
            vec(W),
            mat(W, W), vec(W),
            mat(W, W), vec(W),
            vec(W),
        ],
        out_specs=[
            pl.BlockSpec((chunk, W), lambda b, c: (b * nc + c, 0)),
            pl.BlockSpec((None, 1, W), lambda b, c: (b, 0, 0)),
        ],
        out_shape=[
            jax.ShapeDtypeStruct((batch * t_len, W), _mix_dtype(chunk)),
            jax.ShapeDtypeStruct((batch, 1, W), F32),
        ],
        scratch_shapes=[pltpu.VMEM((SUBLANES, W), F32), pltpu.VMEM((1, W), F32)],
        compiler_params=_cparams(("parallel", "arbitrary")),
        name="rglru",
    )(pxq, pxq, pxq, pxq, conv_prev, h0, prm["conv_w"], prm["conv_b"], prm["wa_bd"], prm["ba"], prm["wx_bd"],
      prm["bx"], prm["lam"])


def _largest_divisor(n, cap):
    return max(u for u in range(1, cap + 1) if n % u == 0)


def _attn_prompt_kernel(q_ref, kil_ref, vil_ref, bias_ref, o_ref, k_ref, v_ref, og_scr, lse_scr, *, seq):
    B = ATTN_SPAN
    scale = ATTN_HEAD ** -0.5
    h = pl.program_id(1)
    R = ATTN_MERGE_ROWS

    def gather_head(src_ref, dst_ref, i):
        dst_ref[pl.ds(i * R, R), :] = src_ref[pl.ds(h + i * (R * ATTN_HEADS), R, stride=ATTN_HEADS), :]

    for i in range(seq // R):
        gather_head(kil_ref, k_ref, i)

    blocks = [(gi, dil, cls + blk * (B * dil), blk == 0)
              for gi, (win, dil) in sorted(enumerate(DIL_PAIRS), key=lambda e: -e[1][1])
              for cls in range(dil) for blk in range(seq // (dil * B))]
    batches = [blocks[i:i + ATTN_UNROLL] for i in range(0, len(blocks), ATTN_UNROLL)]
    assert blocks[-1][1] == 1

    def rows(dil, start, first):
        start, n = (start, B) if first else (start - B * dil, 2 * B)
        return pl.ds(start, n) if dil == 1 else pl.ds(start, n, stride=dil)

    def scores(batch):
        return [_dot_nt(q_ref[rows(dil, qs, True), :], k_ref[rows(dil, qs, first), :]) * scale
                + (bias_ref[gi, :, B:] if first else bias_ref[gi]) for gi, dil, qs, first in batch]

    def softmax(ss):
        out = []
        for s in ss:
            mx = jnp.max(s, axis=-1, keepdims=True)
            pr = jnp.exp(s - mx)
            den = jnp.sum(pr, axis=-1, keepdims=True)
            out.append(((pr / den).astype(BF16), mx + jnp.log(den)))
        return out

    def values_and_store(batch, probs):
        outs = [_dot(pn, v_ref[rows(dil, qs, first), :]) for (gi, dil, qs, first), (pn, _) in zip(batch, probs)]
        for (gi, dil, qs, _), o, (_, lse) in zip(batch, outs, probs):
            og_scr[gi, rows(dil, qs, True), :] = o
            lse_scr[gi, rows(dil, qs, True), :] = jnp.broadcast_to(lse, (B, ATTN_HEAD))

    def merge(i):
        sl = pl.ds(i * R, R)
        ls = [lse_scr[gi, sl, :] for gi in range(len(DIL_PAIRS))]
        mx = functools.reduce(jnp.maximum, ls)
        ws = [jnp.exp(l - mx) for l in ls]
        tot = functools.reduce(lambda a, b: a + b, ws)
        acc = (ws[0] / tot) * og_scr[0, sl, :]
        for gi in range(1, len(DIL_PAIRS)):
            acc = acc + (ws[gi] / tot) * og_scr[gi, sl, :]
        o_ref[sl, :] = acc.astype(o_ref.dtype)

    pending = scores(batches[0])
    for i in range(seq // R):
        gather_head(vil_ref, v_ref, i)
    merged = 0
    contiguous_done = 0
    for bi, batch in enumerate(batches):
        upcoming = scores(batches[bi + 1]) if bi + 1 < len(batches) else None
        values_and_store(batch, softmax(pending))
        pending = upcoming
        contiguous_done += sum(1 for blk in batch if blk[1] == 1)
        while (merged + 1) * R <= contiguous_done * B:
            merge(merged)
            merged += 1
    assert merged == seq // R


def _attn_prompt(pxq, k_all, v_all, bias, layer, batch, seq):
    H = ATTN_HEADS
    G = len(DIL_PAIRS)
    q_col0 = (PXQ_WIDTH - ATTN_WIDTH) // ATTN_HEAD
    kern = functools.partial(_attn_prompt_kernel, seq=seq)
    kv = pl.BlockSpec((seq * H, ATTN_HEAD), lambda b, h: (layer * batch + b, 0))
    return pl.pallas_call(
        kern,
        grid=(batch, H),
        in_specs=[
            pl.BlockSpec((seq, ATTN_HEAD), lambda b, h: (b, q_col0 + h)),
            kv, kv,
            pl.BlockSpec((G, None, ATTN_SPAN, 2 * ATTN_SPAN), lambda b, h: (0, h, 0, 0)),
        ],
        out_specs=pl.BlockSpec((seq, ATTN_HEAD), lambda b, h: (b, h)),
        out_shape=jax.ShapeDtypeStruct((batch * seq, ATTN_WIDTH), BF16),
        scratch_shapes=[pltpu.VMEM((seq, ATTN_HEAD), F32), pltpu.VMEM((seq, ATTN_HEAD), F32),
                        pltpu.VMEM((G, seq, ATTN_HEAD), F32), pltpu.VMEM((G, seq, ATTN_HEAD), F32)],
        compiler_params=_cparams(("parallel", "arbitrary")),
        name="attn_prompt",
    )(pxq, k_all, v_all, bias)


def _attn_sample_kernel(pxq_ref, kn_ref, vn_ref, kc_ref, vc_ref, bias_ref, o_ref, q_scr, kn_scr, vn_scr, *, t_new, win):
    scale = ATTN_HEAD ** -0.5
    H = ATTN_HEADS
    G = len(DIL_PAIRS)
    TP = q_scr.shape[1]
    q_col0 = PXQ_WIDTH - ATTN_WIDTH
    q_scr[...] = jnp.zeros_like(q_scr)
    kn_scr[...] = jnp.zeros_like(kn_scr)
    vn_scr[...] = jnp.zeros_like(vn_scr)
    scores = []
    for h in range(H):
        q_scr[h, 0:t_new, :] = pxq_ref[:, q_col0 + h * ATTN_HEAD:q_col0 + (h + 1) * ATTN_HEAD]
        kn_scr[h, 0:t_new, :] = kn_ref[pl.ds(h, t_new, stride=H), :]
        vn_scr[h, 0:t_new, :] = vn_ref[pl.ds(h, t_new, stride=H), :]
        qb = q_scr[h].astype(BF16)
        kc = kc_ref[pl.ds(h, win, stride=H), :]
        scores.append(jnp.concatenate([_dot_nt(qb, kc), _dot_nt(qb, kn_scr[h])], axis=1) * scale)
    probs = []
    for h, s in enumerate(scores):
        prs, lses = [], []
        for gi in range(G):
            sg = s + bias_ref[gi, h]
            mx = jnp.max(sg, axis=-1, keepdims=True)
            pr = jnp.exp(sg - mx)
            den = jnp.sum(pr, axis=-1, keepdims=True)
            prs.append(pr / den)
            lses.append(mx + jnp.log(den))
        probs.append((jnp.concatenate(prs, axis=0).astype(BF16), lses))
    outs = []
    for h, (pall, _) in enumerate(probs):
        vc = vc_ref[pl.ds(h, win, stride=H), :]
        outs.append(_dot(pall[:, :win], vc) + _dot(pall[:, win:], vn_scr[h]))
    for h, (oall, (_, lses)) in enumerate(zip(outs, probs)):
        mx = functools.reduce(jnp.maximum, lses)
        ws = [jnp.exp(l - mx) for l in lses]
        tot = functools.reduce(lambda a, b: a + b, ws)
        acc = (ws[0] / tot) * oall[0:TP]
        for gi in range(1, G):
            acc = acc + (ws[gi] / tot) * oall[gi * TP:(gi + 1) * TP]
        o_ref[:, h * ATTN_HEAD:(h + 1) * ATTN_HEAD] = acc[0:t_new].astype(o_ref.dtype)


def _attn_sample(pxq, k_new, v_new, k_cache, v_cache, bias, layer, batch, t_new):
    H = ATTN_HEADS
    win = k_cache.shape[2] // H
    tp = bias.shape[2]
    kern = functools.partial(_attn_sample_kernel, t_new=t_new, win=win)
    cache = pl.BlockSpec((None, None, win * H, ATTN_HEAD), lambda b: (layer, b, 0, 0))
    new = pl.BlockSpec((None, t_new * H, ATTN_HEAD), lambda b: (layer * batch + b, 0, 0))
    return pl.pallas_call(
        kern,
        grid=(batch,),
        in_specs=[pl.BlockSpec((None, t_new, pxq.shape[2]), lambda b: (b, 0, 0)), new, new, cache, cache,
                  pl.BlockSpec(bias.shape, lambda b: (0, 0, 0, 0))],
        out_specs=pl.BlockSpec((None, t_new, ATTN_WIDTH), lambda b: (b, 0, 0)),
        out_shape=jax.ShapeDtypeStruct((batch, t_new, ATTN_WIDTH), F32),
        scratch_shapes=[pltpu.VMEM((H, tp, ATTN_HEAD), F32), pltpu.VMEM((H, LANES, ATTN_HEAD), F32),
                        pltpu.VMEM((H, LANES, ATTN_HEAD), F32)],
        compiler_params=_cparams(("parallel",)),
        name="attn_sample",
    )(pxq, k_new, v_new, k_cache, v_cache, bias)


def _bias_kernel(idx_ref, rb_ref, o_ref):
    h = pl.program_id(0)
    idx = idx_ref[...]
    acc = jnp.full(idx.shape, NEG, F32)
    for b in range(N_BUCKETS):
        acc = jnp.where(idx == b, rb_ref[b, h], acc)
    o_ref[...] = acc


def _bias_table(idx, rel_bias):
    g, r, c = idx.shape
    heads = rel_bias.shape[1]
    return pl.pallas_call(
        _bias_kernel,
        grid=(heads,),
        in_specs=[pl.BlockSpec((g, r, c), lambda h: (0, 0, 0)),
                  pl.BlockSpec(memory_space=pltpu.SMEM)],
        out_specs=pl.BlockSpec((g, None, r, c), lambda h: (0, h, 0, 0)),
        out_shape=jax.ShapeDtypeStruct((g, heads, r, c), F32),
        compiler_params=_cparams(("parallel",)),
        name="bias_table",
    )(jnp.asarray(idx, jnp.int32), rel_bias.astype(F32))


def _t5_bucket_static(dist):
    dist = np.asarray(dist, np.int64)
    exact = N_BUCKETS // 2
    d = np.maximum(dist, 1).astype(np.float64)
    large = exact + (np.log(d / exact) / math.log(REL_MAX_DIST / exact) * (N_BUCKETS - exact)).astype(np.int64)
    return np.where(dist < exact, dist, np.minimum(large, N_BUCKETS - 1))


def _prompt_bias_idx():
    B = ATTN_SPAN
    qi = np.arange(B)[:, None]
    kj = np.arange(2 * B)[None, :]
    delta = qi + B - kj
    valid = (delta >= 0) & (delta <= B)
    return np.stack([np.where(valid, _t5_bucket_static(np.clip(delta, 0, B) * dil), -1) for _, dil in DIL_PAIRS])


def _sample_bias_idx(t_new, t_pad, win, total):
    r = np.arange(total)[None, :]
    t = np.arange(t_pad)[:, None]
    dist = win + t - r
    tabs = []
    for _, dil in DIL_PAIRS:
        assert win >= dil * ATTN_SPAN
        valid = (dist >= 0) & (dist % dil == 0) & (dist <= dil * ATTN_SPAN) & (t < t_new)
        tabs.append(np.where(valid, _t5_bucket_static(np.clip(dist, 0, dil * ATTN_SPAN)), -1))
    return np.stack(tabs)


def _block_diag(blocks):
    n, c, d = blocks.shape
    eye = jnp.eye(n, dtype=blocks.dtype)
    return jnp.einsum("ncd,nm->ncmd", blocks, eye).reshape(n * c, n * d)


def _mixer_params(a):
    W = RWKV_WIDTH
    depth = a["rwkv_mu"].shape[0]
    row = lambda t: t.reshape(depth, 1, -1).astype(F32)

    def lora_pad(w, off):
        z = jnp.zeros((depth, LORA_WIDTH, W), F32)
        return z.at[:, off:off + w.shape[1]].set(w).astype(BF16)

    head_of = np.arange(W) // RWKV_HEAD
    ones_bd = jnp.asarray((head_of[:, None] == head_of[None, :]).astype(np.float32), BF16)
    rwkv = dict(
        mu=row(a["rwkv_mu"]), w0=row(a["rwkv_w0"]), a0=row(a["rwkv_a0"]),
        w_up=lora_pad(a["rwkv_w_up"], 0),
        a_up=lora_pad(a["rwkv_a_up"], DECAY_LORA),
        g_up=lora_pad(a["rwkv_g_up"], DECAY_LORA + ICLR_LORA),
        k_k=row(a["rwkv_k_k"]), k_a=row(a["rwkv_k_a"]), r_k=row(a["rwkv_r_k"]),
        lnx_g=row(a["rwkv_lnx_g"]), lnx_b=row(a["rwkv_lnx_b"]),
        ones_bd=ones_bd,
    )
    lru = dict(
        conv_w=a["lru_conv_w"].astype(F32), conv_b=row(a["lru_conv_b"]),
        wa_bd=jax.vmap(_block_diag)(a["lru_wa"]).astype(BF16), ba=row(a["lru_ba"]),
        wx_bd=jax.vmap(_block_diag)(a["lru_wx"]).astype(BF16), bx=row(a["lru_bx"]),
        lam=row(a["lru_lambda"]),
    )
    return dict(rwkv=rwkv, lru=lru)


def _shared_params(a):
    gain = lambda t: t.reshape(t.shape[0], 1, -1).astype(F32)
    return dict(
        g_mix_pre=gain(a["norm_mix_pre"]), g_mix_post=gain(a["norm_mix_post"]),
        g_ffn_pre=gain(a["norm_ffn_pre"]), g_ffn_post=gain(a["norm_ffn_post"]),
    )


def _layer(x, layer, sp, wts, lp, batch, t_len, shift_prev, s0, conv_prev, h0, kv_all, attend):
    emit = wts is None
    src = sp["f32"] if emit else wts
    wl = layer if emit else 0
    copies = {}
    pxq, k_all, v_all, *w_copy = _in_proj(x, sp["g_mix_pre"], src["w_in"], layer, wl, *kv_all, emit_bf16=emit)
    copies["w_in"] = w_copy[0] if emit else None
    pxq3 = pxq.reshape(batch, t_len, pxq.shape[1])
    if t_len % RWKV_CHUNK == 0:
        y_a, s_fin = _rwkv(pxq3, shift_prev, s0, lp["rwkv"], layer, batch, t_len, t_len, RWKV_CHUNK)
        y_a = y_a.reshape(batch * t_len, -1)
        y_b, h_fin = _lru(pxq, conv_prev, h0, lp["lru"], layer, batch, t_len, t_len, LRU_CHUNK)
    else:
        assert t_len <= SUBLANES
        padded = jnp.pad(pxq3, ((0, 0), (0, SUBLANES - t_len), (0, 0)))
        y_a, s_fin = _rwkv(padded, shift_prev, s0, lp["rwkv"], layer, batch, SUBLANES, t_len, SUBLANES)
        y_a = y_a[:, :t_len].reshape(batch * t_len, -1)
        y_b, h_fin = _lru(padded.reshape(batch * SUBLANES, pxq.shape[1]), conv_prev, h0, lp["lru"], layer, batch, SUBLANES,
                          t_len, SUBLANES)
        y_b = y_b.reshape(batch, SUBLANES, -1)[:, :t_len].reshape(batch * t_len, -1)
    y_c = attend(pxq, k_all, v_all)

    x = _out_proj(y_a, y_b, y_c, src["w_out"], sp["g_mix_post"], x, layer, wl, emit_bf16=emit)
    if emit:
        x, copies["w_out"] = x
    x = _ffn(x, sp["g_ffn_pre"], src["w1"], src["w2"], sp["g_ffn_post"], layer, wl, emit_bf16=emit)
    if emit:
        x, copies["w1"], copies["w2"] = x

    lru_x = pxq3[:, :, RWKV_PROJ:RWKV_PROJ + LRU_WIDTH]
    if t_len >= CONV_W - 1:
        conv_new = lru_x[:, t_len - (CONV_W - 1):]
    else:
        conv_new = jnp.concatenate([conv_prev, lru_x], axis=1)[:, -(CONV_W - 1):]
    state = (pxq3[:, -1, :RWKV_PROJ], s_fin, conv_new, h_fin.reshape(batch, LRU_WIDTH))
    return x, state, (k_all, v_all), copies


def kernel(x_prompt, x_sample, state_rwkv_wkv, state_rwkv_shift, state_lru_h, state_lru_conv, cache_attn_k, cache_attn_v, rel_bias, norm_mix_pre, norm_mix_post, norm_ffn_pre, norm_ffn_post, w_in, w_out, rwkv_mu, rwkv_w0, rwkv_w_up, rwkv_a0, rwkv_a_up, rwkv_g_up, rwkv_k_k, rwkv_k_a, rwkv_r_k, rwkv_lnx_g, rwkv_lnx_b, lru_conv_w, lru_conv_b, lru_wa, lru_ba, lru_wx, lru_bx, lru_lambda, ffn_w1, ffn_w2):
    a = dict(norm_mix_pre=norm_mix_pre, norm_mix_post=norm_mix_post, norm_ffn_pre=norm_ffn_pre,
             norm_ffn_post=norm_ffn_post, w_in=w_in, w_out=w_out, rwkv_mu=rwkv_mu, rwkv_w0=rwkv_w0,
             rwkv_w_up=rwkv_w_up, rwkv_a0=rwkv_a0, rwkv_a_up=rwkv_a_up, rwkv_g_up=rwkv_g_up, rwkv_k_k=rwkv_k_k,
             rwkv_k_a=rwkv_k_a, rwkv_r_k=rwkv_r_k, rwkv_lnx_g=rwkv_lnx_g, rwkv_lnx_b=rwkv_lnx_b,
             lru_conv_w=lru_conv_w, lru_conv_b=lru_conv_b, lru_wa=lru_wa, lru_ba=lru_ba, lru_wx=lru_wx,
             lru_bx=lru_bx, lru_lambda=lru_lambda, ffn_w1=ffn_w1, ffn_w2=ffn_w2)
    depth = w_in.shape[0]
    pb, seq, _ = x_prompt.shape
    sb, t_new, _ = x_sample.shape
    win = cache_attn_k.shape[2]
    keep = min(ATTN_WINDOW, seq)
    total = win + LANES
    assert t_new <= LANES and seq % (DIL_PAIRS[-1][1] * ATTN_SPAN) == 0

    bias_p = _bias_table(_prompt_bias_idx(), rel_bias)
    bias_s = _bias_table(_sample_bias_idx(t_new, SUBLANES, win, total), rel_bias)
    k_cache = cache_attn_k.reshape(depth, sb, win * ATTN_HEADS, ATTN_HEAD)
    v_cache = cache_attn_v.reshape(depth, sb, win * ATTN_HEADS, ATTN_HEAD)

    xp = x_prompt.reshape(pb * seq, D_MODEL)
    xs = x_sample.reshape(sb * t_new, D_MODEL)
    new_p, new_s = [], []
    sp = _shared_params(a)
    kv_p = tuple(jnp.zeros((depth * pb * seq * ATTN_HEADS, ATTN_HEAD), F32) for _ in range(2))
    kv_s = tuple(jnp.zeros((depth * sb * t_new * ATTN_HEADS, ATTN_HEAD), F32) for _ in range(2))
    sp["f32"] = dict(w_in=w_in.astype(F32), w_out=w_out.astype(F32), w1=ffn_w1.astype(F32), w2=ffn_w2.astype(F32))
    lp = _mixer_params(a)
    for l in range(depth):
        attend_s = lambda pxq, k_all, v_all, l=l: _attn_sample(
            pxq.reshape(sb, t_new, pxq.shape[1]), k_all.reshape(depth * sb, t_new * ATTN_HEADS, ATTN_HEAD),
            v_all.reshape(depth * sb, t_new * ATTN_HEADS, ATTN_HEAD), k_cache, v_cache, bias_s, l, sb, t_new,
        ).reshape(sb * t_new, ATTN_WIDTH)
        xs, st_s, kv_s, wts = _layer(xs, l, sp, None, lp, sb, t_new,
                                     state_rwkv_shift[l].reshape(sb, 1, RWKV_PROJ), state_rwkv_wkv[l].astype(F32),
                                     state_lru_conv[l], state_lru_h[l].reshape(sb, 1, LRU_WIDTH), kv_s, attend_s)
        attend_p = lambda pxq, k_all, v_all, l=l: _attn_prompt(pxq, k_all, v_all, bias_p, l, pb, seq)
        xp, st_p, kv_p, _ = _layer(xp, l, sp, wts, lp, pb, seq,
                                   jnp.zeros((pb, 1, RWKV_PROJ), F32),
                                   jnp.zeros((pb, RWKV_HEADS, RWKV_HEAD, RWKV_HEAD), F32),
                                   jnp.zeros((pb, CONV_W - 1, LRU_WIDTH), F32), jnp.zeros((pb, 1, LRU_WIDTH), F32),
                                   kv_p, attend_p)
        new_p.append(st_p)
        new_s.append(st_s)

    stack = lambda sts, i: jnp.stack([s[i] for s in sts])
    k_p, v_p = (t.reshape(depth, pb, seq, ATTN_HEADS, ATTN_HEAD)[:, :, seq - keep:] for t in kv_p)
    k_s, v_s = (t.reshape(depth, sb, t_new, ATTN_HEADS, ATTN_HEAD) for t in kv_s)
    return (xp.reshape(pb, seq, D_MODEL), xs.reshape(sb, t_new, D_MODEL),
            stack(new_p, 1), stack(new_s, 1), stack(new_p, 0), stack(new_s, 0),
            stack(new_p, 3), stack(new_s, 3), stack(new_p, 2), stack(new_s, 2),
            k_p, k_s, v_p, v_s)
```

```python
import functools
import math

import numpy as np
import jax
import jax.numpy as jnp
from jax import lax
from jax.experimental import pallas as pl
from jax.experimental.pallas import tpu as pltpu

F32 = jnp.float32
BF16 = jnp.bfloat16

D_MODEL = 2048
RWKV_WIDTH = 512
RWKV_HEAD = 64
RWKV_HEADS = RWKV_WIDTH // RWKV_HEAD
DECAY_LORA = 64
ICLR_LORA = 64
GATE_LORA = 128
LORA_WIDTH = DECAY_LORA + ICLR_LORA + GATE_LORA
RWKV_PROJ = 3 * RWKV_WIDTH + LORA_WIDTH
GN_EPS = 64e-5
LRU_WIDTH = 512
LRU_BLOCKS = 8
LRU_BLOCK = LRU_WIDTH // LRU_BLOCKS
CONV_W = 4
RG_C = 8.0
ATTN_WIDTH = 1024
ATTN_HEAD = 128
ATTN_HEADS = ATTN_WIDTH // ATTN_HEAD
DIL_PAIRS = ((128, 1), (512, 4), (2048, 16))
ATTN_SPAN = 128
ATTN_WINDOW = 2048
N_BUCKETS = 32
REL_MAX_DIST = ATTN_WINDOW
D_FF = 4 * D_MODEL
RMS_EPS = 1e-6
NEG = -1e30

LANES = 128
SUBLANES = 8
MXU_TILE = 256
VMEM_LIMIT_BYTES = 56 * 1024 * 1024
IN_PROJ_VMEM_LIMIT_BYTES = 60 * 1024 * 1024

ROW_TILE = 1024
IN_COL_TILE = 256
IN_FUSED_TILES = 4
OUT_ROW_TILE = 512
FFN_ROW_TILE = 512
FFN_COL_TILE = 1024
FFN_CAST_COL_TILE = 512
RWKV_CHUNK = 64
RWKV_SEQS_PER_STEP = 4
LRU_CHUNK = 256
ATTN_MERGE_ROWS = 256
ATTN_UNROLL = 4
ATTN_HEADS_PER_STEP = 2


def _cparams(sem, vmem_limit_bytes=VMEM_LIMIT_BYTES):
    return pltpu.CompilerParams(dimension_semantics=sem, vmem_limit_bytes=vmem_limit_bytes)


def _dot(a, b):
    return jnp.dot(a.astype(BF16), b.astype(BF16), preferred_element_type=F32)


def _dot_nt(a, b):
    return lax.dot_general(a.astype(BF16), b.astype(BF16), (((1,), (1,)), ((), ())), preferred_element_type=F32)


def _dot_tn(a, b):
    return lax.dot_general(a.astype(BF16), b.astype(BF16), (((0,), (0,)), ((), ())), preferred_element_type=F32)


def _softplus(z):
    return jnp.maximum(z, 0.0) + jnp.log1p(jnp.exp(-jnp.abs(z)))


def _mix_dtype(rows):
    return BF16 if rows % (2 * SUBLANES) == 0 else F32


def _rms(x, g):
    ms = jnp.mean(x * x, axis=-1, keepdims=True)
    return x * lax.rsqrt(ms + RMS_EPS) * g


PXQ_WIDTH = RWKV_PROJ + 2 * LRU_WIDTH + ATTN_WIDTH
PXQ_TILES = PXQ_WIDTH // IN_COL_TILE
KV_TILES = ATTN_WIDTH // IN_COL_TILE
HEADS_PER_TILE = IN_COL_TILE // ATTN_HEAD


def _in_proj_kernel(x_ref, g_ref, *refs, fuse):
    w_refs = refs[:fuse]
    pxq_ref, k_ref, v_ref = refs[fuse + 2:fuse + 5]
    rest = refs[fuse + 5:]
    h_scr = rest[-1]
    j = pl.program_id(1)
    tm = x_ref.shape[0]
    pxq_steps = pl.cdiv(PXQ_TILES, fuse)
    kv_steps = KV_TILES // fuse

    @pl.when(j == 0)
    def _():
        h_scr[...] = _rms(x_ref[...], g_ref[...]).astype(BF16)

    ws = [w_ref[...].astype(BF16) for w_ref in w_refs]
    if len(rest) > 1:
        rest[0][...] = ws[0]
    w = ws[0] if fuse == 1 else jnp.concatenate(ws, axis=1)
    acc = jnp.dot(h_scr[...], w, preferred_element_type=F32)

    @pl.when(j < pxq_steps)
    def _():
        pxq_ref[...] = acc

    def scatter_heads(o_ref, step):
        for hh in range(fuse * HEADS_PER_TILE):
            head = step * (fuse * HEADS_PER_TILE) + hh
            o_ref[pl.ds(head, tm, stride=ATTN_HEADS), :] = acc[:, hh * ATTN_HEAD:(hh + 1) * ATTN_HEAD]

    @pl.when((j >= pxq_steps) & (j < pxq_steps + kv_steps))
    def _():
        scatter_heads(k_ref, j - pxq_steps)

    @pl.when(j >= pxq_steps + kv_steps)
    def _():
        scatter_heads(v_ref, j - pxq_steps - kv_steps)


def _in_proj(x, g, w, layer, w_layer, k_all, v_all, emit_bf16=False):
    m, d = x.shape
    n = w.shape[2]
    assert n == PXQ_WIDTH + 2 * ATTN_WIDTH
    tm = min(ROW_TILE, m)
    row_tiles = m // tm
    assert not emit_bf16 or row_tiles == 1
    fuse = 1 if emit_bf16 else IN_FUSED_TILES
    assert KV_TILES % fuse == 0
    pxq_steps = pl.cdiv(PXQ_TILES, fuse)
    spare = pxq_steps * fuse - PXQ_TILES
    width = fuse * IN_COL_TILE

    def w_spec(slot):
        def index(i, j):
            tile = jnp.where(j < pxq_steps, jnp.minimum(fuse * j + slot, PXQ_TILES - 1), fuse * j + slot - spare)
            return (w_layer, 0, tile)
        return pl.BlockSpec((None, d, IN_COL_TILE), index)

    kv_spec = pl.BlockSpec((tm * ATTN_HEADS, ATTN_HEAD), lambda i, j: (layer * row_tiles + i, 0))
    out_specs = [
        pl.BlockSpec((tm, width), lambda i, j: (i, jnp.minimum(j, pxq_steps - 1))),
        kv_spec, kv_spec,
    ]
    out_shape = [
        jax.ShapeDtypeStruct((m, pxq_steps * width), F32),
        jax.ShapeDtypeStruct(k_all.shape, F32),
        jax.ShapeDtypeStruct(v_all.shape, F32),
    ]
    if emit_bf16:
        out_specs.append(pl.BlockSpec((None, d, IN_COL_TILE), lambda i, j: (0, 0, j)))
        out_shape.append(jax.ShapeDtypeStruct((1, d, n), BF16))
    return pl.pallas_call(
        functools.partial(_in_proj_kernel, fuse=fuse),
        grid=(row_tiles, pxq_steps + 2 * (KV_TILES // fuse)),
        in_specs=[
            pl.BlockSpec((tm, d), lambda i, j: (i, 0)),
            pl.BlockSpec((None, 1, d), lambda i, j: (layer, 0, 0)),
            *[w_spec(slot) for slot in range(fuse)],
            pl.BlockSpec(memory_space=pl.ANY),
            pl.BlockSpec(memory_space=pl.ANY),
        ],
        out_specs=out_specs,
        out_shape=out_shape,
        input_output_aliases={2 + fuse: 1, 3 + fuse: 2},
        scratch_shapes=[pltpu.VMEM((tm, d), BF16)],
        compiler_params=_cparams(("parallel", "arbitrary"), IN_PROJ_VMEM_LIMIT_BYTES),
        name="in_proj",
    )(x, g, *([w] * fuse), k_all, v_all)


def _out_proj_kernel(ya_ref, yb_ref, yc_ref, w_ref, g_ref, x_ref, o_ref, *w_copy):
    c1, c2 = RWKV_WIDTH, RWKV_WIDTH + LRU_WIDTH
    if w_copy:
        acc = None
        for y_ref, rows in [(ya_ref, slice(0, c1)), (yb_ref, slice(c1, c2)), (yc_ref, slice(c2, None))]:
            w = w_ref[rows, :].astype(BF16)
            w_copy[0][rows, :] = w
            t = jnp.dot(y_ref[...].astype(BF16), w, preferred_element_type=F32)
            acc = t if acc is None else acc + t
    else:
        y = jnp.concatenate([r[...].astype(BF16) for r in (ya_ref, yb_ref, yc_ref)], axis=1)
        acc = jnp.dot(y, w_ref[...].astype(BF16), preferred_element_type=F32)
    o_ref[...] = x_ref[...] + _rms(acc, g_ref[...])


def _out_proj(ya, yb, yc, w, g, x, layer, w_layer, emit_bf16=False):
    m, d = x.shape
    tm = min(OUT_ROW_TILE, m)
    assert not emit_bf16 or m == tm
    out_specs = [pl.BlockSpec((tm, d), lambda i: (i, 0))]
    out_shape = [jax.ShapeDtypeStruct((m, d), F32)]
    if emit_bf16:
        out_specs.append(pl.BlockSpec((None, d, d), lambda i: (0, 0, 0)))
        out_shape.append(jax.ShapeDtypeStruct((1, d, d), BF16))
    res = pl.pallas_call(
        _out_proj_kernel,
        grid=(m // tm,),
        in_specs=[
            pl.BlockSpec((tm, RWKV_WIDTH), lambda i: (i, 0)),
            pl.BlockSpec((tm, LRU_WIDTH), lambda i: (i, 0)),
            pl.BlockSpec((tm, ATTN_WIDTH), lambda i: (i, 0)),
            pl.BlockSpec((None, d, d), lambda i: (w_layer, 0, 0)),
            pl.BlockSpec((None, 1, d), lambda i: (layer, 0, 0)),
            pl.BlockSpec((tm, d), lambda i: (i, 0)),
        ],
        out_specs=out_specs,
        out_shape=out_shape,
        compiler_params=_cparams(("parallel",)),
        name="out_proj",
    )(ya, yb, yc, w, g, x)
    return res if emit_bf16 else res[0]


def _ffn_kernel(x_ref, g1_ref, w1_ref, w2_ref, g2_ref, o_ref, *rest):
    h_scr, acc_scr = rest[-2:]
    j = pl.program_id(1)

    @pl.when(j == 0)
    def _():
        h_scr[...] = _rms(x_ref[...], g1_ref[...]).astype(BF16)
        acc_scr[...] = jnp.zeros_like(acc_scr)

    w1 = w1_ref[...].astype(BF16)
    w2 = w2_ref[...].astype(BF16)
    if len(rest) > 2:
        rest[0][...] = w1
        rest[1][...] = w2
    u = jnp.dot(h_scr[...], w1, preferred_element_type=F32)
    u = jnp.square(jnp.maximum(u, 0.0)).astype(BF16)
    acc_scr[...] += jnp.dot(u, w2, preferred_element_type=F32)

    @pl.when(j == pl.num_programs(1) - 1)
    def _():
        o_ref[...] = x_ref[...] + _rms(acc_scr[...], g2_ref[...])


def _ffn(x, g1, w1, w2, g2, layer, w_layer, emit_bf16=False):
    m, d = x.shape
    f = w1.shape[2]
    tm = min(FFN_ROW_TILE, m)
    tf = FFN_CAST_COL_TILE if emit_bf16 else FFN_COL_TILE
    assert not emit_bf16 or m == tm
    out_specs = [pl.BlockSpec((tm, d), lambda i, j: (i, 0))]
    out_shape = [jax.ShapeDtypeStruct((m, d), F32)]
    if emit_bf16:
        out_specs += [pl.BlockSpec((None, d, tf), lambda i, j: (0, 0, j)),
                      pl.BlockSpec((None, tf, d), lambda i, j: (0, j, 0))]
        out_shape += [jax.ShapeDtypeStruct((1, d, f), BF16), jax.ShapeDtypeStruct((1, f, d), BF16)]
    res = pl.pallas_call(
        _ffn_kernel,
        grid=(m // tm, f // tf),
        in_specs=[
            pl.BlockSpec((tm, d), lambda i, j: (i, 0)),
            pl.BlockSpec((None, 1, d), lambda i, j: (layer, 0, 0)),
            pl.BlockSpec((None, d, tf), lambda i, j: (w_layer, 0, j)),
            pl.BlockSpec((None, tf, d), lambda i, j: (w_layer, j, 0)),
            pl.BlockSpec((None, 1, d), lambda i, j: (layer, 0, 0)),
        ],
        out_specs=out_specs,
        out_shape=out_shape,
        scratch_shapes=[pltpu.VMEM((tm, d), BF16), pltpu.VMEM((tm, d), F32)],
        compiler_params=_cparams(("parallel", "arbitrary")),
        name="ffn",
    )(x, g1, w1, w2, g2)
    return res if emit_bf16 else res[0]


def _rwkv_groups(chunk):
    lanes = RWKV_HEADS * chunk
    return lanes // MXU_TILE if lanes % (2 * MXU_TILE) == 0 else 1


def _rwkv_kernel(p_ref, shift_ref, s0_ref, mu_ref, w0_ref, wup_ref, a0_ref, aup_ref, gup_ref, kk_ref, ka_ref,
                 rk_ref, lg_ref, lb_ref, ones_ref,
                 y_ref, sout_ref, s_scr, prev_scr, *, chunk, t_valid):
    C = chunk
    W = RWKV_WIDTH
    H = RWKV_HEADS
    N = RWKV_HEAD
    nb = p_ref.shape[0]
    c = pl.program_id(1)

    G = _rwkv_groups(C)
    hpg = H // G
    wl = W // G
    gc = hpg * C

    @pl.when(c == 0)
    def _():
        s_scr[...] = jnp.zeros_like(s_scr)
        for h in range(H):
            o = (h % hpg) * N
            s_scr[:, h // hpg, o:o + N, o:o + N] = s0_ref[:, h]
        prev_scr[...] = shift_ref[...]

    inv_n = 1.0 / RWKV_HEAD
    shared = dict(wup=[wup_ref], aup=[aup_ref], gup=[gup_ref],
                  ones=[ones_ref.at[g * wl:(g + 1) * wl, g * wl:(g + 1) * wl] for g in range(G)])

    def hi_lo(x):
        hi = x.astype(BF16).astype(F32)
        return [hi, x - hi]

    row = lax.broadcasted_iota(jnp.int32, (C, 1), 0)

    def cumsum_rows(z):
        dlt = 1
        while dlt < C:
            z = z + jnp.where(row >= dlt, pltpu.roll(z, dlt, axis=0), 0.0)
            dlt *= 2
        return z

    ti = lax.broadcasted_iota(jnp.int32, (C, H * C), 0)
    si = lax.broadcasted_iota(jnp.int32, (C, H * C), 1) % C
    strict = ti > si
    incl = ti >= si
    eye = (ti == si).astype(F32)
    blk_r = lax.broadcasted_iota(jnp.int32, (gc, 1), 0) // C
    mask_ch = blk_r == lax.broadcasted_iota(jnp.int32, (1, wl), 1) // N
    mask_cc = blk_r == lax.broadcasted_iota(jnp.int32, (1, gc), 1) // C
    mask_ss = (lax.broadcasted_iota(jnp.int32, (wl, 1), 0) // N
               == lax.broadcasted_iota(jnp.int32, (1, wl), 1) // N)

    def block_diag(x, mask, width):
        out = []
        for g in range(G):
            tiled = jnp.concatenate([x[:, g * width:(g + 1) * width]] * hpg, axis=0)
            out.append(jnp.where(mask, tiled, 0.0).astype(BF16))
        return out

    def per_head(a_cat, bds):
        a_b = a_cat.astype(BF16)
        return jnp.concatenate([jnp.dot(a_b[:, g * gc:(g + 1) * gc], bds[g], preferred_element_type=F32)
                                for g in range(G)], axis=1)

    def one_sequence(bi):
        p = p_ref[bi]
        shifted = jnp.where(row == 0, prev_scr[bi], pltpu.roll(p, 1, axis=0))
        prev_scr[bi] = p_ref[bi, C - 1:C, :]
        m = p + (shifted - p) * mu_ref[...]
        r = m[:, 0:W]
        k = m[:, W:2 * W]
        v = m[:, 2 * W:3 * W]
        x = m[:, 3 * W:]
        lw, la, gate = yield [("wup", jnp.tanh(x)), ("aup", x), ("gup", jax.nn.sigmoid(x))]
        w = w0_ref[...] + lw
        a = jax.nn.sigmoid(a0_ref[...] + la)
        softplus_neg_w = jnp.maximum(-w, 0.0) + jnp.log(1.0 + jnp.exp(-jnp.abs(w)))
        loga = -jnp.exp(-softplus_neg_w - 0.5)
        kk = k * kk_ref[...]
        k2 = k * (1.0 + (a - 1.0) * ka_ref[...])
        (ss,) = yield [("ones", jnp.concatenate(hi_lo(kk * kk) + hi_lo(r * k2 * rk_ref[...]), axis=0))]
        kk = kk / jnp.maximum(jnp.sqrt(ss[0:C] + ss[C:2 * C]), 1e-12)
        bonus = (ss[2 * C:3 * C] + ss[3 * C:]) * v
        if t_valid < C:
            live = row < t_valid
            loga = jnp.where(live, loga, 0.0)
            kk = jnp.where(live, kk, 0.0)
            k2 = jnp.where(live, k2, 0.0)
        cl = cumsum_rows(loga)
        cl_last = cl[C - 1:C, :]
        e_neg = jnp.exp(-cl)
        e_rem = jnp.exp(cl_last - cl)
        kka = kk * a
        al = -kk * jnp.exp(cl - loga)
        rt = r * jnp.exp(cl)
        be_bd = block_diag(kka * e_neg, mask_ch, wl)
        kt_bd = block_diag(k2 * e_neg, mask_ch, wl)
        bh = kka * e_rem
        kh = k2 * e_rem
        lhs = jnp.concatenate([al, rt], axis=0).astype(BF16)
        lhs_g = [lhs[:, g * wl:(g + 1) * wl] for g in range(G)]
        g_b = jnp.concatenate([_dot_nt(lhs_g[g], be_bd[g]) for g in range(G)], axis=1)
        g_k = jnp.concatenate([_dot_nt(lhs_g[g], kt_bd[g]) for g in range(G)], axis=1)
        n_cat = jnp.where(strict, g_b[0:C], 0.0)
        a_ak = jnp.where(strict, g_k[0:C], 0.0)
        a_rb = jnp.where(incl, g_b[C:], 0.0)
        a_rk = jnp.where(incl, g_k[C:], 0.0)
        s_prev = [s_scr[bi, g] for g in range(G)]
        proj = jnp.concatenate([_dot_nt(lhs_g[g], s_prev[g]) for g in range(G)], axis=1)
        doublings = max(int(math.log2(C)) - 1, 0)
        t_cat = eye + n_cat
        pw = n_cat
        if doublings:
            pw = per_head(n_cat, block_diag(n_cat, mask_cc, gc))
        yield None
        for it in range(doublings):
            pw_bd = block_diag(pw, mask_cc, gc)
            if it < doublings - 1:
                both = per_head(jnp.concatenate([t_cat, pw], axis=0), pw_bd)
                t_cat = t_cat + both[0:C]
                pw = both[C:]
            else:
                t_cat = t_cat + per_head(t_cat, pw_bd)
            yield None
        v_bd = block_diag(v, mask_ch, wl)
        rhs = proj[0:C] + per_head(a_ak, v_bd)
        yield None
        u = per_head(t_cat, block_diag(rhs, mask_ch, wl))
        yield None
        y = proj[C:] + per_head(a_rb, block_diag(u, mask_ch, wl)) + per_head(a_rk, v_bd)
        pad = LANES - 2 * C
        uv = jnp.concatenate([u, v] + ([jnp.zeros((pad, W), F32)] if pad > 0 else []), axis=0).astype(BF16)
        bk = jnp.concatenate([bh, kh] + ([jnp.zeros((pad, W), F32)] if pad > 0 else []), axis=0).astype(BF16)
        decay = jnp.exp(cl_last)
        for g in range(G):
            lanes = slice(g * wl, (g + 1) * wl)
            upd = _dot_tn(uv[:, lanes], bk[:, lanes])
            s_scr[bi, g] = s_prev[g] * decay[:, lanes] + jnp.where(mask_ss, upd, 0.0)

        @pl.when(c == pl.num_programs(1) - 1)
        def _():
            for h in range(H):
                o = (h % hpg) * N
                sout_ref[bi, h] = s_scr[bi, h // hpg, o:o + N, o:o + N]

        (sy,) = yield [("ones", jnp.concatenate(hi_lo(y), axis=0))]
        d = y - (sy[0:C] + sy[C:]) * inv_n
        (sd,) = yield [("ones", jnp.concatenate(hi_lo(d * d), axis=0))]
        var = (sd[0:C] + sd[C:]) * inv_n
        yn = d * lax.rsqrt(var + GN_EPS) * lg_ref[...] + lb_ref[...]
        y_ref[bi] = ((yn + bonus) * gate).astype(y_ref.dtype)

    seqs = [one_sequence(bi) for bi in range(nb)]
    replies = [None] * nb
    while True:
        asks, finished = [], 0
        for seq, reply in zip(seqs, replies):
            try:
                asks.append(seq.send(reply))
            except StopIteration:
                finished += 1
        if finished:
            assert finished == nb
            break
        if asks[0] is None:
            replies = [None] * nb
            continue
        replies = [[] for _ in range(nb)]
        for qi, (name, _) in enumerate(asks[0]):
            lhs = jnp.concatenate([ask[qi][1] for ask in asks], axis=0).astype(BF16)
            parts = shared[name]
            kw = lhs.shape[1] // len(parts)
            z = jnp.concatenate([jnp.dot(lhs[:, g * kw:(g + 1) * kw], part[...], preferred_element_type=F32)
                                 for g, part in enumerate(parts)], axis=1)
            rows = z.shape[0] // nb
            for si in range(nb):
                replies[si].append(z[si * rows:(si + 1) * rows])


def _rwkv(p, shift_prev, s0, prm, layer, batch, t_len, t_valid, chunk):
    W = RWKV_WIDTH
    nc = t_len // chunk
    nb = _largest_divisor(batch, RWKV_SEQS_PER_STEP)
    vec = lambda n: pl.BlockSpec((None, 1, n), lambda b, c: (layer, 0, 0))
    mat = lambda r, n: pl.BlockSpec((None, r, n), lambda b, c: (layer, 0, 0))
    state = pl.BlockSpec((nb, RWKV_HEADS, RWKV_HEAD, RWKV_HEAD), lambda b, c: (b, 0, 0, 0))
    groups = _rwkv_groups(chunk)
    kern = functools.partial(_rwkv_kernel, chunk=chunk, t_valid=t_valid)
    return pl.pallas_call(
        kern,
        grid=(batch // nb, nc),
        in_specs=[
            pl.BlockSpec((nb, chunk, RWKV_PROJ), lambda b, c: (b, c, 0)),
            pl.BlockSpec((nb, 1, RWKV_PROJ), lambda b, c: (b, 0, 0)),
            state,
            vec(RWKV_PROJ), vec(W), mat(LORA_WIDTH, W), vec(W), mat(LORA_WIDTH, W), mat(LORA_WIDTH, W),
            vec(W), vec(W), vec(W), vec(W), vec(W), pl.BlockSpec((W, W), lambda b, c: (0, 0)),
        ],
        out_specs=[pl.BlockSpec((nb, chunk, W), lambda b, c: (b, c, 0)), state],
        out_shape=[
            jax.ShapeDtypeStruct((batch, t_len, W), _mix_dtype(chunk)),
            jax.ShapeDtypeStruct((batch, RWKV_HEADS, RWKV_HEAD, RWKV_HEAD), F32),
        ],
        scratch_shapes=[pltpu.VMEM((nb, groups, W // groups, W // groups), F32),
                        pltpu.VMEM((nb, 1, RWKV_PROJ), F32)],
        compiler_params=_cparams(("parallel", "arbitrary")),
        name="rwkv7",
    )(p, shift_prev, s0, prm["mu"], prm["w0"], prm["w_up"], prm["a0"], prm["a_up"], prm["g_up"],
      prm["k_k"], prm["k_a"], prm["r_k"], prm["lnx_g"], prm["lnx_b"], prm["ones_bd"])


def _lru_kernel(x0_ref, x1_ref, g0_ref, g1_ref, conv_ref, h0_ref, cw_ref, cb_ref, wa_ref, ba_ref, wx_ref, bx_ref,
                lam_ref, y_ref, hout_ref, tail_scr, h_scr, *, chunk, t_valid):
    C = chunk
    c = pl.program_id(1)

    @pl.when(c == 0)
    def _():
        tail_scr[...] = jnp.zeros_like(tail_scr)
        tail_scr[SUBLANES - (CONV_W - 1):, :] = conv_ref[...]
        h_scr[...] = h0_ref[...]

    x = jnp.concatenate([x0_ref[...], x1_ref[...]], axis=1)
    gt = jnp.concatenate([g0_ref[...], g1_ref[...]], axis=1)
    row_in_group = lax.broadcasted_iota(jnp.int32, (C, 1), 0) % SUBLANES

    def rotate_in_groups(z, dlt):
        z3 = z.reshape(z.shape[0] // SUBLANES, SUBLANES, z.shape[1])
        return pltpu.roll(z3, dlt, axis=1).reshape(z.shape)

    x_prev = tail_scr[...]
    if C > SUBLANES:
        x_prev = jnp.concatenate([x_prev, x[:C - SUBLANES]], axis=0)
    tail_scr[...] = x[C - SUBLANES:, :]
    xc = cb_ref[...] + x * cw_ref[CONV_W - 1:CONV_W, :]
    for dlt in range(1, CONV_W):
        sh = jnp.where(row_in_group >= dlt, rotate_in_groups(x, dlt), rotate_in_groups(x_prev, dlt))
        xc = xc + sh * cw_ref[CONV_W - 1 - dlt:CONV_W - dlt, :]

    gate_r = jax.nn.sigmoid(_dot(xc, wa_ref[...]) + ba_ref[...])
    gate_i = jax.nn.sigmoid(_dot(xc, wx_ref[...]) + bx_ref[...])
    log_a = -RG_C * gate_r * _softplus(-lam_ref[...])
    a = jnp.exp(log_a)
    th = jnp.tanh(log_a)
    u = jnp.sqrt(-2.0 * th / (1.0 - th)) * (gate_i * xc)

    dlt = 1
    while dlt < SUBLANES:
        keep = row_in_group >= dlt
        a_sh = jnp.where(keep, rotate_in_groups(a, dlt), 1.0)
        u_sh = jnp.where(keep, rotate_in_groups(u, dlt), 0.0)
        u = a * u_sh + u
        a = a * a_sh
        dlt *= 2
    carry = h_scr[...]
    groups = []
    for gi in range(C // SUBLANES):
        rows = slice(gi * SUBLANES, (gi + 1) * SUBLANES)
        groups.append(a[rows] * carry + u[rows])
        carry = groups[-1][SUBLANES - 1:SUBLANES, :]
    h = jnp.concatenate(groups, axis=0)
    last = min(t_valid, C) - 1
    h_scr[...] = h[last:last + 1, :]
    hout_ref[...] = h[last:last + 1, :]
    y_ref[...] = (h * jax.nn.gelu(gt)).astype(y_ref.dtype)


def _lru(pxq, conv_prev, h0, prm, layer, batch, t_len, t_valid, chunk):
    W = LRU_WIDTH
    nc = t_len // chunk
    half = W // 2
    col0 = RWKV_PROJ // half
    assert RWKV_PROJ % half == 0
    part = lambda k: pl.BlockSpec((chunk, half), lambda b, c: (b * nc + c, col0 + k))
    vec = lambda n: pl.BlockSpec((None, 1, n), lambda b, c: (layer, 0, 0))
    mat = lambda r, n: pl.BlockSpec((None, r, n), lambda b, c: (layer, 0, 0))
    kern = functools.partial(_lru_kernel, chunk=chunk, t_valid=t_valid)
    return pl.pallas_call(
        kern,
        grid=(batch, nc),
        in_specs=[
            part(0), part(1), part(2), part(3),
            pl.BlockSpec((None, CONV_W - 1, W), lambda b, c: (b, 0, 0)),
            pl.BlockSpec((None, 1, W), lambda b, c: (b, 0, 0)),
            mat(CONV_W, W),
            vec(W),
            mat(W, W), vec(W),
            mat(W, W), vec(W),
            vec(W),
        ],
        out_specs=[
            pl.BlockSpec((chunk, W), lambda b, c: (b * nc + c, 0)),
            pl.BlockSpec((None, 1, W), lambda b, c: (b, 0, 0)),
        ],
        out_shape=[
            jax.ShapeDtypeStruct((batch * t_len, W), _mix_dtype(chunk)),
            jax.ShapeDtypeStruct((batch, 1, W), F32),
        ],
        scratch_shapes=[pltpu.VMEM((SUBLANES, W), F32), pltpu.VMEM((1, W), F32)],
        compiler_params=_cparams(("parallel", "arbitrary")),
        name="rglru",
    )(pxq, pxq, pxq, pxq, conv_prev, h0, prm["conv_w"], prm["conv_b"], prm["wa_bd"], prm["ba"], prm["wx_bd"],
      prm["bx"], prm["lam"])


def _largest_divisor(n, cap):
    return max(u for u in range(1, cap + 1) if n % u == 0)


def _attn_prompt_kernel(*refs, seq, heads):
    q_refs = refs[:heads]
    kil_ref, vil_ref, bias_ref, o_ref, k_ref, v_ref, og_scr, lse_scr = refs[heads:]
    for hh in range(heads):
        _attn_prompt_head(q_refs[hh], kil_ref, vil_ref, bias_ref.at[:, hh], o_ref.at[:, hh * ATTN_HEAD:(hh + 1) * ATTN_HEAD],
                          k_ref, v_ref, og_scr, lse_scr, pl.program_id(1) * heads + hh, seq)


def _attn_prompt_head(q_ref, kil_ref, vil_ref, bias_ref, o_ref, k_ref, v_ref, og_scr, lse_scr, h, seq):
    B = ATTN_SPAN
    scale = ATTN_HEAD ** -0.5
    R = ATTN_MERGE_ROWS

    def gather_head(src_ref, dst_ref, i):
        dst_ref[pl.ds(i * R, R), :] = src_ref[pl.ds(h + i * (R * ATTN_HEADS), R, stride=ATTN_HEADS), :]

    for i in range(seq // R):
        gather_head(kil_ref, k_ref, i)

    blocks = [(gi, dil, cls + blk * (B * dil), blk == 0)
              for gi, (win, dil) in sorted(enumerate(DIL_PAIRS), key=lambda e: -e[1][1])
              for cls in range(dil) for blk in range(seq // (dil * B))]
    batches = [blocks[i:i + ATTN_UNROLL] for i in range(0, len(blocks), ATTN_UNROLL)]
    assert blocks[-1][1] == 1

    def rows(dil, start, first):
        start, n = (start, B) if first else (start - B * dil, 2 * B)
        return pl.ds(start, n) if dil == 1 else pl.ds(start, n, stride=dil)

    def scores(batch):
        return [_dot_nt(q_ref[rows(dil, qs, True), :], k_ref[rows(dil, qs, first), :]) * scale
                + (bias_ref[gi, :, B:] if first else bias_ref[gi]) for gi, dil, qs, first in batch]

    def softmax(ss):
        out = []
        for s in ss:
            mx = jnp.max(s, axis=-1, keepdims=True)
            pr = jnp.exp(s - mx)
            den = jnp.sum(pr, axis=-1, keepdims=True)
            out.append(((pr / den).astype(BF16), mx + jnp.log(den)))
        return out

    def values_and_store(batch, probs):
        outs = [_dot(pn, v_ref[rows(dil, qs, first), :]) for (gi, dil, qs, first), (pn, _) in zip(batch, probs)]
        for (gi, dil, qs, _), o, (_, lse) in zip(batch, outs, probs):
            og_scr[gi, rows(dil, qs, True), :] = o
            lse_scr[gi, rows(dil, qs, True), :] = jnp.broadcast_to(lse, (B, ATTN_HEAD))

    def merge(i):
        sl = pl.ds(i * R, R)
        ls = [lse_scr[gi, sl, :] for gi in range(len(DIL_PAIRS))]
        mx = functools.reduce(jnp.maximum, ls)
        ws = [jnp.exp(l - mx) for l in ls]
        tot = functools.reduce(lambda a, b: a + b, ws)
        acc = (ws[0] / tot) * og_scr[0, sl, :]
        for gi in range(1, len(DIL_PAIRS)):
            acc = acc + (ws[gi] / tot) * og_scr[gi, sl, :]
        o_ref[sl, :] = acc.astype(o_ref.dtype)

    pending = scores(batches[0])
    for i in range(seq // R):
        gather_head(vil_ref, v_ref, i)
    merged = 0
    contiguous_done = 0
    for bi, batch in enumerate(batches):
        upcoming = scores(batches[bi + 1]) if bi + 1 < len(batches) else None
        values_and_store(batch, softmax(pending))
        pending = upcoming
        contiguous_done += sum(1 for blk in batch if blk[1] == 1)
        while (merged + 1) * R <= contiguous_done * B:
            merge(merged)
            merged += 1
    assert merged == seq // R


def _attn_prompt(pxq, k_all, v_all, bias, layer, batch, seq):
    H = ATTN_HEADS
    G = len(DIL_PAIRS)
    hs = ATTN_HEADS_PER_STEP
    q_col0 = (PXQ_WIDTH - ATTN_WIDTH) // ATTN_HEAD
    kern = functools.partial(_attn_prompt_kernel, seq=seq, heads=hs)
    kv = pl.BlockSpec((seq * H, ATTN_HEAD), lambda b, h: (layer * batch + b, 0))
    q_spec = lambda hh: pl.BlockSpec((seq, ATTN_HEAD), lambda b, h: (b, q_col0 + h * hs + hh))
    return pl.pallas_call(
        kern,
        grid=(batch, H // hs),
        in_specs=[
            *[q_spec(hh) for hh in range(hs)],
            kv, kv,
            pl.BlockSpec((G, hs, ATTN_SPAN, 2 * ATTN_SPAN), lambda b, h: (0, h, 0, 0)),
        ],
        out_specs=pl.BlockSpec((seq, hs * ATTN_HEAD), lambda b, h: (b, h)),
        out_shape=jax.ShapeDtypeStruct((batch * seq, ATTN_WIDTH), BF16),
        scratch_shapes=[pltpu.VMEM((seq, ATTN_HEAD), F32), pltpu.VMEM((seq, ATTN_HEAD), F32),
                        pltpu.VMEM((G, seq, ATTN_HEAD), F32), pltpu.VMEM((G, seq, ATTN_HEAD), F32)],
        compiler_params=_cparams(("parallel", "arbitrary")),
        name="attn_prompt",
    )(*([pxq] * hs), k_all, v_all, bias)


def _attn_sample_kernel(pxq_ref, kn_ref, vn_ref, kc_ref, vc_ref, bias_ref, o_ref, q_scr, kn_scr, vn_scr, *, t_new, win):
    scale = ATTN_HEAD ** -0.5
    H = ATTN_HEADS
    G = len(DIL_PAIRS)
    TP = q_scr.shape[1]
    q_col0 = PXQ_WIDTH - ATTN_WIDTH
    q_scr[...] = jnp.zeros_like(q_scr)
    kn_scr[...] = jnp.zeros_like(kn_scr)
    vn_scr[...] = jnp.zeros_like(vn_scr)
    scores = []
    for h in range(H):
        q_scr[h, 0:t_new, :] = pxq_ref[:, q_col0 + h * ATTN_HEAD:q_col0 + (h + 1) * ATTN_HEAD]
        kn_scr[h, 0:t_new, :] = kn_ref[pl.ds(h, t_new, stride=H), :]
        vn_scr[h, 0:t_new, :] = vn_ref[pl.ds(h, t_new, stride=H), :]
        qb = q_scr[h].astype(BF16)
        kc = kc_ref[pl.ds(h, win, stride=H), :]
        scores.append(jnp.concatenate([_dot_nt(qb, kc), _dot_nt(qb, kn_scr[h])], axis=1) * scale)
    probs = []
    for h, s in enumerate(scores):
        prs, lses = [], []
        for gi in range(G):
            sg = s + bias_ref[gi, h]
            mx = jnp.max(sg, axis=-1, keepdims=True)
            pr = jnp.exp(sg - mx)
            den = jnp.sum(pr, axis=-1, keepdims=True)
            prs.append(pr / den)
            lses.append(mx + jnp.log(den))
        probs.append((jnp.concatenate(prs, axis=0).astype(BF16), lses))
    outs = []
    for h, (pall, _) in enumerate(probs):
        vc = vc_ref[pl.ds(h, win, stride=H), :]
        outs.append(_dot(pall[:, :win], vc) + _dot(pall[:, win:], vn_scr[h]))
    for h, (oall, (_, lses)) in enumerate(zip(outs, probs)):
        mx = functools.reduce(jnp.maximum, lses)
        ws = [jnp.exp(l - mx) for l in lses]
        tot = functools.reduce(lambda a, b: a + b, ws)
        acc = (ws[0] / tot) * oall[0:TP]
        for gi in range(1, G):
            acc = acc + (ws[gi] / tot) * oall[gi * TP:(gi + 1) * TP]
        o_ref[:, h * ATTN_HEAD:(h + 1) * ATTN_HEAD] = acc[0:t_new].astype(o_ref.dtype)


def _attn_sample(pxq, k_new, v_new, k_cache, v_cache, bias, layer, batch, t_new):
    H = ATTN_HEADS
    win = k_cache.shape[2] // H
    tp = bias.shape[2]
    kern = functools.partial(_attn_sample_kernel, t_new=t_new, win=win)
    cache = pl.BlockSpec((None, None, win * H, ATTN_HEAD), lambda b: (layer, b, 0, 0))
    new = pl.BlockSpec((None, t_new * H, ATTN_HEAD), lambda b: (layer * batch + b, 0, 0))
    return pl.pallas_call(
        kern,
        grid=(batch,),
        in_specs=[pl.BlockSpec((None, t_new, pxq.shape[2]), lambda b: (b, 0, 0)), new, new, cache, cache,
                  pl.BlockSpec(bias.shape, lambda b: (0, 0, 0, 0))],
        out_specs=pl.BlockSpec((None, t_new, ATTN_WIDTH), lambda b: (b, 0, 0)),
        out_shape=jax.ShapeDtypeStruct((batch, t_new, ATTN_WIDTH), F32),
        scratch_shapes=[pltpu.VMEM((H, tp, ATTN_HEAD), F32), pltpu.VMEM((H, LANES, ATTN_HEAD), F32),
                        pltpu.VMEM((H, LANES, ATTN_HEAD), F32)],
        compiler_params=_cparams(("parallel",)),
        name="attn_sample",
    )(pxq, k_new, v_new, k_cache, v_cache, bias)


def _bias_kernel(idx_ref, rb_ref, o_ref):
    h = pl.program_id(0)
    idx = idx_ref[...]
    acc = jnp.full(idx.shape, NEG, F32)
    for b in range(N_BUCKETS):
        acc = jnp.where(idx == b, rb_ref[b, h], acc)
    o_ref[...] = acc


def _bias_table(idx, rel_bias):
    g, r, c = idx.shape
    heads = rel_bias.shape[1]
    return pl.pallas_call(
        _bias_kernel,
        grid=(heads,),
        in_specs=[pl.BlockSpec((g, r, c), lambda h: (0, 0, 0)),
                  pl.BlockSpec(memory_space=pltpu.SMEM)],
        out_specs=pl.BlockSpec((g, None, r, c), lambda h: (0, h, 0, 0)),
        out_shape=jax.ShapeDtypeStruct((g, heads, r, c), F32),
        compiler_params=_cparams(("parallel",)),
        name="bias_table",
    )(jnp.asarray(idx, jnp.int32), rel_bias.astype(F32))


def _t5_bucket_static(dist):
    dist = np.asarray(dist, np.int64)
    exact = N_BUCKETS // 2
    d = np.maximum(dist, 1).astype(np.float64)
    large = exact + (np.log(d / exact) / math.log(REL_MAX_DIST / exact) * (N_BUCKETS - exact)).astype(np.int64)
    return np.where(dist < exact, dist, np.minimum(large, N_BUCKETS - 1))


def _prompt_bias_idx():
    B = ATTN_SPAN
    qi = np.arange(B)[:, None]
    kj = np.arange(2 * B)[None, :]
    delta = qi + B - kj
    valid = (delta >= 0) & (delta <= B)
    return np.stack([np.where(valid, _t5_bucket_static(np.clip(delta, 0, B) * dil), -1) for _, dil in DIL_PAIRS])


def _sample_bias_idx(t_new, t_pad, win, total):
    r = np.arange(total)[None, :]
    t = np.arange(t_pad)[:, None]
    dist = win + t - r
    tabs = []
    for _, dil in DIL_PAIRS:
        assert win >= dil * ATTN_SPAN
        valid = (dist >= 0) & (dist % dil == 0) & (dist <= dil * ATTN_SPAN) & (t < t_new)
        tabs.append(np.where(valid, _t5_bucket_static(np.clip(dist, 0, dil * ATTN_SPAN)), -1))
    return np.stack(tabs)


def _block_diag(blocks):
    n, c, d = blocks.shape
    eye = jnp.eye(n, dtype=blocks.dtype)
    return jnp.einsum("ncd,nm->ncmd", blocks, eye).reshape(n * c, n * d)


def _mixer_params(a):
    W = RWKV_WIDTH
    depth = a["rwkv_mu"].shape[0]
    row = lambda t: t.reshape(depth, 1, -1).astype(F32)

    def lora_pad(w, off):
        z = jnp.zeros((depth, LORA_WIDTH, W), F32)
        return z.at[:, off:off + w.shape[1]].set(w).astype(BF16)

    head_of = np.arange(W) // RWKV_HEAD
    ones_bd = jnp.asarray((head_of[:, None] == head_of[None, :]).astype(np.float32), BF16)
    rwkv = dict(
        mu=row(a["rwkv_mu"]), w0=row(a["rwkv_w0"]), a0=row(a["rwkv_a0"]),
        w_up=lora_pad(a["rwkv_w_up"], 0),
        a_up=lora_pad(a["rwkv_a_up"], DECAY_LORA),
        g_up=lora_pad(a["rwkv_g_up"], DECAY_LORA + ICLR_LORA),
        k_k=row(a["rwkv_k_k"]), k_a=row(a["rwkv_k_a"]), r_k=row(a["rwkv_r_k"]),
        lnx_g=row(a["rwkv_lnx_g"]), lnx_b=row(a["rwkv_lnx_b"]),
        ones_bd=ones_bd,
    )
    lru = dict(
        conv_w=a["lru_conv_w"].astype(F32), conv_b=row(a["lru_conv_b"]),
        wa_bd=jax.vmap(_block_diag)(a["lru_wa"]).astype(BF16), ba=row(a["lru_ba"]),
        wx_bd=jax.vmap(_block_diag)(a["lru_wx"]).astype(BF16), bx=row(a["lru_bx"]),
        lam=row(a["lru_lambda"]),
    )
    return dict(rwkv=rwkv, lru=lru)


def _shared_params(a):
    gain = lambda t: t.reshape(t.shape[0], 1, -1).astype(F32)
    return dict(
        g_mix_pre=gain(a["norm_mix_pre"]), g_mix_post=gain(a["norm_mix_post"]),
        g_ffn_pre=gain(a["norm_ffn_pre"]), g_ffn_post=gain(a["norm_ffn_post"]),
    )


def _layer(x, layer, sp, wts, lp, batch, t_len, shift_prev, s0, conv_prev, h0, kv_all, attend):
    emit = wts is None
    src = sp["f32"] if emit else wts
    wl = layer if emit else 0
    copies = {}
    pxq, k_all, v_all, *w_copy = _in_proj(x, sp["g_mix_pre"], src["w_in"], layer, wl, *kv_all, emit_bf16=emit)
    copies["w_in"] = w_copy[0] if emit else None
    pxq3 = pxq.reshape(batch, t_len, pxq.shape[1])
    if t_len % RWKV_CHUNK == 0:
        y_a, s_fin = _rwkv(pxq3, shift_prev, s0, lp["rwkv"], layer, batch, t_len, t_len, RWKV_CHUNK)
        y_a = y_a.reshape(batch * t_len, -1)
        y_b, h_fin = _lru(pxq, conv_prev, h0, lp["lru"], layer, batch, t_len, t_len, LRU_CHUNK)
    else:
        assert t_len <= SUBLANES
        padded = jnp.pad(pxq3, ((0, 0), (0, SUBLANES - t_len), (0, 0)))
        y_a, s_fin = _rwkv(padded, shift_prev, s0, lp["rwkv"], layer, batch, SUBLANES, t_len, SUBLANES)
        y_a = y_a[:, :t_len].reshape(batch * t_len, -1)
        y_b, h_fin = _lru(padded.reshape(batch * SUBLANES, pxq.shape[1]), conv_prev, h0, lp["lru"], layer, batch, SUBLANES,
                          t_len, SUBLANES)
        y_b = y_b.reshape(batch, SUBLANES, -1)[:, :t_len].reshape(batch * t_len, -1)
    y_c = attend(pxq, k_all, v_all)

    x = _out_proj(y_a, y_b, y_c, src["w_out"], sp["g_mix_post"], x, layer, wl, emit_bf16=emit)
    if emit:
        x, copies["w_out"] = x
    x = _ffn(x, sp["g_ffn_pre"], src["w1"], src["w2"], sp["g_ffn_post"], layer, wl, emit_bf16=emit)
    if emit:
        x, copies["w1"], copies["w2"] = x

    lru_x = pxq3[:, :, RWKV_PROJ:RWKV_PROJ + LRU_WIDTH]
    if t_len >= CONV_W - 1:
        conv_new = lru_x[:, t_len - (CONV_W - 1):]
    else:
        conv_new = jnp.concatenate([conv_prev, lru_x], axis=1)[:, -(CONV_W - 1):]
    state = (pxq3[:, -1, :RWKV_PROJ], s_fin, conv_new, h_fin.reshape(batch, LRU_WIDTH))
    return x, state, (k_all, v_all), copies


def kernel(x_prompt, x_sample, state_rwkv_wkv, state_rwkv_shift, state_lru_h, state_lru_conv, cache_attn_k, cache_attn_v, rel_bias, norm_mix_pre, norm_mix_post, norm_ffn_pre, norm_ffn_post, w_in, w_out, rwkv_mu, rwkv_w0, rwkv_w_up, rwkv_a0, rwkv_a_up, rwkv_g_up, rwkv_k_k, rwkv_k_a, rwkv_r_k, rwkv_lnx_g, rwkv_lnx_b, lru_conv_w, lru_conv_b, lru_wa, lru_ba, lru_wx, lru_bx, lru_lambda, ffn_w1, ffn_w2):
    a = dict(norm_mix_pre=norm_mix_pre, norm_mix_post=norm_mix_post, norm_ffn_pre=norm_ffn_pre,
             norm_ffn_post=norm_ffn_post, w_in=w_in, w_out=w_out, rwkv_mu=rwkv_mu, rwkv_w0=rwkv_w0,
             rwkv_w_up=rwkv_w_up, rwkv_a0=rwkv_a0, rwkv_a_up=rwkv_a_up, rwkv_g_up=rwkv_g_up, rwkv_k_k=rwkv_k_k,
             rwkv_k_a=rwkv_k_a, rwkv_r_k=rwkv_r_k, rwkv_lnx_g=rwkv_lnx_g, rwkv_lnx_b=rwkv_lnx_b,
             lru_conv_w=lru_conv_w, lru_conv_b=lru_conv_b, lru_wa=lru_wa, lru_ba=lru_ba, lru_wx=lru_wx,
             lru_bx=lru_bx, lru_lambda=lru_lambda, ffn_w1=ffn_w1, ffn_w2=ffn_w2)
    depth = w_in.shape[0]
    pb, seq, _ = x_prompt.shape
    sb, t_new, _ = x_sample.shape
    win = cache_attn_k.shape[2]
    keep = min(ATTN_WINDOW, seq)
    total = win + LANES
    assert t_new <= LANES and seq % (DIL_PAIRS[-1][1] * ATTN_SPAN) == 0

    bias_p = _bias_table(_prompt_bias_idx(), rel_bias)
    bias_s = _bias_table(_sample_bias_idx(t_new, SUBLANES, win, total), rel_bias)
    k_cache = cache_attn_k.reshape(depth, sb, win * ATTN_HEADS, ATTN_HEAD)
    v_cache = cache_attn_v.reshape(depth, sb, win * ATTN_HEADS, ATTN_HEAD)

    xp = x_prompt.reshape(pb * seq, D_MODEL)
    xs = x_sample.reshape(sb * t_new, D_MODEL)
    new_p, new_s = [], []
    sp = _shared_params(a)
    kv_p = tuple(jnp.zeros((depth * pb * seq * ATTN_HEADS, ATTN_HEAD), F32) for _ in range(2))
    kv_s = tuple(jnp.zeros((depth * sb * t_new * ATTN_HEADS, ATTN_HEAD), F32) for _ in range(2))
    sp["f32"] = dict(w_in=w_in.astype(F32), w_out=w_out.astype(F32), w1=ffn_w1.astype(F32), w2=ffn_w2.astype(F32))
    lp = _mixer_params(a)
    for l in range(depth):
        attend_s = lambda pxq, k_all, v_all, l=l: _attn_sample(
            pxq.reshape(sb, t_new, pxq.shape[1]), k_all.reshape(depth * sb, t_new * ATTN_HEADS, ATTN_HEAD),
            v_all.reshape(depth * sb, t_new * ATTN_HEADS, ATTN_HEAD), k_cache, v_cache, bias_s, l, sb, t_new,
        ).reshape(sb * t_new, ATTN_WIDTH)
        xs, st_s, kv_s, wts = _layer(xs, l, sp, None, lp, sb, t_new,
                                     state_rwkv_shift[l].reshape(sb, 1, RWKV_PROJ), state_rwkv_wkv[l].astype(F32),
                                     state_lru_conv[l], state_lru_h[l].reshape(sb, 1, LRU_WIDTH), kv_s, attend_s)
        attend_p = lambda pxq, k_all, v_all, l=l: _attn_prompt(pxq, k_all, v_all, bias_p, l, pb, seq)
        xp, st_p, kv_p, _ = _layer(xp, l, sp, wts, lp, pb, seq,
                                   jnp.zeros((pb, 1, RWKV_PROJ), F32),
                                   jnp.zeros((pb, RWKV_HEADS, RWKV_HEAD, RWKV_HEAD), F32),
                                   jnp.zeros((pb, CONV_W - 1, LRU_WIDTH), F32), jnp.zeros((pb, 1, LRU_WIDTH), F32),
                                   kv_p, attend_p)
        new_p.append(st_p)
        new_s.append(st_s)

    stack = lambda sts, i: jnp.stack([s[i] for s in sts])
    k_p, v_p = (t.reshape(depth, pb, seq, ATTN_HEADS, ATTN_HEAD)[:, :, seq - keep:] for t in kv_p)
    k_s, v_s = (t.reshape(depth, sb, t_new, ATTN_HEADS, ATTN_HEAD) for t in kv_s)
    return (xp.reshape(pb, seq, D_MODEL), xs.reshape(sb, t_new, D_MODEL),
            stack(new_p, 1), stack(new_s, 1), stack(new_p, 0), stack(new_s, 0),
            stack(new_p, 3), stack(new_s, 3), stack(new_p, 2), stack(new_s, 2),
            k_p, k_s, v_p, v_s)
```

```python
import functools
import math

import numpy as np
import jax
import jax.numpy as jnp
from jax import lax
from jax.experimental import pallas as pl
from jax.experimental.pallas import tpu as pltpu

F32 = jnp.float32
BF16 = jnp.bfloat16

D_MODEL = 2048
RWKV_WIDTH = 512
RWKV_HEAD = 64
RWKV_HEADS = RWKV_WIDTH // RWKV_HEAD
DECAY_LORA = 64
ICLR_LORA = 64
GATE_LORA = 128
LORA_WIDTH = DECAY_LORA + ICLR_LORA + GATE_LORA
RWKV_PROJ = 3 * RWKV_WIDTH + LORA_WIDTH
GN_EPS = 64e-5
LRU_WIDTH = 512
LRU_BLOCKS = 8
LRU_BLOCK = LRU_WIDTH // LRU_BLOCKS
CONV_W = 4
RG_C = 8.0
ATTN_WIDTH = 1024
ATTN_HEAD = 128
ATTN_HEADS = ATTN_WIDTH // ATTN_HEAD
DIL_PAIRS = ((128, 1), (512, 4), (2048, 16))
ATTN_SPAN = 128
ATTN_WINDOW = 2048
N_BUCKETS = 32
REL_MAX_DIST = ATTN_WINDOW
D_FF = 4 * D_MODEL
RMS_EPS = 1e-6
NEG = -1e30

LANES = 128
SUBLANES = 8
MXU_TILE = 256
VMEM_LIMIT_BYTES = 56 * 1024 * 1024
IN_PROJ_VMEM_LIMIT_BYTES = 60 * 1024 * 1024

ROW_TILE = 1024
IN_COL_TILE = 256
IN_FUSED_TILES = 4
OUT_ROW_TILE = 512
FFN_ROW_TILE = 512
FFN_COL_TILE = 1024
FFN_CAST_COL_TILE = 1024
RWKV_CHUNK = 64
RWKV_SEQS_PER_STEP = 4
LRU_CHUNK = 256
ATTN_MERGE_ROWS = 256
ATTN_UNROLL = 4


def _cparams(sem, vmem_limit_bytes=VMEM_LIMIT_BYTES):
    return pltpu.CompilerParams(dimension_semantics=sem, vmem_limit_bytes=vmem_limit_bytes)


def _dot(a, b):
    return jnp.dot(a.astype(BF16), b.astype(BF16), preferred_element_type=F32)


def _dot_nt(a, b):
    return lax.dot_general(a.astype(BF16), b.astype(BF16), (((1,), (1,)), ((), ())), preferred_element_type=F32)


def _dot_tn(a, b):
    return lax.dot_general(a.astype(BF16), b.astype(BF16), (((0,), (0,)), ((), ())), preferred_element_type=F32)


def _softplus(z):
    return jnp.maximum(z, 0.0) + jnp.log1p(jnp.exp(-jnp.abs(z)))


def _mix_dtype(rows):
    return BF16 if rows % (2 * SUBLANES) == 0 else F32


def _rms(x, g):
    ms = jnp.mean(x * x, axis=-1, keepdims=True)
    return x * lax.rsqrt(ms + RMS_EPS) * g


PXQ_WIDTH = RWKV_PROJ + 2 * LRU_WIDTH + ATTN_WIDTH
PXQ_TILES = PXQ_WIDTH // IN_COL_TILE
KV_TILES = ATTN_WIDTH // IN_COL_TILE
HEADS_PER_TILE = IN_COL_TILE // ATTN_HEAD


def _in_proj_kernel(x_ref, g_ref, *refs, fuse):
    w_refs = refs[:fuse]
    pxq_ref, k_ref, v_ref = refs[fuse + 2:fuse + 5]
    rest = refs[fuse + 5:]
    h_scr = rest[-1]
    j = pl.program_id(1)
    tm = x_ref.shape[0]
    pxq_steps = pl.cdiv(PXQ_TILES, fuse)
    kv_steps = KV_TILES // fuse

    @pl.when(j == 0)
    def _():
        h_scr[...] = _rms(x_ref[...], g_ref[...]).astype(BF16)

    ws = [w_ref[...].astype(BF16) for w_ref in w_refs]
    if len(rest) > 1:
        rest[0][...] = ws[0]
    w = ws[0] if fuse == 1 else jnp.concatenate(ws, axis=1)
    acc = jnp.dot(h_scr[...], w, preferred_element_type=F32)

    @pl.when(j < pxq_steps)
    def _():
        pxq_ref[...] = acc

    def scatter_heads(o_ref, step):
        for hh in range(fuse * HEADS_PER_TILE):
            head = step * (fuse * HEADS_PER_TILE) + hh
            o_ref[pl.ds(head, tm, stride=ATTN_HEADS), :] = acc[:, hh * ATTN_HEAD:(hh + 1) * ATTN_HEAD]

    @pl.when((j >= pxq_steps) & (j < pxq_steps + kv_steps))
    def _():
        scatter_heads(k_ref, j - pxq_steps)

    @pl.when(j >= pxq_steps + kv_steps)
    def _():
        scatter_heads(v_ref, j - pxq_steps - kv_steps)


def _in_proj(x, g, w, layer, w_layer, k_all, v_all, emit_bf16=False):
    m, d = x.shape
    n = w.shape[2]
    assert n == PXQ_WIDTH + 2 * ATTN_WIDTH
    tm = min(ROW_TILE, m)
    row_tiles = m // tm
    assert not emit_bf16 or row_tiles == 1
    fuse = 1 if emit_bf16 else IN_FUSED_TILES
    assert KV_TILES % fuse == 0
    pxq_steps = pl.cdiv(PXQ_TILES, fuse)
    spare = pxq_steps * fuse - PXQ_TILES
    width = fuse * IN_COL_TILE

    def w_spec(slot):
        def index(i, j):
            tile = jnp.where(j < pxq_steps, jnp.minimum(fuse * j + slot, PXQ_TILES - 1), fuse * j + slot - spare)
            return (w_layer, 0, tile)
        return pl.BlockSpec((None, d, IN_COL_TILE), index)

    kv_spec = pl.BlockSpec((tm * ATTN_HEADS, ATTN_HEAD), lambda i, j: (layer * row_tiles + i, 0))
    out_specs = [
        pl.BlockSpec((tm, width), lambda i, j: (i, jnp.minimum(j, pxq_steps - 1))),
        kv_spec, kv_spec,
    ]
    out_shape = [
        jax.ShapeDtypeStruct((m, pxq_steps * width), F32),
        jax.ShapeDtypeStruct(k_all.shape, F32),
        jax.ShapeDtypeStruct(v_all.shape, F32),
    ]
    if emit_bf16:
        out_specs.append(pl.BlockSpec((None, d, IN_COL_TILE), lambda i, j: (0, 0, j)))
        out_shape.append(jax.ShapeDtypeStruct((1, d, n), BF16))
    return pl.pallas_call(
        functools.partial(_in_proj_kernel, fuse=fuse),
        grid=(row_tiles, pxq_steps + 2 * (KV_TILES // fuse)),
        in_specs=[
            pl.BlockSpec((tm, d), lambda i, j: (i, 0)),
            pl.BlockSpec((None, 1, d), lambda i, j: (layer, 0, 0)),
            *[w_spec(slot) for slot in range(fuse)],
            pl.BlockSpec(memory_space=pl.ANY),
            pl.BlockSpec(memory_space=pl.ANY),
        ],
        out_specs=out_specs,
        out_shape=out_shape,
        input_output_aliases={2 + fuse: 1, 3 + fuse: 2},
        scratch_shapes=[pltpu.VMEM((tm, d), BF16)],
        compiler_params=_cparams(("parallel", "arbitrary"), IN_PROJ_VMEM_LIMIT_BYTES),
        name="in_proj",
    )(x, g, *([w] * fuse), k_all, v_all)


def _out_proj_kernel(ya_ref, yb_ref, yc_ref, w_ref, g_ref, x_ref, o_ref, *w_copy):
    c1, c2 = RWKV_WIDTH, RWKV_WIDTH + LRU_WIDTH
    if w_copy:
        acc = None
        for y_ref, rows in [(ya_ref, slice(0, c1)), (yb_ref, slice(c1, c2)), (yc_ref, slice(c2, None))]:
            w = w_ref[rows, :].astype(BF16)
            w_copy[0][rows, :] = w
            t = jnp.dot(y_ref[...].astype(BF16), w, preferred_element_type=F32)
            acc = t if acc is None else acc + t
    else:
        y = jnp.concatenate([r[...].astype(BF16) for r in (ya_ref, yb_ref, yc_ref)], axis=1)
        acc = jnp.dot(y, w_ref[...].astype(BF16), preferred_element_type=F32)
    o_ref[...] = x_ref[...] + _rms(acc, g_ref[...])


def _out_proj(ya, yb, yc, w, g, x, layer, w_layer, emit_bf16=False):
    m, d = x.shape
    tm = min(OUT_ROW_TILE, m)
    assert not emit_bf16 or m == tm
    out_specs = [pl.BlockSpec((tm, d), lambda i: (i, 0))]
    out_shape = [jax.ShapeDtypeStruct((m, d), F32)]
    if emit_bf16:
        out_specs.append(pl.BlockSpec((None, d, d), lambda i: (0, 0, 0)))
        out_shape.append(jax.ShapeDtypeStruct((1, d, d), BF16))
    res = pl.pallas_call(
        _out_proj_kernel,
        grid=(m // tm,),
        in_specs=[
            pl.BlockSpec((tm, RWKV_WIDTH), lambda i: (i, 0)),
            pl.BlockSpec((tm, LRU_WIDTH), lambda i: (i, 0)),
            pl.BlockSpec((tm, ATTN_WIDTH), lambda i: (i, 0)),
            pl.BlockSpec((None, d, d), lambda i: (w_layer, 0, 0)),
            pl.BlockSpec((None, 1, d), lambda i: (layer, 0, 0)),
            pl.BlockSpec((tm, d), lambda i: (i, 0)),
        ],
        out_specs=out_specs,
        out_shape=out_shape,
        compiler_params=_cparams(("parallel",)),
        name="out_proj",
    )(ya, yb, yc, w, g, x)
    return res if emit_bf16 else res[0]


def _ffn_kernel(x_ref, g1_ref, w1_ref, w2_ref, g2_ref, o_ref, *rest):
    h_scr, acc_scr = rest[-2:]
    j = pl.program_id(1)

    @pl.when(j == 0)
    def _():
        h_scr[...] = _rms(x_ref[...], g1_ref[...]).astype(BF16)
        acc_scr[...] = jnp.zeros_like(acc_scr)

    w1 = w1_ref[...].astype(BF16)
    w2 = w2_ref[...].astype(BF16)
    if len(rest) > 2:
        rest[0][...] = w1
        rest[1][...] = w2
    u = jnp.dot(h_scr[...], w1, preferred_element_type=F32)
    u = jnp.square(jnp.maximum(u, 0.0)).astype(BF16)
    acc_scr[...] += jnp.dot(u, w2, preferred_element_type=F32)

    @pl.when(j == pl.num_programs(1) - 1)
    def _():
        o_ref[...] = x_ref[...] + _rms(acc_scr[...], g2_ref[...])


def _ffn(x, g1, w1, w2, g2, layer, w_layer, emit_bf16=False):
    m, d = x.shape
    f = w1.shape[2]
    tm = min(FFN_ROW_TILE, m)
    tf = FFN_CAST_COL_TILE if emit_bf16 else FFN_COL_TILE
    assert not emit_bf16 or m == tm
    out_specs = [pl.BlockSpec((tm, d), lambda i, j: (i, 0))]
    out_shape = [jax.ShapeDtypeStruct((m, d), F32)]
    if emit_bf16:
        out_specs += [pl.BlockSpec((None, d, tf), lambda i, j: (0, 0, j)),
                      pl.BlockSpec((None, tf, d), lambda i, j: (0, j, 0))]
        out_shape += [jax.ShapeDtypeStruct((1, d, f), BF16), jax.ShapeDtypeStruct((1, f, d), BF16)]
    res = pl.pallas_call(
        _ffn_kernel,
        grid=(m // tm, f // tf),
        in_specs=[
            pl.BlockSpec((tm, d), lambda i, j: (i, 0)),
            pl.BlockSpec((None, 1, d), lambda i, j: (layer, 0, 0)),
            pl.BlockSpec((None, d, tf), lambda i, j: (w_layer, 0, j)),
            pl.BlockSpec((None, tf, d), lambda i, j: (w_layer, j, 0)),
            pl.BlockSpec((None, 1, d), lambda i, j: (layer, 0, 0)),
        ],
        out_specs=out_specs,
        out_shape=out_shape,
        scratch_shapes=[pltpu.VMEM((tm, d), BF16), pltpu.VMEM((tm, d), F32)],
        compiler_params=_cparams(("parallel", "arbitrary")),
        name="ffn",
    )(x, g1, w1, w2, g2)
    return res if emit_bf16 else res[0]


def _rwkv_groups(chunk):
    lanes = RWKV_HEADS * chunk
    return lanes // MXU_TILE if lanes % (2 * MXU_TILE) == 0 else 1


def _rwkv_kernel(p_ref, shift_ref, s0_ref, mu_ref, w0_ref, wup_ref, a0_ref, aup_ref, gup_ref, kk_ref, ka_ref,
                 rk_ref, lg_ref, lb_ref, ones_ref,
                 y_ref, sout_ref, s_scr, prev_scr, *, chunk, t_valid):
    C = chunk
    W = RWKV_WIDTH
    H = RWKV_HEADS
    N = RWKV_HEAD
    nb = p_ref.shape[0]
    c = pl.program_id(1)

    G = _rwkv_groups(C)
    hpg = H // G
    wl = W // G
    gc = hpg * C

    @pl.when(c == 0)
    def _():
        s_scr[...] = jnp.zeros_like(s_scr)
        for h in range(H):
            o = (h % hpg) * N
            s_scr[:, h // hpg, o:o + N, o:o + N] = s0_ref[:, h]
        prev_scr[...] = shift_ref[...]

    inv_n = 1.0 / RWKV_HEAD
    shared = dict(wup=[wup_ref], aup=[aup_ref], gup=[gup_ref],
                  ones=[ones_ref.at[g * wl:(g + 1) * wl, g * wl:(g + 1) * wl] for g in range(G)])

    def hi_lo(x):
        hi = x.astype(BF16).astype(F32)
        return [hi, x - hi]

    row = lax.broadcasted_iota(jnp.int32, (C, 1), 0)

    def cumsum_rows(z):
        dlt = 1
        while dlt < C:
            z = z + jnp.where(row >= dlt, pltpu.roll(z, dlt, axis=0), 0.0)
            dlt *= 2
        return z

    ti = lax.broadcasted_iota(jnp.int32, (C, H * C), 0)
    si = lax.broadcasted_iota(jnp.int32, (C, H * C), 1) % C
    strict = ti > si
    incl = ti >= si
    eye = (ti == si).astype(F32)
    blk_r = lax.broadcasted_iota(jnp.int32, (gc, 1), 0) // C
    mask_ch = blk_r == lax.broadcasted_iota(jnp.int32, (1, wl), 1) // N
    mask_cc = blk_r == lax.broadcasted_iota(jnp.int32, (1, gc), 1) // C
    mask_ss = (lax.broadcasted_iota(jnp.int32, (wl, 1), 0) // N
               == lax.broadcasted_iota(jnp.int32, (1, wl), 1) // N)

    def block_diag(x, mask, width):
        out = []
        for g in range(G):
            tiled = jnp.concatenate([x[:, g * width:(g + 1) * width]] * hpg, axis=0)
            out.append(jnp.where(mask, tiled, 0.0).astype(BF16))
        return out

    def per_head(a_cat, bds):
        a_b = a_cat.astype(BF16)
        return jnp.concatenate([jnp.dot(a_b[:, g * gc:(g + 1) * gc], bds[g], preferred_element_type=F32)
                                for g in range(G)], axis=1)

    def one_sequence(bi):
        p = p_ref[bi]
        shifted = jnp.where(row == 0, prev_scr[bi], pltpu.roll(p, 1, axis=0))
        prev_scr[bi] = p_ref[bi, C - 1:C, :]
        m = p + (shifted - p) * mu_ref[...]
        r = m[:, 0:W]
        k = m[:, W:2 * W]
        v = m[:, 2 * W:3 * W]
        x = m[:, 3 * W:]
        lw, la, gate = yield [("wup", jnp.tanh(x)), ("aup", x), ("gup", jax.nn.sigmoid(x))]
        w = w0_ref[...] + lw
        a = jax.nn.sigmoid(a0_ref[...] + la)
        softplus_neg_w = jnp.maximum(-w, 0.0) + jnp.log(1.0 + jnp.exp(-jnp.abs(w)))
        loga = -jnp.exp(-softplus_neg_w - 0.5)
        kk = k * kk_ref[...]
        k2 = k * (1.0 + (a - 1.0) * ka_ref[...])
        (ss,) = yield [("ones", jnp.concatenate(hi_lo(kk * kk) + hi_lo(r * k2 * rk_ref[...]), axis=0))]
        kk = kk / jnp.maximum(jnp.sqrt(ss[0:C] + ss[C:2 * C]), 1e-12)
        bonus = (ss[2 * C:3 * C] + ss[3 * C:]) * v
        if t_valid < C:
            live = row < t_valid
            loga = jnp.where(live, loga, 0.0)
            kk = jnp.where(live, kk, 0.0)
            k2 = jnp.where(live, k2, 0.0)
        cl = cumsum_rows(loga)
        cl_last = cl[C - 1:C, :]
        e_neg = jnp.exp(-cl)
        e_rem = jnp.exp(cl_last - cl)
        kka = kk * a
        al = -kk * jnp.exp(cl - loga)
        rt = r * jnp.exp(cl)
        be_bd = block_diag(kka * e_neg, mask_ch, wl)
        kt_bd = block_diag(k2 * e_neg, mask_ch, wl)
        bh = kka * e_rem
        kh = k2 * e_rem
        lhs = jnp.concatenate([al, rt], axis=0).astype(BF16)
        lhs_g = [lhs[:, g * wl:(g + 1) * wl] for g in range(G)]
        g_b = jnp.concatenate([_dot_nt(lhs_g[g], be_bd[g]) for g in range(G)], axis=1)
        g_k = jnp.concatenate([_dot_nt(lhs_g[g], kt_bd[g]) for g in range(G)], axis=1)
        n_cat = jnp.where(strict, g_b[0:C], 0.0)
        a_ak = jnp.where(strict, g_k[0:C], 0.0)
        a_rb = jnp.where(incl, g_b[C:], 0.0)
        a_rk = jnp.where(incl, g_k[C:], 0.0)
        s_prev = [s_scr[bi, g] for g in range(G)]
        proj = jnp.concatenate([_dot_nt(lhs_g[g], s_prev[g]) for g in range(G)], axis=1)
        doublings = max(int(math.log2(C)) - 1, 0)
        t_cat = eye + n_cat
        pw = n_cat
        if doublings:
            pw = per_head(n_cat, block_diag(n_cat, mask_cc, gc))
        yield None
        for it in range(doublings):
            pw_bd = block_diag(pw, mask_cc, gc)
            if it < doublings - 1:
                both = per_head(jnp.concatenate([t_cat, pw], axis=0), pw_bd)
                t_cat = t_cat + both[0:C]
                pw = both[C:]
            else:
                t_cat = t_cat + per_head(t_cat, pw_bd)
            yield None
        v_bd = block_diag(v, mask_ch, wl)
        rhs = proj[0:C] + per_head(a_ak, v_bd)
        yield None
        u = per_head(t_cat, block_diag(rhs, mask_ch, wl))
        yield None
        y = proj[C:] + per_head(a_rb, block_diag(u, mask_ch, wl)) + per_head(a_rk, v_bd)
        pad = LANES - 2 * C
        uv = jnp.concatenate([u, v] + ([jnp.zeros((pad, W), F32)] if pad > 0 else []), axis=0).astype(BF16)
        bk = jnp.concatenate([bh, kh] + ([jnp.zeros((pad, W), F32)] if pad > 0 else []), axis=0).astype(BF16)
        decay = jnp.exp(cl_last)
        for g in range(G):
            lanes = slice(g * wl, (g + 1) * wl)
            upd = _dot_tn(uv[:, lanes], bk[:, lanes])
            s_scr[bi, g] = s_prev[g] * decay[:, lanes] + jnp.where(mask_ss, upd, 0.0)

        @pl.when(c == pl.num_programs(1) - 1)
        def _():
            for h in range(H):
                o = (h % hpg) * N
                sout_ref[bi, h] = s_scr[bi, h // hpg, o:o + N, o:o + N]

        (sy,) = yield [("ones", jnp.concatenate(hi_lo(y), axis=0))]
        d = y - (sy[0:C] + sy[C:]) * inv_n
        (sd,) = yield [("ones", jnp.concatenate(hi_lo(d * d), axis=0))]
        var = (sd[0:C] + sd[C:]) * inv_n
        yn = d * lax.rsqrt(var + GN_EPS) * lg_ref[...] + lb_ref[...]
        y_ref[bi] = ((yn + bonus) * gate).astype(y_ref.dtype)

    seqs = [one_sequence(bi) for bi in range(nb)]
    replies = [None] * nb
    while True:
        asks, finished = [], 0
        for seq, reply in zip(seqs, replies):
            try:
                asks.append(seq.send(reply))
            except StopIteration:
                finished += 1
        if finished:
            assert finished == nb
            break
        if asks[0] is None:
            replies = [None] * nb
            continue
        replies = [[] for _ in range(nb)]
        for qi, (name, _) in enumerate(asks[0]):
            lhs = jnp.concatenate([ask[qi][1] for ask in asks], axis=0).astype(BF16)
            parts = shared[name]
            kw = lhs.shape[1] // len(parts)
            z = jnp.concatenate([jnp.dot(lhs[:, g * kw:(g + 1) * kw], part[...], preferred_element_type=F32)
                                 for g, part in enumerate(parts)], axis=1)
            rows = z.shape[0] // nb
            for si in range(nb):
                replies[si].append(z[si * rows:(si + 1) * rows])


def _rwkv(p, shift_prev, s0, prm, layer, batch, t_len, t_valid, chunk):
    W = RWKV_WIDTH
    nc = t_len // chunk
    nb = _largest_divisor(batch, RWKV_SEQS_PER_STEP)
    vec = lambda n: pl.BlockSpec((None, 1, n), lambda b, c: (layer, 0, 0))
    mat = lambda r, n: pl.BlockSpec((None, r, n), lambda b, c: (layer, 0, 0))
    state = pl.BlockSpec((nb, RWKV_HEADS, RWKV_HEAD, RWKV_HEAD), lambda b, c: (b, 0, 0, 0))
    groups = _rwkv_groups(chunk)
    kern = functools.partial(_rwkv_kernel, chunk=chunk, t_valid=t_valid)
    return pl.pallas_call(
        kern,
        grid=(batch // nb, nc),
        in_specs=[
            pl.BlockSpec((nb, chunk, RWKV_PROJ), lambda b, c: (b, c, 0)),
            pl.BlockSpec((nb, 1, RWKV_PROJ), lambda b, c: (b, 0, 0)),
            state,
            vec(RWKV_PROJ), vec(W), mat(LORA_WIDTH, W), vec(W), mat(LORA_WIDTH, W), mat(LORA_WIDTH, W),
            vec(W), vec(W), vec(W), vec(W), vec(W), pl.BlockSpec((W, W), lambda b, c: (0, 0)),
        ],
        out_specs=[pl.BlockSpec((nb, chunk, W), lambda b, c: (b, c, 0)), state],
        out_shape=[
            jax.ShapeDtypeStruct((batch, t_len, W), _mix_dtype(chunk)),
            jax.ShapeDtypeStruct((batch, RWKV_HEADS, RWKV_HEAD, RWKV_HEAD), F32),
        ],
        scratch_shapes=[pltpu.VMEM((nb, groups, W // groups, W // groups), F32),
                        pltpu.VMEM((nb, 1, RWKV_PROJ), F32)],
        compiler_params=_cparams(("parallel", "arbitrary")),
        name="rwkv7",
    )(p, shift_prev, s0, prm["mu"], prm["w0"], prm["w_up"], prm["a0"], prm["a_up"], prm["g_up"],
      prm["k_k"], prm["k_a"], prm["r_k"], prm["lnx_g"], prm["lnx_b"], prm["ones_bd"])


def _lru_kernel(x0_ref, x1_ref, g0_ref, g1_ref, conv_ref, h0_ref, cw_ref, cb_ref, wa_ref, ba_ref, wx_ref, bx_ref,
                lam_ref, y_ref, hout_ref, tail_scr, h_scr, *, chunk, t_valid):
    C = chunk
    c = pl.program_id(1)

    @pl.when(c == 0)
    def _():
        tail_scr[...] = jnp.zeros_like(tail_scr)
        tail_scr[SUBLANES - (CONV_W - 1):, :] = conv_ref[...]
        h_scr[...] = h0_ref[...]

    x = jnp.concatenate([x0_ref[...], x1_ref[...]], axis=1)
    gt = jnp.concatenate([g0_ref[...], g1_ref[...]], axis=1)
    row_in_group = lax.broadcasted_iota(jnp.int32, (C, 1), 0) % SUBLANES

    def rotate_in_groups(z, dlt):
        z3 = z.reshape(z.shape[0] // SUBLANES, SUBLANES, z.shape[1])
        return pltpu.roll(z3, dlt, axis=1).reshape(z.shape)

    x_prev = tail_scr[...]
    if C > SUBLANES:
        x_prev = jnp.concatenate([x_prev, x[:C - SUBLANES]], axis=0)
    tail_scr[...] = x[C - SUBLANES:, :]
    xc = cb_ref[...] + x * cw_ref[CONV_W - 1:CONV_W, :]
    for dlt in range(1, CONV_W):
        sh = jnp.where(row_in_group >= dlt, rotate_in_groups(x, dlt), rotate_in_groups(x_prev, dlt))
        xc = xc + sh * cw_ref[CONV_W - 1 - dlt:CONV_W - dlt, :]

    gate_r = jax.nn.sigmoid(_dot(xc, wa_ref[...]) + ba_ref[...])
    gate_i = jax.nn.sigmoid(_dot(xc, wx_ref[...]) + bx_ref[...])
    log_a = -RG_C * gate_r * _softplus(-lam_ref[...])
    a = jnp.exp(log_a)
    th = jnp.tanh(log_a)
    u = jnp.sqrt(-2.0 * th / (1.0 - th)) * (gate_i * xc)

    dlt = 1
    while dlt < SUBLANES:
        keep = row_in_group >= dlt
        a_sh = jnp.where(keep, rotate_in_groups(a, dlt), 1.0)
        u_sh = jnp.where(keep, rotate_in_groups(u, dlt), 0.0)
        u = a * u_sh + u
        a = a * a_sh
        dlt *= 2
    carry = h_scr[...]
    groups = []
    for gi in range(C // SUBLANES):
        rows = slice(gi * SUBLANES, (gi + 1) * SUBLANES)
        groups.append(a[rows] * carry + u[rows])
        carry = groups[-1][SUBLANES - 1:SUBLANES, :]
    h = jnp.concatenate(groups, axis=0)
    last = min(t_valid, C) - 1
    h_scr[...] = h[last:last + 1, :]
    hout_ref[...] = h[last:last + 1, :]
    y_ref[...] = (h * jax.nn.gelu(gt)).astype(y_ref.dtype)


def _lru(pxq, conv_prev, h0, prm, layer, batch, t_len, t_valid, chunk):
    W = LRU_WIDTH
    nc = t_len // chunk
    half = W // 2
    col0 = RWKV_PROJ // half
    assert RWKV_PROJ % half == 0
    part = lambda k: pl.BlockSpec((chunk, half), lambda b, c: (b * nc + c, col0 + k))
    vec = lambda n: pl.BlockSpec((None, 1, n), lambda b, c: (layer, 0, 0))
    mat = lambda r, n: pl.BlockSpec((None, r, n), lambda b, c: (layer, 0, 0))
    kern = functools.partial(_lru_kernel, chunk=chunk, t_valid=t_valid)
    return pl.pallas_call(
        kern,
        grid=(batch, nc),
        in_specs=[
            part(0), part(1), part(2), part(3),
            pl.BlockSpec((None, CONV_W - 1, W), lambda b, c: (b, 0, 0)),
            pl.BlockSpec((None, 1, W), lambda b, c: (b, 0, 0)),
            mat(CONV_W, W),
            vec(W),
            mat(W, W), vec(W),
            mat(W, W), vec(W),
            vec(W),
        ],
        out_specs=[
            pl.BlockSpec((chunk, W), lambda b, c: (b * nc + c, 0)),
            pl.BlockSpec((None, 1, W), lambda b, c: (b, 0, 0)),
        ],
        out_shape=[
            jax.ShapeDtypeStruct((batch * t_len, W), _mix_dtype(chunk)),
            jax.ShapeDtypeStruct((batch, 1, W), F32),
        ],
        scratch_shapes=[pltpu.VMEM((SUBLANES, W), F32), pltpu.VMEM((1, W), F32)],
        compiler_params=_cparams(("parallel", "arbitrary")),
        name="rglru",
    )(pxq, pxq, pxq, pxq, conv_prev, h0, prm["conv_w"], prm["conv_b"], prm["wa_bd"], prm["ba"], prm["wx_bd"],
      prm["bx"], prm["lam"])


def _largest_divisor(n, cap):
    return max(u for u in range(1, cap + 1) if n % u == 0)


def _attn_prompt_kernel(q_ref, kil_ref, vil_ref, bias_ref, o_ref, k_ref, v_ref, og_scr, lse_scr, *, seq):
    B = ATTN_SPAN
    scale = ATTN_HEAD ** -0.5
    h = pl.program_id(1)
    R = ATTN_MERGE_ROWS

    def gather_head(src_ref, dst_ref, i):
        dst_ref[pl.ds(i * R, R), :] = src_ref[pl.ds(h + i * (R * ATTN_HEADS), R, stride=ATTN_HEADS), :]

    for i in range(seq // R):
        gather_head(kil_ref, k_ref, i)

    blocks = [(gi, dil, cls + blk * (B * dil), blk == 0)
              for gi, (win, dil) in sorted(enumerate(DIL_PAIRS), key=lambda e: -e[1][1])
              for cls in range(dil) for blk in range(seq // (dil * B))]
    batches = [blocks[i:i + ATTN_UNROLL] for i in range(0, len(blocks), ATTN_UNROLL)]
    assert blocks[-1][1] == 1

    def rows(dil, start, first):
        start, n = (start, B) if first else (start - B * dil, 2 * B)
        return pl.ds(start, n) if dil == 1 else pl.ds(start, n, stride=dil)

    def scores(batch):
        return [_dot_nt(q_ref[rows(dil, qs, True), :], k_ref[rows(dil, qs, first), :]) * scale
                + (bias_ref[gi, :, B:] if first else bias_ref[gi]) for gi, dil, qs, first in batch]

    def softmax(ss):
        out = []
        for s in ss:
            mx = jnp.max(s, axis=-1, keepdims=True)
            pr = jnp.exp(s - mx)
            den = jnp.sum(pr, axis=-1, keepdims=True)
            out.append(((pr / den).astype(BF16), mx + jnp.log(den)))
        return out

    def values_and_store(batch, probs):
        outs = [_dot(pn, v_ref[rows(dil, qs, first), :]) for (gi, dil, qs, first), (pn, _) in zip(batch, probs)]
        for (gi, dil, qs, _), o, (_, lse) in zip(batch, outs, probs):
            og_scr[gi, rows(dil, qs, True), :] = o
            lse_scr[gi, rows(dil, qs, True), :] = jnp.broadcast_to(lse, (B, ATTN_HEAD))

    def merge(i):
        sl = pl.ds(i * R, R)
        ls = [lse_scr[gi, sl, :] for gi in range(len(DIL_PAIRS))]
        mx = functools.reduce(jnp.maximum, ls)
        ws = [jnp.exp(l - mx) for l in ls]
        tot = functools.reduce(lambda a, b: a + b, ws)
        acc = (ws[0] / tot) * og_scr[0, sl, :]
        for gi in range(1, len(DIL_PAIRS)):
            acc = acc + (ws[gi] / tot) * og_scr[gi, sl, :]
        o_ref[sl, :] = acc.astype(o_ref.dtype)

    pending = scores(batches[0])
    for i in range(seq // R):
        gather_head(vil_ref, v_ref, i)
    merged = 0
    contiguous_done = 0
    for bi, batch in enumerate(batches):
        upcoming = scores(batches[bi + 1]) if bi + 1 < len(batches) else None
        values_and_store(batch, softmax(pending))
        pending = upcoming
        contiguous_done += sum(1 for blk in batch if blk[1] == 1)
        while (merged + 1) * R <= contiguous_done * B:
            merge(merged)
            merged += 1
    assert merged == seq // R


def _attn_prompt(pxq, k_all, v_all, bias, layer, batch, seq):
    H = ATTN_HEADS
    G = len(DIL_PAIRS)
    q_col0 = (PXQ_WIDTH - ATTN_WIDTH) // ATTN_HEAD
    kern = functools.partial(_attn_prompt_kernel, seq=seq)
    kv = pl.BlockSpec((seq * H, ATTN_HEAD), lambda b, h: (layer * batch + b, 0))
    return pl.pallas_call(
        kern,
        grid=(batch, H),
        in_specs=[
            pl.BlockSpec((seq, ATTN_HEAD), lambda b, h: (b, q_col0 + h)),
            kv, kv,
            pl.BlockSpec((G, None, ATTN_SPAN, 2 * ATTN_SPAN), lambda b, h: (0, h, 0, 0)),
        ],
        out_specs=pl.BlockSpec((seq, ATTN_HEAD), lambda b, h: (b, h)),
        out_shape=jax.ShapeDtypeStruct((batch * seq, ATTN_WIDTH), BF16),
        scratch_shapes=[pltpu.VMEM((seq, ATTN_HEAD), F32), pltpu.VMEM((seq, ATTN_HEAD), F32),
                        pltpu.VMEM((G, seq, ATTN_HEAD), F32), pltpu.VMEM((G, seq, ATTN_HEAD), F32)],
        compiler_params=_cparams(("parallel", "arbitrary")),
        name="attn_prompt",
    )(pxq, k_all, v_all, bias)


def _attn_sample_kernel(pxq_ref, kn_ref, vn_ref, kc_ref, vc_ref, bias_ref, o_ref, q_scr, kn_scr, vn_scr, *, t_new, win):
    scale = ATTN_HEAD ** -0.5
    H = ATTN_HEADS
    G = len(DIL_PAIRS)
    TP = q_scr.shape[1]
    q_col0 = PXQ_WIDTH - ATTN_WIDTH
    q_scr[...] = jnp.zeros_like(q_scr)
    kn_scr[...] = jnp.zeros_like(kn_scr)
    vn_scr[...] = jnp.zeros_like(vn_scr)
    scores = []
    for h in range(H):
        q_scr[h, 0:t_new, :] = pxq_ref[:, q_col0 + h * ATTN_HEAD:q_col0 + (h + 1) * ATTN_HEAD]
        kn_scr[h, 0:t_new, :] = kn_ref[pl.ds(h, t_new, stride=H), :]
        vn_scr[h, 0:t_new, :] = vn_ref[pl.ds(h, t_new, stride=H), :]
        qb = q_scr[h].astype(BF16)
        kc = kc_ref[pl.ds(h, win, stride=H), :]
        scores.append(jnp.concatenate([_dot_nt(qb, kc), _dot_nt(qb, kn_scr[h])], axis=1) * scale)
    probs = []
    for h, s in enumerate(scores):
        prs, lses = [], []
        for gi in range(G):
            sg = s + bias_ref[gi, h]
            mx = jnp.max(sg, axis=-1, keepdims=True)
            pr = jnp.exp(sg - mx)
            den = jnp.sum(pr, axis=-1, keepdims=True)
            prs.append(pr / den)
            lses.append(mx + jnp.log(den))
        probs.append((jnp.concatenate(prs, axis=0).astype(BF16), lses))
    outs = []
    for h, (pall, _) in enumerate(probs):
        vc = vc_ref[pl.ds(h, win, stride=H), :]
        outs.append(_dot(pall[:, :win], vc) + _dot(pall[:, win:], vn_scr[h]))
    for h, (oall, (_, lses)) in enumerate(zip(outs, probs)):
        mx = functools.reduce(jnp.maximum, lses)
        ws = [jnp.exp(l - mx) for l in lses]
        tot = functools.reduce(lambda a, b: a + b, ws)
        acc = (ws[0] / tot) * oall[0:TP]
        for gi in range(1, G):
            acc = acc + (ws[gi] / tot) * oall[gi * TP:(gi + 1) * TP]
        o_ref[:, h * ATTN_HEAD:(h + 1) * ATTN_HEAD] = acc[0:t_new].astype(o_ref.dtype)


def _attn_sample(pxq, k_new, v_new, k_cache, v_cache, bias, layer, batch, t_new):
    H = ATTN_HEADS
    win = k_cache.shape[2] // H
    tp = bias.shape[2]
    kern = functools.partial(_attn_sample_kernel, t_new=t_new, win=win)
    cache = pl.BlockSpec((None, None, win * H, ATTN_HEAD), lambda b: (layer, b, 0, 0))
    new = pl.BlockSpec((None, t_new * H, ATTN_HEAD), lambda b: (layer * batch + b, 0, 0))
    return pl.pallas_call(
        kern,
        grid=(batch,),
        in_specs=[pl.BlockSpec((None, t_new, pxq.shape[2]), lambda b: (b, 0, 0)), new, new, cache, cache,
                  pl.BlockSpec(bias.shape, lambda b: (0, 0, 0, 0))],
        out_specs=pl.BlockSpec((None, t_new, ATTN_WIDTH), lambda b: (b, 0, 0)),
        out_shape=jax.ShapeDtypeStruct((batch, t_new, ATTN_WIDTH), F32),
        scratch_shapes=[pltpu.VMEM((H, tp, ATTN_HEAD), F32), pltpu.VMEM((H, LANES, ATTN_HEAD), F32),
                        pltpu.VMEM((H, LANES, ATTN_HEAD), F32)],
        compiler_params=_cparams(("parallel",)),
        name="attn_sample",
    )(pxq, k_new, v_new, k_cache, v_cache, bias)


def _bias_kernel(idx_ref, rb_ref, o_ref):
    h = pl.program_id(0)
    idx = idx_ref[...]
    acc = jnp.full(idx.shape, NEG, F32)
    for b in range(N_BUCKETS):
        acc = jnp.where(idx == b, rb_ref[b, h], acc)
    o_ref[...] = acc


def _bias_table(idx, rel_bias):
    g, r, c = idx.shape
    heads = rel_bias.shape[1]
    return pl.pallas_call(
        _bias_kernel,
        grid=(heads,),
        in_specs=[pl.BlockSpec((g, r, c), lambda h: (0, 0, 0)),
                  pl.BlockSpec(memory_space=pltpu.SMEM)],
        out_specs=pl.BlockSpec((g, None, r, c), lambda h: (0, h, 0, 0)),
        out_shape=jax.ShapeDtypeStruct((g, heads, r, c), F32),
        compiler_params=_cparams(("parallel",)),
        name="bias_table",
    )(jnp.asarray(idx, jnp.int32), rel_bias.astype(F32))


def _t5_bucket_static(dist):
    dist = np.asarray(dist, np.int64)
    exact = N_BUCKETS // 2
    d = np.maximum(dist, 1).astype(np.float64)
    large = exact + (np.log(d / exact) / math.log(REL_MAX_DIST / exact) * (N_BUCKETS - exact)).astype(np.int64)
    return np.where(dist < exact, dist, np.minimum(large, N_BUCKETS - 1))


def _prompt_bias_idx():
    B = ATTN_SPAN
    qi = np.arange(B)[:, None]
    kj = np.arange(2 * B)[None, :]
    delta = qi + B - kj
    valid = (delta >= 0) & (delta <= B)
    return np.stack([np.where(valid, _t5_bucket_static(np.clip(delta, 0, B) * dil), -1) for _, dil in DIL_PAIRS])


def _sample_bias_idx(t_new, t_pad, win, total):
    r = np.arange(total)[None, :]
    t = np.arange(t_pad)[:, None]
    dist = win + t - r
    tabs = []
    for _, dil in DIL_PAIRS:
        assert win >= dil * ATTN_SPAN
        valid = (dist >= 0) & (dist % dil == 0) & (dist <= dil * ATTN_SPAN) & (t < t_new)
        tabs.append(np.where(valid, _t5_bucket_static(np.clip(dist, 0, dil * ATTN_SPAN)), -1))
    return np.stack(tabs)


def _block_diag(blocks):
    n, c, d = blocks.shape
    eye = jnp.eye(n, dtype=blocks.dtype)
    return jnp.einsum("ncd,nm->ncmd", blocks, eye).reshape(n * c, n * d)


def _mixer_params(a):
    W = RWKV_WIDTH
    depth = a["rwkv_mu"].shape[0]
    row = lambda t: t.reshape(depth, 1, -1).astype(F32)

    def lora_pad(w, off):
        z = jnp.zeros((depth, LORA_WIDTH, W), F32)
        return z.at[:, off:off + w.shape[1]].set(w).astype(BF16)

    head_of = np.arange(W) // RWKV_HEAD
    ones_bd = jnp.asarray((head_of[:, None] == head_of[None, :]).astype(np.float32), BF16)
    rwkv = dict(
        mu=row(a["rwkv_mu"]), w0=row(a["rwkv_w0"]), a0=row(a["rwkv_a0"]),
        w_up=lora_pad(a["rwkv_w_up"], 0),
        a_up=lora_pad(a["rwkv_a_up"], DECAY_LORA),
        g_up=lora_pad(a["rwkv_g_up"], DECAY_LORA + ICLR_LORA),
        k_k=row(a["rwkv_k_k"]), k_a=row(a["rwkv_k_a"]), r_k=row(a["rwkv_r_k"]),
        lnx_g=row(a["rwkv_lnx_g"]), lnx_b=row(a["rwkv_lnx_b"]),
        ones_bd=ones_bd,
    )
    lru = dict(
        conv_w=a["lru_conv_w"].astype(F32), conv_b=row(a["lru_conv_b"]),
        wa_bd=jax.vmap(_block_diag)(a["lru_wa"]).astype(BF16), ba=row(a["lru_ba"]),
        wx_bd=jax.vmap(_block_diag)(a["lru_wx"]).astype(BF16), bx=row(a["lru_bx"]),
        lam=row(a["lru_lambda"]),
    )
    return dict(rwkv=rwkv, lru=lru)


def _shared_params(a):
    gain = lambda t: t.reshape(t.shape[0], 1, -1).astype(F32)
    return dict(
        g_mix_pre=gain(a["norm_mix_pre"]), g_mix_post=gain(a["norm_mix_post"]),
        g_ffn_pre=gain(a["norm_ffn_pre"]), g_ffn_post=gain(a["norm_ffn_post"]),
    )


def _layer(x, layer, sp, wts, lp, batch, t_len, shift_prev, s0, conv_prev, h0, kv_all, attend):
    emit = wts is None
    src = sp["f32"] if emit else wts
    wl = layer if emit else 0
    copies = {}
    pxq, k_all, v_all, *w_copy = _in_proj(x, sp["g_mix_pre"], src["w_in"], layer, wl, *kv_all, emit_bf16=emit)
    copies["w_in"] = w_copy[0] if emit else None
    pxq3 = pxq.reshape(batch, t_len, pxq.shape[1])
    if t_len % RWKV_CHUNK == 0:
        y_a, s_fin = _rwkv(pxq3, shift_prev, s0, lp["rwkv"], layer, batch, t_len, t_len, RWKV_CHUNK)
        y_a = y_a.reshape(batch * t_len, -1)
        y_b, h_fin = _lru(pxq, conv_prev, h0, lp["lru"], layer, batch, t_len, t_len, LRU_CHUNK)
    else:
        assert t_len <= SUBLANES
        padded = jnp.pad(pxq3, ((0, 0), (0, SUBLANES - t_len), (0, 0)))
        y_a, s_fin = _rwkv(padded, shift_prev, s0, lp["rwkv"], layer, batch, SUBLANES, t_len, SUBLANES)
        y_a = y_a[:, :t_len].reshape(batch * t_len, -1)
        y_b, h_fin = _lru(padded.reshape(batch * SUBLANES, pxq.shape[1]), conv_prev, h0, lp["lru"], layer, batch, SUBLANES,
                          t_len, SUBLANES)
        y_b = y_b.reshape(batch, SUBLANES, -1)[:, :t_len].reshape(batch * t_len, -1)
    y_c = attend(pxq, k_all, v_all)

    x = _out_proj(y_a, y_b, y_c, src["w_out"], sp["g_mix_post"], x, layer, wl, emit_bf16=emit)
    if emit:
        x, copies["w_out"] = x
    x = _ffn(x, sp["g_ffn_pre"], src["w1"], src["w2"], sp["g_ffn_post"], layer, wl, emit_bf16=emit)
    if emit:
        x, copies["w1"], copies["w2"] = x

    lru_x = pxq3[:, :, RWKV_PROJ:RWKV_PROJ + LRU_WIDTH]
    if t_len >= CONV_W - 1:
        conv_new = lru_x[:, t_len - (CONV_W - 1):]
    else:
        conv_new = jnp.concatenate([conv_prev, lru_x], axis=1)[:, -(CONV_W - 1):]
    state = (pxq3[:, -1, :RWKV_PROJ], s_fin, conv_new, h_fin.reshape(batch, LRU_WIDTH))
    return x, state, (k_all, v_all), copies


def kernel(x_prompt, x_sample, state_rwkv_wkv, state_rwkv_shift, state_lru_h, state_lru_conv, cache_attn_k, cache_attn_v, rel_bias, norm_mix_pre, norm_mix_post, norm_ffn_pre, norm_ffn_post, w_in, w_out, rwkv_mu, rwkv_w0, rwkv_w_up, rwkv_a0, rwkv_a_up, rwkv_g_up, rwkv_k_k, rwkv_k_a, rwkv_r_k, rwkv_lnx_g, rwkv_lnx_b, lru_conv_w, lru_conv_b, lru_wa, lru_ba, lru_wx, lru_bx, lru_lambda, ffn_w1, ffn_w2):
    a = dict(norm_mix_pre=norm_mix_pre, norm_mix_post=norm_mix_post, norm_ffn_pre=norm_ffn_pre,
             norm_ffn_post=norm_ffn_post, w_in=w_in, w_out=w_out, rwkv_mu=rwkv_mu, rwkv_w0=rwkv_w0,
             rwkv_w_up=rwkv_w_up, rwkv_a0=rwkv_a0, rwkv_a_up=rwkv_a_up, rwkv_g_up=rwkv_g_up, rwkv_k_k=rwkv_k_k,
             rwkv_k_a=rwkv_k_a, rwkv_r_k=rwkv_r_k, rwkv_lnx_g=rwkv_lnx_g, rwkv_lnx_b=rwkv_lnx_b,
             lru_conv_w=lru_conv_w, lru_conv_b=lru_conv_b, lru_wa=lru_wa, lru_ba=lru_ba, lru_wx=lru_wx,
             lru_bx=lru_bx, lru_lambda=lru_lambda, ffn_w1=ffn_w1, ffn_w2=ffn_w2)
    depth = w_in.shape[0]
    pb, seq, _ = x_prompt.shape
    sb, t_new, _ = x_sample.shape
    win = cache_attn_k.shape[2]
    keep = min(ATTN_WINDOW, seq)
    total = win + LANES
    assert t_new <= LANES and seq % (DIL_PAIRS[-1][1] * ATTN_SPAN) == 0

    bias_p = _bias_table(_prompt_bias_idx(), rel_bias)
    bias_s = _bias_table(_sample_bias_idx(t_new, SUBLANES, win, total), rel_bias)
    k_cache = cache_attn_k.reshape(depth, sb, win * ATTN_HEADS, ATTN_HEAD)
    v_cache = cache_attn_v.reshape(depth, sb, win * ATTN_HEADS, ATTN_HEAD)

    xp = x_prompt.reshape(pb * seq, D_MODEL)
    xs = x_sample.reshape(sb * t_new, D_MODEL)
    new_p, new_s = [], []
    sp = _shared_params(a)
    kv_p = tuple(jnp.zeros((depth * pb * seq * ATTN_HEADS, ATTN_HEAD), F32) for _ in range(2))
    kv_s = tuple(jnp.zeros((depth * sb * t_new * ATTN_HEADS, ATTN_HEAD), F32) for _ in range(2))
    sp["f32"] = dict(w_in=w_in.astype(F32), w_out=w_out.astype(F32), w1=ffn_w1.astype(F32), w2=ffn_w2.astype(F32))
    lp = _mixer_params(a)
    for l in range(depth):
        attend_s = lambda pxq, k_all, v_all, l=l: _attn_sample(
            pxq.reshape(sb, t_new, pxq.shape[1]), k_all.reshape(depth * sb, t_new * ATTN_HEADS, ATTN_HEAD),
            v_all.reshape(depth * sb, t_new * ATTN_HEADS, ATTN_HEAD), k_cache, v_cache, bias_s, l, sb, t_new,
        ).reshape(sb * t_new, ATTN_WIDTH)
        xs, st_s, kv_s, wts = _layer(xs, l, sp, None, lp, sb, t_new,
                                     state_rwkv_shift[l].reshape(sb, 1, RWKV_PROJ), state_rwkv_wkv[l].astype(F32),
                                     state_lru_conv[l], state_lru_h[l].reshape(sb, 1, LRU_WIDTH), kv_s, attend_s)
        attend_p = lambda pxq, k_all, v_all, l=l: _attn_prompt(pxq, k_all, v_all, bias_p, l, pb, seq)
        xp, st_p, kv_p, _ = _layer(xp, l, sp, wts, lp, pb, seq,
                                   jnp.zeros((pb, 1, RWKV_PROJ), F32),
                                   jnp.zeros((pb, RWKV_HEADS, RWKV_HEAD, RWKV_HEAD), F32),
                                   jnp.zeros((pb, CONV_W - 1, LRU_WIDTH), F32), jnp.zeros((pb, 1, LRU_WIDTH), F32),
                                   kv_p, attend_p)
        new_p.append(st_p)
        new_s.append(st_s)

    stack = lambda sts, i: jnp.stack([s[i] for s in sts])
    k_p, v_p = (t.reshape(depth, pb, seq, ATTN_HEADS, ATTN_HEAD)[:, :, seq - keep:] for t in kv_p)
    k_s, v_s = (t.reshape(depth, sb, t_new, ATTN_HEADS, ATTN_HEAD) for t in kv_s)
    return (xp.reshape(pb, seq, D_MODEL), xs.reshape(sb, t_new, D_MODEL),
            stack(new_p, 1), stack(new_s, 1), stack(new_p, 0), stack(new_s, 0),
            stack(new_p, 3), stack(new_s, 3), stack(new_p, 2), stack(new_s, 2),
            k_p, k_s, v_p, v_s)
```

```python
import functools
import math

import numpy as np
import jax
import jax.numpy as jnp
from jax import lax
from jax.experimental import pallas as pl
from jax.experimental.pallas import tpu as pltpu

F32 = jnp.float32
BF16 = jnp.bfloat16

D_MODEL = 2048
RWKV_WIDTH = 512
RWKV_HEAD = 64
RWKV_HEADS = RWKV_WIDTH // RWKV_HEAD
DECAY_LORA = 64
ICLR_LORA = 64
GATE_LORA = 128
LORA_WIDTH = DECAY_LORA + ICLR_LORA + GATE_LORA
RWKV_PROJ = 3 * RWKV_WIDTH + LORA_WIDTH
GN_EPS = 64e-5
LRU_WIDTH = 512
LRU_BLOCKS = 8
LRU_BLOCK = LRU_WIDTH // LRU_BLOCKS
CONV_W = 4
RG_C = 8.0
ATTN_WIDTH = 1024
ATTN_HEAD = 128
ATTN_HEADS = ATTN_WIDTH // ATTN_HEAD
DIL_PAIRS = ((128, 1), (512, 4), (2048, 16))
ATTN_SPAN = 128
ATTN_WINDOW = 2048
N_BUCKETS = 32
REL_MAX_DIST = ATTN_WINDOW
D_FF = 4 * D_MODEL
RMS_EPS = 1e-6
NEG = -1e30

LANES = 128
SUBLANES = 8
MXU_TILE = 256
VMEM_LIMIT_BYTES = 56 * 1024 * 1024
IN_PROJ_VMEM_LIMIT_BYTES = 60 * 1024 * 1024

ROW_TILE = 1024
IN_COL_TILE = 256
IN_FUSED_TILES = 4
OUT_ROW_TILE = 512
FFN_ROW_TILE = 512
FFN_COL_TILE = 1024
FFN_CAST_COL_TILE = 1024
RWKV_CHUNK = 64
RWKV_SEQS_PER_STEP = 4
LRU_CHUNK = 256
ATTN_MERGE_ROWS = 256
NORM_ROW_CHUNK = 128
ATTN_UNROLL = 4


def _cparams(sem, vmem_limit_bytes=VMEM_LIMIT_BYTES):
    return pltpu.CompilerParams(dimension_semantics=sem, vmem_limit_bytes=vmem_limit_bytes)


def _dot(a, b):
    return jnp.dot(a.astype(BF16), b.astype(BF16), preferred_element_type=F32)


def _dot_nt(a, b):
    return lax.dot_general(a.astype(BF16), b.astype(BF16), (((1,), (1,)), ((), ())), preferred_element_type=F32)


def _dot_tn(a, b):
    return lax.dot_general(a.astype(BF16), b.astype(BF16), (((0,), (0,)), ((), ())), preferred_element_type=F32)


def _softplus(z):
    return jnp.maximum(z, 0.0) + jnp.log1p(jnp.exp(-jnp.abs(z)))


def _mix_dtype(rows):
    return BF16 if rows % (2 * SUBLANES) == 0 else F32


def _rms(x, g):
    ms = jnp.mean(x * x, axis=-1, keepdims=True)
    return x * lax.rsqrt(ms + RMS_EPS) * g


def _for_row_chunks(n_rows, body):
    chunk = min(NORM_ROW_CHUNK, n_rows)

    def step(i, carry):
        body(pl.ds(pl.multiple_of(i * chunk, chunk), chunk))
        return carry

    lax.fori_loop(0, n_rows // chunk, step, 0)


PXQ_WIDTH = RWKV_PROJ + 2 * LRU_WIDTH + ATTN_WIDTH
PXQ_TILES = PXQ_WIDTH // IN_COL_TILE
KV_TILES = ATTN_WIDTH // IN_COL_TILE
HEADS_PER_TILE = IN_COL_TILE // ATTN_HEAD


def _in_proj_kernel(x_ref, g_ref, *refs, fuse):
    w_refs = refs[:fuse]
    pxq_ref, k_ref, v_ref = refs[fuse + 2:fuse + 5]
    rest = refs[fuse + 5:]
    h_scr = rest[-1]
    j = pl.program_id(1)
    tm = x_ref.shape[0]
    pxq_steps = pl.cdiv(PXQ_TILES, fuse)
    kv_steps = KV_TILES // fuse

    @pl.when(j == 0)
    def _():
        def norm_rows(rows):
            h_scr[rows, :] = _rms(x_ref[rows, :], g_ref[...]).astype(BF16)

        _for_row_chunks(tm, norm_rows)

    ws = [w_ref[...].astype(BF16) for w_ref in w_refs]
    if len(rest) > 1:
        rest[0][...] = ws[0]
    w = ws[0] if fuse == 1 else jnp.concatenate(ws, axis=1)
    acc = jnp.dot(h_scr[...], w, preferred_element_type=F32)

    @pl.when(j < pxq_steps)
    def _():
        pxq_ref[...] = acc

    def scatter_heads(o_ref, step):
        for hh in range(fuse * HEADS_PER_TILE):
            head = step * (fuse * HEADS_PER_TILE) + hh
            o_ref[pl.ds(head, tm, stride=ATTN_HEADS), :] = acc[:, hh * ATTN_HEAD:(hh + 1) * ATTN_HEAD]

    @pl.when((j >= pxq_steps) & (j < pxq_steps + kv_steps))
    def _():
        scatter_heads(k_ref, j - pxq_steps)

    @pl.when(j >= pxq_steps + kv_steps)
    def _():
        scatter_heads(v_ref, j - pxq_steps - kv_steps)


def _in_proj(x, g, w, layer, w_layer, k_all, v_all, emit_bf16=False):
    m, d = x.shape
    n = w.shape[2]
    assert n == PXQ_WIDTH + 2 * ATTN_WIDTH
    tm = min(ROW_TILE, m)
    row_tiles = m // tm
    assert not emit_bf16 or row_tiles == 1
    fuse = 1 if emit_bf16 else IN_FUSED_TILES
    assert KV_TILES % fuse == 0
    pxq_steps = pl.cdiv(PXQ_TILES, fuse)
    spare = pxq_steps * fuse - PXQ_TILES
    width = fuse * IN_COL_TILE

    def w_spec(slot):
        def index(i, j):
            tile = jnp.where(j < pxq_steps, jnp.minimum(fuse * j + slot, PXQ_TILES - 1), fuse * j + slot - spare)
            return (w_layer, 0, tile)
        return pl.BlockSpec((None, d, IN_COL_TILE), index)

    kv_spec = pl.BlockSpec((tm * ATTN_HEADS, ATTN_HEAD), lambda i, j: (layer * row_tiles + i, 0))
    out_specs = [
        pl.BlockSpec((tm, width), lambda i, j: (i, jnp.minimum(j, pxq_steps - 1))),
        kv_spec, kv_spec,
    ]
    out_shape = [
        jax.ShapeDtypeStruct((m, pxq_steps * width), F32),
        jax.ShapeDtypeStruct(k_all.shape, F32),
        jax.ShapeDtypeStruct(v_all.shape, F32),
    ]
    if emit_bf16:
        out_specs.append(pl.BlockSpec((None, d, IN_COL_TILE), lambda i, j: (0, 0, j)))
        out_shape.append(jax.ShapeDtypeStruct((1, d, n), BF16))
    return pl.pallas_call(
        functools.partial(_in_proj_kernel, fuse=fuse),
        grid=(row_tiles, pxq_steps + 2 * (KV_TILES // fuse)),
        in_specs=[
            pl.BlockSpec((tm, d), lambda i, j: (i, 0)),
            pl.BlockSpec((None, 1, d), lambda i, j: (layer, 0, 0)),
            *[w_spec(slot) for slot in range(fuse)],
            pl.BlockSpec(memory_space=pl.ANY),
            pl.BlockSpec(memory_space=pl.ANY),
        ],
        out_specs=out_specs,
        out_shape=out_shape,
        input_output_aliases={2 + fuse: 1, 3 + fuse: 2},
        scratch_shapes=[pltpu.VMEM((tm, d), BF16)],
        compiler_params=_cparams(("parallel", "arbitrary"), IN_PROJ_VMEM_LIMIT_BYTES),
        name="in_proj",
    )(x, g, *([w] * fuse), k_all, v_all)


def _out_proj_kernel(ya_ref, yb_ref, yc_ref, w_ref, g_ref, x_ref, o_ref, *w_copy):
    c1, c2 = RWKV_WIDTH, RWKV_WIDTH + LRU_WIDTH
    if w_copy:
        acc = None
        for y_ref, rows in [(ya_ref, slice(0, c1)), (yb_ref, slice(c1, c2)), (yc_ref, slice(c2, None))]:
            w = w_ref[rows, :].astype(BF16)
            w_copy[0][rows, :] = w
            t = jnp.dot(y_ref[...].astype(BF16), w, preferred_element_type=F32)
            acc = t if acc is None else acc + t
    else:
        y = jnp.concatenate([r[...].astype(BF16) for r in (ya_ref, yb_ref, yc_ref)], axis=1)
        acc = jnp.dot(y, w_ref[...].astype(BF16), preferred_element_type=F32)
    o_ref[...] = x_ref[...] + _rms(acc, g_ref[...])


def _out_proj(ya, yb, yc, w, g, x, layer, w_layer, emit_bf16=False):
    m, d = x.shape
    tm = min(OUT_ROW_TILE, m)
    assert not emit_bf16 or m == tm
    out_specs = [pl.BlockSpec((tm, d), lambda i: (i, 0))]
    out_shape = [jax.ShapeDtypeStruct((m, d), F32)]
    if emit_bf16:
        out_specs.append(pl.BlockSpec((None, d, d), lambda i: (0, 0, 0)))
        out_shape.append(jax.ShapeDtypeStruct((1, d, d), BF16))
    res = pl.pallas_call(
        _out_proj_kernel,
        grid=(m // tm,),
        in_specs=[
            pl.BlockSpec((tm, RWKV_WIDTH), lambda i: (i, 0)),
            pl.BlockSpec((tm, LRU_WIDTH), lambda i: (i, 0)),
            pl.BlockSpec((tm, ATTN_WIDTH), lambda i: (i, 0)),
            pl.BlockSpec((None, d, d), lambda i: (w_layer, 0, 0)),
            pl.BlockSpec((None, 1, d), lambda i: (layer, 0, 0)),
            pl.BlockSpec((tm, d), lambda i: (i, 0)),
        ],
        out_specs=out_specs,
        out_shape=out_shape,
        compiler_params=_cparams(("parallel",)),
        name="out_proj",
    )(ya, yb, yc, w, g, x)
    return res if emit_bf16 else res[0]


def _ffn_kernel(x_ref, g1_ref, w1_ref, w2_ref, g2_ref, o_ref, *rest):
    h_scr, acc_scr = rest[-2:]
    j = pl.program_id(1)

    @pl.when(j == 0)
    def _():
        def norm_rows(rows):
            h_scr[rows, :] = _rms(x_ref[rows, :], g1_ref[...]).astype(BF16)

        _for_row_chunks(x_ref.shape[0], norm_rows)
        acc_scr[...] = jnp.zeros_like(acc_scr)

    w1 = w1_ref[...].astype(BF16)
    w2 = w2_ref[...].astype(BF16)
    if len(rest) > 2:
        rest[0][...] = w1
        rest[1][...] = w2
    u = jnp.dot(h_scr[...], w1, preferred_element_type=F32)
    u = jnp.square(jnp.maximum(u, 0.0)).astype(BF16)
    acc_scr[...] += jnp.dot(u, w2, preferred_element_type=F32)

    @pl.when(j == pl.num_programs(1) - 1)
    def _():
        def finish_rows(rows):
            o_ref[rows, :] = x_ref[rows, :] + _rms(acc_scr[rows, :], g2_ref[...])

        _for_row_chunks(x_ref.shape[0], finish_rows)


def _ffn(x, g1, w1, w2, g2, layer, w_layer, emit_bf16=False):
    m, d = x.shape
    f = w1.shape[2]
    tm = min(FFN_ROW_TILE, m)
    tf = FFN_CAST_COL_TILE if emit_bf16 else FFN_COL_TILE
    assert not emit_bf16 or m == tm
    out_specs = [pl.BlockSpec((tm, d), lambda i, j: (i, 0))]
    out_shape = [jax.ShapeDtypeStruct((m, d), F32)]
    if emit_bf16:
        out_specs += [pl.BlockSpec((None, d, tf), lambda i, j: (0, 0, j)),
                      pl.BlockSpec((None, tf, d), lambda i, j: (0, j, 0))]
        out_shape += [jax.ShapeDtypeStruct((1, d, f), BF16), jax.ShapeDtypeStruct((1, f, d), BF16)]
    res = pl.pallas_call(
        _ffn_kernel,
        grid=(m // tm, f // tf),
        in_specs=[
            pl.BlockSpec((tm, d), lambda i, j: (i, 0)),
            pl.BlockSpec((None, 1, d), lambda i, j: (layer, 0, 0)),
            pl.BlockSpec((None, d, tf), lambda i, j: (w_layer, 0, j)),
            pl.BlockSpec((None, tf, d), lambda i, j: (w_layer, j, 0)),
            pl.BlockSpec((None, 1, d), lambda i, j: (layer, 0, 0)),
        ],
        out_specs=out_specs,
        out_shape=out_shape,
        scratch_shapes=[pltpu.VMEM((tm, d), BF16), pltpu.VMEM((tm, d), F32)],
        compiler_params=_cparams(("parallel", "arbitrary")),
        name="ffn",
    )(x, g1, w1, w2, g2)
    return res if emit_bf16 else res[0]


def _rwkv_groups(chunk):
    lanes = RWKV_HEADS * chunk
    return lanes // MXU_TILE if lanes % (2 * MXU_TILE) == 0 else 1


def _rwkv_kernel(p_ref, shift_ref, s0_ref, mu_ref, w0_ref, wup_ref, a0_ref, aup_ref, gup_ref, kk_ref, ka_ref,
                 rk_ref, lg_ref, lb_ref, ones_ref,
                 y_ref, sout_ref, s_scr, prev_scr, *, chunk, t_valid):
    C = chunk
    W = RWKV_WIDTH
    H = RWKV_HEADS
    N = RWKV_HEAD
    nb = p_ref.shape[0]
    c = pl.program_id(1)

    G = _rwkv_groups(C)
    hpg = H // G
    wl = W // G
    gc = hpg * C

    @pl.when(c == 0)
    def _():
        s_scr[...] = jnp.zeros_like(s_scr)
        for h in range(H):
            o = (h % hpg) * N
            s_scr[:, h // hpg, o:o + N, o:o + N] = s0_ref[:, h]
        prev_scr[...] = shift_ref[...]

    inv_n = 1.0 / RWKV_HEAD
    shared = dict(wup=[wup_ref], aup=[aup_ref], gup=[gup_ref],
                  ones=[ones_ref.at[g * wl:(g + 1) * wl, g * wl:(g + 1) * wl] for g in range(G)])

    def hi_lo(x):
        hi = x.astype(BF16).astype(F32)
        return [hi, x - hi]

    row = lax.broadcasted_iota(jnp.int32, (C, 1), 0)

    def cumsum_rows(z):
        dlt = 1
        while dlt < C:
            z = z + jnp.where(row >= dlt, pltpu.roll(z, dlt, axis=0), 0.0)
            dlt *= 2
        return z

    ti = lax.broadcasted_iota(jnp.int32, (C, H * C), 0)
    si = lax.broadcasted_iota(jnp.int32, (C, H * C), 1) % C
    strict = ti > si
    incl = ti >= si
    eye = (ti == si).astype(F32)
    blk_r = lax.broadcasted_iota(jnp.int32, (gc, 1), 0) // C
    mask_ch = blk_r == lax.broadcasted_iota(jnp.int32, (1, wl), 1) // N
    mask_cc = blk_r == lax.broadcasted_iota(jnp.int32, (1, gc), 1) // C
    mask_ss = (lax.broadcasted_iota(jnp.int32, (wl, 1), 0) // N
               == lax.broadcasted_iota(jnp.int32, (1, wl), 1) // N)

    def block_diag(x, mask, width):
        out = []
        for g in range(G):
            tiled = jnp.concatenate([x[:, g * width:(g + 1) * width]] * hpg, axis=0)
            out.append(jnp.where(mask, tiled, 0.0).astype(BF16))
        return out

    def per_head(a_cat, bds):
        a_b = a_cat.astype(BF16)
        return jnp.concatenate([jnp.dot(a_b[:, g * gc:(g + 1) * gc], bds[g], preferred_element_type=F32)
                                for g in range(G)], axis=1)

    def one_sequence(bi):
        p = p_ref[bi]
        shifted = jnp.where(row == 0, prev_scr[bi], pltpu.roll(p, 1, axis=0))
        prev_scr[bi] = p_ref[bi, C - 1:C, :]
        m = p + (shifted - p) * mu_ref[...]
        r = m[:, 0:W]
        k = m[:, W:2 * W]
        v = m[:, 2 * W:3 * W]
        x = m[:, 3 * W:]
        lw, la, gate = yield [("wup", jnp.tanh(x)), ("aup", x), ("gup", jax.nn.sigmoid(x))]
        w = w0_ref[...] + lw
        a = jax.nn.sigmoid(a0_ref[...] + la)
        softplus_neg_w = jnp.maximum(-w, 0.0) + jnp.log(1.0 + jnp.exp(-jnp.abs(w)))
        loga = -jnp.exp(-softplus_neg_w - 0.5)
        kk = k * kk_ref[...]
        k2 = k * (1.0 + (a - 1.0) * ka_ref[...])
        (ss,) = yield [("ones", jnp.concatenate(hi_lo(kk * kk) + hi_lo(r * k2 * rk_ref[...]), axis=0))]
        kk = kk / jnp.maximum(jnp.sqrt(ss[0:C] + ss[C:2 * C]), 1e-12)
        bonus = (ss[2 * C:3 * C] + ss[3 * C:]) * v
        if t_valid < C:
            live = row < t_valid
            loga = jnp.where(live, loga, 0.0)
            kk = jnp.where(live, kk, 0.0)
            k2 = jnp.where(live, k2, 0.0)
        cl = cumsum_rows(loga)
        cl_last = cl[C - 1:C, :]
        e_neg = jnp.exp(-cl)
        e_rem = jnp.exp(cl_last - cl)
        kka = kk * a
        al = -kk * jnp.exp(cl - loga)
        rt = r * jnp.exp(cl)
        be_bd = block_diag(kka * e_neg, mask_ch, wl)
        kt_bd = block_diag(k2 * e_neg, mask_ch, wl)
        bh = kka * e_rem
        kh = k2 * e_rem
        lhs = jnp.concatenate([al, rt], axis=0).astype(BF16)
        lhs_g = [lhs[:, g * wl:(g + 1) * wl] for g in range(G)]
        g_b = jnp.concatenate([_dot_nt(lhs_g[g], be_bd[g]) for g in range(G)], axis=1)
        g_k = jnp.concatenate([_dot_nt(lhs_g[g], kt_bd[g]) for g in range(G)], axis=1)
        n_cat = jnp.where(strict, g_b[0:C], 0.0)
        a_ak = jnp.where(strict, g_k[0:C], 0.0)
        a_rb = jnp.where(incl, g_b[C:], 0.0)
        a_rk = jnp.where(incl, g_k[C:], 0.0)
        s_prev = [s_scr[bi, g] for g in range(G)]
        proj = jnp.concatenate([_dot_nt(lhs_g[g], s_prev[g]) for g in range(G)], axis=1)
        doublings = max(int(math.log2(C)) - 1, 0)
        t_cat = eye + n_cat
        pw = n_cat
        if doublings:
            pw = per_head(n_cat, block_diag(n_cat, mask_cc, gc))
        yield None
        for it in range(doublings):
            pw_bd = block_diag(pw, mask_cc, gc)
            if it < doublings - 1:
                both = per_head(jnp.concatenate([t_cat, pw], axis=0), pw_bd)
                t_cat = t_cat + both[0:C]
                pw = both[C:]
            else:
                t_cat = t_cat + per_head(t_cat, pw_bd)
            yield None
        v_bd = block_diag(v, mask_ch, wl)
        rhs = proj[0:C] + per_head(a_ak, v_bd)
        yield None
        u = per_head(t_cat, block_diag(rhs, mask_ch, wl))
        yield None
        y = proj[C:] + per_head(a_rb, block_diag(u, mask_ch, wl)) + per_head(a_rk, v_bd)
        pad = LANES - 2 * C
        uv = jnp.concatenate([u, v] + ([jnp.zeros((pad, W), F32)] if pad > 0 else []), axis=0).astype(BF16)
        bk = jnp.concatenate([bh, kh] + ([jnp.zeros((pad, W), F32)] if pad > 0 else []), axis=0).astype(BF16)
        decay = jnp.exp(cl_last)
        for g in range(G):
            lanes = slice(g * wl, (g + 1) * wl)
            upd = _dot_tn(uv[:, lanes], bk[:, lanes])
            s_scr[bi, g] = s_prev[g] * decay[:, lanes] + jnp.where(mask_ss, upd, 0.0)

        @pl.when(c == pl.num_programs(1) - 1)
        def _():
            for h in range(H):
                o = (h % hpg) * N
                sout_ref[bi, h] = s_scr[bi, h // hpg, o:o + N, o:o + N]

        (sy,) = yield [("ones", jnp.concatenate(hi_lo(y), axis=0))]
        d = y - (sy[0:C] + sy[C:]) * inv_n
        (sd,) = yield [("ones", jnp.concatenate(hi_lo(d * d), axis=0))]
        var = (sd[0:C] + sd[C:]) * inv_n
        yn = d * lax.rsqrt(var + GN_EPS) * lg_ref[...] + lb_ref[...]
        y_ref[bi] = ((yn + bonus) * gate).astype(y_ref.dtype)

    seqs = [one_sequence(bi) for bi in range(nb)]
    replies = [None] * nb
    while True:
        asks, finished = [], 0
        for seq, reply in zip(seqs, replies):
            try:
                asks.append(seq.send(reply))
            except StopIteration:
                finished += 1
        if finished:
            assert finished == nb
            break
        if asks[0] is None:
            replies = [None] * nb
            continue
        replies = [[] for _ in range(nb)]
        for qi, (name, _) in enumerate(asks[0]):
            lhs = jnp.concatenate([ask[qi][1] for ask in asks], axis=0).astype(BF16)
            parts = shared[name]
            kw = lhs.shape[1] // len(parts)
            z = jnp.concatenate([jnp.dot(lhs[:, g * kw:(g + 1) * kw], part[...], preferred_element_type=F32)
                                 for g, part in enumerate(parts)], axis=1)
            rows = z.shape[0] // nb
            for si in range(nb):
                replies[si].append(z[si * rows:(si + 1) * rows])


def _rwkv(p, shift_prev, s0, prm, layer, batch, t_len, t_valid, chunk):
    W = RWKV_WIDTH
    nc = t_len // chunk
    nb = _largest_divisor(batch, RWKV_SEQS_PER_STEP)
    vec = lambda n: pl.BlockSpec((None, 1, n), lambda b, c: (layer, 0, 0))
    mat = lambda r, n: pl.BlockSpec((None, r, n), lambda b, c: (layer, 0, 0))
    state = pl.BlockSpec((nb, RWKV_HEADS, RWKV_HEAD, RWKV_HEAD), lambda b, c: (b, 0, 0, 0))
    groups = _rwkv_groups(chunk)
    kern = functools.partial(_rwkv_kernel, chunk=chunk, t_valid=t_valid)
    return pl.pallas_call(
        kern,
        grid=(batch // nb, nc),
        in_specs=[
            pl.BlockSpec((nb, chunk, RWKV_PROJ), lambda b, c: (b, c, 0)),
            pl.BlockSpec((nb, 1, RWKV_PROJ), lambda b, c: (b, 0, 0)),
            state,
            vec(RWKV_PROJ), vec(W), mat(LORA_WIDTH, W), vec(W), mat(LORA_WIDTH, W), mat(LORA_WIDTH, W),
            vec(W), vec(W), vec(W), vec(W), vec(W), pl.BlockSpec((W, W), lambda b, c: (0, 0)),
        ],
        out_specs=[pl.BlockSpec((nb, chunk, W), lambda b, c: (b, c, 0)), state],
        out_shape=[
            jax.ShapeDtypeStruct((batch, t_len, W), _mix_dtype(chunk)),
            jax.ShapeDtypeStruct((batch, RWKV_HEADS, RWKV_HEAD, RWKV_HEAD), F32),
        ],
        scratch_shapes=[pltpu.VMEM((nb, groups, W // groups, W // groups), F32),
                        pltpu.VMEM((nb, 1, RWKV_PROJ), F32)],
        compiler_params=_cparams(("parallel", "arbitrary")),
        name="rwkv7",
    )(p, shift_prev, s0, prm["mu"], prm["w0"], prm["w_up"], prm["a0"], prm["a_up"], prm["g_up"],
      prm["k_k"], prm["k_a"], prm["r_k"], prm["lnx_g"], prm["lnx_b"], prm["ones_bd"])


def _lru_kernel(x0_ref, x1_ref, g0_ref, g1_ref, conv_ref, h0_ref, cw_ref, cb_ref, wa_ref, ba_ref, wx_ref, bx_ref,
                lam_ref, y_ref, hout_ref, tail_scr, h_scr, *, chunk, t_valid):
    C = chunk
    c = pl.program_id(1)

    @pl.when(c == 0)
    def _():
        tail_scr[...] = jnp.zeros_like(tail_scr)
        tail_scr[SUBLANES - (CONV_W - 1):, :] = conv_ref[...]
        h_scr[...] = h0_ref[...]

    x = jnp.concatenate([x0_ref[...], x1_ref[...]], axis=1)
    gt = jnp.concatenate([g0_ref[...], g1_ref[...]], axis=1)
    row_in_group = lax.broadcasted_iota(jnp.int32, (C, 1), 0) % SUBLANES

    def rotate_in_groups(z, dlt):
        z3 = z.reshape(z.shape[0] // SUBLANES, SUBLANES, z.shape[1])
        return pltpu.roll(z3, dlt, axis=1).reshape(z.shape)

    x_prev = tail_scr[...]
    if C > SUBLANES:
        x_prev = jnp.concatenate([x_prev, x[:C - SUBLANES]], axis=0)
    tail_scr[...] = x[C - SUBLANES:, :]
    xc = cb_ref[...] + x * cw_ref[CONV_W - 1:CONV_W, :]
    for dlt in range(1, CONV_W):
        sh = jnp.where(row_in_group >= dlt, rotate_in_groups(x, dlt), rotate_in_groups(x_prev, dlt))
        xc = xc + sh * cw_ref[CONV_W - 1 - dlt:CONV_W - dlt, :]

    gate_r = jax.nn.sigmoid(_dot(xc, wa_ref[...]) + ba_ref[...])
    gate_i = jax.nn.sigmoid(_dot(xc, wx_ref[...]) + bx_ref[...])
    log_a = -RG_C * gate_r * _softplus(-lam_ref[...])
    a = jnp.exp(log_a)
    th = jnp.tanh(log_a)
    u = jnp.sqrt(-2.0 * th / (1.0 - th)) * (gate_i * xc)

    dlt = 1
    while dlt < SUBLANES:
        keep = row_in_group >= dlt
        a_sh = jnp.where(keep, rotate_in_groups(a, dlt), 1.0)
        u_sh = jnp.where(keep, rotate_in_groups(u, dlt), 0.0)
        u = a * u_sh + u
        a = a * a_sh
        dlt *= 2
    carry = h_scr[...]
    groups = []
    for gi in range(C // SUBLANES):
        rows = slice(gi * SUBLANES, (gi + 1) * SUBLANES)
        groups.append(a[rows] * carry + u[rows])
        carry = groups[-1][SUBLANES - 1:SUBLANES, :]
    h = jnp.concatenate(groups, axis=0)
    last = min(t_valid, C) - 1
    h_scr[...] = h[last:last + 1, :]
    hout_ref[...] = h[last:last + 1, :]
    y_ref[...] = (h * jax.nn.gelu(gt)).astype(y_ref.dtype)


def _lru(pxq, conv_prev, h0, prm, layer, batch, t_len, t_valid, chunk):
    W = LRU_WIDTH
    nc = t_len // chunk
    half = W // 2
    col0 = RWKV_PROJ // half
    assert RWKV_PROJ % half == 0
    part = lambda k: pl.BlockSpec((chunk, half), lambda b, c: (b * nc + c, col0 + k))
    vec = lambda n: pl.BlockSpec((None, 1, n), lambda b, c: (layer, 0, 0))
    mat = lambda r, n: pl.BlockSpec((None, r, n), lambda b, c: (layer, 0, 0))
    kern = functools.partial(_lru_kernel, chunk=chunk, t_valid=t_valid)
    return pl.pallas_call(
        kern,
        grid=(batch, nc),
        in_specs=[
            part(0), part(1), part(2), part(3),
            pl.BlockSpec((None, CONV_W - 1, W), lambda b, c: (b, 0, 0)),
            pl.BlockSpec((None, 1, W), lambda b, c: (b, 0, 0)),
            mat(CONV_W, W),
            vec(W),
            mat(W, W), vec(W),
            mat(W, W), vec(W),
            vec(W),
        ],
        out_specs=[
            pl.BlockSpec((chunk, W), lambda b, c: (b * nc + c, 0)),
            pl.BlockSpec((None, 1, W), lambda b, c: (b, 0, 0)),
        ],
        out_shape=[
            jax.ShapeDtypeStruct((batch * t_len, W), _mix_dtype(chunk)),
            jax.ShapeDtypeStruct((batch, 1, W), F32),
        ],
        scratch_shapes=[pltpu.VMEM((SUBLANES, W), F32), pltpu.VMEM((1, W), F32)],
        compiler_params=_cparams(("parallel", "arbitrary")),
        name="rglru",
    )(pxq, pxq, pxq, pxq, conv_prev, h0, prm["conv_w"], prm["conv_b"], prm["wa_bd"], prm["ba"], prm["wx_bd"],
      prm["bx"], prm["lam"])


def _largest_divisor(n, cap):
    return max(u for u in range(1, cap + 1) if n % u == 0)


def _attn_prompt_kernel(q_ref, kil_ref, vil_ref, bias_ref, o_ref, k_ref, v_ref, og_scr, lse_scr, *, seq):
    B = ATTN_SPAN
    scale = ATTN_HEAD ** -0.5
    h = pl.program_id(1)
    R = ATTN_MERGE_ROWS

    def gather_head(src_ref, dst_ref, i):
        dst_ref[pl.ds(i * R, R), :] = src_ref[pl.ds(h + i * (R * ATTN_HEADS), R, stride=ATTN_HEADS), :]

    for i in range(seq // R):
        gather_head(kil_ref, k_ref, i)

    blocks = [(gi, dil, cls + blk * (B * dil), blk == 0)
              for gi, (win, dil) in sorted(enumerate(DIL_PAIRS), key=lambda e: -e[1][1])
              for cls in range(dil) for blk in range(seq // (dil * B))]
    batches = [blocks[i:i + ATTN_UNROLL] for i in range(0, len(blocks), ATTN_UNROLL)]
    assert blocks[-1][1] == 1

    def rows(dil, start, first):
        start, n = (start, B) if first else (start - B * dil, 2 * B)
        return pl.ds(start, n) if dil == 1 else pl.ds(start, n, stride=dil)

    def scores(batch):
        return [_dot_nt(q_ref[rows(dil, qs, True), :], k_ref[rows(dil, qs, first), :]) * scale
                + (bias_ref[gi, :, B:] if first else bias_ref[gi]) for gi, dil, qs, first in batch]

    def softmax(ss):
        out = []
        for s in ss:
            mx = jnp.max(s, axis=-1, keepdims=True)
            pr = jnp.exp(s - mx)
            den = jnp.sum(pr, axis=-1, keepdims=True)
            out.append(((pr / den).astype(BF16), mx + jnp.log(den)))
        return out

    def values_and_store(batch, probs):
        outs = [_dot(pn, v_ref[rows(dil, qs, first), :]) for (gi, dil, qs, first), (pn, _) in zip(batch, probs)]
        for (gi, dil, qs, _), o, (_, lse) in zip(batch, outs, probs):
            og_scr[gi, rows(dil, qs, True), :] = o
            lse_scr[gi, rows(dil, qs, True), :] = jnp.broadcast_to(lse, (B, ATTN_HEAD))

    def merge(i):
        sl = pl.ds(i * R, R)
        ls = [lse_scr[gi, sl, :] for gi in range(len(DIL_PAIRS))]
        mx = functools.reduce(jnp.maximum, ls)
        ws = [jnp.exp(l - mx) for l in ls]
        tot = functools.reduce(lambda a, b: a + b, ws)
        acc = (ws[0] / tot) * og_scr[0, sl, :]
        for gi in range(1, len(DIL_PAIRS)):
            acc = acc + (ws[gi] / tot) * og_scr[gi, sl, :]
        o_ref[sl, :] = acc.astype(o_ref.dtype)

    pending = scores(batches[0])
    for i in range(seq // R):
        gather_head(vil_ref, v_ref, i)
    merged = 0
    contiguous_done = 0
    for bi, batch in enumerate(batches):
        upcoming = scores(batches[bi + 1]) if bi + 1 < len(batches) else None
        values_and_store(batch, softmax(pending))
        pending = upcoming
        contiguous_done += sum(1 for blk in batch if blk[1] == 1)
        while (merged + 1) * R <= contiguous_done * B:
            merge(merged)
            merged += 1
    assert merged == seq // R


def _attn_prompt(pxq, k_all, v_all, bias, layer, batch, seq):
    H = ATTN_HEADS
    G = len(DIL_PAIRS)
    q_col0 = (PXQ_WIDTH - ATTN_WIDTH) // ATTN_HEAD
    kern = functools.partial(_attn_prompt_kernel, seq=seq)
    kv = pl.BlockSpec((seq * H, ATTN_HEAD), lambda b, h: (layer * batch + b, 0))
    return pl.pallas_call(
        kern,
        grid=(batch, H),
        in_specs=[
            pl.BlockSpec((seq, ATTN_HEAD), lambda b, h: (b, q_col0 + h)),
            kv, kv,
            pl.BlockSpec((G, None, ATTN_SPAN, 2 * ATTN_SPAN), lambda b, h: (0, h, 0, 0)),
        ],
        out_specs=pl.BlockSpec((seq, ATTN_HEAD), lambda b, h: (b, h)),
        out_shape=jax.ShapeDtypeStruct((batch * seq, ATTN_WIDTH), BF16),
        scratch_shapes=[pltpu.VMEM((seq, ATTN_HEAD), F32), pltpu.VMEM((seq, ATTN_HEAD), F32),
                        pltpu.VMEM((G, seq, ATTN_HEAD), F32), pltpu.VMEM((G, seq, ATTN_HEAD), F32)],
        compiler_params=_cparams(("parallel", "arbitrary")),
        name="attn_prompt",
    )(pxq, k_all, v_all, bias)


def _attn_sample_kernel(pxq_ref, kn_ref, vn_ref, kc_ref, vc_ref, bias_ref, o_ref, q_scr, kn_scr, vn_scr, *, t_new, win):
    scale = ATTN_HEAD ** -0.5
    H = ATTN_HEADS
    G = len(DIL_PAIRS)
    TP = q_scr.shape[1]
    q_col0 = PXQ_WIDTH - ATTN_WIDTH
    q_scr[...] = jnp.zeros_like(q_scr)
    kn_scr[...] = jnp.zeros_like(kn_scr)
    vn_scr[...] = jnp.zeros_like(vn_scr)
    scores = []
    for h in range(H):
        q_scr[h, 0:t_new, :] = pxq_ref[:, q_col0 + h * ATTN_HEAD:q_col0 + (h + 1) * ATTN_HEAD]
        kn_scr[h, 0:t_new, :] = kn_ref[pl.ds(h, t_new, stride=H), :]
        vn_scr[h, 0:t_new, :] = vn_ref[pl.ds(h, t_new, stride=H), :]
        qb = q_scr[h].astype(BF16)
        kc = kc_ref[pl.ds(h, win, stride=H), :]
        scores.append(jnp.concatenate([_dot_nt(qb, kc), _dot_nt(qb, kn_scr[h])], axis=1) * scale)
    probs = []
    for h, s in enumerate(scores):
        prs, lses = [], []
        for gi in range(G):
            sg = s + bias_ref[gi, h]
            mx = jnp.max(sg, axis=-1, keepdims=True)
            pr = jnp.exp(sg - mx)
            den = jnp.sum(pr, axis=-1, keepdims=True)
            prs.append(pr / den)
            lses.append(mx + jnp.log(den))
        probs.append((jnp.concatenate(prs, axis=0).astype(BF16), lses))
    outs = []
    for h, (pall, _) in enumerate(probs):
        vc = vc_ref[pl.ds(h, win, stride=H), :]
        outs.append(_dot(pall[:, :win], vc) + _dot(pall[:, win:], vn_scr[h]))
    for h, (oall, (_, lses)) in enumerate(zip(outs, probs)):
        mx = functools.reduce(jnp.maximum, lses)
        ws = [jnp.exp(l - mx) for l in lses]
        tot = functools.reduce(lambda a, b: a + b, ws)
        acc = (ws[0] / tot) * oall[0:TP]
        for gi in range(1, G):
            acc = acc + (ws[gi] / tot) * oall[gi * TP:(gi + 1) * TP]
        o_ref[:, h * ATTN_HEAD:(h + 1) * ATTN_HEAD] = acc[0:t_new].astype(o_ref.dtype)


def _attn_sample(pxq, k_new, v_new, k_cache, v_cache, bias, layer, batch, t_new):
    H = ATTN_HEADS
    win = k_cache.shape[2] // H
    tp = bias.shape[2]
    kern = functools.partial(_attn_sample_kernel, t_new=t_new, win=win)
    cache = pl.BlockSpec((None, None, win * H, ATTN_HEAD), lambda b: (layer, b, 0, 0))
    new = pl.BlockSpec((None, t_new * H, ATTN_HEAD), lambda b: (layer * batch + b, 0, 0))
    return pl.pallas_call(
        kern,
        grid=(batch,),
        in_specs=[pl.BlockSpec((None, t_new, pxq.shape[2]), lambda b: (b, 0, 0)), new, new, cache, cache,
                  pl.BlockSpec(bias.shape, lambda b: (0, 0, 0, 0))],
        out_specs=pl.BlockSpec((None, t_new, ATTN_WIDTH), lambda b: (b, 0, 0)),
        out_shape=jax.ShapeDtypeStruct((batch, t_new, ATTN_WIDTH), F32),
        scratch_shapes=[pltpu.VMEM((H, tp, ATTN_HEAD), F32), pltpu.VMEM((H, LANES, ATTN_HEAD), F32),
                        pltpu.VMEM((H, LANES, ATTN_HEAD), F32)],
        compiler_params=_cparams(("parallel",)),
        name="attn_sample",
    )(pxq, k_new, v_new, k_cache, v_cache, bias)


def _bias_kernel(idx_ref, rb_ref, o_ref):
    h = pl.program_id(0)
    idx = idx_ref[...]
    acc = jnp.full(idx.shape, NEG, F32)
    for b in range(N_BUCKETS):
        acc = jnp.where(idx == b, rb_ref[b, h], acc)
    o_ref[...] = acc


def _bias_table(idx, rel_bias):
    g, r, c = idx.shape
    heads = rel_bias.shape[1]
    return pl.pallas_call(
        _bias_kernel,
        grid=(heads,),
        in_specs=[pl.BlockSpec((g, r, c), lambda h: (0, 0, 0)),
                  pl.BlockSpec(memory_space=pltpu.SMEM)],
        out_specs=pl.BlockSpec((g, None, r, c), lambda h: (0, h, 0, 0)),
        out_shape=jax.ShapeDtypeStruct((g, heads, r, c), F32),
        compiler_params=_cparams(("parallel",)),
        name="bias_table",
    )(jnp.asarray(idx, jnp.int32), rel_bias.astype(F32))


def _t5_bucket_static(dist):
    dist = np.asarray(dist, np.int64)
    exact = N_BUCKETS // 2
    d = np.maximum(dist, 1).astype(np.float64)
    large = exact + (np.log(d / exact) / math.log(REL_MAX_DIST / exact) * (N_BUCKETS - exact)).astype(np.int64)
    return np.where(dist < exact, dist, np.minimum(large, N_BUCKETS - 1))


def _prompt_bias_idx():
    B = ATTN_SPAN
    qi = np.arange(B)[:, None]
    kj = np.arange(2 * B)[None, :]
    delta = qi + B - kj
    valid = (delta >= 0) & (delta <= B)
    return np.stack([np.where(valid, _t5_bucket_static(np.clip(delta, 0, B) * dil), -1) for _, dil in DIL_PAIRS])


def _sample_bias_idx(t_new, t_pad, win, total):
    r = np.arange(total)[None, :]
    t = np.arange(t_pad)[:, None]
    dist = win + t - r
    tabs = []
    for _, dil in DIL_PAIRS:
        assert win >= dil * ATTN_SPAN
        valid = (dist >= 0) & (dist % dil == 0) & (dist <= dil * ATTN_SPAN) & (t < t_new)
        tabs.append(np.where(valid, _t5_bucket_static(np.clip(dist, 0, dil * ATTN_SPAN)), -1))
    return np.stack(tabs)


def _block_diag(blocks):
    n, c, d = blocks.shape
    eye = jnp.eye(n, dtype=blocks.dtype)
    return jnp.einsum("ncd,nm->ncmd", blocks, eye).reshape(n * c, n * d)


def _mixer_params(a):
    W = RWKV_WIDTH
    depth = a["rwkv_mu"].shape[0]
    row = lambda t: t.reshape(depth, 1, -1).astype(F32)

    def lora_pad(w, off):
        z = jnp.zeros((depth, LORA_WIDTH, W), F32)
        return z.at[:, off:off + w.shape[1]].set(w).astype(BF16)

    head_of = np.arange(W) // RWKV_HEAD
    ones_bd = jnp.asarray((head_of[:, None] == head_of[None, :]).astype(np.float32), BF16)
    rwkv = dict(
        mu=row(a["rwkv_mu"]), w0=row(a["rwkv_w0"]), a0=row(a["rwkv_a0"]),
        w_up=lora_pad(a["rwkv_w_up"], 0),
        a_up=lora_pad(a["rwkv_a_up"], DECAY_LORA),
        g_up=lora_pad(a["rwkv_g_up"], DECAY_LORA + ICLR_LORA),
        k_k=row(a["rwkv_k_k"]), k_a=row(a["rwkv_k_a"]), r_k=row(a["rwkv_r_k"]),
        lnx_g=row(a["rwkv_lnx_g"]), lnx_b=row(a["rwkv_lnx_b"]),
        ones_bd=ones_bd,
    )
    lru = dict(
        conv_w=a["lru_conv_w"].astype(F32), conv_b=row(a["lru_conv_b"]),
        wa_bd=jax.vmap(_block_diag)(a["lru_wa"]).astype(BF16), ba=row(a["lru_ba"]),
        wx_bd=jax.vmap(_block_diag)(a["lru_wx"]).astype(BF16), bx=row(a["lru_bx"]),
        lam=row(a["lru_lambda"]),
    )
    return dict(rwkv=rwkv, lru=lru)


def _shared_params(a):
    gain = lambda t: t.reshape(t.shape[0], 1, -1).astype(F32)
    return dict(
        g_mix_pre=gain(a["norm_mix_pre"]), g_mix_post=gain(a["norm_mix_post"]),
        g_ffn_pre=gain(a["norm_ffn_pre"]), g_ffn_post=gain(a["norm_ffn_post"]),
    )


def _layer(x, layer, sp, wts, lp, batch, t_len, shift_prev, s0, conv_prev, h0, kv_all, attend):
    emit = wts is None
    src = sp["f32"] if emit else wts
    wl = layer if emit else 0
    copies = {}
    pxq, k_all, v_all, *w_copy = _in_proj(x, sp["g_mix_pre"], src["w_in"], layer, wl, *kv_all, emit_bf16=emit)
    copies["w_in"] = w_copy[0] if emit else None
    pxq3 = pxq.reshape(batch, t_len, pxq.shape[1])
    if t_len % RWKV_CHUNK == 0:
        y_a, s_fin = _rwkv(pxq3, shift_prev, s0, lp["rwkv"], layer, batch, t_len, t_len, RWKV_CHUNK)
        y_a = y_a.reshape(batch * t_len, -1)
        y_b, h_fin = _lru(pxq, conv_prev, h0, lp["lru"], layer, batch, t_len, t_len, LRU_CHUNK)
    else:
        assert t_len <= SUBLANES
        padded = jnp.pad(pxq3, ((0, 0), (0, SUBLANES - t_len), (0, 0)))
        y_a, s_fin = _rwkv(padded, shift_prev, s0, lp["rwkv"], layer, batch, SUBLANES, t_len, SUBLANES)
        y_a = y_a[:, :t_len].reshape(batch * t_len, -1)
        y_b, h_fin = _lru(padded.reshape(batch * SUBLANES, pxq.shape[1]), conv_prev, h0, lp["lru"], layer, batch, SUBLANES,
                          t_len, SUBLANES)
        y_b = y_b.reshape(batch, SUBLANES, -1)[:, :t_len].reshape(batch * t_len, -1)
    y_c = attend(pxq, k_all, v_all)

    x = _out_proj(y_a, y_b, y_c, src["w_out"], sp["g_mix_post"], x, layer, wl, emit_bf16=emit)
    if emit:
        x, copies["w_out"] = x
    x = _ffn(x, sp["g_ffn_pre"], src["w1"], src["w2"], sp["g_ffn_post"], layer, wl, emit_bf16=emit)
    if emit:
        x, copies["w1"], copies["w2"] = x

    lru_x = pxq3[:, :, RWKV_PROJ:RWKV_PROJ + LRU_WIDTH]
    if t_len >= CONV_W - 1:
        conv_new = lru_x[:, t_len - (CONV_W - 1):]
    else:
        conv_new = jnp.concatenate([conv_prev, lru_x], axis=1)[:, -(CONV_W - 1):]
    state = (pxq3[:, -1, :RWKV_PROJ], s_fin, conv_new, h_fin.reshape(batch, LRU_WIDTH))
    return x, state, (k_all, v_all), copies


def kernel(x_prompt, x_sample, state_rwkv_wkv, state_rwkv_shift, state_lru_h, state_lru_conv, cache_attn_k, cache_attn_v, rel_bias, norm_mix_pre, norm_mix_post, norm_ffn_pre, norm_ffn_post, w_in, w_out, rwkv_mu, rwkv_w0, rwkv_w_up, rwkv_a0, rwkv_a_up, rwkv_g_up, rwkv_k_k, rwkv_k_a, rwkv_r_k, rwkv_lnx_g, rwkv_lnx_b, lru_conv_w, lru_conv_b, lru_wa, lru_ba, lru_wx, lru_bx, lru_lambda, ffn_w1, ffn_w2):
    a = dict(norm_mix_pre=norm_mix_pre, norm_mix_post=norm_mix_post, norm_ffn_pre=norm_ffn_pre,
             norm_ffn_post=norm_ffn_post, w_in=w_in, w_out=w_out, rwkv_mu=rwkv_mu, rwkv_w0=rwkv_w0,
             rwkv_w_up=rwkv_w_up, rwkv_a0=rwkv_a0, rwkv_a_up=rwkv_a_up, rwkv_g_up=rwkv_g_up, rwkv_k_k=rwkv_k_k,
             rwkv_k_a=rwkv_k_a, rwkv_r_k=rwkv_r_k, rwkv_lnx_g=rwkv_lnx_g, rwkv_lnx_b=rwkv_lnx_b,
             lru_conv_w=lru_conv_w, lru_conv_b=lru_conv_b, lru_wa=lru_wa, lru_ba=lru_ba, lru_wx=lru_wx,
             lru_bx=lru_bx, lru_lambda=lru_lambda, ffn_w1=ffn_w1, ffn_w2=ffn_w2)
    depth = w_in.shape[0]
    pb, seq, _ = x_prompt.shape
    sb, t_new, _ = x_sample.shape
    win = cache_attn_k.shape[2]
    keep = min(ATTN_WINDOW, seq)
    total = win + LANES
    assert t_new <= LANES and seq % (DIL_PAIRS[-1][1] * ATTN_SPAN) == 0

    bias_p = _bias_table(_prompt_bias_idx(), rel_bias)
    bias_s = _bias_table(_sample_bias_idx(t_new, SUBLANES, win, total), rel_bias)
    k_cache = cache_attn_k.reshape(depth, sb, win * ATTN_HEADS, ATTN_HEAD)
    v_cache = cache_attn_v.reshape(depth, sb, win * ATTN_HEADS, ATTN_HEAD)

    xp = x_prompt.reshape(pb * seq, D_MODEL)
    xs = x_sample.reshape(sb * t_new, D_MODEL)
    new_p, new_s = [], []
    sp = _shared_params(a)
    kv_p = tuple(jnp.zeros((depth * pb * seq * ATTN_HEADS, ATTN_HEAD), F32) for _ in range(2))
    kv_s = tuple(jnp.zeros((depth * sb * t_new * ATTN_HEADS, ATTN_HEAD), F32) for _ in range(2))
    sp["f32"] = dict(w_in=w_in.astype(F32), w_out=w_out.astype(F32), w1=ffn_w1.astype(F32), w2=ffn_w2.astype(F32))
    lp = _mixer_params(a)
    for l in range(depth):
        attend_s = lambda pxq, k_all, v_all, l=l: _attn_sample(
            pxq.reshape(sb, t_new, pxq.shape[1]), k_all.reshape(depth * sb, t_new * ATTN_HEADS, ATTN_HEAD),
            v_all.reshape(depth * sb, t_new * ATTN_HEADS, ATTN_HEAD), k_cache, v_cache, bias_s, l, sb, t_new,
        ).reshape(sb * t_new, ATTN_WIDTH)
        xs, st_s, kv_s, wts = _layer(xs, l, sp, None, lp, sb, t_new,
                                     state_rwkv_shift[l].reshape(sb, 1, RWKV_PROJ), state_rwkv_wkv[l].astype(F32),
                                     state_lru_conv[l], state_lru_h[l].reshape(sb, 1, LRU_WIDTH), kv_s, attend_s)
        attend_p = lambda pxq, k_all, v_all, l=l: _attn_prompt(pxq, k_all, v_all, bias_p, l, pb, seq)
        xp, st_p, kv_p, _ = _layer(xp, l, sp, wts, lp, pb, seq,
                                   jnp.zeros((pb, 1, RWKV_PROJ), F32),
                                   jnp.zeros((pb, RWKV_HEADS, RWKV_HEAD, RWKV_HEAD), F32),
                                   jnp.zeros((pb, CONV_W - 1, LRU_WIDTH), F32), jnp.zeros((pb, 1, LRU_WIDTH), F32),
                                   kv_p, attend_p)
        new_p.append(st_p)
        new_s.append(st_s)

    stack = lambda sts, i: jnp.stack([s[i] for s in sts])
    k_p, v_p = (t.reshape(depth, pb, seq, ATTN_HEADS, ATTN_HEAD)[:, :, seq - keep:] for t in kv_p)
    k_s, v_s = (t.reshape(depth, sb, t_new, ATTN_HEADS, ATTN_HEAD) for t in kv_s)
    return (xp.reshape(pb, seq, D_MODEL), xs.reshape(sb, t_new, D_MODEL),
            stack(new_p, 1), stack(new_s, 1), stack(new_p, 0), stack(new_s, 0),
            stack(new_p, 3), stack(new_s, 3), stack(new_p, 2), stack(new_s, 2),
            k_p, k_s, v_p, v_s)
```
